```python
import jax, jax.numpy as jnp
from jax import lax
import numpy as np

D_MODEL = 2048
BATCH = 2
SEQ = 4096
DEPTH = 1
DEC_BATCH = 16
DEC_SEQ = 32
PAST_LEN = 4096

CHUNK = 64
Q_BLOCK = 128
EPS = 1e-6
POOL_WINDOWS = (2, 4, 8, 16)
N_POOL_GROUPS = 4
POOL_W = D_MODEL // 2
POOL_GW = POOL_W // N_POOL_GROUPS
POOL_OUT_GW = D_MODEL // N_POOL_GROUPS
POOL_STATE = max(POOL_WINDOWS) - 1
MLA_HEADS = 16
Q_LORA = 512
KV_LORA = 512
QK_NOPE = 128
QK_ROPE = 64
QK_HEAD = QK_NOPE + QK_ROPE
V_HEAD = 128
ROPE_THETA = 10000.0
MLA_SCALE = QK_HEAD ** -0.5
N_MEM = 256
MEM_HEADS = 4
MEM_HEAD_DIM = 256
MEM_W = MEM_HEADS * MEM_HEAD_DIM
MEM_SCALE = MEM_HEAD_DIM ** -0.5
N_BRANCH = 3
D_FF = 4 * D_MODEL
D_IN = POOL_W + Q_LORA + KV_LORA + QK_ROPE + MEM_W + N_BRANCH * D_MODEL

kernel_name = "hybrid_pool_mla_memory_stream_step"


def rmsnorm(x, g):
    xf = x.astype(jnp.float32)
    y = xf * lax.rsqrt(jnp.mean(xf * xf, axis=-1, keepdims=True) + EPS)
    return (y * g.astype(jnp.float32)).astype(x.dtype)


def rope(x, pos):
    half = QK_ROPE // 2
    inv = 1.0 / (ROPE_THETA ** (jnp.arange(half, dtype=jnp.float32) * (2.0 / QK_ROPE)))
    ang = pos.astype(jnp.float32)[:, None] * inv[None, :]
    cos = jnp.cos(ang)[:, None, :]
    sin = jnp.sin(ang)[:, None, :]
    xf = x.astype(jnp.float32)
    x1, x2 = xf[..., :half], xf[..., half:]
    out = jnp.concatenate([x1 * cos - x2 * sin, x2 * cos + x1 * sin], axis=-1)
    return out.astype(x.dtype)


def attend(q, k, v, scale, mask=None):
    s = jnp.einsum('bqhd,bkhd->bhqk', q, k).astype(jnp.float32) * scale
    if mask is not None:
        s = jnp.where(mask, s, -1e30)
    p = jax.nn.softmax(s, axis=-1).astype(v.dtype)
    return jnp.einsum('bhqk,bkhd->bqhd', p, v)


def chunk_attend(q, q_pos, k, v, k_pos):
    mask = (q_pos // CHUNK)[:, None] >= (k_pos // CHUNK)[None, :]
    return attend(q, k, v, MLA_SCALE, mask)


def split_in(z):
    sizes = (POOL_W, Q_LORA, KV_LORA, QK_ROPE, MEM_W, N_BRANCH * D_MODEL)
    parts, off = [], 0
    for s in sizes:
        parts.append(z[..., off:off + s])
        off += s
    return parts


def pool_branch(u, left, pos, w_pool, pool_scale):
    B, L, _ = u.shape
    xp = jnp.concatenate([left.astype(jnp.float32), u.astype(jnp.float32)], axis=1)
    cs = jnp.concatenate([jnp.zeros((B, 1, POOL_W), jnp.float32), jnp.cumsum(xp, axis=1)], axis=1)
    top = cs[:, POOL_STATE + 1:]
    means = []
    for g, win in enumerate(POOL_WINDOWS):
        sl = slice(g * POOL_GW, (g + 1) * POOL_GW)
        lo = POOL_STATE + 1 - win
        wsum = top[..., sl] - cs[:, lo:lo + L, sl]
        cnt = jnp.minimum(pos + 1, win).astype(jnp.float32)[None, :, None]
        means.append(wsum / cnt)
    d = jnp.concatenate(means, axis=-1) - u.astype(jnp.float32)
    d = d.astype(u.dtype).reshape(B, L, N_POOL_GROUPS, POOL_GW)
    y = jnp.einsum('blgc,gco->blgo', d, w_pool).reshape(B, L, D_MODEL)
    new_state = xp[:, -POOL_STATE:].astype(u.dtype)
    return y * pool_scale, new_state


def mla_queries(q_lat, pos, w):
    B, L, _ = q_lat.shape
    q = (rmsnorm(q_lat, w['g_q_lat']) @ w['w_qb']).reshape(B, L, MLA_HEADS, QK_HEAD)
    q = rmsnorm(q, w['g_q_head'])
    return jnp.concatenate([q[..., :QK_NOPE], rope(q[..., QK_NOPE:], pos)], axis=-1)


def mla_keys_values(c_kv, k_pe, pos, w):
    B, L, _ = c_kv.shape
    k_nope = (c_kv @ w['w_kb']).reshape(B, L, MLA_HEADS, QK_NOPE)
    kpe = jnp.broadcast_to(k_pe[:, :, None, :], (B, L, MLA_HEADS, QK_ROPE))
    k = rmsnorm(jnp.concatenate([k_nope, kpe], axis=-1), w['g_k_head'])
    k = jnp.concatenate([k[..., :QK_NOPE], rope(k[..., QK_NOPE:], pos)], axis=-1)
    v = (c_kv @ w['w_vb']).reshape(B, L, MLA_HEADS, V_HEAD)
    return k, v


def memory_kv(mem, w):
    B = mem.shape[0]
    m = rmsnorm(mem, w['g_mem']) @ w['w_mem_kv']
    k = rmsnorm(m[..., :MEM_W].reshape(B, N_MEM, MEM_HEADS, MEM_HEAD_DIM), w['g_mem_k'])
    v = m[..., MEM_W:].reshape(B, N_MEM, MEM_HEADS, MEM_HEAD_DIM)
    return k, v


def trunk_layer(x, pos, pool_left, mla_past, mem_k, mem_v, w):
    B, L, _ = x.shape
    h = rmsnorm(x, w['g_mix'])
    u, q_lat, kv_lat, k_pe, mq, gate_logits = split_in(h @ w['w_in'])
    y_pool, new_pool = pool_branch(u, pool_left, pos, w['w_pool'], w['pool_scale'])
    c_kv = rmsnorm(kv_lat, w['g_kv_lat'])
    q = mla_queries(q_lat, pos, w)
    if mla_past is None:
        k, v = mla_keys_values(c_kv, k_pe, pos, w)
        nb = L // Q_BLOCK
        qb = q.reshape(B, nb, Q_BLOCK, MLA_HEADS, QK_HEAD).transpose(1, 0, 2, 3, 4)
        pb = pos.reshape(nb, Q_BLOCK)
        o = lax.map(lambda a: chunk_attend(a[0], a[1], k, v, pos), (qb, pb))
        o = o.transpose(1, 0, 2, 3, 4).reshape(B, L, MLA_HEADS, V_HEAD)
    else:
        past_lat, past_kpe = mla_past
        n_past = past_lat.shape[1]
        all_lat = jnp.concatenate([past_lat, c_kv], axis=1)
        all_kpe = jnp.concatenate([past_kpe, k_pe], axis=1)
        k_pos = jnp.arange(n_past + L, dtype=jnp.int32)
        k, v = mla_keys_values(all_lat, all_kpe, k_pos, w)
        o = chunk_attend(q, pos, k, v, k_pos)
    y_mla = o.reshape(B, L, MLA_HEADS * V_HEAD) @ w['w_mla_o']
    qm = rmsnorm(mq.reshape(B, L, MEM_HEADS, MEM_HEAD_DIM), w['g_mem_q'])
    y_mem = attend(qm, mem_k, mem_v, MEM_SCALE).reshape(B, L, MEM_W) @ w['w_mem_o']
    gates = jax.nn.sigmoid((gate_logits + w['b_gate']).astype(jnp.float32)).astype(x.dtype)
    gates = gates.reshape(B, L, N_BRANCH, D_MODEL)
    merged = gates[:, :, 0] * y_pool + gates[:, :, 1] * y_mla + gates[:, :, 2] * y_mem
    x = x + merged @ w['w_out']
    f = rmsnorm(x, w['g_ff']) @ w['w_up']
    x = x + jnp.square(jax.nn.relu(f)) @ w['w_down']
    return x, c_kv, k_pe, new_pool


def setup_inputs(seed: int = 0) -> dict:
    key = jax.random.key(seed)
    ks = iter(jax.random.split(key, 40))
    f32 = jnp.float32

    def nrm(shape, scale=1.0):
        return jax.random.normal(next(ks), shape, f32) * scale

    def gain(n):
        return 1.0 + nrm((n,), 0.02)

    return {
        'x_prompt': nrm((BATCH, SEQ, D_MODEL)),
        'mem_prompt': nrm((BATCH, N_MEM, D_MODEL)),
        'x_sample': nrm((DEC_BATCH, DEC_SEQ, D_MODEL)),
        'cache_mla_latent': nrm((DEC_BATCH, PAST_LEN, KV_LORA)),
        'cache_mla_kpe': nrm((DEC_BATCH, PAST_LEN, QK_ROPE)),
        'state_pool': nrm((DEC_BATCH, POOL_STATE, POOL_W)),
        'cache_mem_k': nrm((DEC_BATCH, N_MEM, MEM_HEADS, MEM_HEAD_DIM)),
        'cache_mem_v': nrm((DEC_BATCH, N_MEM, MEM_HEADS, MEM_HEAD_DIM)),
        'g_mix': gain(D_MODEL),
        'w_in': nrm((D_MODEL, D_IN), D_MODEL ** -0.5),
        'b_gate': nrm((N_BRANCH * D_MODEL,), 0.01),
        'w_pool': nrm((N_POOL_GROUPS, POOL_GW, POOL_OUT_GW), POOL_GW ** -0.5),
        'pool_scale': gain(D_MODEL),
        'g_q_lat': gain(Q_LORA),
        'w_qb': nrm((Q_LORA, MLA_HEADS * QK_HEAD), Q_LORA ** -0.5),
        'g_q_head': gain(QK_HEAD),
        'g_kv_lat': gain(KV_LORA),
        'w_kb': nrm((KV_LORA, MLA_HEADS * QK_NOPE), KV_LORA ** -0.5),
        'w_vb': nrm((KV_LORA, MLA_HEADS * V_HEAD), KV_LORA ** -0.5),
        'g_k_head': gain(QK_HEAD),
        'w_mla_o': nrm((MLA_HEADS * V_HEAD, D_MODEL), (MLA_HEADS * V_HEAD) ** -0.5),
        'g_mem': gain(D_MODEL),
        'w_mem_kv': nrm((D_MODEL, 2 * MEM_W), D_MODEL ** -0.5),
        'g_mem_q': gain(MEM_HEAD_DIM),
        'g_mem_k': gain(MEM_HEAD_DIM),
        'w_mem_o': nrm((MEM_W, D_MODEL), MEM_W ** -0.5),
        'w_out': nrm((D_MODEL, D_MODEL), D_MODEL ** -0.5),
        'g_ff': gain(D_MODEL),
        'w_up': nrm((D_MODEL, D_FF), D_MODEL ** -0.5),
        'w_down': nrm((D_FF, D_MODEL), D_FF ** -0.5),
    }


def reference(x_prompt, mem_prompt, x_sample, cache_mla_latent, cache_mla_kpe, state_pool,
              cache_mem_k, cache_mem_v, g_mix, w_in, b_gate, w_pool, pool_scale, g_q_lat, w_qb,
              g_q_head, g_kv_lat, w_kb, w_vb, g_k_head, w_mla_o, g_mem, w_mem_kv, g_mem_q,
              g_mem_k, w_mem_o, w_out, g_ff, w_up, w_down):
    w = dict(g_mix=g_mix, w_in=w_in, b_gate=b_gate, w_pool=w_pool, pool_scale=pool_scale,
             g_q_lat=g_q_lat, w_qb=w_qb, g_q_head=g_q_head, g_kv_lat=g_kv_lat, w_kb=w_kb,
             w_vb=w_vb, g_k_head=g_k_head, w_mla_o=w_mla_o, g_mem=g_mem, w_mem_kv=w_mem_kv,
             g_mem_q=g_mem_q, g_mem_k=g_mem_k, w_mem_o=w_mem_o, w_out=w_out, g_ff=g_ff,
             w_up=w_up, w_down=w_down)
    L_p = x_prompt.shape[1]
    pos_p = jnp.arange(L_p, dtype=jnp.int32)
    mem_k_p, mem_v_p = memory_kv(mem_prompt, w)
    y_p = x_prompt
    pool_left_p = jnp.zeros((x_prompt.shape[0], POOL_STATE, POOL_W), x_prompt.dtype)
    for _ in range(DEPTH):
        y_p, lat_p, kpe_p, pool_p = trunk_layer(y_p, pos_p, pool_left_p, None, mem_k_p, mem_v_p, w)
    n_past = cache_mla_latent.shape[1]
    pos_s = n_past + jnp.arange(x_sample.shape[1], dtype=jnp.int32)
    y_s = x_sample
    for _ in range(DEPTH):
        y_s, lat_s, kpe_s, pool_s = trunk_layer(y_s, pos_s, state_pool,
                                                (cache_mla_latent, cache_mla_kpe),
                                                cache_mem_k, cache_mem_v, w)
    return (y_p, y_s, lat_p, kpe_p, pool_p, mem_k_p, mem_v_p, lat_s, kpe_s, pool_s)
```

```python
import functools

import jax
import jax.numpy as jnp
from jax import lax
from jax.experimental import pallas as pl
from jax.experimental.pallas import tpu as pltpu

F32 = jnp.float32
BF16 = jnp.bfloat16

EPS = 1e-6
CHUNK = 64
D_MODEL = 2048
POOL_WINDOWS = (2, 4, 8, 16)
POOL_W = 1024
POOL_GW = 256
POOL_OUT_GW = 512
POOL_STATE = 15
POOL_HALO = 16
MLA_HEADS = 16
Q_LORA = 512
KV_LORA = 512
QK_NOPE = 128
QK_ROPE = 64
QK_HEAD = QK_NOPE + QK_ROPE
HEAD_PAD = 256
V_HEAD = 128
ROPE_THETA = 10000.0
MLA_SCALE = QK_HEAD ** -0.5
N_MEM = 256
MEM_HEADS = 4
MEM_HEAD_DIM = 256
MEM_W = MEM_HEADS * MEM_HEAD_DIM
MEM_SCALE = MEM_HEAD_DIM ** -0.5
D_FF = 4 * D_MODEL
NEG_INF = -1e30

Z_U = 0
Z_QLAT = Z_U + POOL_W
Z_KVLAT = Z_QLAT + Q_LORA
Z_MQ = Z_KVLAT + KV_LORA
Z_KPE = Z_MQ + MEM_W
Z_W = Z_KPE + 128

V7X_VMEM_BYTES = 64 * 1024 * 1024
VMEM_CEILING = V7X_VMEM_BYTES - 6 * 1024 * 1024


def _params(semantics, vmem_bytes):
    return pltpu.CompilerParams(dimension_semantics=semantics,
                                vmem_limit_bytes=int(min(vmem_bytes, VMEM_CEILING)))


def _rms(x, g):
    return x * lax.rsqrt(jnp.mean(x * x, axis=-1, keepdims=True) + EPS) * g


def _dot(a, b):
    return jnp.dot(a, b, preferred_element_type=F32)


def _dot_nt(a, b):
    return lax.dot_general(a, b, (((1,), (1,)), ((), ())), preferred_element_type=F32)


def _rope128(x, c, sa, sb):
    return x * c + pltpu.roll(x, 32, 1) * sa + pltpu.roll(x, 96, 1) * sb


def _in_proj_kernel(x_ref, gmix_ref, wz_ref, gq_ref, gkv_ref, gmq_ref,
                    u_ref, hq_ref, ckv_ref, kpe_ref, qm_ref):
    h = _rms(x_ref[...], gmix_ref[...]).astype(BF16)
    u_ref[...] = _dot(h, wz_ref[:, Z_U:Z_QLAT])
    hq_ref[...] = _rms(_dot(h, wz_ref[:, Z_QLAT:Z_KVLAT]), gq_ref[...]).astype(BF16)
    ckv_ref[...] = _rms(_dot(h, wz_ref[:, Z_KVLAT:Z_MQ]), gkv_ref[...])
    kpe_ref[...] = _dot(h, wz_ref[:, Z_KPE:Z_W])
    for hh in range(MEM_HEADS):
        lo = hh * MEM_HEAD_DIM
        m = _dot(h, wz_ref[:, Z_MQ + lo:Z_MQ + lo + MEM_HEAD_DIM])
        qm_ref[:, lo:lo + MEM_HEAD_DIM] = _rms(m, gmq_ref[...]).astype(BF16)


def _in_proj(x, gmix, wz, gq, gkv, gmq_scaled, tm):
    t = x.shape[0]
    row = lambda w: pl.BlockSpec((tm, w), lambda i: (i, 0))
    full = lambda a: pl.BlockSpec(a.shape, lambda i: (0, 0))
    vmem = 2 * (tm * D_MODEL * 4 + wz.size * 2 + tm * (POOL_W * 4 + Q_LORA * 2 + KV_LORA * 4 + 128 * 4 + MEM_W * 2))
    vmem += 6 * tm * POOL_W * 4
    return pl.pallas_call(
        _in_proj_kernel,
        grid=(t // tm,),
        in_specs=[row(D_MODEL), full(gmix), full(wz), full(gq), full(gkv), full(gmq_scaled)],
        out_specs=[row(POOL_W), row(Q_LORA), row(KV_LORA), row(128), row(MEM_W)],
        out_shape=[jax.ShapeDtypeStruct((t, POOL_W), F32), jax.ShapeDtypeStruct((t, Q_LORA), BF16),
                   jax.ShapeDtypeStruct((t, KV_LORA), F32), jax.ShapeDtypeStruct((t, 128), F32),
                   jax.ShapeDtypeStruct((t, MEM_W), BF16)],
        compiler_params=_params(("parallel",), vmem),
        name="in_proj",
    )(x, gmix, wz, gq, gkv, gmq_scaled)


def _q_heads(hq, wq_ref, gq_ref, c, sa, sb, q_ref):
    for h in range(MLA_HEADS):
        lo = h * HEAD_PAD
        qh = _dot(hq, wq_ref[:, lo:lo + HEAD_PAD])
        r = lax.rsqrt(jnp.sum(qh * qh, axis=-1, keepdims=True) * (1.0 / QK_HEAD) + EPS)
        qh = qh * r * gq_ref[...]
        q_ref[:, lo:lo + QK_NOPE] = qh[:, :QK_NOPE].astype(BF16)
        q_ref[:, lo + QK_NOPE:lo + HEAD_PAD] = _rope128(qh[:, QK_NOPE:], c, sa, sb).astype(BF16)


def _qkv_kernel(hq_ref, ckv_ref, kpe_ref, wq_ref, wkb_ref, wvb_ref, gq_ref, gkn_ref, gkr_ref,
                c_ref, sa_ref, sb_ref, q_ref, k_ref, v_ref):
    c, sa, sb = c_ref[...], sa_ref[...], sb_ref[...]
    _q_heads(hq_ref[...], wq_ref, gq_ref, c, sa, sb, q_ref)
    ckv = ckv_ref[...].astype(BF16)
    kpe = kpe_ref[...]
    ss_pe = jnp.sum(kpe * kpe, axis=-1, keepdims=True)
    kr = _rope128(kpe * gkr_ref[...], c, sa, sb)
    for p in range(MLA_HEADS // 2):
        kn2 = _dot(ckv, wkb_ref[:, p * 2 * QK_NOPE:(p + 1) * 2 * QK_NOPE])
        for s in range(2):
            lo = (2 * p + s) * HEAD_PAD
            kn = kn2[:, s * QK_NOPE:(s + 1) * QK_NOPE]
            r = lax.rsqrt((jnp.sum(kn * kn, axis=-1, keepdims=True) + ss_pe) * (1.0 / QK_HEAD) + EPS)
            k_ref[:, lo:lo + QK_NOPE] = (kn * r * gkn_ref[...]).astype(BF16)
            k_ref[:, lo + QK_NOPE:lo + HEAD_PAD] = (kr * r).astype(BF16)
    v_ref[...] = _dot(ckv, wvb_ref[...]).astype(BF16)


def _qkv(hq, ckv, kpe, wq, wkb, wvb, gq, gkn, gkr, tabs, tm, seq_blocks):
    t = hq.shape[0]
    row = lambda w: pl.BlockSpec((tm, w), lambda i: (i, 0))
    full = lambda a: pl.BlockSpec(a.shape, lambda i: (0, 0))
    tab = pl.BlockSpec((tm, 128), lambda i: (i % seq_blocks, 0))
    vmem = 2 * (tm * (Q_LORA * 2 + KV_LORA * 4 + 128 * 4 + 3 * 128 * 4) + (wq.size + wkb.size + wvb.size) * 2
                + tm * (2 * MLA_HEADS * HEAD_PAD * 2 + MLA_HEADS * V_HEAD * 2))
    vmem += 8 * tm * HEAD_PAD * 4 + tm * MLA_HEADS * V_HEAD * 4
    return pl.pallas_call(
        _qkv_kernel,
        grid=(t // tm,),
        in_specs=[row(Q_LORA), row(KV_LORA), row(128), full(wq), full(wkb), full(wvb),
                  full(gq), full(gkn), full(gkr), tab, tab, tab],
        out_specs=[row(MLA_HEADS * HEAD_PAD), row(MLA_HEADS * HEAD_PAD), row(MLA_HEADS * V_HEAD)],
        out_shape=[jax.ShapeDtypeStruct((t, MLA_HEADS * HEAD_PAD), BF16),
                   jax.ShapeDtypeStruct((t, MLA_HEADS * HEAD_PAD), BF16),
                   jax.ShapeDtypeStruct((t, MLA_HEADS * V_HEAD), BF16)],
        compiler_params=_params(("parallel",), vmem),
        name="qkv_proj",
    )(hq, ckv, kpe, wq, wkb, wvb, gq, gkn, gkr, *tabs)


def _q_only_kernel(hq_ref, wq_ref, gq_ref, c_ref, sa_ref, sb_ref, q_ref):
    _q_heads(hq_ref[...], wq_ref, gq_ref, c_ref[...], sa_ref[...], sb_ref[...], q_ref)


def _q_only(hq, wq, gq, tabs, tm):
    t = hq.shape[0]
    row = lambda w: pl.BlockSpec((tm, w), lambda i: (i, 0))
    full = lambda a: pl.BlockSpec(a.shape, lambda i: (0, 0))
    tab = pl.BlockSpec((tm, 128), lambda i: (0, 0))
    vmem = 2 * (tm * (Q_LORA * 2 + 3 * 128 * 4 + MLA_HEADS * HEAD_PAD * 2) + wq.size * 2) + 8 * tm * HEAD_PAD * 4
    return pl.pallas_call(
        _q_only_kernel,
        grid=(t // tm,),
        in_specs=[row(Q_LORA), full(wq), full(gq), tab, tab, tab],
        out_specs=row(MLA_HEADS * HEAD_PAD),
        out_shape=jax.ShapeDtypeStruct((t, MLA_HEADS * HEAD_PAD), BF16),
        compiler_params=_params(("parallel",), vmem),
        name="q_proj",
    )(hq, wq, gq, *tabs)


def _flash_kernel(q_ref, k_ref, v_ref, o_ref, *, tq):
    qi = pl.program_id(2)
    q = q_ref[...]

    def step(kj, carry, masked):
        m, l, acc = carry
        start = pl.multiple_of(kj * tq, tq)
        s = _dot_nt(q, k_ref[pl.ds(start, tq), :])
        if masked:
            rows = lax.broadcasted_iota(jnp.int32, (tq, tq), 0) // CHUNK
            cols = lax.broadcasted_iota(jnp.int32, (tq, tq), 1) // CHUNK
            s = jnp.where(rows >= cols, s, NEG_INF)
        m_new = jnp.maximum(m, jnp.max(s, axis=-1, keepdims=True))
        alpha = jnp.exp(m - m_new)
        p = jnp.exp(s - m_new)
        l = alpha * l + jnp.sum(p, axis=-1, keepdims=True)
        acc = alpha * acc + _dot(p.astype(BF16), v_ref[pl.ds(start, tq), :])
        return m_new, l, acc

    init = (jnp.full((tq, 1), NEG_INF, F32), jnp.zeros((tq, 1), F32), jnp.zeros((tq, V_HEAD), F32))
    carry = lax.fori_loop(0, qi, lambda kj, c: step(kj, c, False), init)
    _, l, acc = step(qi, carry, True)
    o_ref[...] = (acc / l).astype(BF16)


def _flash(q, k, v, batch, seq, tq):
    nq = seq // tq
    vmem = 2 * (tq * HEAD_PAD * 2 + seq * HEAD_PAD * 2 + seq * V_HEAD * 2 + tq * V_HEAD * 2) + 8 * tq * tq * 4
    return pl.pallas_call(
        functools.partial(_flash_kernel, tq=tq),
        grid=(batch, MLA_HEADS, nq),
        in_specs=[pl.BlockSpec((tq, HEAD_PAD), lambda b, h, i: (b * nq + i, h)),
                  pl.BlockSpec((seq, HEAD_PAD), lambda b, h, i: (b, h)),
                  pl.BlockSpec((seq, V_HEAD), lambda b, h, i: (b, h))],
        out_specs=pl.BlockSpec((tq, V_HEAD), lambda b, h, i: (b * nq + i, h)),
        out_shape=jax.ShapeDtypeStruct((batch * seq, MLA_HEADS * V_HEAD), BF16),
        compiler_params=_params(("parallel", "parallel", "arbitrary"), vmem),
        name="mla_flash",
    )(q, k, v)


def _q_absorb_kernel(q_ref, wkb_ref, qa_ref, qr_ref):
    qa_ref[...] = _dot_nt(q_ref[:, :QK_NOPE], wkb_ref[...]).astype(BF16)
    qr_ref[...] = q_ref[:, QK_NOPE:]


def _q_absorb(q, wkb):
    t = q.shape[0]
    vmem = 2 * (t * HEAD_PAD * 2 + KV_LORA * QK_NOPE * 2 + t * KV_LORA * 2 + t * 128 * 2) + 2 * t * KV_LORA * 4
    return pl.pallas_call(
        _q_absorb_kernel,
        grid=(MLA_HEADS,),
        in_specs=[pl.BlockSpec((t, HEAD_PAD), lambda h: (0, h)),
                  pl.BlockSpec((KV_LORA, QK_NOPE), lambda h: (0, h))],
        out_specs=[pl.BlockSpec((None, t, KV_LORA), lambda h: (h, 0, 0)),
                   pl.BlockSpec((None, t, 128), lambda h: (h, 0, 0))],
        out_shape=[jax.ShapeDtypeStruct((MLA_HEADS, t, KV_LORA), BF16),
                   jax.ShapeDtypeStruct((MLA_HEADS, t, 128), BF16)],
        compiler_params=_params(("parallel",), vmem),
        name="q_absorb",
    )(q, wkb)


def _cache_attn_kernel(wkbt_ref, qa_ref, qr_ref, lat_ref, kpet_ref, ct_ref, st_ref,
                       nlat_ref, nkpet_ref, nct_ref, nst_ref, gkr_ref,
                       o_ref, s_sc, m_sc, l_sc, acc_sc, *, nq, n_new, past, tk):
    j = pl.program_id(1)
    last = pl.num_programs(1) - 1
    rows = MLA_HEADS * nq
    half = QK_ROPE // 2

    @pl.when(j == 0)
    def _():
        m_sc[...] = jnp.full(m_sc.shape, NEG_INF, F32)
        l_sc[...] = jnp.zeros(l_sc.shape, F32)
        acc_sc[...] = jnp.zeros(acc_sc.shape, F32)

    def block(lat, kpet, ct, st, width, visible):
        c = lat.astype(BF16)
        kt = _dot_nt(wkbt_ref[...], c)
        s = _dot_nt(qa_ref[...].reshape(rows, KV_LORA), c)
        ss_pe = jnp.sum(kpet * kpet, axis=0, keepdims=True)
        kr = kpet * gkr_ref[...]
        k1, k2 = kr[:half], kr[half:]
        krot = jnp.concatenate([k1 * ct - k2 * st, k2 * ct + k1 * st], axis=0).astype(BF16)
        s = s + _dot(qr_ref[...].reshape(rows, 128)[:, :QK_ROPE], krot)
        for h in range(MLA_HEADS):
            kh = kt[h * QK_NOPE:(h + 1) * QK_NOPE]
            r = lax.rsqrt((jnp.sum(kh * kh, axis=0, keepdims=True) + ss_pe) * (1.0 / QK_HEAD) + EPS)
            s_sc[h * nq:(h + 1) * nq, :width] = s[h * nq:(h + 1) * nq] * r
        s = s_sc[:, :width]
        if visible is not None:
            s = jnp.where(visible, s, NEG_INF)
        m = m_sc[...]
        m_new = jnp.maximum(m, jnp.max(s, axis=-1, keepdims=True))
        alpha = jnp.exp(m - m_new)
        p = jnp.exp(s - m_new)
        l_sc[...] = alpha * l_sc[...] + jnp.sum(p, axis=-1, keepdims=True)
        acc_sc[...] = alpha * acc_sc[...] + _dot(p.astype(BF16), c)
        m_sc[...] = m_new

    @pl.when(j < last)
    def _():
        block(lat_ref[...], kpet_ref[...], ct_ref[...], st_ref[...], tk, None)

    @pl.when(j == last)
    def _():
        kk = lax.broadcasted_iota(jnp.int32, (1, 128), 1)
        visible = kk < n_new
        if (past + n_new - 1) // CHUNK != past // CHUNK:
            qpos = past + lax.broadcasted_iota(jnp.int32, (rows, 1), 0) % nq
            visible = visible & (qpos // CHUNK >= (past + kk) // CHUNK)
        block(nlat_ref[...], nkpet_ref[...], nct_ref[...], nst_ref[...], 128, visible)
        o_ref[...] = (acc_sc[...] / l_sc[...]).astype(BF16)


def _cache_attn(wkbt, qa, qr, lat, kpet, ct, st, nlat, nkpet, nct, nst, gkr, nq, n_new, tk):
    batch, past, _ = lat.shape
    nkb = past // tk
    rows = MLA_HEADS * nq
    cj = lambda j: jnp.minimum(j, nkb - 1)
    kern = functools.partial(_cache_attn_kernel, nq=nq, n_new=n_new, past=past, tk=tk)
    vmem = 2 * (wkbt.size * 2 + rows * (KV_LORA + 128) * 2 + tk * KV_LORA * 4 + 3 * 64 * tk * 4 + rows * KV_LORA * 2)
    vmem += (MLA_HEADS * QK_NOPE + 4 * rows) * tk * 4 + tk * KV_LORA * 2 + rows * KV_LORA * 8
    return pl.pallas_call(
        kern,
        grid=(batch, nkb + 1),
        in_specs=[pl.BlockSpec(wkbt.shape, lambda b, j: (0, 0)),
                  pl.BlockSpec((MLA_HEADS, nq, KV_LORA), lambda b, j: (0, b, 0)),
                  pl.BlockSpec((MLA_HEADS, nq, 128), lambda b, j: (0, b, 0)),
                  pl.BlockSpec((None, tk, KV_LORA), lambda b, j: (b, cj(j), 0)),
                  pl.BlockSpec((None, QK_ROPE, tk), lambda b, j: (b, 0, cj(j))),
                  pl.BlockSpec((QK_ROPE // 2, tk), lambda b, j: (0, cj(j))),
                  pl.BlockSpec((QK_ROPE // 2, tk), lambda b, j: (0, cj(j))),
                  pl.BlockSpec((None, 128, KV_LORA), lambda b, j: (b, 0, 0)),
                  pl.BlockSpec((None, QK_ROPE, 128), lambda b, j: (b, 0, 0)),
                  pl.BlockSpec((QK_ROPE // 2, 128), lambda b, j: (0, 0)),
                  pl.BlockSpec((QK_ROPE // 2, 128), lambda b, j: (0, 0)),
                  pl.BlockSpec((QK_ROPE, 1), lambda b, j: (0, 0))],
        out_specs=pl.BlockSpec((None, rows, KV_LORA), lambda b, j: (b, 0, 0)),
        out_shape=jax.ShapeDtypeStruct((batch, rows, KV_LORA), BF16),
        scratch_shapes=[pltpu.VMEM((rows, tk), F32), pltpu.VMEM((rows, 1), F32),
                        pltpu.VMEM((rows, 1), F32), pltpu.VMEM((rows, KV_LORA), F32)],
        compiler_params=_params(("parallel", "arbitrary"), vmem),
        name="mla_cache_attn",
    )(wkbt, qa, qr, lat, kpet, ct, st, nlat, nkpet, nct, nst, gkr)


def _v_up_kernel(ol_ref, wvb_ref, o_ref):
    b, nq, _ = ol_ref.shape
    o_ref[...] = _dot(ol_ref[...].reshape(b * nq, KV_LORA), wvb_ref[...]).astype(BF16)


def _v_up(o_lat, wvb, nq):
    batch = o_lat.shape[0]
    t = batch * nq
    vmem = 2 * (t * KV_LORA * 2 + KV_LORA * V_HEAD * 2 + t * V_HEAD * 2) + 2 * t * V_HEAD * 4
    return pl.pallas_call(
        _v_up_kernel,
        grid=(MLA_HEADS,),
        in_specs=[pl.BlockSpec((batch, nq, KV_LORA), lambda h: (0, h, 0)),
                  pl.BlockSpec((KV_LORA, V_HEAD), lambda h: (0, h))],
        out_specs=pl.BlockSpec((t, V_HEAD), lambda h: (0, h)),
        out_shape=jax.ShapeDtypeStruct((t, MLA_HEADS * V_HEAD), BF16),
        compiler_params=_params(("parallel",), vmem),
        name="v_up",
    )(o_lat, wvb)


def _mem_kv_kernel(mem_ref, gmem_ref, w_ref, gk_ref, kv_ref):
    j = pl.program_id(0)
    h = _rms(mem_ref[...], gmem_ref[...]).astype(BF16)

    @pl.when(j == 0)
    def _():
        for hh in range(MEM_HEADS):
            lo = hh * MEM_HEAD_DIM
            kv_ref[:, lo:lo + MEM_HEAD_DIM] = _rms(_dot(h, w_ref[:, lo:lo + MEM_HEAD_DIM]), gk_ref[...])

    @pl.when(j == 1)
    def _():
        kv_ref[...] = _dot(h, w_ref[...])


def _mem_kv(mem, gmem, w, gk):
    t = mem.shape[0]
    vmem = 2 * (t * D_MODEL * 4 + D_MODEL * MEM_W * 2 + t * MEM_W * 4) + t * D_MODEL * 6 + t * MEM_W * 4
    return pl.pallas_call(
        _mem_kv_kernel,
        grid=(2,),
        in_specs=[pl.BlockSpec((t, D_MODEL), lambda j: (0, 0)), pl.BlockSpec((1, D_MODEL), lambda j: (0, 0)),
                  pl.BlockSpec((D_MODEL, MEM_W), lambda j: (0, j)), pl.BlockSpec((1, MEM_HEAD_DIM), lambda j: (0, 0))],
        out_specs=pl.BlockSpec((t, MEM_W), lambda j: (0, j)),
        out_shape=jax.ShapeDtypeStruct((t, 2 * MEM_W), F32),
        compiler_params=_params(("parallel",), vmem),
        name="memory_kv",
    )(mem, gmem, w, gk)


def _mem_attn_kernel(qm_ref, k_ref, v_ref, o_ref):
    for hh in range(MEM_HEADS):
        sl = slice(hh * MEM_HEAD_DIM, (hh + 1) * MEM_HEAD_DIM)
        s = _dot_nt(qm_ref[:, sl], k_ref[:, sl].astype(BF16))
        p = jnp.exp(s - jnp.max(s, axis=-1, keepdims=True))
        l = jnp.sum(p, axis=-1, keepdims=True)
        o_ref[:, sl] = (_dot(p.astype(BF16), v_ref[:, sl].astype(BF16)) / l).astype(BF16)


def _mem_attn(qm, k_arr, v_arr, k_col, v_col, tm, blocks_per_batch):
    t = qm.shape[0]
    vmem = 2 * (2 * tm * MEM_W * 2 + 2 * N_MEM * MEM_W * 4) + 6 * tm * N_MEM * 4 + 2 * N_MEM * MEM_W * 2
    return pl.pallas_call(
        _mem_attn_kernel,
        grid=(t // tm,),
        in_specs=[pl.BlockSpec((tm, MEM_W), lambda i: (i, 0)),
                  pl.BlockSpec((N_MEM, MEM_W), lambda i: (i // blocks_per_batch, k_col)),
                  pl.BlockSpec((N_MEM, MEM_W), lambda i: (i // blocks_per_batch, v_col))],
        out_specs=pl.BlockSpec((tm, MEM_W), lambda i: (i, 0)),
        out_shape=jax.ShapeDtypeStruct((t, MEM_W), BF16),
        compiler_params=_params(("parallel",), vmem),
        name="memory_attn",
    )(qm, k_arr, v_arr)


def _pool_kernel(u_ref, prev_ref, left_ref, d_ref, xp_sc, *, tm, blocks_per_seq, pos0):
    i = pl.program_id(0)
    sb = i % blocks_per_seq
    xp_sc[:POOL_HALO, :] = jnp.where(sb == 0, left_ref[...], prev_ref[...])
    xp_sc[POOL_HALO:, :] = u_ref[...]
    pos = pos0 + sb * tm + lax.broadcasted_iota(jnp.int32, (tm, 1), 0)
    for g, win in enumerate(POOL_WINDOWS):
        sl = slice(g * POOL_GW, (g + 1) * POOL_GW)
        wsum = xp_sc[POOL_HALO:, sl]
        for k in range(1, win):
            wsum = wsum + xp_sc[POOL_HALO - k:POOL_HALO - k + tm, sl]
        cnt = jnp.minimum(pos + 1, win).astype(F32)
        d_ref[:, sl] = (wsum / cnt - xp_sc[POOL_HALO:, sl]).astype(BF16)


def _pool(u, left, tm, blocks_per_seq, pos0):
    t = u.shape[0]
    halo_per_blk = tm // POOL_HALO
    kern = functools.partial(_pool_kernel, tm=tm, blocks_per_seq=blocks_per_seq, pos0=pos0)
    vmem = 2 * (tm * POOL_W * 4 + 2 * POOL_HALO * POOL_W * 4 + tm * POOL_W * 2) + (tm + POOL_HALO) * POOL_W * 4
    vmem += 6 * tm * POOL_GW * 4
    return pl.pallas_call(
        kern,
        grid=(t // tm,),
        in_specs=[pl.BlockSpec((tm, POOL_W), lambda i: (i, 0)),
                  pl.BlockSpec((POOL_HALO, POOL_W), lambda i: (jnp.maximum(i * halo_per_blk - 1, 0), 0)),
                  pl.BlockSpec((None, POOL_HALO, POOL_W), lambda i: (i // blocks_per_seq, 0, 0))],
        out_specs=pl.BlockSpec((tm, POOL_W), lambda i: (i, 0)),
        out_shape=jax.ShapeDtypeStruct((t, POOL_W), BF16),
        scratch_shapes=[pltpu.VMEM((tm + POOL_HALO, POOL_W), F32)],
        compiler_params=_params(("parallel",), vmem),
        name="pool_windows",
    )(u, u, left)


def _merge_kernel(x_ref, gmix_ref, wg0_ref, wg1_ref, wg2_ref, bg0_ref, bg1_ref, bg2_ref,
                  d_ref, wpool_ref, pscale_ref, o_ref, wmla_ref, am_ref, wmem_ref,
                  out_ref, h_sc):
    @pl.when(pl.program_id(1) == 0)
    def _():
        h_sc[...] = _rms(x_ref[...], gmix_ref[...]).astype(BF16)

    h = h_sc[...]
    y_pool = _dot(d_ref[...], wpool_ref[...]) * pscale_ref[...]
    merged = jax.nn.sigmoid(_dot(h, wg0_ref[...]) + bg0_ref[...]) * y_pool
    merged += jax.nn.sigmoid(_dot(h, wg1_ref[...]) + bg1_ref[...]) * _dot(o_ref[...], wmla_ref[...])
    merged += jax.nn.sigmoid(_dot(h, wg2_ref[...]) + bg2_ref[...]) * _dot(am_ref[...], wmem_ref[...])
    out_ref[...] = merged.astype(BF16)


def _merge(x, gmix, wgate, bgate, d, wpool, pscale, o, wmla, am, wmem, tm):
    t = x.shape[0]
    tn = POOL_OUT_GW
    nj = D_MODEL // tn
    rowfull = lambda w: pl.BlockSpec((tm, w), lambda i, j: (i, 0))
    gate_w = lambda b: pl.BlockSpec((D_MODEL, tn), lambda i, j: (0, b * nj + j))
    gate_b = lambda b: pl.BlockSpec((1, tn), lambda i, j: (0, b * nj + j))
    col = lambda k: pl.BlockSpec((k, tn), lambda i, j: (0, j))
    vmem = 2 * (tm * D_MODEL * 4 + 3 * D_MODEL * tn * 2 + tm * POOL_GW * 2 + POOL_GW * tn * 2 + tm * D_MODEL * 2
                + D_MODEL * tn * 2 + tm * MEM_W * 2 + MEM_W * tn * 2 + tm * tn * 2)
    vmem += tm * D_MODEL * 2 + tm * D_MODEL * 4 + 8 * tm * tn * 4
    return pl.pallas_call(
        _merge_kernel,
        grid=(t // tm, nj),
        in_specs=[rowfull(D_MODEL), pl.BlockSpec((1, D_MODEL), lambda i, j: (0, 0)),
                  gate_w(0), gate_w(1), gate_w(2), gate_b(0), gate_b(1), gate_b(2),
                  pl.BlockSpec((tm, POOL_GW), lambda i, j: (i, j)),
                  pl.BlockSpec((None, POOL_GW, tn), lambda i, j: (j, 0, 0)),
                  pl.BlockSpec((1, tn), lambda i, j: (0, j)),
                  rowfull(D_MODEL), col(D_MODEL), rowfull(MEM_W), col(MEM_W)],
        out_specs=pl.BlockSpec((tm, tn), lambda i, j: (i, j)),
        out_shape=jax.ShapeDtypeStruct((t, D_MODEL), BF16),
        scratch_shapes=[pltpu.VMEM((tm, D_MODEL), BF16)],
        compiler_params=_params(("parallel", "arbitrary"), vmem),
        name="gated_merge",
    )(x, gmix, wgate, wgate, wgate, bgate, bgate, bgate, d, wpool, pscale, o, wmla, am, wmem)


def _out_proj_kernel(x_ref, m_ref, w_ref, y_ref):
    y_ref[...] = x_ref[...] + _dot(m_ref[...], w_ref[...])


def _out_proj(x, merged, w, tm, tn):
    t = x.shape[0]
    vmem = 2 * (2 * tm * tn * 4 + tm * D_MODEL * 2 + D_MODEL * tn * 2) + 2 * tm * tn * 4
    return pl.pallas_call(
        _out_proj_kernel,
        grid=(t // tm, D_MODEL // tn),
        in_specs=[pl.BlockSpec((tm, tn), lambda i, j: (i, j)),
                  pl.BlockSpec((tm, D_MODEL), lambda i, j: (i, 0)),
                  pl.BlockSpec((D_MODEL, tn), lambda i, j: (0, j))],
        out_specs=pl.BlockSpec((tm, tn), lambda i, j: (i, j)),
        out_shape=jax.ShapeDtypeStruct((t, D_MODEL), F32),
        compiler_params=_params(("parallel", "parallel"), vmem),
        name="out_proj",
    )(x, merged, w)


def _ffn_kernel(x_ref, g_ref, wup_ref, wdown_ref, y_ref, h_sc, acc_sc):
    j = pl.program_id(1)

    @pl.when(j == 0)
    def _():
        h_sc[...] = _rms(x_ref[...], g_ref[...]).astype(BF16)
        acc_sc[...] = jnp.zeros(acc_sc.shape, F32)

    f = jnp.maximum(_dot(h_sc[...], wup_ref[...]), 0.0)
    acc_sc[...] += _dot((f * f).astype(BF16), wdown_ref[...])

    @pl.when(j == pl.num_programs(1) - 1)
    def _():
        y_ref[...] = x_ref[...] + acc_sc[...]


def _ffn(x, g, wup, wdown, tm, tf):
    t = x.shape[0]
    vmem = 2 * (2 * tm * D_MODEL * 4 + 2 * D_MODEL * tf * 2) + tm * D_MODEL * 6 + 3 * tm * tf * 4 + tm * D_MODEL * 4
    return pl.pallas_call(
        _ffn_kernel,
        grid=(t // tm, D_FF // tf),
        in_specs=[pl.BlockSpec((tm, D_MODEL), lambda i, j: (i, 0)),
                  pl.BlockSpec((1, D_MODEL), lambda i, j: (0, 0)),
                  pl.BlockSpec((D_MODEL, tf), lambda i, j: (0, j)),
                  pl.BlockSpec((tf, D_MODEL), lambda i, j: (j, 0))],
        out_specs=pl.BlockSpec((tm, D_MODEL), lambda i, j: (i, 0)),
        out_shape=jax.ShapeDtypeStruct((t, D_MODEL), F32),
        scratch_shapes=[pltpu.VMEM((tm, D_MODEL), BF16), pltpu.VMEM((tm, D_MODEL), F32)],
        compiler_params=_params(("parallel", "arbitrary"), vmem),
        name="ffn",
    )(x, g, wup, wdown)


def _rope_tables(pos):
    half = QK_ROPE // 2
    inv = 1.0 / (ROPE_THETA ** (jnp.arange(half, dtype=F32) * (2.0 / QK_ROPE)))
    ang = pos.astype(F32)[:, None] * inv[None, :]
    return jnp.cos(ang), jnp.sin(ang)


def _lane_tables(cos, sin):
    z = jnp.zeros_like(cos)
    c = jnp.concatenate([cos, cos, z, z], axis=1)
    sa = jnp.concatenate([z, sin, z, z], axis=1)
    sb = jnp.concatenate([-sin, z, z, z], axis=1)
    return c, sa, sb


def _row(v):
    return v.astype(F32).reshape(1, -1)


def _pad_last(a, width):
    return jnp.pad(a, [(0, 0)] * (a.ndim - 1) + [(0, width - a.shape[-1])])


def kernel(x_prompt, mem_prompt, x_sample, cache_mla_latent, cache_mla_kpe, state_pool, cache_mem_k, cache_mem_v,
           g_mix, w_in, b_gate, w_pool, pool_scale, g_q_lat, w_qb, g_q_head, g_kv_lat, w_kb, w_vb, g_k_head,
           w_mla_o, g_mem, w_mem_kv, g_mem_q, g_mem_k, w_mem_o, w_out, g_ff, w_up, w_down):
    batch, seq, _ = x_prompt.shape
    dec_batch, dec_seq, _ = x_sample.shape
    past = cache_mla_latent.shape[1]

    c_u, c_q, c_kv, c_pe, c_mq = 0, POOL_W, POOL_W + Q_LORA, POOL_W + Q_LORA + KV_LORA, POOL_W + Q_LORA + KV_LORA + QK_ROPE
    c_gate = c_mq + MEM_W
    wz = jnp.concatenate([w_in[:, c_u:c_pe], w_in[:, c_mq:c_gate], w_in[:, c_pe:c_mq],
                          jnp.zeros((D_MODEL, 128 - QK_ROPE), w_in.dtype)], axis=1).astype(BF16)
    wgate = w_in[:, c_gate:].astype(BF16)
    bgate = _row(b_gate)
    wq = _pad_last(w_qb.reshape(Q_LORA, MLA_HEADS, QK_HEAD), HEAD_PAD).reshape(Q_LORA, MLA_HEADS * HEAD_PAD).astype(BF16)
    wkb = w_kb.astype(BF16)
    wvb = w_vb.astype(BF16)
    gq_p = _row(_pad_last(g_q_head.astype(F32) * MLA_SCALE, HEAD_PAD))
    gkn = _row(g_k_head[:QK_NOPE])
    gkr = _row(_pad_last(g_k_head[QK_NOPE:], 128))
    gq_s = _row(_pad_last(jnp.concatenate([g_q_head[:QK_NOPE].astype(F32) * g_k_head[:QK_NOPE].astype(F32),
                                           g_q_head[QK_NOPE:].astype(F32)]) * MLA_SCALE, HEAD_PAD))
    gmq = _row(g_mem_q.astype(F32) * MEM_SCALE)
    wpool = w_pool.astype(BF16)
    wmla = w_mla_o.astype(BF16)
    wmem_o = w_mem_o.astype(BF16)
    wmem_kv = w_mem_kv.astype(BF16)
    wout = w_out.astype(BF16)
    wup = w_up.astype(BF16)
    wdown = w_down.astype(BF16)
    gmix, gql, gkvl, gff, gmem, gmk, pscale = map(_row, (g_mix, g_q_lat, g_kv_lat, g_ff, g_mem, g_mem_k, pool_scale))

    def dense_tail(x, d, o, am, tm):
        merged = _merge(x, gmix, wgate, bgate, d, wpool, pscale, o, wmla, am, wmem_o, tm)
        x1 = _out_proj(x, merged, wout, tm, 1024)
        return _ffn(x1, gff, wup, wdown, tm, 1024)

    tm = 512
    xp = x_prompt.reshape(batch * seq, D_MODEL)
    mem_kv = _mem_kv(mem_prompt.reshape(batch * N_MEM, D_MODEL), gmem, wmem_kv, gmk)
    u, hq, ckv, kpe, qm = _in_proj(xp, gmix, wz, gql, gkvl, gmq, tm)
    cos_p, sin_p = _rope_tables(jnp.arange(seq, dtype=jnp.int32))
    q, k, v = _qkv(hq, ckv, kpe, wq, wkb, wvb, gq_p, gkn, gkr, _lane_tables(cos_p, sin_p), tm, seq // tm)
    o = _flash(q, k, v, batch, seq, 512)
    am = _mem_attn(qm, mem_kv, mem_kv, 0, 1, tm, seq // tm)
    d = _pool(u, jnp.zeros((batch, POOL_HALO, POOL_W), F32), tm, seq // tm, 0)
    y_p = dense_tail(xp, d, o, am, tm).reshape(batch, seq, D_MODEL)
    lat_p = ckv.reshape(batch, seq, KV_LORA)
    kpe_p = kpe[:, :QK_ROPE].reshape(batch, seq, QK_ROPE)
    pool_p = u.reshape(batch, seq, POOL_W)[:, seq - POOL_STATE:]
    mem_k_p = mem_kv[:, :MEM_W].reshape(batch, N_MEM, MEM_HEADS, MEM_HEAD_DIM)
    mem_v_p = mem_kv[:, MEM_W:].reshape(batch, N_MEM, MEM_HEADS, MEM_HEAD_DIM)

    ts = dec_batch * dec_seq
    xs = x_sample.reshape(ts, D_MODEL)
    u_s, hq_s, ckv_s, kpe_s, qm_s = _in_proj(xs, gmix, wz, gql, gkvl, gmq, ts)
    pos_s = past + jnp.arange(dec_seq, dtype=jnp.int32)
    cos_s, sin_s = _rope_tables(pos_s)
    tabs_s = tuple(jnp.tile(a, (dec_batch, 1)) for a in _lane_tables(cos_s, sin_s))
    q_s = _q_only(hq_s, wq, gq_s, tabs_s, ts)
    qa, qr = _q_absorb(q_s, wkb)
    cos_c, sin_c = _rope_tables(jnp.arange(past, dtype=jnp.int32))
    pad_new = 128 - dec_seq
    nlat = jnp.pad(ckv_s.reshape(dec_batch, dec_seq, KV_LORA), ((0, 0), (0, pad_new), (0, 0)))
    nkpet = jnp.pad(jnp.swapaxes(kpe_s[:, :QK_ROPE].reshape(dec_batch, dec_seq, QK_ROPE), 1, 2),
                    ((0, 0), (0, 0), (0, pad_new)))
    o_lat = _cache_attn(wkb.T, qa, qr, cache_mla_latent, jnp.swapaxes(cache_mla_kpe, 1, 2),
                        cos_c.T, sin_c.T, nlat, nkpet,
                        jnp.pad(cos_s.T, ((0, 0), (0, pad_new))), jnp.pad(sin_s.T, ((0, 0), (0, pad_new))),
                        g_k_head[QK_NOPE:].astype(F32).reshape(QK_ROPE, 1),
                        dec_seq, dec_seq, 1024)
    o_s = _v_up(o_lat, wvb, dec_seq)
    am_s = _mem_attn(qm_s, cache_mem_k.reshape(dec_batch * N_MEM, MEM_W), cache_mem_v.reshape(dec_batch * N_MEM, MEM_W),
                     0, 0, dec_seq, 1)
    left_s = jnp.pad(state_pool.astype(F32), ((0, 0), (POOL_HALO - POOL_STATE, 0), (0, 0)))
    d_s = _pool(u_s, left_s, dec_seq, 1, past)
    y_s = dense_tail(xs, d_s, o_s, am_s, ts).reshape(dec_batch, dec_seq, D_MODEL)
    lat_s = ckv_s.reshape(dec_batch, dec_seq, KV_LORA)
    kpe_s_out = kpe_s[:, :QK_ROPE].reshape(dec_batch, dec_seq, QK_ROPE)
    pool_s = u_s.reshape(dec_batch, dec_seq, POOL_W)[:, dec_seq - POOL_STATE:]

    return (y_p, y_s, lat_p, kpe_p, pool_p, mem_k_p, mem_v_p, lat_s, kpe_s_out, pool_s)
```

```python
import functools

import jax
import jax.numpy as jnp
from jax import lax
from jax.experimental import pallas as pl
from jax.experimental.pallas import tpu as pltpu

F32 = jnp.float32
BF16 = jnp.bfloat16

EPS = 1e-6
CHUNK = 64
D_MODEL = 2048
POOL_WINDOWS = (2, 4, 8, 16)
POOL_W = 1024
POOL_GW = 256
POOL_OUT_GW = 512
POOL_STATE = 15
POOL_HALO = 16
MLA_HEADS = 16
Q_LORA = 512
KV_LORA = 512
QK_NOPE = 128
QK_ROPE = 64
QK_HEAD = QK_NOPE + QK_ROPE
HEAD_PAD = 256
V_HEAD = 128
V_ROWS = 144
ROPE_THETA = 10000.0
LOG2_E = 1.4426950408889634
MLA_SCALE = QK_HEAD ** -0.5
MLA_SCALE_LOG2 = MLA_SCALE * LOG2_E
N_MEM = 256
MEM_HEADS = 4
MEM_HEAD_DIM = 256
MEM_W = MEM_HEADS * MEM_HEAD_DIM
MEM_SCALE = MEM_HEAD_DIM ** -0.5
D_FF = 4 * D_MODEL
NEG_INF = -1e30

Z_U = 0
Z_QLAT = Z_U + POOL_W
Z_KVLAT = Z_QLAT + Q_LORA
Z_MQ = Z_KVLAT + KV_LORA
Z_KPE = Z_MQ + MEM_W
Z_W = Z_KPE + 128

V7X_VMEM_BYTES = 64 * 1024 * 1024
VMEM_CEILING = V7X_VMEM_BYTES - 6 * 1024 * 1024


def _params(semantics, vmem_bytes):
    return pltpu.CompilerParams(dimension_semantics=semantics,
                                vmem_limit_bytes=int(min(vmem_bytes, VMEM_CEILING)))


def _rms(x, g):
    return x * lax.rsqrt(jnp.mean(x * x, axis=-1, keepdims=True) + EPS) * g


def _dot(a, b):
    return jnp.dot(a, b, preferred_element_type=F32)


def _dot_nt(a, b):
    return lax.dot_general(a, b, (((1,), (1,)), ((), ())), preferred_element_type=F32)


def _rope128(x, c, sa, sb):
    return x * c + pltpu.roll(x, 32, 1) * sa + pltpu.roll(x, 96, 1) * sb


def _in_proj_kernel(x_ref, gmix_ref, wz_ref, gq_ref, gkv_ref, gmq_ref,
                    u_ref, hq_ref, ckv_ref, kpe_ref, qm_ref):
    h = _rms(x_ref[...], gmix_ref[...]).astype(BF16)
    u_ref[...] = _dot(h, wz_ref[:, Z_U:Z_QLAT])
    hq_ref[...] = _rms(_dot(h, wz_ref[:, Z_QLAT:Z_KVLAT]), gq_ref[...]).astype(BF16)
    ckv_ref[...] = _rms(_dot(h, wz_ref[:, Z_KVLAT:Z_MQ]), gkv_ref[...])
    kpe_ref[...] = _dot(h, wz_ref[:, Z_KPE:Z_W])
    for hh in range(MEM_HEADS):
        lo = hh * MEM_HEAD_DIM
        m = _dot(h, wz_ref[:, Z_MQ + lo:Z_MQ + lo + MEM_HEAD_DIM])
        qm_ref[:, lo:lo + MEM_HEAD_DIM] = _rms(m, gmq_ref[...]).astype(BF16)


def _in_proj(x, gmix, wz, gq, gkv, gmq_scaled, tm):
    t = x.shape[0]
    row = lambda w: pl.BlockSpec((tm, w), lambda i: (i, 0))
    full = lambda a: pl.BlockSpec(a.shape, lambda i: (0, 0))
    vmem = 2 * (tm * D_MODEL * 4 + wz.size * 2 + tm * (POOL_W * 4 + Q_LORA * 2 + KV_LORA * 4 + 128 * 4 + MEM_W * 2))
    vmem += 6 * tm * POOL_W * 4
    return pl.pallas_call(
        _in_proj_kernel,
        grid=(t // tm,),
        in_specs=[row(D_MODEL), full(gmix), full(wz), full(gq), full(gkv), full(gmq_scaled)],
        out_specs=[row(POOL_W), row(Q_LORA), row(KV_LORA), row(128), row(MEM_W)],
        out_shape=[jax.ShapeDtypeStruct((t, POOL_W), F32), jax.ShapeDtypeStruct((t, Q_LORA), BF16),
                   jax.ShapeDtypeStruct((t, KV_LORA), F32), jax.ShapeDtypeStruct((t, 128), F32),
                   jax.ShapeDtypeStruct((t, MEM_W), BF16)],
        compiler_params=_params(("parallel",), vmem),
        name="in_proj",
    )(x, gmix, wz, gq, gkv, gmq_scaled)


def _q_heads(hq, wq_ref, gq_ref, c, sa, sb, q_ref):
    for h in range(MLA_HEADS):
        lo = h * HEAD_PAD
        qh = _dot(hq, wq_ref[:, lo:lo + HEAD_PAD])
        r = lax.rsqrt(jnp.sum(qh * qh, axis=-1, keepdims=True) * (1.0 / QK_HEAD) + EPS)
        qh = qh * r * gq_ref[...]
        q_ref[:, lo:lo + QK_NOPE] = qh[:, :QK_NOPE].astype(BF16)
        q_ref[:, lo + QK_NOPE:lo + HEAD_PAD] = _rope128(qh[:, QK_NOPE:], c, sa, sb).astype(BF16)


def _qkv_kernel(hq_ref, ckv_ref, kpe_ref, wq_ref, wkb_ref, wvbt_ref, gq_ref, gkn_ref, gkr_ref,
                c_ref, sa_ref, sb_ref, q_ref, k_ref, vt_ref):
    c, sa, sb = c_ref[...], sa_ref[...], sb_ref[...]
    _q_heads(hq_ref[...], wq_ref, gq_ref, c, sa, sb, q_ref)
    ckv = ckv_ref[...].astype(BF16)
    kpe = kpe_ref[...]
    ss_pe = jnp.sum(kpe * kpe, axis=-1, keepdims=True)
    kr = _rope128(kpe * gkr_ref[...], c, sa, sb)
    for p in range(MLA_HEADS // 2):
        kn2 = _dot(ckv, wkb_ref[:, p * 2 * QK_NOPE:(p + 1) * 2 * QK_NOPE])
        for s in range(2):
            lo = (2 * p + s) * HEAD_PAD
            kn = kn2[:, s * QK_NOPE:(s + 1) * QK_NOPE]
            r = lax.rsqrt((jnp.sum(kn * kn, axis=-1, keepdims=True) + ss_pe) * (1.0 / QK_HEAD) + EPS)
            k_ref[:, lo:lo + QK_NOPE] = (kn * r * gkn_ref[...]).astype(BF16)
            k_ref[:, lo + QK_NOPE:lo + HEAD_PAD] = (kr * r).astype(BF16)
    vt = _dot_nt(wvbt_ref[...], ckv)
    tm = vt.shape[1]
    ones_row = (lax.broadcasted_iota(jnp.int32, (V_ROWS - V_HEAD, tm), 0) == 0).astype(BF16)
    for h in range(MLA_HEADS):
        vt_ref[h, :V_HEAD, :] = vt[h * V_HEAD:(h + 1) * V_HEAD].astype(BF16)
        vt_ref[h, V_HEAD:, :] = ones_row


def _qkv(hq, ckv, kpe, wq, wkb, wvbt, gq, gkn, gkr, tabs, tm, seq_blocks):
    t = hq.shape[0]
    row = lambda w: pl.BlockSpec((tm, w), lambda i: (i, 0))
    full = lambda a: pl.BlockSpec(a.shape, lambda i: (0, 0))
    tab = pl.BlockSpec((tm, 128), lambda i: (i % seq_blocks, 0))
    vmem = 2 * (tm * (Q_LORA * 2 + KV_LORA * 4 + 128 * 4 + 3 * 128 * 4) + (wq.size + wkb.size + wvbt.size) * 2
                + tm * (2 * MLA_HEADS * HEAD_PAD * 2 + MLA_HEADS * V_ROWS * 2))
    vmem += 8 * tm * HEAD_PAD * 4 + 2 * tm * MLA_HEADS * V_HEAD * 4
    return pl.pallas_call(
        _qkv_kernel,
        grid=(t // tm,),
        in_specs=[row(Q_LORA), row(KV_LORA), row(128), full(wq), full(wkb), full(wvbt),
                  full(gq), full(gkn), full(gkr), tab, tab, tab],
        out_specs=[row(MLA_HEADS * HEAD_PAD), row(MLA_HEADS * HEAD_PAD),
                   pl.BlockSpec((None, MLA_HEADS, None, V_ROWS, tm),
                                lambda i: (i // seq_blocks, 0, i % seq_blocks, 0, 0))],
        out_shape=[jax.ShapeDtypeStruct((t, MLA_HEADS * HEAD_PAD), BF16),
                   jax.ShapeDtypeStruct((t, MLA_HEADS * HEAD_PAD), BF16),
                   jax.ShapeDtypeStruct((t // (tm * seq_blocks), MLA_HEADS, seq_blocks, V_ROWS, tm), BF16)],
        compiler_params=_params(("parallel",), vmem),
        name="qkv_proj",
    )(hq, ckv, kpe, wq, wkb, wvbt, gq, gkn, gkr, *tabs)


def _q_only_kernel(hq_ref, wq_ref, gq_ref, c_ref, sa_ref, sb_ref, q_ref):
    _q_heads(hq_ref[...], wq_ref, gq_ref, c_ref[...], sa_ref[...], sb_ref[...], q_ref)


def _q_only(hq, wq, gq, tabs, tm):
    t = hq.shape[0]
    row = lambda w: pl.BlockSpec((tm, w), lambda i: (i, 0))
    full = lambda a: pl.BlockSpec(a.shape, lambda i: (0, 0))
    tab = pl.BlockSpec((tm, 128), lambda i: (0, 0))
    vmem = 2 * (tm * (Q_LORA * 2 + 3 * 128 * 4 + MLA_HEADS * HEAD_PAD * 2) + wq.size * 2) + 8 * tm * HEAD_PAD * 4
    return pl.pallas_call(
        _q_only_kernel,
        grid=(t // tm,),
        in_specs=[row(Q_LORA), full(wq), full(gq), tab, tab, tab],
        out_specs=row(MLA_HEADS * HEAD_PAD),
        out_shape=jax.ShapeDtypeStruct((t, MLA_HEADS * HEAD_PAD), BF16),
        compiler_params=_params(("parallel",), vmem),
        name="q_proj",
    )(hq, wq, gq, *tabs)


def _flash_kernel(q_ref, k_ref, vt_ref, o_ref, s0_sc, s1_sc, m_sc, acc_sc, *, tq, tk):
    qi = pl.program_id(2)
    m_sc[...] = jnp.full(m_sc.shape, NEG_INF, F32)
    acc_sc[...] = jnp.zeros(acc_sc.shape, F32)

    def scores(kj, s_sc, q_lo=0):
        start = pl.multiple_of(kj * tk, tk)
        s_sc[:, q_lo:] = _dot_nt(k_ref[pl.ds(start, tk), :], q_ref[q_lo:, :])

    def consume(kj, s_sc, key_chunk0=None, q_lo=0):
        s = s_sc[:, q_lo:]
        if key_chunk0 is not None:
            kc = key_chunk0 + lax.broadcasted_iota(jnp.int32, s.shape, 0) // CHUNK
            qc = (q_lo + lax.broadcasted_iota(jnp.int32, s.shape, 1)) // CHUNK
            s = jnp.where(qc >= kc, s, NEG_INF)
        m = m_sc[:, q_lo:]
        m_new = jnp.maximum(m, jnp.max(s, axis=0, keepdims=True))
        alpha = jnp.exp2(m - m_new)
        p = jnp.exp2(s - m_new).astype(BF16)
        acc_sc[:, q_lo:] = alpha * acc_sc[:, q_lo:] + _dot(vt_ref[kj], p)
        m_sc[:, q_lo:] = m_new

    scores(0, s0_sc)

    def pair(t, carry):
        scores(2 * t + 1, s1_sc)
        consume(2 * t, s0_sc)
        scores(2 * t + 2, s0_sc)
        consume(2 * t + 1, s1_sc)
        return carry

    n_diag = tq // tk
    lax.fori_loop(0, qi * (n_diag // 2), pair, 0)
    base = qi * n_diag
    bufs = (s0_sc, s1_sc)
    for d in range(n_diag):
        if d + 1 < n_diag:
            scores(base + d + 1, bufs[(d + 1) % 2], (d + 1) * tk)
        consume(base + d, bufs[d % 2], d * tk // CHUNK, d * tk)
    acc = acc_sc[...]
    o_ref[...] = (acc[:V_HEAD] / acc[V_HEAD:V_HEAD + 1]).T.astype(BF16)


def _flash(q, k, vt, batch, seq, tq):
    tk = vt.shape[-1]
    assert tq % (2 * tk) == 0 and tk % CHUNK == 0 and seq % tq == 0
    nq = seq // tq
    vmem = 2 * (tq * HEAD_PAD * 2 + seq * HEAD_PAD * 2 + seq * V_ROWS * 2 + tq * V_HEAD * 2)
    vmem += 2 * tk * tq * 4 + V_ROWS * tq * 4 + 4 * tk * tq * 4
    return pl.pallas_call(
        functools.partial(_flash_kernel, tq=tq, tk=tk),
        grid=(batch, MLA_HEADS, nq),
        in_specs=[pl.BlockSpec((tq, HEAD_PAD), lambda b, h, i: (b * nq + i, h)),
                  pl.BlockSpec((seq, HEAD_PAD), lambda b, h, i: (b, h)),
                  pl.BlockSpec((None, None, seq // tk, V_ROWS, tk), lambda b, h, i: (b, h, 0, 0, 0))],
        out_specs=pl.BlockSpec((tq, V_HEAD), lambda b, h, i: (b * nq + i, h)),
        out_shape=jax.ShapeDtypeStruct((batch * seq, MLA_HEADS * V_HEAD), BF16),
        scratch_shapes=[pltpu.VMEM((tk, tq), F32), pltpu.VMEM((tk, tq), F32),
                        pltpu.VMEM((1, tq), F32), pltpu.VMEM((V_ROWS, tq), F32)],
        compiler_params=_params(("parallel", "parallel", "arbitrary"), vmem),
        name="mla_flash",
    )(q, k, vt)


def _q_absorb_kernel(q_ref, wkb_ref, qa_ref, qr_ref):
    qa_ref[...] = _dot_nt(q_ref[:, :QK_NOPE], wkb_ref[...]).astype(BF16)
    qr_ref[...] = q_ref[:, QK_NOPE:]


def _q_absorb(q, wkb):
    t = q.shape[0]
    vmem = 2 * (t * HEAD_PAD * 2 + KV_LORA * QK_NOPE * 2 + t * KV_LORA * 2 + t * 128 * 2) + 2 * t * KV_LORA * 4
    return pl.pallas_call(
        _q_absorb_kernel,
        grid=(MLA_HEADS,),
        in_specs=[pl.BlockSpec((t, HEAD_PAD), lambda h: (0, h)),
                  pl.BlockSpec((KV_LORA, QK_NOPE), lambda h: (0, h))],
        out_specs=[pl.BlockSpec((None, t, KV_LORA), lambda h: (h, 0, 0)),
                   pl.BlockSpec((None, t, 128), lambda h: (h, 0, 0))],
        out_shape=[jax.ShapeDtypeStruct((MLA_HEADS, t, KV_LORA), BF16),
                   jax.ShapeDtypeStruct((MLA_HEADS, t, 128), BF16)],
        compiler_params=_params(("parallel",), vmem),
        name="q_absorb",
    )(q, wkb)


def _cache_attn_kernel(wkbt_ref, qa_ref, qr_ref, lat_ref, kpet_ref, ct_ref, st_ref,
                       nlat_ref, nkpet_ref, nct_ref, nst_ref, gkr_ref,
                       o_ref, s_sc, m_sc, l_sc, acc_sc, *, nq, n_new, past, tk):
    j = pl.program_id(1)
    last = pl.num_programs(1) - 1
    rows = MLA_HEADS * nq
    half = QK_ROPE // 2

    @pl.when(j == 0)
    def _():
        m_sc[...] = jnp.full(m_sc.shape, NEG_INF, F32)
        l_sc[...] = jnp.zeros(l_sc.shape, F32)
        acc_sc[...] = jnp.zeros(acc_sc.shape, F32)

    def block(lat, kpet, ct, st, width, visible):
        c = lat.astype(BF16)
        kt = _dot_nt(wkbt_ref[...], c)
        s = _dot_nt(qa_ref[...].reshape(rows, KV_LORA), c)
        ss_pe = jnp.sum(kpet * kpet, axis=0, keepdims=True)
        kr = kpet * gkr_ref[...]
        k1, k2 = kr[:half], kr[half:]
        krot = jnp.concatenate([k1 * ct - k2 * st, k2 * ct + k1 * st], axis=0).astype(BF16)
        s = s + _dot(qr_ref[...].reshape(rows, 128)[:, :QK_ROPE], krot)
        for h in range(MLA_HEADS):
            kh = kt[h * QK_NOPE:(h + 1) * QK_NOPE]
            r = lax.rsqrt((jnp.sum(kh * kh, axis=0, keepdims=True) + ss_pe) * (1.0 / QK_HEAD) + EPS)
            s_sc[h * nq:(h + 1) * nq, :width] = s[h * nq:(h + 1) * nq] * r
        s = s_sc[:, :width]
        if visible is not None:
            s = jnp.where(visible, s, NEG_INF)
        m = m_sc[...]
        m_new = jnp.maximum(m, jnp.max(s, axis=-1, keepdims=True))
        alpha = jnp.exp2(m - m_new)
        p = jnp.exp2(s - m_new)
        l_sc[...] = alpha * l_sc[...] + jnp.sum(p, axis=-1, keepdims=True)
        acc_sc[...] = alpha * acc_sc[...] + _dot(p.astype(BF16), c)
        m_sc[...] = m_new

    @pl.when(j < last)
    def _():
        block(lat_ref[...], kpet_ref[...], ct_ref[...], st_ref[...], tk, None)

    @pl.when(j == last)
    def _():
        kk = lax.broadcasted_iota(jnp.int32, (1, 128), 1)
        visible = kk < n_new
        if (past + n_new - 1) // CHUNK != past // CHUNK:
            qpos = past + lax.broadcasted_iota(jnp.int32, (rows, 1), 0) % nq
            visible = visible & (qpos // CHUNK >= (past + kk) // CHUNK)
        block(nlat_ref[...], nkpet_ref[...], nct_ref[...], nst_ref[...], 128, visible)
        o_ref[...] = (acc_sc[...] / l_sc[...]).astype(BF16)


def _cache_attn(wkbt, qa, qr, lat, kpet, ct, st, nlat, nkpet, nct, nst, gkr, nq, n_new, tk):
    batch, past, _ = lat.shape
    nkb = past // tk
    rows = MLA_HEADS * nq
    cj = lambda j: jnp.minimum(j, nkb - 1)
    kern = functools.partial(_cache_attn_kernel, nq=nq, n_new=n_new, past=past, tk=tk)
    vmem = 2 * (wkbt.size * 2 + rows * (KV_LORA + 128) * 2 + tk * KV_LORA * 4 + 3 * 64 * tk * 4 + rows * KV_LORA * 2)
    vmem += (MLA_HEADS * QK_NOPE + 4 * rows) * tk * 4 + tk * KV_LORA * 2 + rows * KV_LORA * 8
    return pl.pallas_call(
        kern,
        grid=(batch, nkb + 1),
        in_specs=[pl.BlockSpec(wkbt.shape, lambda b, j: (0, 0)),
                  pl.BlockSpec((MLA_HEADS, nq, KV_LORA), lambda b, j: (0, b, 0)),
                  pl.BlockSpec((MLA_HEADS, nq, 128), lambda b, j: (0, b, 0)),
                  pl.BlockSpec((None, tk, KV_LORA), lambda b, j: (b, cj(j), 0)),
                  pl.BlockSpec((None, QK_ROPE, tk), lambda b, j: (b, 0, cj(j))),
                  pl.BlockSpec((QK_ROPE // 2, tk), lambda b, j: (0, cj(j))),
                  pl.BlockSpec((QK_ROPE // 2, tk), lambda b, j: (0, cj(j))),
                  pl.BlockSpec((None, 128, KV_LORA), lambda b, j: (b, 0, 0)),
                  pl.BlockSpec((None, QK_ROPE, 128), lambda b, j: (b, 0, 0)),
                  pl.BlockSpec((QK_ROPE // 2, 128), lambda b, j: (0, 0)),
                  pl.BlockSpec((QK_ROPE // 2, 128), lambda b, j: (0, 0)),
                  pl.BlockSpec((QK_ROPE, 1), lambda b, j: (0, 0))],
        out_specs=pl.BlockSpec((None, rows, KV_LORA), lambda b, j: (b, 0, 0)),
        out_shape=jax.ShapeDtypeStruct((batch, rows, KV_LORA), BF16),
        scratch_shapes=[pltpu.VMEM((rows, tk), F32), pltpu.VMEM((rows, 1), F32),
                        pltpu.VMEM((rows, 1), F32), pltpu.VMEM((rows, KV_LORA), F32)],
        compiler_params=_params(("parallel", "arbitrary"), vmem),
        name="mla_cache_attn",
    )(wkbt, qa, qr, lat, kpet, ct, st, nlat, nkpet, nct, nst, gkr)


def _v_up_kernel(ol_ref, wvb_ref, o_ref):
    b, nq, _ = ol_ref.shape
    o_ref[...] = _dot(ol_ref[...].reshape(b * nq, KV_LORA), wvb_ref[...]).astype(BF16)


def _v_up(o_lat, wvb, nq):
    batch = o_lat.shape[0]
    t = batch * nq
    vmem = 2 * (t * KV_LORA * 2 + KV_LORA * V_HEAD * 2 + t * V_HEAD * 2) + 2 * t * V_HEAD * 4
    return pl.pallas_call(
        _v_up_kernel,
        grid=(MLA_HEADS,),
        in_specs=[pl.BlockSpec((batch, nq, KV_LORA), lambda h: (0, h, 0)),
                  pl.BlockSpec((KV_LORA, V_HEAD), lambda h: (0, h))],
        out_specs=pl.BlockSpec((t, V_HEAD), lambda h: (0, h)),
        out_shape=jax.ShapeDtypeStruct((t, MLA_HEADS * V_HEAD), BF16),
        compiler_params=_params(("parallel",), vmem),
        name="v_up",
    )(o_lat, wvb)


def _mem_kv_kernel(mem_ref, gmem_ref, w_ref, gk_ref, kv_ref):
    j = pl.program_id(0)
    h = _rms(mem_ref[...], gmem_ref[...]).astype(BF16)

    @pl.when(j == 0)
    def _():
        for hh in range(MEM_HEADS):
            lo = hh * MEM_HEAD_DIM
            kv_ref[:, lo:lo + MEM_HEAD_DIM] = _rms(_dot(h, w_ref[:, lo:lo + MEM_HEAD_DIM]), gk_ref[...])

    @pl.when(j == 1)
    def _():
        kv_ref[...] = _dot(h, w_ref[...])


def _mem_kv(mem, gmem, w, gk):
    t = mem.shape[0]
    vmem = 2 * (t * D_MODEL * 4 + D_MODEL * MEM_W * 2 + t * MEM_W * 4) + t * D_MODEL * 6 + t * MEM_W * 4
    return pl.pallas_call(
        _mem_kv_kernel,
        grid=(2,),
        in_specs=[pl.BlockSpec((t, D_MODEL), lambda j: (0, 0)), pl.BlockSpec((1, D_MODEL), lambda j: (0, 0)),
                  pl.BlockSpec((D_MODEL, MEM_W), lambda j: (0, j)), pl.BlockSpec((1, MEM_HEAD_DIM), lambda j: (0, 0))],
        out_specs=pl.BlockSpec((t, MEM_W), lambda j: (0, j)),
        out_shape=jax.ShapeDtypeStruct((t, 2 * MEM_W), F32),
        compiler_params=_params(("parallel",), vmem),
        name="memory_kv",
    )(mem, gmem, w, gk)


def _mem_attn_kernel(qm_ref, k_ref, v_ref, o_ref):
    for hh in range(MEM_HEADS):
        sl = slice(hh * MEM_HEAD_DIM, (hh + 1) * MEM_HEAD_DIM)
        s = _dot_nt(qm_ref[:, sl], k_ref[:, sl].astype(BF16))
        p = jnp.exp(s - jnp.max(s, axis=-1, keepdims=True))
        l = jnp.sum(p, axis=-1, keepdims=True)
        o_ref[:, sl] = (_dot(p.astype(BF16), v_ref[:, sl].astype(BF16)) / l).astype(BF16)


def _mem_attn(qm, k_arr, v_arr, k_col, v_col, tm, blocks_per_batch):
    t = qm.shape[0]
    vmem = 2 * (2 * tm * MEM_W * 2 + 2 * N_MEM * MEM_W * 4) + 6 * tm * N_MEM * 4 + 2 * N_MEM * MEM_W * 2
    return pl.pallas_call(
        _mem_attn_kernel,
        grid=(t // tm,),
        in_specs=[pl.BlockSpec((tm, MEM_W), lambda i: (i, 0)),
                  pl.BlockSpec((N_MEM, MEM_W), lambda i: (i // blocks_per_batch, k_col)),
                  pl.BlockSpec((N_MEM, MEM_W), lambda i: (i // blocks_per_batch, v_col))],
        out_specs=pl.BlockSpec((tm, MEM_W), lambda i: (i, 0)),
        out_shape=jax.ShapeDtypeStruct((t, MEM_W), BF16),
        compiler_params=_params(("parallel",), vmem),
        name="memory_attn",
    )(qm, k_arr, v_arr)


def _pool_kernel(u_ref, prev_ref, left_ref, d_ref, xp_sc, *, tm, blocks_per_seq, pos0):
    i = pl.program_id(0)
    sb = i % blocks_per_seq
    xp_sc[:POOL_HALO, :] = jnp.where(sb == 0, left_ref[...], prev_ref[...])
    xp_sc[POOL_HALO:, :] = u_ref[...]
    pos = pos0 + sb * tm + lax.broadcasted_iota(jnp.int32, (tm, 1), 0)
    for g, win in enumerate(POOL_WINDOWS):
        sl = slice(g * POOL_GW, (g + 1) * POOL_GW)
        wsum = xp_sc[POOL_HALO:, sl]
        for k in range(1, win):
            wsum = wsum + xp_sc[POOL_HALO - k:POOL_HALO - k + tm, sl]
        cnt = jnp.minimum(pos + 1, win).astype(F32)
        d_ref[:, sl] = (wsum / cnt - xp_sc[POOL_HALO:, sl]).astype(BF16)


def _pool(u, left, tm, blocks_per_seq, pos0):
    t = u.shape[0]
    halo_per_blk = tm // POOL_HALO
    kern = functools.partial(_pool_kernel, tm=tm, blocks_per_seq=blocks_per_seq, pos0=pos0)
    vmem = 2 * (tm * POOL_W * 4 + 2 * POOL_HALO * POOL_W * 4 + tm * POOL_W * 2) + (tm + POOL_HALO) * POOL_W * 4
    vmem += 6 * tm * POOL_GW * 4
    return pl.pallas_call(
        kern,
        grid=(t // tm,),
        in_specs=[pl.BlockSpec((tm, POOL_W), lambda i: (i, 0)),
                  pl.BlockSpec((POOL_HALO, POOL_W), lambda i: (jnp.maximum(i * halo_per_blk - 1, 0), 0)),
                  pl.BlockSpec((None, POOL_HALO, POOL_W), lambda i: (i // blocks_per_seq, 0, 0))],
        out_specs=pl.BlockSpec((tm, POOL_W), lambda i: (i, 0)),
        out_shape=jax.ShapeDtypeStruct((t, POOL_W), BF16),
        scratch_shapes=[pltpu.VMEM((tm + POOL_HALO, POOL_W), F32)],
        compiler_params=_params(("parallel",), vmem),
        name="pool_windows",
    )(u, u, left)


def _merge_kernel(x_ref, gmix_ref, wg0_ref, wg1_ref, wg2_ref, bg0_ref, bg1_ref, bg2_ref,
                  d_ref, wpool_ref, pscale_ref, o_ref, wmla_ref, am_ref, wmem_ref,
                  out_ref, h_sc):
    @pl.when(pl.program_id(1) == 0)
    def _():
        h_sc[...] = _rms(x_ref[...], gmix_ref[...]).astype(BF16)

    h = h_sc[...]
    y_pool = _dot(d_ref[...], wpool_ref[...]) * pscale_ref[...]
    merged = jax.nn.sigmoid(_dot(h, wg0_ref[...]) + bg0_ref[...]) * y_pool
    merged += jax.nn.sigmoid(_dot(h, wg1_ref[...]) + bg1_ref[...]) * _dot(o_ref[...], wmla_ref[...])
    merged += jax.nn.sigmoid(_dot(h, wg2_ref[...]) + bg2_ref[...]) * _dot(am_ref[...], wmem_ref[...])
    out_ref[...] = merged.astype(BF16)


def _merge(x, gmix, wgate, bgate, d, wpool, pscale, o, wmla, am, wmem, tm):
    t = x.shape[0]
    tn = POOL_OUT_GW
    nj = D_MODEL // tn
    rowfull = lambda w: pl.BlockSpec((tm, w), lambda i, j: (i, 0))
    gate_w = lambda b: pl.BlockSpec((D_MODEL, tn), lambda i, j: (0, b * nj + j))
    gate_b = lambda b: pl.BlockSpec((1, tn), lambda i, j: (0, b * nj + j))
    col = lambda k: pl.BlockSpec((k, tn), lambda i, j: (0, j))
    vmem = 2 * (tm * D_MODEL * 4 + 3 * D_MODEL * tn * 2 + tm * POOL_GW * 2 + POOL_GW * tn * 2 + tm * D_MODEL * 2
                + D_MODEL * tn * 2 + tm * MEM_W * 2 + MEM_W * tn * 2 + tm * tn * 2)
    vmem += tm * D_MODEL * 2 + tm * D_MODEL * 4 + 8 * tm * tn * 4
    return pl.pallas_call(
        _merge_kernel,
        grid=(t // tm, nj),
        in_specs=[rowfull(D_MODEL), pl.BlockSpec((1, D_MODEL), lambda i, j: (0, 0)),
                  gate_w(0), gate_w(1), gate_w(2), gate_b(0), gate_b(1), gate_b(2),
                  pl.BlockSpec((tm, POOL_GW), lambda i, j: (i, j)),
                  pl.BlockSpec((None, POOL_GW, tn), lambda i, j: (j, 0, 0)),
                  pl.BlockSpec((1, tn), lambda i, j: (0, j)),
                  rowfull(D_MODEL), col(D_MODEL), rowfull(MEM_W), col(MEM_W)],
        out_specs=pl.BlockSpec((tm, tn), lambda i, j: (i, j)),
        out_shape=jax.ShapeDtypeStruct((t, D_MODEL), BF16),
        scratch_shapes=[pltpu.VMEM((tm, D_MODEL), BF16)],
        compiler_params=_params(("parallel", "arbitrary"), vmem),
        name="gated_merge",
    )(x, gmix, wgate, wgate, wgate, bgate, bgate, bgate, d, wpool, pscale, o, wmla, am, wmem)


def _out_proj_kernel(x_ref, m_ref, w_ref, y_ref):
    y_ref[...] = x_ref[...] + _dot(m_ref[...], w_ref[...])


def _out_proj(x, merged, w, tm, tn):
    t = x.shape[0]
    vmem = 2 * (2 * tm * tn * 4 + tm * D_MODEL * 2 + D_MODEL * tn * 2) + 2 * tm * tn * 4
    return pl.pallas_call(
        _out_proj_kernel,
        grid=(t // tm, D_MODEL // tn),
        in_specs=[pl.BlockSpec((tm, tn), lambda i, j: (i, j)),
                  pl.BlockSpec((tm, D_MODEL), lambda i, j: (i, 0)),
                  pl.BlockSpec((D_MODEL, tn), lambda i, j: (0, j))],
        out_specs=pl.BlockSpec((tm, tn), lambda i, j: (i, j)),
        out_shape=jax.ShapeDtypeStruct((t, D_MODEL), F32),
        compiler_params=_params(("parallel", "parallel"), vmem),
        name="out_proj",
    )(x, merged, w)


def _ffn_kernel(x_ref, g_ref, wup_ref, wdown_ref, y_ref, h_sc, acc_sc):
    j = pl.program_id(1)

    @pl.when(j == 0)
    def _():
        h_sc[...] = _rms(x_ref[...], g_ref[...]).astype(BF16)
        acc_sc[...] = jnp.zeros(acc_sc.shape, F32)

    f = jnp.maximum(_dot(h_sc[...], wup_ref[...]), 0.0)
    acc_sc[...] += _dot((f * f).astype(BF16), wdown_ref[...])

    @pl.when(j == pl.num_programs(1) - 1)
    def _():
        y_ref[...] = x_ref[...] + acc_sc[...]


def _ffn(x, g, wup, wdown, tm, tf):
    t = x.shape[0]
    vmem = 2 * (2 * tm * D_MODEL * 4 + 2 * D_MODEL * tf * 2) + tm * D_MODEL * 6 + 3 * tm * tf * 4 + tm * D_MODEL * 4
    return pl.pallas_call(
        _ffn_kernel,
        grid=(t // tm, D_FF // tf),
        in_specs=[pl.BlockSpec((tm, D_MODEL), lambda i, j: (i, 0)),
                  pl.BlockSpec((1, D_MODEL), lambda i, j: (0, 0)),
                  pl.BlockSpec((D_MODEL, tf), lambda i, j: (0, j)),
                  pl.BlockSpec((tf, D_MODEL), lambda i, j: (j, 0))],
        out_specs=pl.BlockSpec((tm, D_MODEL), lambda i, j: (i, 0)),
        out_shape=jax.ShapeDtypeStruct((t, D_MODEL), F32),
        scratch_shapes=[pltpu.VMEM((tm, D_MODEL), BF16), pltpu.VMEM((tm, D_MODEL), F32)],
        compiler_params=_params(("parallel", "arbitrary"), vmem),
        name="ffn",
    )(x, g, wup, wdown)


def _rope_tables(pos):
    half = QK_ROPE // 2
    inv = 1.0 / (ROPE_THETA ** (jnp.arange(half, dtype=F32) * (2.0 / QK_ROPE)))
    ang = pos.astype(F32)[:, None] * inv[None, :]
    return jnp.cos(ang), jnp.sin(ang)


def _lane_tables(cos, sin):
    z = jnp.zeros_like(cos)
    c = jnp.concatenate([cos, cos, z, z], axis=1)
    sa = jnp.concatenate([z, sin, z, z], axis=1)
    sb = jnp.concatenate([-sin, z, z, z], axis=1)
    return c, sa, sb


def _row(v):
    return v.astype(F32).reshape(1, -1)


def _pad_last(a, width):
    return jnp.pad(a, [(0, 0)] * (a.ndim - 1) + [(0, width - a.shape[-1])])


def kernel(x_prompt, mem_prompt, x_sample, cache_mla_latent, cache_mla_kpe, state_pool, cache_mem_k, cache_mem_v,
           g_mix, w_in, b_gate, w_pool, pool_scale, g_q_lat, w_qb, g_q_head, g_kv_lat, w_kb, w_vb, g_k_head,
           w_mla_o, g_mem, w_mem_kv, g_mem_q, g_mem_k, w_mem_o, w_out, g_ff, w_up, w_down):
    batch, seq, _ = x_prompt.shape
    dec_batch, dec_seq, _ = x_sample.shape
    past = cache_mla_latent.shape[1]

    c_u, c_q, c_kv, c_pe, c_mq = 0, POOL_W, POOL_W + Q_LORA, POOL_W + Q_LORA + KV_LORA, POOL_W + Q_LORA + KV_LORA + QK_ROPE
    c_gate = c_mq + MEM_W
    wz = jnp.concatenate([w_in[:, c_u:c_pe], w_in[:, c_mq:c_gate], w_in[:, c_pe:c_mq],
                          jnp.zeros((D_MODEL, 128 - QK_ROPE), w_in.dtype)], axis=1).astype(BF16)
    wgate = w_in[:, c_gate:].astype(BF16)
    bgate = _row(b_gate)
    wq = _pad_last(w_qb.reshape(Q_LORA, MLA_HEADS, QK_HEAD), HEAD_PAD).reshape(Q_LORA, MLA_HEADS * HEAD_PAD).astype(BF16)
    wkb = w_kb.astype(BF16)
    wvb = w_vb.astype(BF16)
    wvbt = wvb.T
    gq_p = _row(_pad_last(g_q_head.astype(F32) * MLA_SCALE_LOG2, HEAD_PAD))
    gkn = _row(g_k_head[:QK_NOPE])
    gkr = _row(_pad_last(g_k_head[QK_NOPE:], 128))
    gq_s = _row(_pad_last(jnp.concatenate([g_q_head[:QK_NOPE].astype(F32) * g_k_head[:QK_NOPE].astype(F32),
                                           g_q_head[QK_NOPE:].astype(F32)]) * MLA_SCALE_LOG2, HEAD_PAD))
    gmq = _row(g_mem_q.astype(F32) * MEM_SCALE)
    wpool = w_pool.astype(BF16)
    wmla = w_mla_o.astype(BF16)
    wmem_o = w_mem_o.astype(BF16)
    wmem_kv = w_mem_kv.astype(BF16)
    wout = w_out.astype(BF16)
    wup = w_up.astype(BF16)
    wdown = w_down.astype(BF16)
    gmix, gql, gkvl, gff, gmem, gmk, pscale = map(_row, (g_mix, g_q_lat, g_kv_lat, g_ff, g_mem, g_mem_k, pool_scale))

    def dense_tail(x, d, o, am, tm):
        merged = _merge(x, gmix, wgate, bgate, d, wpool, pscale, o, wmla, am, wmem_o, tm)
        x1 = _out_proj(x, merged, wout, tm, 1024)
        return _ffn(x1, gff, wup, wdown, tm, 1024)

    tm = 512
    xp = x_prompt.reshape(batch * seq, D_MODEL)
    mem_kv = _mem_kv(mem_prompt.reshape(batch * N_MEM, D_MODEL), gmem, wmem_kv, gmk)
    u, hq, ckv, kpe, qm = _in_proj(xp, gmix, wz, gql, gkvl, gmq, tm)
    cos_p, sin_p = _rope_tables(jnp.arange(seq, dtype=jnp.int32))
    q, k, vt = _qkv(hq, ckv, kpe, wq, wkb, wvbt, gq_p, gkn, gkr, _lane_tables(cos_p, sin_p), tm, seq // tm)
    o = _flash(q, k, vt, batch, seq, 2048)
    am = _mem_attn(qm, mem_kv, mem_kv, 0, 1, tm, seq // tm)
    d = _pool(u, jnp.zeros((batch, POOL_HALO, POOL_W), F32), tm, seq // tm, 0)
    y_p = dense_tail(xp, d, o, am, tm).reshape(batch, seq, D_MODEL)
    lat_p = ckv.reshape(batch, seq, KV_LORA)
    kpe_p = kpe[:, :QK_ROPE].reshape(batch, seq, QK_ROPE)
    pool_p = u.reshape(batch, seq, POOL_W)[:, seq - POOL_STATE:]
    mem_k_p = mem_kv[:, :MEM_W].reshape(batch, N_MEM, MEM_HEADS, MEM_HEAD_DIM)
    mem_v_p = mem_kv[:, MEM_W:].reshape(batch, N_MEM, MEM_HEADS, MEM_HEAD_DIM)

    ts = dec_batch * dec_seq
    xs = x_sample.reshape(ts, D_MODEL)
    u_s, hq_s, ckv_s, kpe_s, qm_s = _in_proj(xs, gmix, wz, gql, gkvl, gmq, ts)
    pos_s = past + jnp.arange(dec_seq, dtype=jnp.int32)
    cos_s, sin_s = _rope_tables(pos_s)
    tabs_s = tuple(jnp.tile(a, (dec_batch, 1)) for a in _lane_tables(cos_s, sin_s))
    q_s = _q_only(hq_s, wq, gq_s, tabs_s, ts)
    qa, qr = _q_absorb(q_s, wkb)
    cos_c, sin_c = _rope_tables(jnp.arange(past, dtype=jnp.int32))
    pad_new = 128 - dec_seq
    nlat = jnp.pad(ckv_s.reshape(dec_batch, dec_seq, KV_LORA), ((0, 0), (0, pad_new), (0, 0)))
    nkpet = jnp.pad(jnp.swapaxes(kpe_s[:, :QK_ROPE].reshape(dec_batch, dec_seq, QK_ROPE), 1, 2),
                    ((0, 0), (0, 0), (0, pad_new)))
    o_lat = _cache_attn(wkb.T, qa, qr, cache_mla_latent, jnp.swapaxes(cache_mla_kpe, 1, 2),
                        cos_c.T, sin_c.T, nlat, nkpet,
                        jnp.pad(cos_s.T, ((0, 0), (0, pad_new))), jnp.pad(sin_s.T, ((0, 0), (0, pad_new))),
                        g_k_head[QK_NOPE:].astype(F32).reshape(QK_ROPE, 1),
                        dec_seq, dec_seq, 1024)
    o_s = _v_up(o_lat, wvb, dec_seq)
    am_s = _mem_attn(qm_s, cache_mem_k.reshape(dec_batch * N_MEM, MEM_W), cache_mem_v.reshape(dec_batch * N_MEM, MEM_W),
                     0, 0, dec_seq, 1)
    left_s = jnp.pad(state_pool.astype(F32), ((0, 0), (POOL_HALO - POOL_STATE, 0), (0, 0)))
    d_s = _pool(u_s, left_s, dec_seq, 1, past)
    y_s = dense_tail(xs, d_s, o_s, am_s, ts).reshape(dec_batch, dec_seq, D_MODEL)
    lat_s = ckv_s.reshape(dec_batch, dec_seq, KV_LORA)
    kpe_s_out = kpe_s[:, :QK_ROPE].reshape(dec_batch, dec_seq, QK_ROPE)
    pool_s = u_s.reshape(dec_batch, dec_seq, POOL_W)[:, dec_seq - POOL_STATE:]

    return (y_p, y_s, lat_p, kpe_p, pool_p, mem_k_p, mem_v_p, lat_s, kpe_s_out, pool_s)
```

```python
import functools

import jax
import jax.numpy as jnp
from jax import lax
from jax.experimental import pallas as pl
from jax.experimental.pallas import tpu as pltpu

F32 = jnp.float32
BF16 = jnp.bfloat16

EPS = 1e-6
CHUNK = 64
D_MODEL = 2048
POOL_WINDOWS = (2, 4, 8, 16)
POOL_W = 1024
POOL_GW = 256
POOL_OUT_GW = 512
POOL_STATE = 15
POOL_HALO = 16
MLA_HEADS = 16
Q_LORA = 512
KV_LORA = 512
QK_NOPE = 128
QK_ROPE = 64
QK_HEAD = QK_NOPE + QK_ROPE
HEAD_PAD = 256
V_HEAD = 128
V_ROWS = 144
ROPE_THETA = 10000.0
LOG2_E = 1.4426950408889634
MLA_SCALE = QK_HEAD ** -0.5
MLA_SCALE_LOG2 = MLA_SCALE * LOG2_E
N_MEM = 256
MEM_HEADS = 4
MEM_HEAD_DIM = 256
MEM_W = MEM_HEADS * MEM_HEAD_DIM
MEM_SCALE = MEM_HEAD_DIM ** -0.5
D_FF = 4 * D_MODEL
NEG_INF = -1e30

Z_U = 0
Z_QLAT = Z_U + POOL_W
Z_KVLAT = Z_QLAT + Q_LORA
Z_MQ = Z_KVLAT + KV_LORA
W_IN_KPE = Z_MQ
W_IN_MQ = W_IN_KPE + QK_ROPE
W_IN_GATE = W_IN_MQ + MEM_W

BF16_SUBLANES = 16
V7X_VMEM_BYTES = 64 * 1024 * 1024
VMEM_CEILING = V7X_VMEM_BYTES - 6 * 1024 * 1024


def _params(semantics, vmem_bytes):
    return pltpu.CompilerParams(dimension_semantics=semantics,
                                vmem_limit_bytes=int(min(vmem_bytes, VMEM_CEILING)))


def _rms(x, g):
    return x * lax.rsqrt(jnp.mean(x * x, axis=-1, keepdims=True) + EPS) * g


def _dot(a, b):
    return jnp.dot(a, b, preferred_element_type=F32)


def _dot_nt(a, b):
    return lax.dot_general(a, b, (((1,), (1,)), ((), ())), preferred_element_type=F32)


def _rope128(x, c, sa, sb):
    return x * c + pltpu.roll(x, 32, 1) * sa + pltpu.roll(x, 96, 1) * sb


def _in_proj_kernel(x_ref, gmix_ref, wa_ref, wkpe_ref, wmq_ref, gq_ref, gkv_ref, gmq_ref,
                    u_ref, hq_ref, ckv_ref, kpe_ref, qm_ref):
    h = _rms(x_ref[...], gmix_ref[...]).astype(BF16)
    u_ref[...] = _dot_nt(h, wa_ref[Z_U:Z_QLAT, :])
    hq_ref[...] = _rms(_dot_nt(h, wa_ref[Z_QLAT:Z_KVLAT, :]), gq_ref[...]).astype(BF16)
    ckv_ref[...] = _rms(_dot_nt(h, wa_ref[Z_KVLAT:Z_MQ, :]), gkv_ref[...])
    kpe_ref[:, :QK_ROPE] = _dot_nt(h, wkpe_ref[...])
    kpe_ref[:, QK_ROPE:] = jnp.zeros((h.shape[0], 128 - QK_ROPE), F32)
    for hh in range(MEM_HEADS):
        lo = hh * MEM_HEAD_DIM
        m = _dot_nt(h, wmq_ref[lo:lo + MEM_HEAD_DIM, :])
        qm_ref[:, lo:lo + MEM_HEAD_DIM] = _rms(m, gmq_ref[...]).astype(BF16)


def _in_proj(x, gmix, w_in_t, gq, gkv, gmq_scaled, tm):
    t = x.shape[0]
    row = lambda w: pl.BlockSpec((tm, w), lambda i: (i, 0))
    full = lambda a: pl.BlockSpec(a.shape, lambda i: (0, 0))
    rows_at = lambda n, start: pl.BlockSpec((pl.Element(n), pl.Element(D_MODEL)),
                                            lambda i: (pl.multiple_of(start + 0 * i, BF16_SUBLANES), 0))
    w_elems = (Z_MQ + QK_ROPE + MEM_W) * D_MODEL
    vmem = 2 * (tm * D_MODEL * 4 + w_elems * 2 + tm * (POOL_W * 4 + Q_LORA * 2 + KV_LORA * 4 + 128 * 4 + MEM_W * 2))
    vmem += 6 * tm * POOL_W * 4
    return pl.pallas_call(
        _in_proj_kernel,
        grid=(t // tm,),
        in_specs=[row(D_MODEL), full(gmix), rows_at(Z_MQ, 0), rows_at(QK_ROPE, W_IN_KPE), rows_at(MEM_W, W_IN_MQ),
                  full(gq), full(gkv), full(gmq_scaled)],
        out_specs=[row(POOL_W), row(Q_LORA), row(KV_LORA), row(128), row(MEM_W)],
        out_shape=[jax.ShapeDtypeStruct((t, POOL_W), F32), jax.ShapeDtypeStruct((t, Q_LORA), BF16),
                   jax.ShapeDtypeStruct((t, KV_LORA), F32), jax.ShapeDtypeStruct((t, 128), F32),
                   jax.ShapeDtypeStruct((t, MEM_W), BF16)],
        compiler_params=_params(("parallel",), vmem),
        name="in_proj",
    )(x, gmix, w_in_t, w_in_t, w_in_t, gq, gkv, gmq_scaled)


def _q_heads(hq, wq_ref, gq_ref, qtab, q_ref):
    first = lax.broadcasted_iota(jnp.int32, (1, 128), 1) < QK_ROPE
    for h in range(MLA_HEADS):
        lo = h * HEAD_PAD
        qh = _dot(hq, wq_ref[:, lo:lo + HEAD_PAD])
        qn, qr = qh[:, :QK_NOPE], qh[:, QK_NOPE:]
        ss = jnp.sum(qn * qn + jnp.where(first, qr * qr, 0.0), axis=-1, keepdims=True)
        r = lax.rsqrt(ss * (1.0 / QK_HEAD) + EPS)
        q_ref[:, lo:lo + QK_NOPE] = (qn * r * gq_ref[:, :QK_NOPE]).astype(BF16)
        q_ref[:, lo + QK_NOPE:lo + HEAD_PAD] = (qr * r * gq_ref[:, QK_NOPE:] * qtab).astype(BF16)


def _qkv_kernel(hq_ref, ckv_ref, kpe_ref, wq_ref, wkb_ref, wvbt_ref, gq_ref, gkn_ref, gkr_ref,
                qtab_ref, c_ref, sa_ref, sb_ref, q_ref, k_ref, vt_ref):
    _q_heads(hq_ref[...], wq_ref, gq_ref, qtab_ref[...], q_ref)
    ckv = ckv_ref[...].astype(BF16)
    kpe = kpe_ref[...]
    ss_pe = jnp.sum(kpe * kpe, axis=-1, keepdims=True)
    kr = _rope128(kpe * gkr_ref[...], c_ref[...], sa_ref[...], sb_ref[...])
    kr = kr + pltpu.roll(kr, QK_ROPE, 1)
    for p in range(MLA_HEADS // 2):
        kn2 = _dot(ckv, wkb_ref[:, p * 2 * QK_NOPE:(p + 1) * 2 * QK_NOPE])
        for s in range(2):
            lo = (2 * p + s) * HEAD_PAD
            kn = kn2[:, s * QK_NOPE:(s + 1) * QK_NOPE]
            r = lax.rsqrt((jnp.sum(kn * kn, axis=-1, keepdims=True) + ss_pe) * (1.0 / QK_HEAD) + EPS)
            k_ref[:, lo:lo + QK_NOPE] = (kn * r * gkn_ref[...]).astype(BF16)
            k_ref[:, lo + QK_NOPE:lo + HEAD_PAD] = (kr * r).astype(BF16)
    vt = _dot_nt(wvbt_ref[...], ckv)
    tm = vt.shape[1]
    ones_row = (lax.broadcasted_iota(jnp.int32, (V_ROWS - V_HEAD, tm), 0) == 0).astype(BF16)
    for h in range(MLA_HEADS):
        vt_ref[h, :V_HEAD, :] = vt[h * V_HEAD:(h + 1) * V_HEAD].astype(BF16)
        vt_ref[h, V_HEAD:, :] = ones_row


def _qkv(hq, ckv, kpe, wq, wkb, wvbt, gq, gkn, gkr, tabs, tm, seq_blocks):
    t = hq.shape[0]
    row = lambda w: pl.BlockSpec((tm, w), lambda i: (i, 0))
    full = lambda a: pl.BlockSpec(a.shape, lambda i: (0, 0))
    tab = pl.BlockSpec((tm, 128), lambda i: (i % seq_blocks, 0))
    vmem = 2 * (tm * (Q_LORA * 2 + KV_LORA * 4 + 128 * 4 + 4 * 128 * 4) + (wq.size + wkb.size + wvbt.size) * 2
                + tm * (2 * MLA_HEADS * HEAD_PAD * 2 + MLA_HEADS * V_ROWS * 2))
    vmem += 8 * tm * HEAD_PAD * 4 + 2 * tm * MLA_HEADS * V_HEAD * 4
    return pl.pallas_call(
        _qkv_kernel,
        grid=(t // tm,),
        in_specs=[row(Q_LORA), row(KV_LORA), row(128), full(wq), full(wkb), full(wvbt),
                  full(gq), full(gkn), full(gkr), tab, tab, tab, tab],
        out_specs=[row(MLA_HEADS * HEAD_PAD), row(MLA_HEADS * HEAD_PAD),
                   pl.BlockSpec((None, MLA_HEADS, None, V_ROWS, tm),
                                lambda i: (i // seq_blocks, 0, i % seq_blocks, 0, 0))],
        out_shape=[jax.ShapeDtypeStruct((t, MLA_HEADS * HEAD_PAD), BF16),
                   jax.ShapeDtypeStruct((t, MLA_HEADS * HEAD_PAD), BF16),
                   jax.ShapeDtypeStruct((t // (tm * seq_blocks), MLA_HEADS, seq_blocks, V_ROWS, tm), BF16)],
        compiler_params=_params(("parallel",), vmem),
        name="qkv_proj",
    )(hq, ckv, kpe, wq, wkb, wvbt, gq, gkn, gkr, *tabs)


def _q_only_kernel(hq_ref, wq_ref, gq_ref, qtab_ref, q_ref):
    _q_heads(hq_ref[...], wq_ref, gq_ref, qtab_ref[...], q_ref)


def _q_only(hq, wq, gq, qtab, tm):
    t = hq.shape[0]
    row = lambda w: pl.BlockSpec((tm, w), lambda i: (i, 0))
    full = lambda a: pl.BlockSpec(a.shape, lambda i: (0, 0))
    vmem = 2 * (tm * (Q_LORA * 2 + 128 * 4 + MLA_HEADS * HEAD_PAD * 2) + wq.size * 2) + 8 * tm * HEAD_PAD * 4
    return pl.pallas_call(
        _q_only_kernel,
        grid=(t // tm,),
        in_specs=[row(Q_LORA), full(wq), full(gq), pl.BlockSpec((tm, 128), lambda i: (0, 0))],
        out_specs=row(MLA_HEADS * HEAD_PAD),
        out_shape=jax.ShapeDtypeStruct((t, MLA_HEADS * HEAD_PAD), BF16),
        compiler_params=_params(("parallel",), vmem),
        name="q_proj",
    )(hq, wq, gq, qtab)


def _flash_kernel(q_ref, k_ref, vt_ref, o_ref, s0_sc, s1_sc, m_sc, acc_sc, *, tq, tk):
    qi = pl.program_id(2)
    m_sc[...] = jnp.full(m_sc.shape, NEG_INF, F32)
    acc_sc[...] = jnp.zeros(acc_sc.shape, F32)

    def scores(kj, s_sc, q_lo=0):
        start = pl.multiple_of(kj * tk, tk)
        s_sc[:, q_lo:] = _dot_nt(k_ref[pl.ds(start, tk), :], q_ref[q_lo:, :])

    def consume(kj, s_sc, key_chunk0=None, q_lo=0):
        s = s_sc[:, q_lo:]
        if key_chunk0 is not None:
            kc = key_chunk0 + lax.broadcasted_iota(jnp.int32, s.shape, 0) // CHUNK
            qc = (q_lo + lax.broadcasted_iota(jnp.int32, s.shape, 1)) // CHUNK
            s = jnp.where(qc >= kc, s, NEG_INF)
        m = m_sc[:, q_lo:]
        m_new = jnp.maximum(m, jnp.max(s, axis=0, keepdims=True))
        alpha = jnp.exp2(m - m_new)
        p = jnp.exp2(s - m_new).astype(BF16)
        acc_sc[:, q_lo:] = alpha * acc_sc[:, q_lo:] + _dot(vt_ref[kj], p)
        m_sc[:, q_lo:] = m_new

    scores(0, s0_sc)

    def pair(t, carry):
        scores(2 * t + 1, s1_sc)
        consume(2 * t, s0_sc)
        scores(2 * t + 2, s0_sc)
        consume(2 * t + 1, s1_sc)
        return carry

    n_diag = tq // tk
    lax.fori_loop(0, qi * (n_diag // 2), pair, 0)
    base = qi * n_diag
    bufs = (s0_sc, s1_sc)
    for d in range(n_diag):
        if d + 1 < n_diag:
            scores(base + d + 1, bufs[(d + 1) % 2], (d + 1) * tk)
        consume(base + d, bufs[d % 2], d * tk // CHUNK, d * tk)
    acc = acc_sc[...]
    o_ref[...] = (acc[:V_HEAD] / acc[V_HEAD:V_HEAD + 1]).T.astype(BF16)


def _flash(q, k, vt, batch, seq, tq):
    tk = vt.shape[-1]
    assert tq % (2 * tk) == 0 and tk % CHUNK == 0 and seq % tq == 0
    nq = seq // tq
    vmem = 2 * (tq * HEAD_PAD * 2 + seq * HEAD_PAD * 2 + seq * V_ROWS * 2 + tq * V_HEAD * 2)
    vmem += 2 * tk * tq * 4 + V_ROWS * tq * 4 + 4 * tk * tq * 4
    return pl.pallas_call(
        functools.partial(_flash_kernel, tq=tq, tk=tk),
        grid=(batch, MLA_HEADS, nq),
        in_specs=[pl.BlockSpec((tq, HEAD_PAD), lambda b, h, i: (b * nq + i, h)),
                  pl.BlockSpec((seq, HEAD_PAD), lambda b, h, i: (b, h)),
                  pl.BlockSpec((None, None, seq // tk, V_ROWS, tk), lambda b, h, i: (b, h, 0, 0, 0))],
        out_specs=pl.BlockSpec((tq, V_HEAD), lambda b, h, i: (b * nq + i, h)),
        out_shape=jax.ShapeDtypeStruct((batch * seq, MLA_HEADS * V_HEAD), BF16),
        scratch_shapes=[pltpu.VMEM((tk, tq), F32), pltpu.VMEM((tk, tq), F32),
                        pltpu.VMEM((1, tq), F32), pltpu.VMEM((V_ROWS, tq), F32)],
        compiler_params=_params(("parallel", "parallel", "arbitrary"), vmem),
        name="mla_flash",
    )(q, k, vt)


def _q_absorb_kernel(q_ref, wkb_ref, qa_ref, qr_ref):
    qa_ref[...] = _dot_nt(q_ref[:, :QK_NOPE], wkb_ref[...]).astype(BF16)
    qr_ref[...] = q_ref[:, QK_NOPE:]


def _q_absorb(q, wkb):
    t = q.shape[0]
    vmem = 2 * (t * HEAD_PAD * 2 + KV_LORA * QK_NOPE * 2 + t * KV_LORA * 2 + t * 128 * 2) + 2 * t * KV_LORA * 4
    return pl.pallas_call(
        _q_absorb_kernel,
        grid=(MLA_HEADS,),
        in_specs=[pl.BlockSpec((t, HEAD_PAD), lambda h: (0, h)),
                  pl.BlockSpec((KV_LORA, QK_NOPE), lambda h: (0, h))],
        out_specs=[pl.BlockSpec((None, t, KV_LORA), lambda h: (h, 0, 0)),
                   pl.BlockSpec((None, t, 128), lambda h: (h, 0, 0))],
        out_shape=[jax.ShapeDtypeStruct((MLA_HEADS, t, KV_LORA), BF16),
                   jax.ShapeDtypeStruct((MLA_HEADS, t, 128), BF16)],
        compiler_params=_params(("parallel",), vmem),
        name="q_absorb",
    )(q, wkb)


def _cache_attn_kernel(wkbt_ref, qa_ref, qr_ref, lat_ref, kpet_ref, ct_ref, st_ref,
                       nlat_ref, nkpet_ref, nct_ref, nst_ref, gkr_ref,
                       o_ref, s_sc, m_sc, l_sc, acc_sc, *, nq, n_new, past, tk):
    j = pl.program_id(1)
    last = pl.num_programs(1) - 1
    rows = MLA_HEADS * nq
    half = QK_ROPE // 2

    @pl.when(j == 0)
    def _():
        m_sc[...] = jnp.full(m_sc.shape, NEG_INF, F32)
        l_sc[...] = jnp.zeros(l_sc.shape, F32)
        acc_sc[...] = jnp.zeros(acc_sc.shape, F32)

    def block(lat, kpet, ct, st, width, visible):
        c = lat.astype(BF16)
        kt = _dot_nt(wkbt_ref[...], c)
        s = _dot_nt(qa_ref[...].reshape(rows, KV_LORA), c)
        ss_pe = jnp.sum(kpet * kpet, axis=0, keepdims=True)
        kr = kpet * gkr_ref[...]
        k1, k2 = kr[:half], kr[half:]
        o1, o2 = k1 * ct - k2 * st, k2 * ct + k1 * st
        krot = jnp.concatenate([o1, o2, o1, o2], axis=0).astype(BF16)
        s = s + _dot(qr_ref[...].reshape(rows, 128), krot)
        for h in range(MLA_HEADS):
            kh = kt[h * QK_NOPE:(h + 1) * QK_NOPE]
            r = lax.rsqrt((jnp.sum(kh * kh, axis=0, keepdims=True) + ss_pe) * (1.0 / QK_HEAD) + EPS)
            s_sc[h * nq:(h + 1) * nq, :width] = s[h * nq:(h + 1) * nq] * r
        s = s_sc[:, :width]
        if visible is not None:
            s = jnp.where(visible, s, NEG_INF)
        m = m_sc[...]
        m_new = jnp.maximum(m, jnp.max(s, axis=-1, keepdims=True))
        alpha = jnp.exp2(m - m_new)
        p = jnp.exp2(s - m_new)
        l_sc[...] = alpha * l_sc[...] + jnp.sum(p, axis=-1, keepdims=True)
        acc_sc[...] = alpha * acc_sc[...] + _dot(p.astype(BF16), c)
        m_sc[...] = m_new

    @pl.when(j < last)
    def _():
        block(lat_ref[...], kpet_ref[...], ct_ref[...], st_ref[...], tk, None)

    @pl.when(j == last)
    def _():
        kk = lax.broadcasted_iota(jnp.int32, (1, 128), 1)
        visible = kk < n_new
        if (past + n_new - 1) // CHUNK != past // CHUNK:
            qpos = past + lax.broadcasted_iota(jnp.int32, (rows, 1), 0) % nq
            visible = visible & (qpos // CHUNK >= (past + kk) // CHUNK)
        block(nlat_ref[...], nkpet_ref[...], nct_ref[...], nst_ref[...], 128, visible)
        o_ref[...] = (acc_sc[...] / l_sc[...]).astype(BF16)


def _cache_attn(wkbt, qa, qr, lat, kpet, ct, st, nlat, nkpet, nct, nst, gkr, nq, n_new, tk):
    batch, past, _ = lat.shape
    nkb = past // tk
    rows = MLA_HEADS * nq
    cj = lambda j: jnp.minimum(j, nkb - 1)
    kern = functools.partial(_cache_attn_kernel, nq=nq, n_new=n_new, past=past, tk=tk)
    vmem = 2 * (wkbt.size * 2 + rows * (KV_LORA + 128) * 2 + tk * KV_LORA * 4 + 3 * 64 * tk * 4 + rows * KV_LORA * 2)
    vmem += (MLA_HEADS * QK_NOPE + 4 * rows) * tk * 4 + tk * KV_LORA * 2 + rows * KV_LORA * 8
    return pl.pallas_call(
        kern,
        grid=(batch, nkb + 1),
        in_specs=[pl.BlockSpec(wkbt.shape, lambda b, j: (0, 0)),
                  pl.BlockSpec((MLA_HEADS, nq, KV_LORA), lambda b, j: (0, b, 0)),
                  pl.BlockSpec((MLA_HEADS, nq, 128), lambda b, j: (0, b, 0)),
                  pl.BlockSpec((None, tk, KV_LORA), lambda b, j: (b, cj(j), 0)),
                  pl.BlockSpec((None, QK_ROPE, tk), lambda b, j: (b, 0, cj(j))),
                  pl.BlockSpec((QK_ROPE // 2, tk), lambda b, j: (0, cj(j))),
                  pl.BlockSpec((QK_ROPE // 2, tk), lambda b, j: (0, cj(j))),
                  pl.BlockSpec((None, 128, KV_LORA), lambda b, j: (b, 0, 0)),
                  pl.BlockSpec((None, QK_ROPE, 128), lambda b, j: (b, 0, 0)),
                  pl.BlockSpec((QK_ROPE // 2, 128), lambda b, j: (0, 0)),
                  pl.BlockSpec((QK_ROPE // 2, 128), lambda b, j: (0, 0)),
                  pl.BlockSpec((QK_ROPE, 1), lambda b, j: (0, 0))],
        out_specs=pl.BlockSpec((None, rows, KV_LORA), lambda b, j: (b, 0, 0)),
        out_shape=jax.ShapeDtypeStruct((batch, rows, KV_LORA), BF16),
        scratch_shapes=[pltpu.VMEM((rows, tk), F32), pltpu.VMEM((rows, 1), F32),
                        pltpu.VMEM((rows, 1), F32), pltpu.VMEM((rows, KV_LORA), F32)],
        compiler_params=_params(("parallel", "arbitrary"), vmem),
        name="mla_cache_attn",
    )(wkbt, qa, qr, lat, kpet, ct, st, nlat, nkpet, nct, nst, gkr)


def _v_up_kernel(ol_ref, wvb_ref, o_ref):
    b, nq, _ = ol_ref.shape
    o_ref[...] = _dot(ol_ref[...].reshape(b * nq, KV_LORA), wvb_ref[...]).astype(BF16)


def _v_up(o_lat, wvb, nq):
    batch = o_lat.shape[0]
    t = batch * nq
    vmem = 2 * (t * KV_LORA * 2 + KV_LORA * V_HEAD * 2 + t * V_HEAD * 2) + 2 * t * V_HEAD * 4
    return pl.pallas_call(
        _v_up_kernel,
        grid=(MLA_HEADS,),
        in_specs=[pl.BlockSpec((batch, nq, KV_LORA), lambda h: (0, h, 0)),
                  pl.BlockSpec((KV_LORA, V_HEAD), lambda h: (0, h))],
        out_specs=pl.BlockSpec((t, V_HEAD), lambda h: (0, h)),
        out_shape=jax.ShapeDtypeStruct((t, MLA_HEADS * V_HEAD), BF16),
        compiler_params=_params(("parallel",), vmem),
        name="v_up",
    )(o_lat, wvb)


def _mem_kv_kernel(mem_ref, gmem_ref, w_ref, gk_ref, kv_ref):
    j = pl.program_id(0)
    h = _rms(mem_ref[...], gmem_ref[...]).astype(BF16)

    @pl.when(j == 0)
    def _():
        for hh in range(MEM_HEADS):
            lo = hh * MEM_HEAD_DIM
            kv_ref[:, lo:lo + MEM_HEAD_DIM] = _rms(_dot(h, w_ref[:, lo:lo + MEM_HEAD_DIM]), gk_ref[...])

    @pl.when(j == 1)
    def _():
        kv_ref[...] = _dot(h, w_ref[...])


def _mem_kv(mem, gmem, w, gk):
    t = mem.shape[0]
    vmem = 2 * (t * D_MODEL * 4 + D_MODEL * MEM_W * 2 + t * MEM_W * 4) + t * D_MODEL * 6 + t * MEM_W * 4
    return pl.pallas_call(
        _mem_kv_kernel,
        grid=(2,),
        in_specs=[pl.BlockSpec((t, D_MODEL), lambda j: (0, 0)), pl.BlockSpec((1, D_MODEL), lambda j: (0, 0)),
                  pl.BlockSpec((D_MODEL, MEM_W), lambda j: (0, j)), pl.BlockSpec((1, MEM_HEAD_DIM), lambda j: (0, 0))],
        out_specs=pl.BlockSpec((t, MEM_W), lambda j: (0, j)),
        out_shape=jax.ShapeDtypeStruct((t, 2 * MEM_W), F32),
        compiler_params=_params(("parallel",), vmem),
        name="memory_kv",
    )(mem, gmem, w, gk)


def _mem_attn_kernel(qm_ref, k_ref, v_ref, o_ref):
    for hh in range(MEM_HEADS):
        sl = slice(hh * MEM_HEAD_DIM, (hh + 1) * MEM_HEAD_DIM)
        s = _dot_nt(qm_ref[:, sl], k_ref[:, sl].astype(BF16))
        p = jnp.exp(s - jnp.max(s, axis=-1, keepdims=True))
        l = jnp.sum(p, axis=-1, keepdims=True)
        o_ref[:, sl] = (_dot(p.astype(BF16), v_ref[:, sl].astype(BF16)) / l).astype(BF16)


def _mem_attn(qm, k_arr, v_arr, k_col, v_col, tm, blocks_per_batch):
    t = qm.shape[0]
    vmem = 2 * (2 * tm * MEM_W * 2 + 2 * N_MEM * MEM_W * 4) + 6 * tm * N_MEM * 4 + 2 * N_MEM * MEM_W * 2
    return pl.pallas_call(
        _mem_attn_kernel,
        grid=(t // tm,),
        in_specs=[pl.BlockSpec((tm, MEM_W), lambda i: (i, 0)),
                  pl.BlockSpec((N_MEM, MEM_W), lambda i: (i // blocks_per_batch, k_col)),
                  pl.BlockSpec((N_MEM, MEM_W), lambda i: (i // blocks_per_batch, v_col))],
        out_specs=pl.BlockSpec((tm, MEM_W), lambda i: (i, 0)),
        out_shape=jax.ShapeDtypeStruct((t, MEM_W), BF16),
        compiler_params=_params(("parallel",), vmem),
        name="memory_attn",
    )(qm, k_arr, v_arr)


def _pool_kernel(u_ref, prev_ref, left_ref, d_ref, xp_sc, *, tm, blocks_per_seq, pos0):
    i = pl.program_id(0)
    sb = i % blocks_per_seq
    xp_sc[:POOL_HALO, :] = jnp.where(sb == 0, left_ref[...], prev_ref[...])
    xp_sc[POOL_HALO:, :] = u_ref[...]
    pos = pos0 + sb * tm + lax.broadcasted_iota(jnp.int32, (tm, 1), 0)
    for g, win in enumerate(POOL_WINDOWS):
        sl = slice(g * POOL_GW, (g + 1) * POOL_GW)
        wsum = xp_sc[POOL_HALO:, sl]
        for k in range(1, win):
            wsum = wsum + xp_sc[POOL_HALO - k:POOL_HALO - k + tm, sl]
        cnt = jnp.minimum(pos + 1, win).astype(F32)
        d_ref[:, sl] = (wsum / cnt - xp_sc[POOL_HALO:, sl]).astype(BF16)


def _pool(u, left, tm, blocks_per_seq, pos0):
    t = u.shape[0]
    halo_per_blk = tm // POOL_HALO
    kern = functools.partial(_pool_kernel, tm=tm, blocks_per_seq=blocks_per_seq, pos0=pos0)
    vmem = 2 * (tm * POOL_W * 4 + 2 * POOL_HALO * POOL_W * 4 + tm * POOL_W * 2) + (tm + POOL_HALO) * POOL_W * 4
    vmem += 6 * tm * POOL_GW * 4
    return pl.pallas_call(
        kern,
        grid=(t // tm,),
        in_specs=[pl.BlockSpec((tm, POOL_W), lambda i: (i, 0)),
                  pl.BlockSpec((POOL_HALO, POOL_W), lambda i: (jnp.maximum(i * halo_per_blk - 1, 0), 0)),
                  pl.BlockSpec((None, POOL_HALO, POOL_W), lambda i: (i // blocks_per_seq, 0, 0))],
        out_specs=pl.BlockSpec((tm, POOL_W), lambda i: (i, 0)),
        out_shape=jax.ShapeDtypeStruct((t, POOL_W), BF16),
        scratch_shapes=[pltpu.VMEM((tm + POOL_HALO, POOL_W), F32)],
        compiler_params=_params(("parallel",), vmem),
        name="pool_windows",
    )(u, u, left)


def _merge_kernel(x_ref, gmix_ref, wg0_ref, wg1_ref, wg2_ref, bg0_ref, bg1_ref, bg2_ref,
                  d_ref, wpool_ref, pscale_ref, o_ref, wmla_ref, am_ref, wmem_ref,
                  out_ref, h_sc):
    @pl.when(pl.program_id(1) == 0)
    def _():
        h_sc[...] = _rms(x_ref[...], gmix_ref[...]).astype(BF16)

    h = h_sc[...]
    y_pool = _dot(d_ref[...], wpool_ref[...]) * pscale_ref[...]
    merged = jax.nn.sigmoid(_dot_nt(h, wg0_ref[...]) + bg0_ref[...]) * y_pool
    merged += jax.nn.sigmoid(_dot_nt(h, wg1_ref[...]) + bg1_ref[...]) * _dot(o_ref[...], wmla_ref[...])
    merged += jax.nn.sigmoid(_dot_nt(h, wg2_ref[...]) + bg2_ref[...]) * _dot(am_ref[...], wmem_ref[...])
    out_ref[...] = merged.astype(BF16)


def _merge(x, gmix, w_in_t, bgate, d, wpool, pscale, o, wmla, am, wmem, tm):
    t = x.shape[0]
    tn = POOL_OUT_GW
    nj = D_MODEL // tn
    rowfull = lambda w: pl.BlockSpec((tm, w), lambda i, j: (i, 0))
    gate_w = lambda b: pl.BlockSpec((pl.Element(tn), pl.Element(D_MODEL)),
                                    lambda i, j: (pl.multiple_of(W_IN_GATE + (b * nj + j) * tn, BF16_SUBLANES), 0))
    gate_b = lambda b: pl.BlockSpec((1, tn), lambda i, j: (0, b * nj + j))
    col = lambda k: pl.BlockSpec((k, tn), lambda i, j: (0, j))
    vmem = 2 * (tm * D_MODEL * 4 + 3 * D_MODEL * tn * 2 + tm * POOL_GW * 2 + POOL_GW * tn * 2 + tm * D_MODEL * 2
                + D_MODEL * tn * 2 + tm * MEM_W * 2 + MEM_W * tn * 2 + tm * tn * 2)
    vmem += tm * D_MODEL * 2 + tm * D_MODEL * 4 + 8 * tm * tn * 4
    return pl.pallas_call(
        _merge_kernel,
        grid=(t // tm, nj),
        in_specs=[rowfull(D_MODEL), pl.BlockSpec((1, D_MODEL), lambda i, j: (0, 0)),
                  gate_w(0), gate_w(1), gate_w(2), gate_b(0), gate_b(1), gate_b(2),
                  pl.BlockSpec((tm, POOL_GW), lambda i, j: (i, j)),
                  pl.BlockSpec((None, POOL_GW, tn), lambda i, j: (j, 0, 0)),
                  pl.BlockSpec((1, tn), lambda i, j: (0, j)),
                  rowfull(D_MODEL), col(D_MODEL), rowfull(MEM_W), col(MEM_W)],
        out_specs=pl.BlockSpec((tm, tn), lambda i, j: (i, j)),
        out_shape=jax.ShapeDtypeStruct((t, D_MODEL), BF16),
        scratch_shapes=[pltpu.VMEM((tm, D_MODEL), BF16)],
        compiler_params=_params(("parallel", "arbitrary"), vmem),
        name="gated_merge",
    )(x, gmix, w_in_t, w_in_t, w_in_t, bgate, bgate, bgate, d, wpool, pscale, o, wmla, am, wmem)


def _out_proj_kernel(x_ref, m_ref, w_ref, y_ref):
    y_ref[...] = x_ref[...] + _dot(m_ref[...], w_ref[...])


def _out_proj(x, merged, w, tm, tn):
    t = x.shape[0]
    vmem = 2 * (2 * tm * tn * 4 + tm * D_MODEL * 2 + D_MODEL * tn * 2) + 2 * tm * tn * 4
    return pl.pallas_call(
        _out_proj_kernel,
        grid=(t // tm, D_MODEL // tn),
        in_specs=[pl.BlockSpec((tm, tn), lambda i, j: (i, j)),
                  pl.BlockSpec((tm, D_MODEL), lambda i, j: (i, 0)),
                  pl.BlockSpec((D_MODEL, tn), lambda i, j: (0, j))],
        out_specs=pl.BlockSpec((tm, tn), lambda i, j: (i, j)),
        out_shape=jax.ShapeDtypeStruct((t, D_MODEL), F32),
        compiler_params=_params(("parallel", "parallel"), vmem),
        name="out_proj",
    )(x, merged, w)


def _ffn_kernel(x_ref, g_ref, wup_ref, wdown_ref, y_ref, h_sc, acc_sc):
    j = pl.program_id(1)

    @pl.when(j == 0)
    def _():
        h_sc[...] = _rms(x_ref[...], g_ref[...]).astype(BF16)
        acc_sc[...] = jnp.zeros(acc_sc.shape, F32)

    f = jnp.maximum(_dot(h_sc[...], wup_ref[...]), 0.0)
    acc_sc[...] += _dot((f * f).astype(BF16), wdown_ref[...])

    @pl.when(j == pl.num_programs(1) - 1)
    def _():
        y_ref[...] = x_ref[...] + acc_sc[...]


def _ffn(x, g, wup, wdown, tm, tf):
    t = x.shape[0]
    vmem = 2 * (2 * tm * D_MODEL * 4 + 2 * D_MODEL * tf * 2) + tm * D_MODEL * 6 + 3 * tm * tf * 4 + tm * D_MODEL * 4
    return pl.pallas_call(
        _ffn_kernel,
        grid=(t // tm, D_FF // tf),
        in_specs=[pl.BlockSpec((tm, D_MODEL), lambda i, j: (i, 0)),
                  pl.BlockSpec((1, D_MODEL), lambda i, j: (0, 0)),
                  pl.BlockSpec((D_MODEL, tf), lambda i, j: (0, j)),
                  pl.BlockSpec((tf, D_MODEL), lambda i, j: (j, 0))],
        out_specs=pl.BlockSpec((tm, D_MODEL), lambda i, j: (i, 0)),
        out_shape=jax.ShapeDtypeStruct((t, D_MODEL), F32),
        scratch_shapes=[pltpu.VMEM((tm, D_MODEL), BF16), pltpu.VMEM((tm, D_MODEL), F32)],
        compiler_params=_params(("parallel", "arbitrary"), vmem),
        name="ffn",
    )(x, g, wup, wdown)


def _rope_tables(pos):
    half = QK_ROPE // 2
    inv = 1.0 / (ROPE_THETA ** (jnp.arange(half, dtype=F32) * (2.0 / QK_ROPE)))
    ang = pos.astype(F32)[:, None] * inv[None, :]
    return jnp.cos(ang), jnp.sin(ang)


def _lane_tables(cos, sin):
    z = jnp.zeros_like(cos)
    c = jnp.concatenate([cos, cos, z, z], axis=1)
    sa = jnp.concatenate([z, sin, z, z], axis=1)
    sb = jnp.concatenate([-sin, z, z, z], axis=1)
    return c, sa, sb


def _query_table(cos, sin):
    return jnp.concatenate([cos, cos, -sin, sin], axis=1)


def _row(v):
    return v.astype(F32).reshape(1, -1)


def _pad_last(a, width):
    return jnp.pad(a, [(0, 0)] * (a.ndim - 1) + [(0, width - a.shape[-1])])


def kernel(x_prompt, mem_prompt, x_sample, cache_mla_latent, cache_mla_kpe, state_pool, cache_mem_k, cache_mem_v,
           g_mix, w_in, b_gate, w_pool, pool_scale, g_q_lat, w_qb, g_q_head, g_kv_lat, w_kb, w_vb, g_k_head,
           w_mla_o, g_mem, w_mem_kv, g_mem_q, g_mem_k, w_mem_o, w_out, g_ff, w_up, w_down):
    batch, seq, _ = x_prompt.shape
    dec_batch, dec_seq, _ = x_sample.shape
    past = cache_mla_latent.shape[1]

    w_in_t = w_in.T.astype(BF16)
    bgate = _row(b_gate)
    half = QK_ROPE // 2

    def swap_halves(a):
        return jnp.concatenate([a[..., half:], a[..., :half]], axis=-1)

    def head_layout(a):
        rope = a[..., QK_NOPE:]
        return jnp.concatenate([a, swap_halves(rope)], axis=-1)

    wq = head_layout(w_qb.reshape(Q_LORA, MLA_HEADS, QK_HEAD)).reshape(Q_LORA, MLA_HEADS * HEAD_PAD).astype(BF16)
    wkb = w_kb.astype(BF16)
    wvb = w_vb.astype(BF16)
    wvbt = wvb.T
    gq_p = _row(head_layout(g_q_head.astype(F32) * MLA_SCALE_LOG2))
    gkn = _row(g_k_head[:QK_NOPE])
    gkr = _row(_pad_last(g_k_head[QK_NOPE:], 128))
    gq_s = _row(head_layout(jnp.concatenate([g_q_head[:QK_NOPE].astype(F32) * g_k_head[:QK_NOPE].astype(F32),
                                             g_q_head[QK_NOPE:].astype(F32)]) * MLA_SCALE_LOG2))
    gmq = _row(g_mem_q.astype(F32) * MEM_SCALE)
    wpool = w_pool.astype(BF16)
    wmla = w_mla_o.astype(BF16)
    wmem_o = w_mem_o.astype(BF16)
    wmem_kv = w_mem_kv.astype(BF16)
    wout = w_out.astype(BF16)
    wup = w_up.astype(BF16)
    wdown = w_down.astype(BF16)
    gmix, gql, gkvl, gff, gmem, gmk, pscale = map(_row, (g_mix, g_q_lat, g_kv_lat, g_ff, g_mem, g_mem_k, pool_scale))

    def dense_tail(x, d, o, am, tm):
        merged = _merge(x, gmix, w_in_t, bgate, d, wpool, pscale, o, wmla, am, wmem_o, tm)
        x1 = _out_proj(x, merged, wout, tm, 1024)
        return _ffn(x1, gff, wup, wdown, tm, 1024)

    tm = 512
    xp = x_prompt.reshape(batch * seq, D_MODEL)
    mem_kv = _mem_kv(mem_prompt.reshape(batch * N_MEM, D_MODEL), gmem, wmem_kv, gmk)
    u, hq, ckv, kpe, qm = _in_proj(xp, gmix, w_in_t, gql, gkvl, gmq, tm)
    cos_p, sin_p = _rope_tables(jnp.arange(seq, dtype=jnp.int32))
    q, k, vt = _qkv(hq, ckv, kpe, wq, wkb, wvbt, gq_p, gkn, gkr,
                    (_query_table(cos_p, sin_p),) + _lane_tables(cos_p, sin_p), tm, seq // tm)
    o = _flash(q, k, vt, batch, seq, 2048)
    am = _mem_attn(qm, mem_kv, mem_kv, 0, 1, tm, seq // tm)
    d = _pool(u, jnp.zeros((batch, POOL_HALO, POOL_W), F32), tm, seq // tm, 0)
    y_p = dense_tail(xp, d, o, am, tm).reshape(batch, seq, D_MODEL)
    lat_p = ckv.reshape(batch, seq, KV_LORA)
    kpe_p = kpe[:, :QK_ROPE].reshape(batch, seq, QK_ROPE)
    pool_p = u.reshape(batch, seq, POOL_W)[:, seq - POOL_STATE:]
    mem_k_p = mem_kv[:, :MEM_W].reshape(batch, N_MEM, MEM_HEADS, MEM_HEAD_DIM)
    mem_v_p = mem_kv[:, MEM_W:].reshape(batch, N_MEM, MEM_HEADS, MEM_HEAD_DIM)

    ts = dec_batch * dec_seq
    xs = x_sample.reshape(ts, D_MODEL)
    u_s, hq_s, ckv_s, kpe_s, qm_s = _in_proj(xs, gmix, w_in_t, gql, gkvl, gmq, ts)
    pos_s = past + jnp.arange(dec_seq, dtype=jnp.int32)
    cos_s, sin_s = _rope_tables(pos_s)
    q_s = _q_only(hq_s, wq, gq_s, jnp.tile(_query_table(cos_s, sin_s), (dec_batch, 1)), ts)
    qa, qr = _q_absorb(q_s, wkb)
    cos_c, sin_c = _rope_tables(jnp.arange(past, dtype=jnp.int32))
    pad_new = 128 - dec_seq
    nlat = jnp.pad(ckv_s.reshape(dec_batch, dec_seq, KV_LORA), ((0, 0), (0, pad_new), (0, 0)))
    nkpet = jnp.pad(jnp.swapaxes(kpe_s[:, :QK_ROPE].reshape(dec_batch, dec_seq, QK_ROPE), 1, 2),
                    ((0, 0), (0, 0), (0, pad_new)))
    o_lat = _cache_attn(wkb.T, qa, qr, cache_mla_latent, jnp.swapaxes(cache_mla_kpe, 1, 2),
                        cos_c.T, sin_c.T, nlat, nkpet,
                        jnp.pad(cos_s.T, ((0, 0), (0, pad_new))), jnp.pad(sin_s.T, ((0, 0), (0, pad_new))),
                        g_k_head[QK_NOPE:].astype(F32).reshape(QK_ROPE, 1),
                        dec_seq, dec_seq, 1024)
    o_s = _v_up(o_lat, wvb, dec_seq)
    am_s = _mem_attn(qm_s, cache_mem_k.reshape(dec_batch * N_MEM, MEM_W), cache_mem_v.reshape(dec_batch * N_MEM, MEM_W),
                     0, 0, dec_seq, 1)
    left_s = jnp.pad(state_pool.astype(F32), ((0, 0), (POOL_HALO - POOL_STATE, 0), (0, 0)))
    d_s = _pool(u_s, left_s, dec_seq, 1, past)
    y_s = dense_tail(xs, d_s, o_s, am_s, ts).reshape(dec_batch, dec_seq, D_MODEL)
    lat_s = ckv_s.reshape(dec_batch, dec_seq, KV_LORA)
    kpe_s_out = kpe_s[:, :QK_ROPE].reshape(dec_batch, dec_seq, QK_ROPE)
    pool_s = u_s.reshape(dec_batch, dec_seq, POOL_W)[:, dec_seq - POOL_STATE:]

    return (y_p, y_s, lat_p, kpe_p, pool_p, mem_k_p, mem_v_p, lat_s, kpe_s_out, pool_s)
```

```python
import functools

import jax
import jax.numpy as jnp
from jax import lax
from jax.experimental import pallas as pl
from jax.experimental.pallas import tpu as pltpu

F32 = jnp.float32
BF16 = jnp.bfloat16

EPS = 1e-6
CHUNK = 64
D_MODEL = 2048
POOL_WINDOWS = (2, 4, 8, 16)
POOL_W = 1024
POOL_GW = 256
POOL_OUT_GW = 512
POOL_STATE = 15
POOL_HALO = 16
assert all(w & (w - 1) == 0 and w <= POOL_HALO for w in POOL_WINDOWS)
MLA_HEADS = 16
Q_LORA = 512
KV_LORA = 512
QK_NOPE = 128
QK_ROPE = 64
QK_HEAD = QK_NOPE + QK_ROPE
HEAD_PAD = 256
V_HEAD = 128
V_ROWS = 144
ROPE_THETA = 10000.0
LOG2_E = 1.4426950408889634
MLA_SCALE = QK_HEAD ** -0.5
MLA_SCALE_LOG2 = MLA_SCALE * LOG2_E
N_MEM = 256
MEM_HEADS = 4
MEM_HEAD_DIM = 256
MEM_W = MEM_HEADS * MEM_HEAD_DIM
MEM_SCALE = MEM_HEAD_DIM ** -0.5
D_FF = 4 * D_MODEL
NEG_INF = -1e30

Z_U = 0
Z_QLAT = Z_U + POOL_W
Z_KVLAT = Z_QLAT + Q_LORA
Z_MQ = Z_KVLAT + KV_LORA
W_IN_KPE = Z_MQ
W_IN_MQ = W_IN_KPE + QK_ROPE
W_IN_GATE = W_IN_MQ + MEM_W

BF16_SUBLANES = 16
V7X_VMEM_BYTES = 64 * 1024 * 1024
VMEM_CEILING = V7X_VMEM_BYTES - 6 * 1024 * 1024
VMEM_FLOOR = VMEM_CEILING - 4 * 1024 * 1024


def _params(semantics, vmem_bytes):
    return pltpu.CompilerParams(dimension_semantics=semantics,
                                vmem_limit_bytes=int(min(max(vmem_bytes, VMEM_FLOOR), VMEM_CEILING)))


def _rms(x, g):
    return x * lax.rsqrt(jnp.mean(x * x, axis=-1, keepdims=True) + EPS) * g


def _dot(a, b):
    return jnp.dot(a, b, preferred_element_type=F32)


def _dot_nt(a, b):
    return lax.dot_general(a, b, (((1,), (1,)), ((), ())), preferred_element_type=F32)


def _rope128(x, c, sa, sb):
    return x * c + pltpu.roll(x, 32, 1) * sa + pltpu.roll(x, 96, 1) * sb


def _in_proj_kernel(x_ref, gmix_ref, wa_ref, wkpe_ref, wmq_ref, gq_ref, gkv_ref, gmq_ref,
                    u_ref, hq_ref, ckv_ref, kpe_ref, qm_ref):
    h = _rms(x_ref[...], gmix_ref[...]).astype(BF16)
    u_ref[...] = _dot_nt(h, wa_ref[Z_U:Z_QLAT, :])
    hq_ref[...] = _rms(_dot_nt(h, wa_ref[Z_QLAT:Z_KVLAT, :]), gq_ref[...]).astype(BF16)
    ckv_ref[...] = _rms(_dot_nt(h, wa_ref[Z_KVLAT:Z_MQ, :]), gkv_ref[...])
    kpe_ref[:, :QK_ROPE] = _dot_nt(h, wkpe_ref[...])
    kpe_ref[:, QK_ROPE:] = jnp.zeros((h.shape[0], 128 - QK_ROPE), F32)
    for hh in range(MEM_HEADS):
        lo = hh * MEM_HEAD_DIM
        m = _dot_nt(h, wmq_ref[lo:lo + MEM_HEAD_DIM, :])
        qm_ref[:, lo:lo + MEM_HEAD_DIM] = _rms(m, gmq_ref[...]).astype(BF16)


def _w_in_rows(n, start):
    return pl.BlockSpec((pl.Element(n), pl.Element(D_MODEL)),
                        lambda i: (pl.multiple_of(start + 0 * i, BF16_SUBLANES), 0))


def _front_kernel(x_ref, gmix_ref, wa_ref, wkpe_ref, wmq_ref, gq_ref, gkv_ref, gmq_ref, left_ref, mk_ref, mv_ref,
                  d_ref, hq_ref, ckv_ref, kpe_ref, am_ref, utail_ref, xp_sc, qm_sc, *, tm, blocks_per_seq, pos0):
    sb = pl.program_id(0) % blocks_per_seq
    h = _rms(x_ref[...], gmix_ref[...]).astype(BF16)

    @pl.when(sb == 0)
    def _():
        xp_sc[:POOL_HALO, :] = left_ref[...]

    @pl.when(sb != 0)
    def _():
        xp_sc[:POOL_HALO, :] = xp_sc[tm:, :]

    xp_sc[POOL_HALO:, :] = _dot_nt(h, wa_ref[Z_U:Z_QLAT, :])
    utail_ref[...] = xp_sc[tm:, :]
    hq_ref[...] = _rms(_dot_nt(h, wa_ref[Z_QLAT:Z_KVLAT, :]), gq_ref[...]).astype(BF16)
    ckv_ref[...] = _rms(_dot_nt(h, wa_ref[Z_KVLAT:Z_MQ, :]), gkv_ref[...])
    kpe_ref[:, :QK_ROPE] = _dot_nt(h, wkpe_ref[...])
    kpe_ref[:, QK_ROPE:] = jnp.zeros((tm, 128 - QK_ROPE), F32)
    for hh in range(MEM_HEADS):
        lo = hh * MEM_HEAD_DIM
        m = _dot_nt(h, wmq_ref[lo:lo + MEM_HEAD_DIM, :])
        qm_sc[:, lo:lo + MEM_HEAD_DIM] = _rms(m, gmq_ref[...]).astype(BF16)
    _pool_windows(xp_sc, pos0 + sb * tm, tm, d_ref)
    _mem_attn_kernel(qm_sc, mk_ref, mv_ref, am_ref)


def _front(x, gmix, w_in_t, gq, gkv, gmq_scaled, left, mem_kv, tm, blocks_per_seq, pos0):
    t = x.shape[0]
    n_seq = t // (tm * blocks_per_seq)
    row = lambda w: pl.BlockSpec((tm, w), lambda i: (i, 0))
    full = lambda a: pl.BlockSpec(a.shape, lambda i: (0, 0))
    w_elems = (Z_MQ + QK_ROPE + MEM_W) * D_MODEL
    vmem = 2 * (tm * D_MODEL * 4 + w_elems * 2 + tm * (POOL_W * 2 + Q_LORA * 2 + KV_LORA * 4 + 128 * 4 + MEM_W * 2)
                + 2 * N_MEM * MEM_W * 4 + 2 * POOL_HALO * POOL_W * 4)
    vmem += (tm + POOL_HALO) * POOL_W * 4 + tm * MEM_W * 2 + 8 * tm * POOL_W * 4
    kern = functools.partial(_front_kernel, tm=tm, blocks_per_seq=blocks_per_seq, pos0=pos0)
    return pl.pallas_call(
        kern,
        grid=(t // tm,),
        in_specs=[row(D_MODEL), full(gmix), _w_in_rows(Z_MQ, 0), _w_in_rows(QK_ROPE, W_IN_KPE),
                  _w_in_rows(MEM_W, W_IN_MQ), full(gq), full(gkv), full(gmq_scaled),
                  pl.BlockSpec((None, POOL_HALO, POOL_W), lambda i: (i // blocks_per_seq, 0, 0)),
                  pl.BlockSpec((N_MEM, MEM_W), lambda i: (i // blocks_per_seq, 0)),
                  pl.BlockSpec((N_MEM, MEM_W), lambda i: (i // blocks_per_seq, 1))],
        out_specs=[row(POOL_W), row(Q_LORA), row(KV_LORA), row(128), row(MEM_W),
                   pl.BlockSpec((None, POOL_HALO, POOL_W), lambda i: (i // blocks_per_seq, 0, 0))],
        out_shape=[jax.ShapeDtypeStruct((t, POOL_W), BF16), jax.ShapeDtypeStruct((t, Q_LORA), BF16),
                   jax.ShapeDtypeStruct((t, KV_LORA), F32), jax.ShapeDtypeStruct((t, 128), F32),
                   jax.ShapeDtypeStruct((t, MEM_W), BF16), jax.ShapeDtypeStruct((n_seq, POOL_HALO, POOL_W), F32)],
        scratch_shapes=[pltpu.VMEM((tm + POOL_HALO, POOL_W), F32), pltpu.VMEM((tm, MEM_W), BF16)],
        compiler_params=_params(("arbitrary",), vmem),
        name="front",
    )(x, gmix, w_in_t, w_in_t, w_in_t, gq, gkv, gmq_scaled, left, mem_kv, mem_kv)


def _in_proj(x, gmix, w_in_t, gq, gkv, gmq_scaled, tm):
    t = x.shape[0]
    row = lambda w: pl.BlockSpec((tm, w), lambda i: (i, 0))
    full = lambda a: pl.BlockSpec(a.shape, lambda i: (0, 0))
    rows_at = _w_in_rows
    w_elems = (Z_MQ + QK_ROPE + MEM_W) * D_MODEL
    vmem = 2 * (tm * D_MODEL * 4 + w_elems * 2 + tm * (POOL_W * 4 + Q_LORA * 2 + KV_LORA * 4 + 128 * 4 + MEM_W * 2))
    vmem += 6 * tm * POOL_W * 4
    return pl.pallas_call(
        _in_proj_kernel,
        grid=(t // tm,),
        in_specs=[row(D_MODEL), full(gmix), rows_at(Z_MQ, 0), rows_at(QK_ROPE, W_IN_KPE), rows_at(MEM_W, W_IN_MQ),
                  full(gq), full(gkv), full(gmq_scaled)],
        out_specs=[row(POOL_W), row(Q_LORA), row(KV_LORA), row(128), row(MEM_W)],
        out_shape=[jax.ShapeDtypeStruct((t, POOL_W), F32), jax.ShapeDtypeStruct((t, Q_LORA), BF16),
                   jax.ShapeDtypeStruct((t, KV_LORA), F32), jax.ShapeDtypeStruct((t, 128), F32),
                   jax.ShapeDtypeStruct((t, MEM_W), BF16)],
        compiler_params=_params(("parallel",), vmem),
        name="in_proj",
    )(x, gmix, w_in_t, w_in_t, w_in_t, gq, gkv, gmq_scaled)


def _q_heads(hq, wq_ref, gq_ref, qtab, q_ref):
    first = lax.broadcasted_iota(jnp.int32, (1, 128), 1) < QK_ROPE
    for h in range(MLA_HEADS):
        lo = h * HEAD_PAD
        qh = _dot(hq, wq_ref[:, lo:lo + HEAD_PAD])
        qn, qr = qh[:, :QK_NOPE], qh[:, QK_NOPE:]
        ss = jnp.sum(qn * qn + jnp.where(first, qr * qr, 0.0), axis=-1, keepdims=True)
        r = lax.rsqrt(ss * (1.0 / QK_HEAD) + EPS)
        q_ref[:, lo:lo + QK_NOPE] = (qn * r * gq_ref[:, :QK_NOPE]).astype(BF16)
        q_ref[:, lo + QK_NOPE:lo + HEAD_PAD] = (qr * r * gq_ref[:, QK_NOPE:] * qtab).astype(BF16)


def _qkv_kernel(hq_ref, ckv_ref, kpe_ref, wq_ref, wkb_ref, wvbt_ref, gq_ref, gkn_ref, gkr_ref,
                qtab_ref, c_ref, sa_ref, sb_ref, q_ref, k_ref, vt_ref):
    _q_heads(hq_ref[...], wq_ref, gq_ref, qtab_ref[...], q_ref)
    ckv = ckv_ref[...].astype(BF16)
    kpe = kpe_ref[...]
    ss_pe = jnp.sum(kpe * kpe, axis=-1, keepdims=True)
    kr = _rope128(kpe * gkr_ref[...], c_ref[...], sa_ref[...], sb_ref[...])
    kr = kr + pltpu.roll(kr, QK_ROPE, 1)
    for p in range(MLA_HEADS // 2):
        kn2 = _dot(ckv, wkb_ref[:, p * 2 * QK_NOPE:(p + 1) * 2 * QK_NOPE])
        for s in range(2):
            lo = (2 * p + s) * HEAD_PAD
            kn = kn2[:, s * QK_NOPE:(s + 1) * QK_NOPE]
            r = lax.rsqrt((jnp.sum(kn * kn, axis=-1, keepdims=True) + ss_pe) * (1.0 / QK_HEAD) + EPS)
            k_ref[:, lo:lo + QK_NOPE] = (kn * r * gkn_ref[...]).astype(BF16)
            k_ref[:, lo + QK_NOPE:lo + HEAD_PAD] = (kr * r).astype(BF16)
    vt = _dot_nt(wvbt_ref[...], ckv)
    tm = vt.shape[1]
    ones_row = (lax.broadcasted_iota(jnp.int32, (V_ROWS - V_HEAD, tm), 0) == 0).astype(BF16)
    for h in range(MLA_HEADS):
        vt_ref[h, :V_HEAD, :] = vt[h * V_HEAD:(h + 1) * V_HEAD].astype(BF16)
        vt_ref[h, V_HEAD:, :] = ones_row


def _qkv(hq, ckv, kpe, wq, wkb, wvbt, gq, gkn, gkr, tabs, tm, seq_blocks):
    t = hq.shape[0]
    row = lambda w: pl.BlockSpec((tm, w), lambda i: (i, 0))
    full = lambda a: pl.BlockSpec(a.shape, lambda i: (0, 0))
    tab = pl.BlockSpec((tm, 128), lambda i: (i % seq_blocks, 0))
    vmem = 2 * (tm * (Q_LORA * 2 + KV_LORA * 4 + 128 * 4 + 4 * 128 * 4) + (wq.size + wkb.size + wvbt.size) * 2
                + tm * (2 * MLA_HEADS * HEAD_PAD * 2 + MLA_HEADS * V_ROWS * 2))
    vmem += 8 * tm * HEAD_PAD * 4 + 2 * tm * MLA_HEADS * V_HEAD * 4
    return pl.pallas_call(
        _qkv_kernel,
        grid=(t // tm,),
        in_specs=[row(Q_LORA), row(KV_LORA), row(128), full(wq), full(wkb), full(wvbt),
                  full(gq), full(gkn), full(gkr), tab, tab, tab, tab],
        out_specs=[row(MLA_HEADS * HEAD_PAD), row(MLA_HEADS * HEAD_PAD),
                   pl.BlockSpec((None, MLA_HEADS, None, V_ROWS, tm),
                                lambda i: (i // seq_blocks, 0, i % seq_blocks, 0, 0))],
        out_shape=[jax.ShapeDtypeStruct((t, MLA_HEADS * HEAD_PAD), BF16),
                   jax.ShapeDtypeStruct((t, MLA_HEADS * HEAD_PAD), BF16),
                   jax.ShapeDtypeStruct((t // (tm * seq_blocks), MLA_HEADS, seq_blocks, V_ROWS, tm), BF16)],
        compiler_params=_params(("parallel",), vmem),
        name="qkv_proj",
    )(hq, ckv, kpe, wq, wkb, wvbt, gq, gkn, gkr, *tabs)


def _q_only_kernel(hq_ref, wq_ref, gq_ref, qtab_ref, q_ref):
    _q_heads(hq_ref[...], wq_ref, gq_ref, qtab_ref[...], q_ref)


def _q_only(hq, wq, gq, qtab, tm):
    t = hq.shape[0]
    row = lambda w: pl.BlockSpec((tm, w), lambda i: (i, 0))
    full = lambda a: pl.BlockSpec(a.shape, lambda i: (0, 0))
    vmem = 2 * (tm * (Q_LORA * 2 + 128 * 4 + MLA_HEADS * HEAD_PAD * 2) + wq.size * 2) + 8 * tm * HEAD_PAD * 4
    return pl.pallas_call(
        _q_only_kernel,
        grid=(t // tm,),
        in_specs=[row(Q_LORA), full(wq), full(gq), pl.BlockSpec((tm, 128), lambda i: (0, 0))],
        out_specs=row(MLA_HEADS * HEAD_PAD),
        out_shape=jax.ShapeDtypeStruct((t, MLA_HEADS * HEAD_PAD), BF16),
        compiler_params=_params(("parallel",), vmem),
        name="q_proj",
    )(hq, wq, gq, qtab)


def _flash_kernel(q_ref, k_ref, vt_ref, o_ref, s0_sc, s1_sc, m_sc, acc_sc, *, tq, tk):
    qi = pl.program_id(2)
    m_sc[...] = jnp.full(m_sc.shape, NEG_INF, F32)
    acc_sc[...] = jnp.zeros(acc_sc.shape, F32)

    def scores(kj, s_sc, q_lo=0):
        start = pl.multiple_of(kj * tk, tk)
        s_sc[:, q_lo:] = _dot_nt(k_ref[pl.ds(start, tk), :], q_ref[q_lo:, :])

    def consume(kj, s_sc, key_chunk0=None, q_lo=0):
        s = s_sc[:, q_lo:]
        if key_chunk0 is not None:
            kc = key_chunk0 + lax.broadcasted_iota(jnp.int32, s.shape, 0) // CHUNK
            qc = (q_lo + lax.broadcasted_iota(jnp.int32, s.shape, 1)) // CHUNK
            s = jnp.where(qc >= kc, s, NEG_INF)
        m = m_sc[:, q_lo:]
        m_new = jnp.maximum(m, jnp.max(s, axis=0, keepdims=True))
        alpha = jnp.exp2(m - m_new)
        p = jnp.exp2(s - m_new).astype(BF16)
        acc_sc[:, q_lo:] = alpha * acc_sc[:, q_lo:] + _dot(vt_ref[kj], p)
        m_sc[:, q_lo:] = m_new

    scores(0, s0_sc)

    def pair(t, carry):
        scores(2 * t + 1, s1_sc)
        consume(2 * t, s0_sc)
        scores(2 * t + 2, s0_sc)
        consume(2 * t + 1, s1_sc)
        return carry

    n_diag = tq // tk
    lax.fori_loop(0, qi * (n_diag // 2), pair, 0)
    base = qi * n_diag
    bufs = (s0_sc, s1_sc)
    for d in range(n_diag):
        if d + 1 < n_diag:
            scores(base + d + 1, bufs[(d + 1) % 2], (d + 1) * tk)
        consume(base + d, bufs[d % 2], d * tk // CHUNK, d * tk)
    acc = acc_sc[...]
    o_ref[...] = (acc[:V_HEAD] / acc[V_HEAD:V_HEAD + 1]).T.astype(BF16)


def _flash(q, k, vt, batch, seq, tq):
    tk = vt.shape[-1]
    assert tq % (2 * tk) == 0 and tk % CHUNK == 0 and seq % tq == 0
    nq = seq // tq
    vmem = 2 * (tq * HEAD_PAD * 2 + seq * HEAD_PAD * 2 + seq * V_ROWS * 2 + tq * V_HEAD * 2)
    vmem += 2 * tk * tq * 4 + V_ROWS * tq * 4 + 4 * tk * tq * 4
    return pl.pallas_call(
        functools.partial(_flash_kernel, tq=tq, tk=tk),
        grid=(batch, MLA_HEADS, nq),
        in_specs=[pl.BlockSpec((tq, HEAD_PAD), lambda b, h, i: (b * nq + i, h)),
                  pl.BlockSpec((seq, HEAD_PAD), lambda b, h, i: (b, h)),
                  pl.BlockSpec((None, None, seq // tk, V_ROWS, tk), lambda b, h, i: (b, h, 0, 0, 0))],
        out_specs=pl.BlockSpec((tq, V_HEAD), lambda b, h, i: (b * nq + i, h)),
        out_shape=jax.ShapeDtypeStruct((batch * seq, MLA_HEADS * V_HEAD), BF16),
        scratch_shapes=[pltpu.VMEM((tk, tq), F32), pltpu.VMEM((tk, tq), F32),
                        pltpu.VMEM((1, tq), F32), pltpu.VMEM((V_ROWS, tq), F32)],
        compiler_params=_params(("parallel", "parallel", "arbitrary"), vmem),
        name="mla_flash",
    )(q, k, vt)


def _q_absorb_kernel(q_ref, wkb_ref, qa_ref, qr_ref):
    qa_ref[...] = _dot_nt(q_ref[:, :QK_NOPE], wkb_ref[...]).astype(BF16)
    qr_ref[...] = q_ref[:, QK_NOPE:]


def _q_absorb(q, wkb):
    t = q.shape[0]
    vmem = 2 * (t * HEAD_PAD * 2 + KV_LORA * QK_NOPE * 2 + t * KV_LORA * 2 + t * 128 * 2) + 2 * t * KV_LORA * 4
    return pl.pallas_call(
        _q_absorb_kernel,
        grid=(MLA_HEADS,),
        in_specs=[pl.BlockSpec((t, HEAD_PAD), lambda h: (0, h)),
                  pl.BlockSpec((KV_LORA, QK_NOPE), lambda h: (0, h))],
        out_specs=[pl.BlockSpec((None, t, KV_LORA), lambda h: (h, 0, 0)),
                   pl.BlockSpec((None, t, 128), lambda h: (h, 0, 0))],
        out_shape=[jax.ShapeDtypeStruct((MLA_HEADS, t, KV_LORA), BF16),
                   jax.ShapeDtypeStruct((MLA_HEADS, t, 128), BF16)],
        compiler_params=_params(("parallel",), vmem),
        name="q_absorb",
    )(q, wkb)


def _cache_attn_kernel(wkbt_ref, qa_ref, qr_ref, lat_ref, kpet_ref, ct_ref, st_ref,
                       nlat_ref, nkpet_ref, nct_ref, nst_ref, gkr_ref,
                       o_ref, s_sc, m_sc, l_sc, acc_sc, *, nq, n_new, past, tk):
    j = pl.program_id(1)
    last = pl.num_programs(1) - 1
    rows = MLA_HEADS * nq
    half = QK_ROPE // 2

    @pl.when(j == 0)
    def _():
        m_sc[...] = jnp.full(m_sc.shape, NEG_INF, F32)
        l_sc[...] = jnp.zeros(l_sc.shape, F32)
        acc_sc[...] = jnp.zeros(acc_sc.shape, F32)

    def block(lat, kpet, ct, st, width, visible):
        c = lat.astype(BF16)
        kt = _dot_nt(wkbt_ref[...], c)
        s = _dot_nt(qa_ref[...].reshape(rows, KV_LORA), c)
        ss_pe = jnp.sum(kpet * kpet, axis=0, keepdims=True)
        kr = kpet * gkr_ref[...]
        k1, k2 = kr[:half], kr[half:]
        o1, o2 = k1 * ct - k2 * st, k2 * ct + k1 * st
        krot = jnp.concatenate([o1, o2, o1, o2], axis=0).astype(BF16)
        s = s + _dot(qr_ref[...].reshape(rows, 128), krot)
        for h in range(MLA_HEADS):
            kh = kt[h * QK_NOPE:(h + 1) * QK_NOPE]
            r = lax.rsqrt((jnp.sum(kh * kh, axis=0, keepdims=True) + ss_pe) * (1.0 / QK_HEAD) + EPS)
            s_sc[h * nq:(h + 1) * nq, :width] = s[h * nq:(h + 1) * nq] * r
        s = s_sc[:, :width]
        if visible is not None:
            s = jnp.where(visible, s, NEG_INF)
        m = m_sc[...]
        m_new = jnp.maximum(m, jnp.max(s, axis=-1, keepdims=True))
        alpha = jnp.exp2(m - m_new)
        p = jnp.exp2(s - m_new)
        l_sc[...] = alpha * l_sc[...] + jnp.sum(p, axis=-1, keepdims=True)
        m_sc[...] = m_new
        acc_sc[...] = alpha * acc_sc[...] + _dot(p.astype(BF16), c)

    @pl.when(j < last)
    def _():
        block(lat_ref[...], kpet_ref[...], ct_ref[...], st_ref[...], tk, None)

    @pl.when(j == last)
    def _():
        kk = lax.broadcasted_iota(jnp.int32, (1, 128), 1)
        visible = kk < n_new
        if (past + n_new - 1) // CHUNK != past // CHUNK:
            qpos = past + lax.broadcasted_iota(jnp.int32, (rows, 1), 0) % nq
            visible = visible & (qpos // CHUNK >= (past + kk) // CHUNK)
        block(nlat_ref[...], nkpet_ref[...], nct_ref[...], nst_ref[...], 128, visible)
        o_ref[...] = (acc_sc[...] / l_sc[...]).astype(BF16)


def _cache_attn(wkbt, qa, qr, lat, kpet, ct, st, nlat, nkpet, nct, nst, gkr, nq, n_new, tk):
    batch, past, _ = lat.shape
    assert past % tk == 0 and past >= tk
    nkb = past // tk
    rows = MLA_HEADS * nq
    cj = lambda j: jnp.minimum(j, nkb - 1)
    kern = functools.partial(_cache_attn_kernel, nq=nq, n_new=n_new, past=past, tk=tk)
    vmem = 2 * (wkbt.size * 2 + rows * (KV_LORA + 128) * 2 + tk * KV_LORA * 4 + 3 * 64 * tk * 4 + rows * KV_LORA * 2)
    vmem += (MLA_HEADS * QK_NOPE + 4 * rows) * tk * 4 + tk * KV_LORA * 2 + rows * KV_LORA * 8
    return pl.pallas_call(
        kern,
        grid=(batch, nkb + 1),
        in_specs=[pl.BlockSpec(wkbt.shape, lambda b, j: (0, 0)),
                  pl.BlockSpec((MLA_HEADS, nq, KV_LORA), lambda b, j: (0, b, 0)),
                  pl.BlockSpec((MLA_HEADS, nq, 128), lambda b, j: (0, b, 0)),
                  pl.BlockSpec((None, tk, KV_LORA), lambda b, j: (b, cj(j), 0)),
                  pl.BlockSpec((None, QK_ROPE, tk), lambda b, j: (b, 0, cj(j))),
                  pl.BlockSpec((QK_ROPE // 2, tk), lambda b, j: (0, cj(j))),
                  pl.BlockSpec((QK_ROPE // 2, tk), lambda b, j: (0, cj(j))),
                  pl.BlockSpec((None, 128, KV_LORA), lambda b, j: (b, 0, 0)),
                  pl.BlockSpec((None, QK_ROPE, 128), lambda b, j: (b, 0, 0)),
                  pl.BlockSpec((QK_ROPE // 2, 128), lambda b, j: (0, 0)),
                  pl.BlockSpec((QK_ROPE // 2, 128), lambda b, j: (0, 0)),
                  pl.BlockSpec((QK_ROPE, 1), lambda b, j: (0, 0))],
        out_specs=pl.BlockSpec((None, rows, KV_LORA), lambda b, j: (b, 0, 0)),
        out_shape=jax.ShapeDtypeStruct((batch, rows, KV_LORA), BF16),
        scratch_shapes=[pltpu.VMEM((rows, tk), F32), pltpu.VMEM((rows, 1), F32),
                        pltpu.VMEM((rows, 1), F32), pltpu.VMEM((rows, KV_LORA), F32)],
        compiler_params=_params(("parallel", "arbitrary"), vmem),
        name="mla_cache_attn",
    )(wkbt, qa, qr, lat, kpet, ct, st, nlat, nkpet, nct, nst, gkr)


def _v_up_kernel(ol_ref, wvb_ref, o_ref):
    b, nq, _ = ol_ref.shape
    o_ref[...] = _dot(ol_ref[...].reshape(b * nq, KV_LORA), wvb_ref[...]).astype(BF16)


def _v_up(o_lat, wvb, nq):
    batch = o_lat.shape[0]
    t = batch * nq
    vmem = 2 * (t * KV_LORA * 2 + KV_LORA * V_HEAD * 2 + t * V_HEAD * 2) + 2 * t * V_HEAD * 4
    return pl.pallas_call(
        _v_up_kernel,
        grid=(MLA_HEADS,),
        in_specs=[pl.BlockSpec((batch, nq, KV_LORA), lambda h: (0, h, 0)),
                  pl.BlockSpec((KV_LORA, V_HEAD), lambda h: (0, h))],
        out_specs=pl.BlockSpec((t, V_HEAD), lambda h: (0, h)),
        out_shape=jax.ShapeDtypeStruct((t, MLA_HEADS * V_HEAD), BF16),
        compiler_params=_params(("parallel",), vmem),
        name="v_up",
    )(o_lat, wvb)


def _mem_kv_kernel(mem_ref, gmem_ref, w_ref, gk_ref, kv_ref):
    j = pl.program_id(0)
    h = _rms(mem_ref[...], gmem_ref[...]).astype(BF16)

    @pl.when(j == 0)
    def _():
        for hh in range(MEM_HEADS):
            lo = hh * MEM_HEAD_DIM
            kv_ref[:, lo:lo + MEM_HEAD_DIM] = _rms(_dot(h, w_ref[:, lo:lo + MEM_HEAD_DIM]), gk_ref[...])

    @pl.when(j == 1)
    def _():
        kv_ref[...] = _dot(h, w_ref[...])


def _mem_kv(mem, gmem, w, gk):
    t = mem.shape[0]
    vmem = 2 * (t * D_MODEL * 4 + D_MODEL * MEM_W * 2 + t * MEM_W * 4) + t * D_MODEL * 6 + t * MEM_W * 4
    return pl.pallas_call(
        _mem_kv_kernel,
        grid=(2,),
        in_specs=[pl.BlockSpec((t, D_MODEL), lambda j: (0, 0)), pl.BlockSpec((1, D_MODEL), lambda j: (0, 0)),
                  pl.BlockSpec((D_MODEL, MEM_W), lambda j: (0, j)), pl.BlockSpec((1, MEM_HEAD_DIM), lambda j: (0, 0))],
        out_specs=pl.BlockSpec((t, MEM_W), lambda j: (0, j)),
        out_shape=jax.ShapeDtypeStruct((t, 2 * MEM_W), F32),
        compiler_params=_params(("parallel",), vmem),
        name="memory_kv",
    )(mem, gmem, w, gk)


def _mem_attn_kernel(qm_ref, k_ref, v_ref, o_ref):
    for hh in range(MEM_HEADS):
        sl = slice(hh * MEM_HEAD_DIM, (hh + 1) * MEM_HEAD_DIM)
        s = _dot_nt(qm_ref[:, sl], k_ref[:, sl].astype(BF16))
        p = jnp.exp(s - jnp.max(s, axis=-1, keepdims=True))
        l = jnp.sum(p, axis=-1, keepdims=True)
        o_ref[:, sl] = (_dot(p.astype(BF16), v_ref[:, sl].astype(BF16)) / l).astype(BF16)


def _mem_attn(qm, k_arr, v_arr, k_col, v_col, tm, blocks_per_batch):
    t = qm.shape[0]
    vmem = 2 * (2 * tm * MEM_W * 2 + 2 * N_MEM * MEM_W * 4) + 6 * tm * N_MEM * 4 + 2 * N_MEM * MEM_W * 2
    return pl.pallas_call(
        _mem_attn_kernel,
        grid=(t // tm,),
        in_specs=[pl.BlockSpec((tm, MEM_W), lambda i: (i, 0)),
                  pl.BlockSpec((N_MEM, MEM_W), lambda i: (i // blocks_per_batch, k_col)),
                  pl.BlockSpec((N_MEM, MEM_W), lambda i: (i // blocks_per_batch, v_col))],
        out_specs=pl.BlockSpec((tm, MEM_W), lambda i: (i, 0)),
        out_shape=jax.ShapeDtypeStruct((t, MEM_W), BF16),
        compiler_params=_params(("parallel",), vmem),
        name="memory_attn",
    )(qm, k_arr, v_arr)


def _pool_windows(xp_sc, first_pos, tm, d_ref):
    pos = first_pos + lax.broadcasted_iota(jnp.int32, (tm, 1), 0)
    for g, win in enumerate(POOL_WINDOWS):
        sl = slice(g * POOL_GW, (g + 1) * POOL_GW)
        x = xp_sc[:, sl]
        wsum, k = x, 1
        while k < win:
            wsum = wsum + pltpu.roll(wsum, k, 0)
            k *= 2
        cnt = jnp.minimum(pos + 1, win).astype(F32)
        d_ref[:, sl] = (wsum[POOL_HALO:] / cnt - x[POOL_HALO:]).astype(BF16)


def _pool_kernel(u_ref, prev_ref, left_ref, d_ref, xp_sc, *, tm, blocks_per_seq, pos0):
    sb = pl.program_id(0) % blocks_per_seq
    xp_sc[:POOL_HALO, :] = jnp.where(sb == 0, left_ref[...], prev_ref[...])
    xp_sc[POOL_HALO:, :] = u_ref[...]
    _pool_windows(xp_sc, pos0 + sb * tm, tm, d_ref)


def _pool(u, left, tm, blocks_per_seq, pos0):
    t = u.shape[0]
    halo_per_blk = tm // POOL_HALO
    kern = functools.partial(_pool_kernel, tm=tm, blocks_per_seq=blocks_per_seq, pos0=pos0)
    vmem = 2 * (tm * POOL_W * 4 + 2 * POOL_HALO * POOL_W * 4 + tm * POOL_W * 2) + (tm + POOL_HALO) * POOL_W * 4
    vmem += 6 * tm * POOL_GW * 4
    return pl.pallas_call(
        kern,
        grid=(t // tm,),
        in_specs=[pl.BlockSpec((tm, POOL_W), lambda i: (i, 0)),
                  pl.BlockSpec((POOL_HALO, POOL_W), lambda i: (jnp.maximum(i * halo_per_blk - 1, 0), 0)),
                  pl.BlockSpec((None, POOL_HALO, POOL_W), lambda i: (i // blocks_per_seq, 0, 0))],
        out_specs=pl.BlockSpec((tm, POOL_W), lambda i: (i, 0)),
        out_shape=jax.ShapeDtypeStruct((t, POOL_W), BF16),
        scratch_shapes=[pltpu.VMEM((tm + POOL_HALO, POOL_W), F32)],
        compiler_params=_params(("parallel",), vmem),
        name="pool_windows",
    )(u, u, left)


def _merge_kernel(x_ref, gmix_ref, wg0_ref, wg1_ref, wg2_ref, bg0_ref, bg1_ref, bg2_ref,
                  d_ref, wpool_ref, pscale_ref, o_ref, wmla_ref, am_ref, wmem_ref,
                  out_ref, h_sc):
    @pl.when(pl.program_id(1) == 0)
    def _():
        h_sc[...] = _rms(x_ref[...], gmix_ref[...]).astype(BF16)

    h = h_sc[...]
    y_pool = _dot(d_ref[...], wpool_ref[...]) * pscale_ref[...]
    merged = jax.nn.sigmoid(_dot_nt(h, wg0_ref[...]) + bg0_ref[...]) * y_pool
    merged += jax.nn.sigmoid(_dot_nt(h, wg1_ref[...]) + bg1_ref[...]) * _dot(o_ref[...], wmla_ref[...])
    merged += jax.nn.sigmoid(_dot_nt(h, wg2_ref[...]) + bg2_ref[...]) * _dot(am_ref[...], wmem_ref[...])
    out_ref[...] = merged.astype(BF16)


def _merge(x, gmix, w_in_t, bgate, d, wpool, pscale, o, wmla, am, wmem, tm):
    t = x.shape[0]
    tn = POOL_OUT_GW
    nj = D_MODEL // tn
    rowfull = lambda w: pl.BlockSpec((tm, w), lambda i, j: (i, 0))
    gate_w = lambda b: pl.BlockSpec((pl.Element(tn), pl.Element(D_MODEL)),
                                    lambda i, j: (pl.multiple_of(W_IN_GATE + (b * nj + j) * tn, BF16_SUBLANES), 0))
    gate_b = lambda b: pl.BlockSpec((1, tn), lambda i, j: (0, b * nj + j))
    col = lambda k: pl.BlockSpec((k, tn), lambda i, j: (0, j))
    vmem = 2 * (tm * D_MODEL * 4 + 3 * D_MODEL * tn * 2 + tm * POOL_GW * 2 + POOL_GW * tn * 2 + tm * D_MODEL * 2
                + D_MODEL * tn * 2 + tm * MEM_W * 2 + MEM_W * tn * 2 + tm * tn * 2)
    vmem += tm * D_MODEL * 2 + tm * D_MODEL * 4 + 8 * tm * tn * 4
    return pl.pallas_call(
        _merge_kernel,
        grid=(t // tm, nj),
        in_specs=[rowfull(D_MODEL), pl.BlockSpec((1, D_MODEL), lambda i, j: (0, 0)),
                  gate_w(0), gate_w(1), gate_w(2), gate_b(0), gate_b(1), gate_b(2),
                  pl.BlockSpec((tm, POOL_GW), lambda i, j: (i, j)),
                  pl.BlockSpec((None, POOL_GW, tn), lambda i, j: (j, 0, 0)),
                  pl.BlockSpec((1, tn), lambda i, j: (0, j)),
                  rowfull(D_MODEL), col(D_MODEL), rowfull(MEM_W), col(MEM_W)],
        out_specs=pl.BlockSpec((tm, tn), lambda i, j: (i, j)),
        out_shape=jax.ShapeDtypeStruct((t, D_MODEL), BF16),
        scratch_shapes=[pltpu.VMEM((tm, D_MODEL), BF16)],
        compiler_params=_params(("parallel", "arbitrary"), vmem),
        name="gated_merge",
    )(x, gmix, w_in_t, w_in_t, w_in_t, bgate, bgate, bgate, d, wpool, pscale, o, wmla, am, wmem)


def _out_proj_kernel(x_ref, m_ref, w_ref, y_ref):
    y_ref[...] = x_ref[...] + _dot(m_ref[...], w_ref[...])


def _out_proj(x, merged, w, tm, tn):
    t = x.shape[0]
    vmem = 2 * (2 * tm * tn * 4 + tm * D_MODEL * 2 + D_MODEL * tn * 2) + 2 * tm * tn * 4
    return pl.pallas_call(
        _out_proj_kernel,
        grid=(t // tm, D_MODEL // tn),
        in_specs=[pl.BlockSpec((tm, tn), lambda i, j: (i, j)),
                  pl.BlockSpec((tm, D_MODEL), lambda i, j: (i, 0)),
                  pl.BlockSpec((D_MODEL, tn), lambda i, j: (0, j))],
        out_specs=pl.BlockSpec((tm, tn), lambda i, j: (i, j)),
        out_shape=jax.ShapeDtypeStruct((t, D_MODEL), F32),
        compiler_params=_params(("parallel", "parallel"), vmem),
        name="out_proj",
    )(x, merged, w)


def _ffn_kernel(x_ref, g_ref, wup_ref, wdown_ref, y_ref, h_sc, acc_sc):
    j = pl.program_id(1)

    @pl.when(j == 0)
    def _():
        h_sc[...] = _rms(x_ref[...], g_ref[...]).astype(BF16)
        acc_sc[...] = jnp.zeros(acc_sc.shape, F32)

    f = jnp.maximum(_dot(h_sc[...], wup_ref[...]), 0.0)
    acc_sc[...] += _dot((f * f).astype(BF16), wdown_ref[...])

    @pl.when(j == pl.num_programs(1) - 1)
    def _():
        y_ref[...] = x_ref[...] + acc_sc[...]


def _ffn(x, g, wup, wdown, tm, tf):
    t = x.shape[0]
    vmem = 2 * (2 * tm * D_MODEL * 4 + 2 * D_MODEL * tf * 2) + tm * D_MODEL * 6 + 3 * tm * tf * 4 + tm * D_MODEL * 4
    return pl.pallas_call(
        _ffn_kernel,
        grid=(t // tm, D_FF // tf),
        in_specs=[pl.BlockSpec((tm, D_MODEL), lambda i, j: (i, 0)),
                  pl.BlockSpec((1, D_MODEL), lambda i, j: (0, 0)),
                  pl.BlockSpec((D_MODEL, tf), lambda i, j: (0, j)),
                  pl.BlockSpec((tf, D_MODEL), lambda i, j: (j, 0))],
        out_specs=pl.BlockSpec((tm, D_MODEL), lambda i, j: (i, 0)),
        out_shape=jax.ShapeDtypeStruct((t, D_MODEL), F32),
        scratch_shapes=[pltpu.VMEM((tm, D_MODEL), BF16), pltpu.VMEM((tm, D_MODEL), F32)],
        compiler_params=_params(("parallel", "arbitrary"), vmem),
        name="ffn",
    )(x, g, wup, wdown)


def _rope_tables(pos):
    half = QK_ROPE // 2
    inv = 1.0 / (ROPE_THETA ** (jnp.arange(half, dtype=F32) * (2.0 / QK_ROPE)))
    ang = pos.astype(F32)[:, None] * inv[None, :]
    return jnp.cos(ang), jnp.sin(ang)


def _lane_tables(cos, sin):
    z = jnp.zeros_like(cos)
    c = jnp.concatenate([cos, cos, z, z], axis=1)
    sa = jnp.concatenate([z, sin, z, z], axis=1)
    sb = jnp.concatenate([-sin, z, z, z], axis=1)
    return c, sa, sb


def _query_table(cos, sin):
    return jnp.concatenate([cos, cos, -sin, sin], axis=1)


def _row(v):
    return v.astype(F32).reshape(1, -1)


def _pad_last(a, width):
    return jnp.pad(a, [(0, 0)] * (a.ndim - 1) + [(0, width - a.shape[-1])])


def kernel(x_prompt, mem_prompt, x_sample, cache_mla_latent, cache_mla_kpe, state_pool, cache_mem_k, cache_mem_v,
           g_mix, w_in, b_gate, w_pool, pool_scale, g_q_lat, w_qb, g_q_head, g_kv_lat, w_kb, w_vb, g_k_head,
           w_mla_o, g_mem, w_mem_kv, g_mem_q, g_mem_k, w_mem_o, w_out, g_ff, w_up, w_down):
    batch, seq, _ = x_prompt.shape
    dec_batch, dec_seq, _ = x_sample.shape
    past = cache_mla_latent.shape[1]

    w_in_t = w_in.T.astype(BF16)
    bgate = _row(b_gate)
    half = QK_ROPE // 2

    def swap_halves(a):
        return jnp.concatenate([a[..., half:], a[..., :half]], axis=-1)

    def head_layout(a):
        rope = a[..., QK_NOPE:]
        return jnp.concatenate([a, swap_halves(rope)], axis=-1)

    wq = head_layout(w_qb.reshape(Q_LORA, MLA_HEADS, QK_HEAD)).reshape(Q_LORA, MLA_HEADS * HEAD_PAD).astype(BF16)
    wkb = w_kb.astype(BF16)
    wvb = w_vb.astype(BF16)
    wvbt = wvb.T
    gq_p = _row(head_layout(g_q_head.astype(F32) * MLA_SCALE_LOG2))
    gkn = _row(g_k_head[:QK_NOPE])
    gkr = _row(_pad_last(g_k_head[QK_NOPE:], 128))
    gq_s = _row(head_layout(jnp.concatenate([g_q_head[:QK_NOPE].astype(F32) * g_k_head[:QK_NOPE].astype(F32),
                                             g_q_head[QK_NOPE:].astype(F32)]) * MLA_SCALE_LOG2))
    gmq = _row(g_mem_q.astype(F32) * MEM_SCALE)
    wpool = w_pool.astype(BF16)
    wmla = w_mla_o.astype(BF16)
    wmem_o = w_mem_o.astype(BF16)
    wmem_kv = w_mem_kv.astype(BF16)
    wout = w_out.astype(BF16)
    wup = w_up.astype(BF16)
    wdown = w_down.astype(BF16)
    gmix, gql, gkvl, gff, gmem, gmk, pscale = map(_row, (g_mix, g_q_lat, g_kv_lat, g_ff, g_mem, g_mem_k, pool_scale))

    def dense_tail(x, d, o, am, tm):
        merged = _merge(x, gmix, w_in_t, bgate, d, wpool, pscale, o, wmla, am, wmem_o, tm)
        x1 = _out_proj(x, merged, wout, tm, 1024)
        return _ffn(x1, gff, wup, wdown, tm, 1024)

    tm = 512
    xp = x_prompt.reshape(batch * seq, D_MODEL)
    mem_kv = _mem_kv(mem_prompt.reshape(batch * N_MEM, D_MODEL), gmem, wmem_kv, gmk)
    d, hq, ckv, kpe, am, u_tail = _front(xp, gmix, w_in_t, gql, gkvl, gmq,
                                         jnp.zeros((batch, POOL_HALO, POOL_W), F32), mem_kv, tm, seq // tm, 0)
    cos_p, sin_p = _rope_tables(jnp.arange(seq, dtype=jnp.int32))
    q, k, vt = _qkv(hq, ckv, kpe, wq, wkb, wvbt, gq_p, gkn, gkr,
                    (_query_table(cos_p, sin_p),) + _lane_tables(cos_p, sin_p), tm, seq // tm)
    o = _flash(q, k, vt, batch, seq, 2048)
    y_p = dense_tail(xp, d, o, am, tm).reshape(batch, seq, D_MODEL)
    lat_p = ckv.reshape(batch, seq, KV_LORA)
    kpe_p = kpe[:, :QK_ROPE].reshape(batch, seq, QK_ROPE)
    pool_p = u_tail[:, POOL_HALO - POOL_STATE:]
    mem_k_p = mem_kv[:, :MEM_W].reshape(batch, N_MEM, MEM_HEADS, MEM_HEAD_DIM)
    mem_v_p = mem_kv[:, MEM_W:].reshape(batch, N_MEM, MEM_HEADS, MEM_HEAD_DIM)

    ts = dec_batch * dec_seq
    xs = x_sample.reshape(ts, D_MODEL)
    u_s, hq_s, ckv_s, kpe_s, qm_s = _in_proj(xs, gmix, w_in_t, gql, gkvl, gmq, ts)
    pos_s = past + jnp.arange(dec_seq, dtype=jnp.int32)
    cos_s, sin_s = _rope_tables(pos_s)
    q_s = _q_only(hq_s, wq, gq_s, jnp.tile(_query_table(cos_s, sin_s), (dec_batch, 1)), ts)
    qa, qr = _q_absorb(q_s, wkb)
    cos_c, sin_c = _rope_tables(jnp.arange(past, dtype=jnp.int32))
    pad_new = 128 - dec_seq
    nlat = jnp.pad(ckv_s.reshape(dec_batch, dec_seq, KV_LORA), ((0, 0), (0, pad_new), (0, 0)))
    nkpet = jnp.pad(jnp.swapaxes(kpe_s[:, :QK_ROPE].reshape(dec_batch, dec_seq, QK_ROPE), 1, 2),
                    ((0, 0), (0, 0), (0, pad_new)))
    o_lat = _cache_attn(wkb.T, qa, qr, cache_mla_latent, jnp.swapaxes(cache_mla_kpe, 1, 2),
                        cos_c.T, sin_c.T, nlat, nkpet,
                        jnp.pad(cos_s.T, ((0, 0), (0, pad_new))), jnp.pad(sin_s.T, ((0, 0), (0, pad_new))),
                        g_k_head[QK_NOPE:].astype(F32).reshape(QK_ROPE, 1),
                        dec_seq, dec_seq, 1024)
    o_s = _v_up(o_lat, wvb, dec_seq)
    am_s = _mem_attn(qm_s, cache_mem_k.reshape(dec_batch * N_MEM, MEM_W), cache_mem_v.reshape(dec_batch * N_MEM, MEM_W),
                     0, 0, dec_seq, 1)
    left_s = jnp.pad(state_pool.astype(F32), ((0, 0), (POOL_HALO - POOL_STATE, 0), (0, 0)))
    d_s = _pool(u_s, left_s, dec_seq, 1, past)
    y_s = dense_tail(xs, d_s, o_s, am_s, ts).reshape(dec_batch, dec_seq, D_MODEL)
    lat_s = ckv_s.reshape(dec_batch, dec_seq, KV_LORA)
    kpe_s_out = kpe_s[:, :QK_ROPE].reshape(dec_batch, dec_seq, QK_ROPE)
    pool_s = u_s.reshape(dec_batch, dec_seq, POOL_W)[:, dec_seq - POOL_STATE:]

    return (y_p, y_s, lat_p, kpe_p, pool_p, mem_k_p, mem_v_p, lat_s, kpe_s_out, pool_s)
```

```python
import functools

import jax
import jax.numpy as jnp
from jax import lax
from jax.experimental import pallas as pl
from jax.experimental.pallas import tpu as pltpu

F32 = jnp.float32
BF16 = jnp.bfloat16

EPS = 1e-6
CHUNK = 64
D_MODEL = 2048
POOL_WINDOWS = (2, 4, 8, 16)
POOL_W = 1024
POOL_GW = 256
POOL_OUT_GW = 512
POOL_STATE = 15
POOL_HALO = 16
assert all(w & (w - 1) == 0 and w <= POOL_HALO for w in POOL_WINDOWS)
MLA_HEADS = 16
Q_LORA = 512
KV_LORA = 512
QK_NOPE = 128
QK_ROPE = 64
QK_HEAD = QK_NOPE + QK_ROPE
HEAD_PAD = 256
V_HEAD = 128
V_ROWS = 144
ROPE_THETA = 10000.0
LOG2_E = 1.4426950408889634
MLA_SCALE = QK_HEAD ** -0.5
MLA_SCALE_LOG2 = MLA_SCALE * LOG2_E
N_MEM = 256
MEM_HEADS = 4
MEM_HEAD_DIM = 256
MEM_W = MEM_HEADS * MEM_HEAD_DIM
MEM_SCALE = MEM_HEAD_DIM ** -0.5
D_FF = 4 * D_MODEL
NEG_INF = -1e30

Z_U = 0
Z_QLAT = Z_U + POOL_W
Z_KVLAT = Z_QLAT + Q_LORA
Z_MQ = Z_KVLAT + KV_LORA
W_IN_KPE = Z_MQ
W_IN_MQ = W_IN_KPE + QK_ROPE
W_IN_GATE = W_IN_MQ + MEM_W

BF16_SUBLANES = 16
V7X_VMEM_BYTES = 64 * 1024 * 1024
VMEM_CEILING = V7X_VMEM_BYTES - 6 * 1024 * 1024


def _params(semantics, vmem_bytes):
    return pltpu.CompilerParams(dimension_semantics=semantics,
                                vmem_limit_bytes=int(min(vmem_bytes, VMEM_CEILING)))


def _as_bf16(w_ref, copy_ref):
    w = w_ref[...].astype(BF16)
    if copy_ref is not None:
        copy_ref[...] = w
    return w


def _rms(x, g):
    return x * lax.rsqrt(jnp.mean(x * x, axis=-1, keepdims=True) + EPS) * g


def _dot(a, b):
    return jnp.dot(a, b, preferred_element_type=F32)


def _dot_nt(a, b):
    return lax.dot_general(a, b, (((1,), (1,)), ((), ())), preferred_element_type=F32)


def _rope128(x, c, sa, sb):
    return x * c + pltpu.roll(x, 32, 1) * sa + pltpu.roll(x, 96, 1) * sb


def _in_proj_kernel(x_ref, gmix_ref, wa_ref, wkpe_ref, wmq_ref, gq_ref, gkv_ref, gmq_ref,
                    u_ref, hq_ref, ckv_ref, kpe_ref, qm_ref):
    h = _rms(x_ref[...], gmix_ref[...]).astype(BF16)
    u_ref[...] = _dot_nt(h, wa_ref[Z_U:Z_QLAT, :])
    hq_ref[...] = _rms(_dot_nt(h, wa_ref[Z_QLAT:Z_KVLAT, :]), gq_ref[...]).astype(BF16)
    ckv_ref[...] = _rms(_dot_nt(h, wa_ref[Z_KVLAT:Z_MQ, :]), gkv_ref[...])
    kpe_ref[:, :QK_ROPE] = _dot_nt(h, wkpe_ref[...])
    kpe_ref[:, QK_ROPE:] = jnp.zeros((h.shape[0], 128 - QK_ROPE), F32)
    for hh in range(MEM_HEADS):
        lo = hh * MEM_HEAD_DIM
        m = _dot_nt(h, wmq_ref[lo:lo + MEM_HEAD_DIM, :])
        qm_ref[:, lo:lo + MEM_HEAD_DIM] = _rms(m, gmq_ref[...]).astype(BF16)


def _w_in_rows(n, start):
    return pl.BlockSpec((pl.Element(n), pl.Element(D_MODEL)),
                        lambda i: (pl.multiple_of(start + 0 * i, BF16_SUBLANES), 0))


def _front_kernel(x_ref, gmix_ref, wa_ref, wkpe_ref, wmq_ref, gq_ref, gkv_ref, gmq_ref, left_ref, mk_ref, mv_ref,
                  d_ref, hq_ref, ckv_ref, kpe_ref, am_ref, utail_ref, xp_sc, qm_sc, *, tm, blocks_per_seq, pos0):
    sb = pl.program_id(0) % blocks_per_seq
    h = _rms(x_ref[...], gmix_ref[...]).astype(BF16)

    @pl.when(sb == 0)
    def _():
        xp_sc[:POOL_HALO, :] = left_ref[...]

    @pl.when(sb != 0)
    def _():
        xp_sc[:POOL_HALO, :] = xp_sc[tm:, :]

    xp_sc[POOL_HALO:, :] = _dot_nt(h, wa_ref[Z_U:Z_QLAT, :])
    utail_ref[...] = xp_sc[tm:, :]
    hq_ref[...] = _rms(_dot_nt(h, wa_ref[Z_QLAT:Z_KVLAT, :]), gq_ref[...]).astype(BF16)
    ckv_ref[...] = _rms(_dot_nt(h, wa_ref[Z_KVLAT:Z_MQ, :]), gkv_ref[...])
    kpe_ref[:, :QK_ROPE] = _dot_nt(h, wkpe_ref[...])
    kpe_ref[:, QK_ROPE:] = jnp.zeros((tm, 128 - QK_ROPE), F32)
    for hh in range(MEM_HEADS):
        lo = hh * MEM_HEAD_DIM
        m = _dot_nt(h, wmq_ref[lo:lo + MEM_HEAD_DIM, :])
        qm_sc[:, lo:lo + MEM_HEAD_DIM] = _rms(m, gmq_ref[...]).astype(BF16)
    _pool_windows(xp_sc, pos0 + sb * tm, tm, d_ref)
    _mem_attn_kernel(qm_sc, mk_ref, mv_ref, am_ref)


def _front(x, gmix, w_in_t, gq, gkv, gmq_scaled, left, mem_kv, tm, blocks_per_seq, pos0):
    t = x.shape[0]
    n_seq = t // (tm * blocks_per_seq)
    row = lambda w: pl.BlockSpec((tm, w), lambda i: (i, 0))
    full = lambda a: pl.BlockSpec(a.shape, lambda i: (0, 0))
    w_elems = (Z_MQ + QK_ROPE + MEM_W) * D_MODEL
    vmem = 2 * (tm * D_MODEL * 4 + w_elems * 2 + tm * (POOL_W * 2 + Q_LORA * 2 + KV_LORA * 4 + 128 * 4 + MEM_W * 2)
                + 2 * N_MEM * MEM_W * 4 + 2 * POOL_HALO * POOL_W * 4)
    vmem += (tm + POOL_HALO) * POOL_W * 4 + tm * MEM_W * 2 + 8 * tm * POOL_W * 4
    kern = functools.partial(_front_kernel, tm=tm, blocks_per_seq=blocks_per_seq, pos0=pos0)
    return pl.pallas_call(
        kern,
        grid=(t // tm,),
        in_specs=[row(D_MODEL), full(gmix), _w_in_rows(Z_MQ, 0), _w_in_rows(QK_ROPE, W_IN_KPE),
                  _w_in_rows(MEM_W, W_IN_MQ), full(gq), full(gkv), full(gmq_scaled),
                  pl.BlockSpec((None, POOL_HALO, POOL_W), lambda i: (i // blocks_per_seq, 0, 0)),
                  pl.BlockSpec((N_MEM, MEM_W), lambda i: (i // blocks_per_seq, 0)),
                  pl.BlockSpec((N_MEM, MEM_W), lambda i: (i // blocks_per_seq, 1))],
        out_specs=[row(POOL_W), row(Q_LORA), row(KV_LORA), row(128), row(MEM_W),
                   pl.BlockSpec((None, POOL_HALO, POOL_W), lambda i: (i // blocks_per_seq, 0, 0))],
        out_shape=[jax.ShapeDtypeStruct((t, POOL_W), BF16), jax.ShapeDtypeStruct((t, Q_LORA), BF16),
                   jax.ShapeDtypeStruct((t, KV_LORA), F32), jax.ShapeDtypeStruct((t, 128), F32),
                   jax.ShapeDtypeStruct((t, MEM_W), BF16), jax.ShapeDtypeStruct((n_seq, POOL_HALO, POOL_W), F32)],
        scratch_shapes=[pltpu.VMEM((tm + POOL_HALO, POOL_W), F32), pltpu.VMEM((tm, MEM_W), BF16)],
        compiler_params=_params(("arbitrary",), vmem),
        name="front",
    )(x, gmix, w_in_t, w_in_t, w_in_t, gq, gkv, gmq_scaled, left, mem_kv, mem_kv)


def _in_proj(x, gmix, w_in_t, gq, gkv, gmq_scaled, tm):
    t = x.shape[0]
    row = lambda w: pl.BlockSpec((tm, w), lambda i: (i, 0))
    full = lambda a: pl.BlockSpec(a.shape, lambda i: (0, 0))
    rows_at = _w_in_rows
    w_elems = (Z_MQ + QK_ROPE + MEM_W) * D_MODEL
    vmem = 2 * (tm * D_MODEL * 4 + w_elems * 2 + tm * (POOL_W * 4 + Q_LORA * 2 + KV_LORA * 4 + 128 * 4 + MEM_W * 2))
    vmem += 6 * tm * POOL_W * 4
    return pl.pallas_call(
        _in_proj_kernel,
        grid=(t // tm,),
        in_specs=[row(D_MODEL), full(gmix), rows_at(Z_MQ, 0), rows_at(QK_ROPE, W_IN_KPE), rows_at(MEM_W, W_IN_MQ),
                  full(gq), full(gkv), full(gmq_scaled)],
        out_specs=[row(POOL_W), row(Q_LORA), row(KV_LORA), row(128), row(MEM_W)],
        out_shape=[jax.ShapeDtypeStruct((t, POOL_W), F32), jax.ShapeDtypeStruct((t, Q_LORA), BF16),
                   jax.ShapeDtypeStruct((t, KV_LORA), F32), jax.ShapeDtypeStruct((t, 128), F32),
                   jax.ShapeDtypeStruct((t, MEM_W), BF16)],
        compiler_params=_params(("parallel",), vmem),
        name="in_proj",
    )(x, gmix, w_in_t, w_in_t, w_in_t, gq, gkv, gmq_scaled)


def _q_heads(hq, wq_ref, gq_ref, qtab, q_ref):
    first = lax.broadcasted_iota(jnp.int32, (1, 128), 1) < QK_ROPE
    for h in range(MLA_HEADS):
        lo = h * HEAD_PAD
        qh = _dot(hq, wq_ref[:, lo:lo + HEAD_PAD])
        qn, qr = qh[:, :QK_NOPE], qh[:, QK_NOPE:]
        ss = jnp.sum(qn * qn + jnp.where(first, qr * qr, 0.0), axis=-1, keepdims=True)
        r = lax.rsqrt(ss * (1.0 / QK_HEAD) + EPS)
        q_ref[:, lo:lo + QK_NOPE] = (qn * r * gq_ref[:, :QK_NOPE]).astype(BF16)
        q_ref[:, lo + QK_NOPE:lo + HEAD_PAD] = (qr * r * gq_ref[:, QK_NOPE:] * qtab).astype(BF16)


def _qkv_kernel(hq_ref, ckv_ref, kpe_ref, wq_ref, wkb_ref, wvbt_ref, gq_ref, gkn_ref, gkr_ref,
                qtab_ref, c_ref, sa_ref, sb_ref, q_ref, k_ref, vt_ref):
    _q_heads(hq_ref[...], wq_ref, gq_ref, qtab_ref[...], q_ref)
    ckv = ckv_ref[...].astype(BF16)
    kpe = kpe_ref[...]
    ss_pe = jnp.sum(kpe * kpe, axis=-1, keepdims=True)
    kr = _rope128(kpe * gkr_ref[...], c_ref[...], sa_ref[...], sb_ref[...])
    kr = kr + pltpu.roll(kr, QK_ROPE, 1)
    for p in range(MLA_HEADS // 2):
        kn2 = _dot(ckv, wkb_ref[:, p * 2 * QK_NOPE:(p + 1) * 2 * QK_NOPE])
        for s in range(2):
            lo = (2 * p + s) * HEAD_PAD
            kn = kn2[:, s * QK_NOPE:(s + 1) * QK_NOPE]
            r = lax.rsqrt((jnp.sum(kn * kn, axis=-1, keepdims=True) + ss_pe) * (1.0 / QK_HEAD) + EPS)
            k_ref[:, lo:lo + QK_NOPE] = (kn * r * gkn_ref[...]).astype(BF16)
            k_ref[:, lo + QK_NOPE:lo + HEAD_PAD] = (kr * r).astype(BF16)
    vt = _dot_nt(wvbt_ref[...], ckv)
    tm = vt.shape[1]
    ones_row = (lax.broadcasted_iota(jnp.int32, (V_ROWS - V_HEAD, tm), 0) == 0).astype(BF16)
    for h in range(MLA_HEADS):
        vt_ref[h, :V_HEAD, :] = vt[h * V_HEAD:(h + 1) * V_HEAD].astype(BF16)
        vt_ref[h, V_HEAD:, :] = ones_row


def _qkv(hq, ckv, kpe, wq, wkb, wvbt, gq, gkn, gkr, tabs, tm, seq_blocks):
    t = hq.shape[0]
    row = lambda w: pl.BlockSpec((tm, w), lambda i: (i, 0))
    full = lambda a: pl.BlockSpec(a.shape, lambda i: (0, 0))
    tab = pl.BlockSpec((tm, 128), lambda i: (i % seq_blocks, 0))
    vmem = 2 * (tm * (Q_LORA * 2 + KV_LORA * 4 + 128 * 4 + 4 * 128 * 4) + (wq.size + wkb.size + wvbt.size) * 2
                + tm * (2 * MLA_HEADS * HEAD_PAD * 2 + MLA_HEADS * V_ROWS * 2))
    vmem += 8 * tm * HEAD_PAD * 4 + 2 * tm * MLA_HEADS * V_HEAD * 4
    return pl.pallas_call(
        _qkv_kernel,
        grid=(t // tm,),
        in_specs=[row(Q_LORA), row(KV_LORA), row(128), full(wq), full(wkb), full(wvbt),
                  full(gq), full(gkn), full(gkr), tab, tab, tab, tab],
        out_specs=[row(MLA_HEADS * HEAD_PAD), row(MLA_HEADS * HEAD_PAD),
                   pl.BlockSpec((None, MLA_HEADS, None, V_ROWS, tm),
                                lambda i: (i // seq_blocks, 0, i % seq_blocks, 0, 0))],
        out_shape=[jax.ShapeDtypeStruct((t, MLA_HEADS * HEAD_PAD), BF16),
                   jax.ShapeDtypeStruct((t, MLA_HEADS * HEAD_PAD), BF16),
                   jax.ShapeDtypeStruct((t // (tm * seq_blocks), MLA_HEADS, seq_blocks, V_ROWS, tm), BF16)],
        compiler_params=_params(("parallel",), vmem),
        name="qkv_proj",
    )(hq, ckv, kpe, wq, wkb, wvbt, gq, gkn, gkr, *tabs)


def _q_only_kernel(hq_ref, wq_ref, gq_ref, qtab_ref, q_ref):
    _q_heads(hq_ref[...], wq_ref, gq_ref, qtab_ref[...], q_ref)


def _q_only(hq, wq, gq, qtab, tm):
    t = hq.shape[0]
    row = lambda w: pl.BlockSpec((tm, w), lambda i: (i, 0))
    full = lambda a: pl.BlockSpec(a.shape, lambda i: (0, 0))
    vmem = 2 * (tm * (Q_LORA * 2 + 128 * 4 + MLA_HEADS * HEAD_PAD * 2) + wq.size * 2) + 8 * tm * HEAD_PAD * 4
    return pl.pallas_call(
        _q_only_kernel,
        grid=(t // tm,),
        in_specs=[row(Q_LORA), full(wq), full(gq), pl.BlockSpec((tm, 128), lambda i: (0, 0))],
        out_specs=row(MLA_HEADS * HEAD_PAD),
        out_shape=jax.ShapeDtypeStruct((t, MLA_HEADS * HEAD_PAD), BF16),
        compiler_params=_params(("parallel",), vmem),
        name="q_proj",
    )(hq, wq, gq, qtab)


def _flash_kernel(q_ref, k_ref, vt_ref, o_ref, s0_sc, s1_sc, m_sc, acc_sc, *, tq, tk):
    qi = pl.program_id(2)
    m_sc[...] = jnp.full(m_sc.shape, NEG_INF, F32)
    acc_sc[...] = jnp.zeros(acc_sc.shape, F32)

    def scores(kj, s_sc, q_lo=0):
        start = pl.multiple_of(kj * tk, tk)
        s_sc[:, q_lo:] = _dot_nt(k_ref[pl.ds(start, tk), :], q_ref[q_lo:, :])

    def consume(kj, s_sc, key_chunk0=None, q_lo=0):
        s = s_sc[:, q_lo:]
        if key_chunk0 is not None:
            kc = key_chunk0 + lax.broadcasted_iota(jnp.int32, s.shape, 0) // CHUNK
            qc = (q_lo + lax.broadcasted_iota(jnp.int32, s.shape, 1)) // CHUNK
            s = jnp.where(qc >= kc, s, NEG_INF)
        m = m_sc[:, q_lo:]
        m_new = jnp.maximum(m, jnp.max(s, axis=0, keepdims=True))
        alpha = jnp.exp2(m - m_new)
        p = jnp.exp2(s - m_new).astype(BF16)
        acc_sc[:, q_lo:] = alpha * acc_sc[:, q_lo:] + _dot(vt_ref[kj], p)
        m_sc[:, q_lo:] = m_new

    scores(0, s0_sc)

    def pair(t, carry):
        scores(2 * t + 1, s1_sc)
        consume(2 * t, s0_sc)
        scores(2 * t + 2, s0_sc)
        consume(2 * t + 1, s1_sc)
        return carry

    n_diag = tq // tk
    lax.fori_loop(0, qi * (n_diag // 2), pair, 0)
    base = qi * n_diag
    bufs = (s0_sc, s1_sc)
    for d in range(n_diag):
        if d + 1 < n_diag:
            scores(base + d + 1, bufs[(d + 1) % 2], (d + 1) * tk)
        consume(base + d, bufs[d % 2], d * tk // CHUNK, d * tk)
    acc = acc_sc[...]
    o_ref[...] = (acc[:V_HEAD] / acc[V_HEAD:V_HEAD + 1]).T.astype(BF16)


def _flash(q, k, vt, batch, seq, tq):
    tk = vt.shape[-1]
    assert tq % (2 * tk) == 0 and tk % CHUNK == 0 and seq % tq == 0
    nq = seq // tq
    vmem = 2 * (tq * HEAD_PAD * 2 + seq * HEAD_PAD * 2 + seq * V_ROWS * 2 + tq * V_HEAD * 2)
    vmem += 2 * tk * tq * 4 + V_ROWS * tq * 4 + 4 * tk * tq * 4
    return pl.pallas_call(
        functools.partial(_flash_kernel, tq=tq, tk=tk),
        grid=(batch, MLA_HEADS, nq),
        in_specs=[pl.BlockSpec((tq, HEAD_PAD), lambda b, h, i: (b * nq + i, h)),
                  pl.BlockSpec((seq, HEAD_PAD), lambda b, h, i: (b, h)),
                  pl.BlockSpec((None, None, seq // tk, V_ROWS, tk), lambda b, h, i: (b, h, 0, 0, 0))],
        out_specs=pl.BlockSpec((tq, V_HEAD), lambda b, h, i: (b * nq + i, h)),
        out_shape=jax.ShapeDtypeStruct((batch * seq, MLA_HEADS * V_HEAD), BF16),
        scratch_shapes=[pltpu.VMEM((tk, tq), F32), pltpu.VMEM((tk, tq), F32),
                        pltpu.VMEM((1, tq), F32), pltpu.VMEM((V_ROWS, tq), F32)],
        compiler_params=_params(("parallel", "parallel", "arbitrary"), vmem),
        name="mla_flash",
    )(q, k, vt)


def _q_absorb_kernel(q_ref, wkb_ref, qa_ref, qr_ref):
    qa_ref[...] = _dot_nt(q_ref[:, :QK_NOPE], wkb_ref[...]).astype(BF16)
    qr_ref[...] = q_ref[:, QK_NOPE:]


def _q_absorb(q, wkb):
    t = q.shape[0]
    vmem = 2 * (t * HEAD_PAD * 2 + KV_LORA * QK_NOPE * 2 + t * KV_LORA * 2 + t * 128 * 2) + 2 * t * KV_LORA * 4
    return pl.pallas_call(
        _q_absorb_kernel,
        grid=(MLA_HEADS,),
        in_specs=[pl.BlockSpec((t, HEAD_PAD), lambda h: (0, h)),
                  pl.BlockSpec((KV_LORA, QK_NOPE), lambda h: (0, h))],
        out_specs=[pl.BlockSpec((None, t, KV_LORA), lambda h: (h, 0, 0)),
                   pl.BlockSpec((None, t, 128), lambda h: (h, 0, 0))],
        out_shape=[jax.ShapeDtypeStruct((MLA_HEADS, t, KV_LORA), BF16),
                   jax.ShapeDtypeStruct((MLA_HEADS, t, 128), BF16)],
        compiler_params=_params(("parallel",), vmem),
        name="q_absorb",
    )(q, wkb)


def _cache_attn_kernel(wkbt_ref, qa_ref, qr_ref, lat_ref, kpet_ref, ct_ref, st_ref,
                       nlat_ref, nkpet_ref, nct_ref, nst_ref, gkr_ref,
                       o_ref, s_sc, m_sc, l_sc, acc_sc, *, nq, n_new, past, tk):
    j = pl.program_id(1)
    last = pl.num_programs(1) - 1
    rows = MLA_HEADS * nq
    half = QK_ROPE // 2

    @pl.when(j == 0)
    def _():
        m_sc[...] = jnp.full(m_sc.shape, NEG_INF, F32)
        l_sc[...] = jnp.zeros(l_sc.shape, F32)
        acc_sc[...] = jnp.zeros(acc_sc.shape, F32)

    def block(lat, kpet, ct, st, width, visible):
        c = lat.astype(BF16)
        kt = _dot_nt(wkbt_ref[...], c)
        s = _dot_nt(qa_ref[...].reshape(rows, KV_LORA), c)
        ss_pe = jnp.sum(kpet * kpet, axis=0, keepdims=True)
        kr = kpet * gkr_ref[...]
        k1, k2 = kr[:half], kr[half:]
        o1, o2 = k1 * ct - k2 * st, k2 * ct + k1 * st
        krot = jnp.concatenate([o1, o2, o1, o2], axis=0).astype(BF16)
        s = s + _dot(qr_ref[...].reshape(rows, 128), krot)
        for h in range(MLA_HEADS):
            kh = kt[h * QK_NOPE:(h + 1) * QK_NOPE]
            r = lax.rsqrt((jnp.sum(kh * kh, axis=0, keepdims=True) + ss_pe) * (1.0 / QK_HEAD) + EPS)
            s_sc[h * nq:(h + 1) * nq, :width] = s[h * nq:(h + 1) * nq] * r
        s = s_sc[:, :width]
        if visible is not None:
            s = jnp.where(visible, s, NEG_INF)
        m = m_sc[...]
        m_new = jnp.maximum(m, jnp.max(s, axis=-1, keepdims=True))
        alpha = jnp.exp2(m - m_new)
        p = jnp.exp2(s - m_new)
        l_sc[...] = alpha * l_sc[...] + jnp.sum(p, axis=-1, keepdims=True)
        m_sc[...] = m_new
        acc_sc[...] = alpha * acc_sc[...] + _dot(p.astype(BF16), c)

    @pl.when(j < last)
    def _():
        block(lat_ref[...], kpet_ref[...], ct_ref[...], st_ref[...], tk, None)

    @pl.when(j == last)
    def _():
        kk = lax.broadcasted_iota(jnp.int32, (1, 128), 1)
        visible = kk < n_new
        if (past + n_new - 1) // CHUNK != past // CHUNK:
            qpos = past + lax.broadcasted_iota(jnp.int32, (rows, 1), 0) % nq
            visible = visible & (qpos // CHUNK >= (past + kk) // CHUNK)
        block(nlat_ref[...], nkpet_ref[...], nct_ref[...], nst_ref[...], 128, visible)
        o_ref[...] = (acc_sc[...] / l_sc[...]).astype(BF16)


def _cache_attn(wkbt, qa, qr, lat, kpet, ct, st, nlat, nkpet, nct, nst, gkr, nq, n_new, tk):
    batch, past, _ = lat.shape
    assert past % tk == 0 and past >= tk
    nkb = past // tk
    rows = MLA_HEADS * nq
    cj = lambda j: jnp.minimum(j, nkb - 1)
    kern = functools.partial(_cache_attn_kernel, nq=nq, n_new=n_new, past=past, tk=tk)
    vmem = 2 * (wkbt.size * 2 + rows * (KV_LORA + 128) * 2 + tk * KV_LORA * 4 + 3 * 64 * tk * 4 + rows * KV_LORA * 2)
    vmem += (MLA_HEADS * QK_NOPE + 4 * rows) * tk * 4 + tk * KV_LORA * 2 + rows * KV_LORA * 8
    return pl.pallas_call(
        kern,
        grid=(batch, nkb + 1),
        in_specs=[pl.BlockSpec(wkbt.shape, lambda b, j: (0, 0)),
                  pl.BlockSpec((MLA_HEADS, nq, KV_LORA), lambda b, j: (0, b, 0)),
                  pl.BlockSpec((MLA_HEADS, nq, 128), lambda b, j: (0, b, 0)),
                  pl.BlockSpec((None, tk, KV_LORA), lambda b, j: (b, cj(j), 0)),
                  pl.BlockSpec((None, QK_ROPE, tk), lambda b, j: (b, 0, cj(j))),
                  pl.BlockSpec((QK_ROPE // 2, tk), lambda b, j: (0, cj(j))),
                  pl.BlockSpec((QK_ROPE // 2, tk), lambda b, j: (0, cj(j))),
                  pl.BlockSpec((None, 128, KV_LORA), lambda b, j: (b, 0, 0)),
                  pl.BlockSpec((None, QK_ROPE, 128), lambda b, j: (b, 0, 0)),
                  pl.BlockSpec((QK_ROPE // 2, 128), lambda b, j: (0, 0)),
                  pl.BlockSpec((QK_ROPE // 2, 128), lambda b, j: (0, 0)),
                  pl.BlockSpec((QK_ROPE, 1), lambda b, j: (0, 0))],
        out_specs=pl.BlockSpec((None, rows, KV_LORA), lambda b, j: (b, 0, 0)),
        out_shape=jax.ShapeDtypeStruct((batch, rows, KV_LORA), BF16),
        scratch_shapes=[pltpu.VMEM((rows, tk), F32), pltpu.VMEM((rows, 1), F32),
                        pltpu.VMEM((rows, 1), F32), pltpu.VMEM((rows, KV_LORA), F32)],
        compiler_params=_params(("parallel", "arbitrary"), vmem),
        name="mla_cache_attn",
    )(wkbt, qa, qr, lat, kpet, ct, st, nlat, nkpet, nct, nst, gkr)


def _v_up_kernel(ol_ref, wvb_ref, o_ref):
    b, nq, _ = ol_ref.shape
    o_ref[...] = _dot(ol_ref[...].reshape(b * nq, KV_LORA), wvb_ref[...]).astype(BF16)


def _v_up(o_lat, wvb, nq):
    batch = o_lat.shape[0]
    t = batch * nq
    vmem = 2 * (t * KV_LORA * 2 + KV_LORA * V_HEAD * 2 + t * V_HEAD * 2) + 2 * t * V_HEAD * 4
    return pl.pallas_call(
        _v_up_kernel,
        grid=(MLA_HEADS,),
        in_specs=[pl.BlockSpec((batch, nq, KV_LORA), lambda h: (0, h, 0)),
                  pl.BlockSpec((KV_LORA, V_HEAD), lambda h: (0, h))],
        out_specs=pl.BlockSpec((t, V_HEAD), lambda h: (0, h)),
        out_shape=jax.ShapeDtypeStruct((t, MLA_HEADS * V_HEAD), BF16),
        compiler_params=_params(("parallel",), vmem),
        name="v_up",
    )(o_lat, wvb)


def _mem_kv_kernel(mem_ref, gmem_ref, w_ref, gk_ref, kv_ref):
    j = pl.program_id(0)
    h = _rms(mem_ref[...], gmem_ref[...]).astype(BF16)

    @pl.when(j == 0)
    def _():
        for hh in range(MEM_HEADS):
            lo = hh * MEM_HEAD_DIM
            w = w_ref[:, lo:lo + MEM_HEAD_DIM].astype(BF16)
            kv_ref[:, lo:lo + MEM_HEAD_DIM] = _rms(_dot(h, w), gk_ref[...])

    @pl.when(j == 1)
    def _():
        kv_ref[...] = _dot(h, w_ref[...].astype(BF16))


def _mem_kv(mem, gmem, w, gk):
    t = mem.shape[0]
    vmem = 2 * (t * D_MODEL * 4 + D_MODEL * MEM_W * w.dtype.itemsize + t * MEM_W * 4) + t * D_MODEL * 6
    vmem += t * MEM_W * 4 + D_MODEL * MEM_W * 2
    return pl.pallas_call(
        _mem_kv_kernel,
        grid=(2,),
        in_specs=[pl.BlockSpec((t, D_MODEL), lambda j: (0, 0)), pl.BlockSpec((1, D_MODEL), lambda j: (0, 0)),
                  pl.BlockSpec((D_MODEL, MEM_W), lambda j: (0, j)), pl.BlockSpec((1, MEM_HEAD_DIM), lambda j: (0, 0))],
        out_specs=pl.BlockSpec((t, MEM_W), lambda j: (0, j)),
        out_shape=jax.ShapeDtypeStruct((t, 2 * MEM_W), F32),
        compiler_params=_params(("parallel",), vmem),
        name="memory_kv",
    )(mem, gmem, w, gk)


def _mem_attn_kernel(qm_ref, k_ref, v_ref, o_ref):
    for hh in range(MEM_HEADS):
        sl = slice(hh * MEM_HEAD_DIM, (hh + 1) * MEM_HEAD_DIM)
        s = _dot_nt(qm_ref[:, sl], k_ref[:, sl].astype(BF16))
        p = jnp.exp(s - jnp.max(s, axis=-1, keepdims=True))
        l = jnp.sum(p, axis=-1, keepdims=True)
        o_ref[:, sl] = (_dot(p.astype(BF16), v_ref[:, sl].astype(BF16)) / l).astype(BF16)


def _mem_attn(qm, k_arr, v_arr, k_col, v_col, tm, blocks_per_batch):
    t = qm.shape[0]
    vmem = 2 * (2 * tm * MEM_W * 2 + 2 * N_MEM * MEM_W * 4) + 6 * tm * N_MEM * 4 + 2 * N_MEM * MEM_W * 2
    return pl.pallas_call(
        _mem_attn_kernel,
        grid=(t // tm,),
        in_specs=[pl.BlockSpec((tm, MEM_W), lambda i: (i, 0)),
                  pl.BlockSpec((N_MEM, MEM_W), lambda i: (i // blocks_per_batch, k_col)),
                  pl.BlockSpec((N_MEM, MEM_W), lambda i: (i // blocks_per_batch, v_col))],
        out_specs=pl.BlockSpec((tm, MEM_W), lambda i: (i, 0)),
        out_shape=jax.ShapeDtypeStruct((t, MEM_W), BF16),
        compiler_params=_params(("parallel",), vmem),
        name="memory_attn",
    )(qm, k_arr, v_arr)


def _pool_windows(xp_sc, first_pos, tm, d_ref):
    pos = first_pos + lax.broadcasted_iota(jnp.int32, (tm, 1), 0)
    for g, win in enumerate(POOL_WINDOWS):
        sl = slice(g * POOL_GW, (g + 1) * POOL_GW)
        x = xp_sc[:, sl]
        wsum, k = x, 1
        while k < win:
            wsum = wsum + pltpu.roll(wsum, k, 0)
            k *= 2
        cnt = jnp.minimum(pos + 1, win).astype(F32)
        d_ref[:, sl] = (wsum[POOL_HALO:] / cnt - x[POOL_HALO:]).astype(BF16)


def _pool_kernel(u_ref, prev_ref, left_ref, d_ref, xp_sc, *, tm, blocks_per_seq, pos0):
    sb = pl.program_id(0) % blocks_per_seq
    xp_sc[:POOL_HALO, :] = jnp.where(sb == 0, left_ref[...], prev_ref[...])
    xp_sc[POOL_HALO:, :] = u_ref[...]
    _pool_windows(xp_sc, pos0 + sb * tm, tm, d_ref)


def _pool(u, left, tm, blocks_per_seq, pos0):
    t = u.shape[0]
    halo_per_blk = tm // POOL_HALO
    kern = functools.partial(_pool_kernel, tm=tm, blocks_per_seq=blocks_per_seq, pos0=pos0)
    vmem = 2 * (tm * POOL_W * 4 + 2 * POOL_HALO * POOL_W * 4 + tm * POOL_W * 2) + (tm + POOL_HALO) * POOL_W * 4
    vmem += 6 * tm * POOL_GW * 4
    return pl.pallas_call(
        kern,
        grid=(t // tm,),
        in_specs=[pl.BlockSpec((tm, POOL_W), lambda i: (i, 0)),
                  pl.BlockSpec((POOL_HALO, POOL_W), lambda i: (jnp.maximum(i * halo_per_blk - 1, 0), 0)),
                  pl.BlockSpec((None, POOL_HALO, POOL_W), lambda i: (i // blocks_per_seq, 0, 0))],
        out_specs=pl.BlockSpec((tm, POOL_W), lambda i: (i, 0)),
        out_shape=jax.ShapeDtypeStruct((t, POOL_W), BF16),
        scratch_shapes=[pltpu.VMEM((tm + POOL_HALO, POOL_W), F32)],
        compiler_params=_params(("parallel",), vmem),
        name="pool_windows",
    )(u, u, left)


def _merge_kernel(x_ref, gmix_ref, wg0_ref, wg1_ref, wg2_ref, bg0_ref, bg1_ref, bg2_ref,
                  d_ref, wpool_ref, pscale_ref, o_ref, wmla_ref, am_ref, wmem_ref,
                  out_ref, *rest):
    *w_copies, h_sc = rest
    wpool_copy, wmla_copy, wmem_copy = w_copies or (None, None, None)

    @pl.when(pl.program_id(1) == 0)
    def _():
        h_sc[...] = _rms(x_ref[...], gmix_ref[...]).astype(BF16)

    h = h_sc[...]
    y_pool = _dot(d_ref[...], _as_bf16(wpool_ref, wpool_copy)) * pscale_ref[...]
    y_mla = _dot(o_ref[...], _as_bf16(wmla_ref, wmla_copy))
    y_mem = _dot(am_ref[...], _as_bf16(wmem_ref, wmem_copy))
    merged = jax.nn.sigmoid(_dot_nt(h, wg0_ref[...]) + bg0_ref[...]) * y_pool
    merged += jax.nn.sigmoid(_dot_nt(h, wg1_ref[...]) + bg1_ref[...]) * y_mla
    merged += jax.nn.sigmoid(_dot_nt(h, wg2_ref[...]) + bg2_ref[...]) * y_mem
    out_ref[...] = merged.astype(BF16)


def _merge(x, gmix, w_in_t, bgate, d, wpool, pscale, o, wmla, am, wmem, tm, emit_bf16=False):
    t = x.shape[0]
    assert not emit_bf16 or t == tm
    tn = POOL_OUT_GW
    nj = D_MODEL // tn
    rowfull = lambda w: pl.BlockSpec((tm, w), lambda i, j: (i, 0))
    gate_w = lambda b: pl.BlockSpec((pl.Element(tn), pl.Element(D_MODEL)),
                                    lambda i, j: (pl.multiple_of(W_IN_GATE + (b * nj + j) * tn, BF16_SUBLANES), 0))
    gate_b = lambda b: pl.BlockSpec((1, tn), lambda i, j: (0, b * nj + j))
    col = lambda k: pl.BlockSpec((k, tn), lambda i, j: (0, j))
    pool_spec = pl.BlockSpec((None, POOL_GW, tn), lambda i, j: (j, 0, 0))
    w_bytes = wmla.dtype.itemsize + 2 * emit_bf16
    vmem = 2 * (tm * D_MODEL * 4 + 3 * D_MODEL * tn * 2 + tm * POOL_GW * 2 + POOL_GW * tn * w_bytes + tm * D_MODEL * 2
                + D_MODEL * tn * w_bytes + tm * MEM_W * 2 + MEM_W * tn * w_bytes + tm * tn * 2)
    vmem += tm * D_MODEL * 2 + tm * D_MODEL * 4 + 8 * tm * tn * 4 + (POOL_GW + D_MODEL + MEM_W) * tn * 2 * emit_bf16
    y_spec = pl.BlockSpec((tm, tn), lambda i, j: (i, j))
    y_shape = jax.ShapeDtypeStruct((t, D_MODEL), BF16)
    copies = [jax.ShapeDtypeStruct(w.shape, BF16) for w in (wpool, wmla, wmem)]
    return pl.pallas_call(
        _merge_kernel,
        grid=(t // tm, nj),
        in_specs=[rowfull(D_MODEL), pl.BlockSpec((1, D_MODEL), lambda i, j: (0, 0)),
                  gate_w(0), gate_w(1), gate_w(2), gate_b(0), gate_b(1), gate_b(2),
                  pl.BlockSpec((tm, POOL_GW), lambda i, j: (i, j)), pool_spec,
                  pl.BlockSpec((1, tn), lambda i, j: (0, j)),
                  rowfull(D_MODEL), col(D_MODEL), rowfull(MEM_W), col(MEM_W)],
        out_specs=[y_spec, pool_spec, col(D_MODEL), col(MEM_W)] if emit_bf16 else y_spec,
        out_shape=[y_shape] + copies if emit_bf16 else y_shape,
        scratch_shapes=[pltpu.VMEM((tm, D_MODEL), BF16)],
        compiler_params=_params(("parallel", "arbitrary"), vmem),
        name="gated_merge",
    )(x, gmix, w_in_t, w_in_t, w_in_t, bgate, bgate, bgate, d, wpool, pscale, o, wmla, am, wmem)


def _out_proj_kernel(x_ref, m_ref, w_ref, y_ref, *w_copy):
    y_ref[...] = x_ref[...] + _dot(m_ref[...], _as_bf16(w_ref, *w_copy or (None,)))


def _out_proj(x, merged, w, tm, tn, emit_bf16=False):
    t = x.shape[0]
    assert not emit_bf16 or t == tm
    w_spec = pl.BlockSpec((D_MODEL, tn), lambda i, j: (0, j))
    vmem = 2 * (2 * tm * tn * 4 + tm * D_MODEL * 2 + D_MODEL * tn * (w.dtype.itemsize + 2 * emit_bf16)) + 2 * tm * tn * 4
    y_spec = pl.BlockSpec((tm, tn), lambda i, j: (i, j))
    y_shape = jax.ShapeDtypeStruct((t, D_MODEL), F32)
    return pl.pallas_call(
        _out_proj_kernel,
        grid=(t // tm, D_MODEL // tn),
        in_specs=[pl.BlockSpec((tm, tn), lambda i, j: (i, j)),
                  pl.BlockSpec((tm, D_MODEL), lambda i, j: (i, 0)), w_spec],
        out_specs=[y_spec, w_spec] if emit_bf16 else y_spec,
        out_shape=[y_shape, jax.ShapeDtypeStruct(w.shape, BF16)] if emit_bf16 else y_shape,
        compiler_params=_params(("parallel", "arbitrary" if emit_bf16 else "parallel"), vmem),
        name="out_proj",
    )(x, merged, w)


def _ffn_kernel(x_ref, g_ref, wup_ref, wdown_ref, y_ref, *rest):
    *w_copies, h_sc, acc_sc = rest
    wup_copy, wdown_copy = w_copies or (None, None)
    j = pl.program_id(1)

    @pl.when(j == 0)
    def _():
        h_sc[...] = _rms(x_ref[...], g_ref[...]).astype(BF16)
        acc_sc[...] = jnp.zeros(acc_sc.shape, F32)

    f = jnp.maximum(_dot(h_sc[...], _as_bf16(wup_ref, wup_copy)), 0.0)
    acc_sc[...] += _dot((f * f).astype(BF16), _as_bf16(wdown_ref, wdown_copy))

    @pl.when(j == pl.num_programs(1) - 1)
    def _():
        y_ref[...] = x_ref[...] + acc_sc[...]


def _ffn(x, g, wup, wdown, tm, tf, emit_bf16=False):
    t = x.shape[0]
    assert not emit_bf16 or t == tm
    w_bytes = wup.dtype.itemsize + 2 * emit_bf16
    vmem = 2 * (2 * tm * D_MODEL * 4 + 2 * D_MODEL * tf * w_bytes) + tm * D_MODEL * 6 + 3 * tm * tf * 4 + tm * D_MODEL * 4
    vmem += 2 * D_MODEL * tf * 2 * emit_bf16
    up_spec = pl.BlockSpec((D_MODEL, tf), lambda i, j: (0, j))
    down_spec = pl.BlockSpec((tf, D_MODEL), lambda i, j: (j, 0))
    y_spec = pl.BlockSpec((tm, D_MODEL), lambda i, j: (i, 0))
    y_shape = jax.ShapeDtypeStruct((t, D_MODEL), F32)
    return pl.pallas_call(
        _ffn_kernel,
        grid=(t // tm, D_FF // tf),
        in_specs=[pl.BlockSpec((tm, D_MODEL), lambda i, j: (i, 0)),
                  pl.BlockSpec((1, D_MODEL), lambda i, j: (0, 0)), up_spec, down_spec],
        out_specs=[y_spec, up_spec, down_spec] if emit_bf16 else y_spec,
        out_shape=([y_shape, jax.ShapeDtypeStruct(wup.shape, BF16), jax.ShapeDtypeStruct(wdown.shape, BF16)]
                   if emit_bf16 else y_shape),
        scratch_shapes=[pltpu.VMEM((tm, D_MODEL), BF16), pltpu.VMEM((tm, D_MODEL), F32)],
        compiler_params=_params(("parallel", "arbitrary"), vmem),
        name="ffn",
    )(x, g, wup, wdown)


def _rope_tables(pos):
    half = QK_ROPE // 2
    inv = 1.0 / (ROPE_THETA ** (jnp.arange(half, dtype=F32) * (2.0 / QK_ROPE)))
    ang = pos.astype(F32)[:, None] * inv[None, :]
    return jnp.cos(ang), jnp.sin(ang)


def _lane_tables(cos, sin):
    z = jnp.zeros_like(cos)
    c = jnp.concatenate([cos, cos, z, z], axis=1)
    sa = jnp.concatenate([z, sin, z, z], axis=1)
    sb = jnp.concatenate([-sin, z, z, z], axis=1)
    return c, sa, sb


def _query_table(cos, sin):
    return jnp.concatenate([cos, cos, -sin, sin], axis=1)


def _row(v):
    return v.astype(F32).reshape(1, -1)


def _pad_last(a, width):
    return jnp.pad(a, [(0, 0)] * (a.ndim - 1) + [(0, width - a.shape[-1])])


def kernel(x_prompt, mem_prompt, x_sample, cache_mla_latent, cache_mla_kpe, state_pool, cache_mem_k, cache_mem_v,
           g_mix, w_in, b_gate, w_pool, pool_scale, g_q_lat, w_qb, g_q_head, g_kv_lat, w_kb, w_vb, g_k_head,
           w_mla_o, g_mem, w_mem_kv, g_mem_q, g_mem_k, w_mem_o, w_out, g_ff, w_up, w_down):
    batch, seq, _ = x_prompt.shape
    dec_batch, dec_seq, _ = x_sample.shape
    past = cache_mla_latent.shape[1]

    w_in_t = w_in.T.astype(BF16)
    bgate = _row(b_gate)
    half = QK_ROPE // 2

    def swap_halves(a):
        return jnp.concatenate([a[..., half:], a[..., :half]], axis=-1)

    def head_layout(a):
        rope = a[..., QK_NOPE:]
        return jnp.concatenate([a, swap_halves(rope)], axis=-1)

    wq = head_layout(w_qb.reshape(Q_LORA, MLA_HEADS, QK_HEAD)).reshape(Q_LORA, MLA_HEADS * HEAD_PAD).astype(BF16)
    wkb = w_kb.astype(BF16)
    wvb = w_vb.astype(BF16)
    wvbt = wvb.T
    gq_p = _row(head_layout(g_q_head.astype(F32) * MLA_SCALE_LOG2))
    gkn = _row(g_k_head[:QK_NOPE])
    gkr = _row(_pad_last(g_k_head[QK_NOPE:], 128))
    gq_s = _row(head_layout(jnp.concatenate([g_q_head[:QK_NOPE].astype(F32) * g_k_head[:QK_NOPE].astype(F32),
                                             g_q_head[QK_NOPE:].astype(F32)]) * MLA_SCALE_LOG2))
    gmq = _row(g_mem_q.astype(F32) * MEM_SCALE)
    gmix, gql, gkvl, gff, gmem, gmk, pscale = map(_row, (g_mix, g_q_lat, g_kv_lat, g_ff, g_mem, g_mem_k, pool_scale))

    ts = dec_batch * dec_seq
    xs = x_sample.reshape(ts, D_MODEL)
    u_s, hq_s, ckv_s, kpe_s, qm_s = _in_proj(xs, gmix, w_in_t, gql, gkvl, gmq, ts)
    pos_s = past + jnp.arange(dec_seq, dtype=jnp.int32)
    cos_s, sin_s = _rope_tables(pos_s)
    q_s = _q_only(hq_s, wq, gq_s, jnp.tile(_query_table(cos_s, sin_s), (dec_batch, 1)), ts)
    qa, qr = _q_absorb(q_s, wkb)
    cos_c, sin_c = _rope_tables(jnp.arange(past, dtype=jnp.int32))
    pad_new = 128 - dec_seq
    nlat = jnp.pad(ckv_s.reshape(dec_batch, dec_seq, KV_LORA), ((0, 0), (0, pad_new), (0, 0)))
    nkpet = jnp.pad(jnp.swapaxes(kpe_s[:, :QK_ROPE].reshape(dec_batch, dec_seq, QK_ROPE), 1, 2),
                    ((0, 0), (0, 0), (0, pad_new)))
    o_lat = _cache_attn(wkb.T, qa, qr, cache_mla_latent, jnp.swapaxes(cache_mla_kpe, 1, 2),
                        cos_c.T, sin_c.T, nlat, nkpet,
                        jnp.pad(cos_s.T, ((0, 0), (0, pad_new))), jnp.pad(sin_s.T, ((0, 0), (0, pad_new))),
                        g_k_head[QK_NOPE:].astype(F32).reshape(QK_ROPE, 1),
                        dec_seq, dec_seq, 1024)
    o_s = _v_up(o_lat, wvb, dec_seq)
    am_s = _mem_attn(qm_s, cache_mem_k.reshape(dec_batch * N_MEM, MEM_W), cache_mem_v.reshape(dec_batch * N_MEM, MEM_W),
                     0, 0, dec_seq, 1)
    left_s = jnp.pad(state_pool.astype(F32), ((0, 0), (POOL_HALO - POOL_STATE, 0), (0, 0)))
    d_s = _pool(u_s, left_s, dec_seq, 1, past)
    merged_s, wpool, wmla, wmem_o = _merge(xs, gmix, w_in_t, bgate, d_s, w_pool, pscale, o_s, w_mla_o, am_s, w_mem_o,
                                           ts, emit_bf16=True)
    x1_s, wout = _out_proj(xs, merged_s, w_out, ts, 512, emit_bf16=True)
    y_s, wup, wdown = _ffn(x1_s, gff, w_up, w_down, ts, 512, emit_bf16=True)
    y_s = y_s.reshape(dec_batch, dec_seq, D_MODEL)
    lat_s = ckv_s.reshape(dec_batch, dec_seq, KV_LORA)
    kpe_s_out = kpe_s[:, :QK_ROPE].reshape(dec_batch, dec_seq, QK_ROPE)
    pool_s = u_s.reshape(dec_batch, dec_seq, POOL_W)[:, dec_seq - POOL_STATE:]

    tm = 512
    xp = x_prompt.reshape(batch * seq, D_MODEL)
    mem_kv = _mem_kv(mem_prompt.reshape(batch * N_MEM, D_MODEL), gmem, w_mem_kv, gmk)
    d, hq, ckv, kpe, am, u_tail = _front(xp, gmix, w_in_t, gql, gkvl, gmq,
                                         jnp.zeros((batch, POOL_HALO, POOL_W), F32), mem_kv, tm, seq // tm, 0)
    cos_p, sin_p = _rope_tables(jnp.arange(seq, dtype=jnp.int32))
    q, k, vt = _qkv(hq, ckv, kpe, wq, wkb, wvbt, gq_p, gkn, gkr,
                    (_query_table(cos_p, sin_p),) + _lane_tables(cos_p, sin_p), tm, seq // tm)
    o = _flash(q, k, vt, batch, seq, 2048)
    merged = _merge(xp, gmix, w_in_t, bgate, d, wpool, pscale, o, wmla, am, wmem_o, tm)
    y_p = _ffn(_out_proj(xp, merged, wout, tm, 1024), gff, wup, wdown, tm, 1024).reshape(batch, seq, D_MODEL)
    lat_p = ckv.reshape(batch, seq, KV_LORA)
    kpe_p = kpe[:, :QK_ROPE].reshape(batch, seq, QK_ROPE)
    pool_p = u_tail[:, POOL_HALO - POOL_STATE:]
    mem_k_p = mem_kv[:, :MEM_W].reshape(batch, N_MEM, MEM_HEADS, MEM_HEAD_DIM)
    mem_v_p = mem_kv[:, MEM_W:].reshape(batch, N_MEM, MEM_HEADS, MEM_HEAD_DIM)

    return (y_p, y_s, lat_p, kpe_p, pool_p, mem_k_p, mem_v_p, lat_s, kpe_s_out, pool_s)
```

```python
import functools

import jax
import jax.numpy as jnp
from jax import lax
from jax.experimental import pallas as pl
from jax.experimental.pallas import tpu as pltpu

F32 = jnp.float32
BF16 = jnp.bfloat16

EPS = 1e-6
CHUNK = 64
D_MODEL = 2048
POOL_WINDOWS = (2, 4, 8, 16)
POOL_W = 1024
POOL_GW = 256
POOL_OUT_GW = 512
POOL_STATE = 15
POOL_HALO = 16
assert all(w & (w - 1) == 0 and w <= POOL_HALO for w in POOL_WINDOWS)
MLA_HEADS = 16
Q_LORA = 512
KV_LORA = 512
QK_NOPE = 128
QK_ROPE = 64
QK_HEAD = QK_NOPE + QK_ROPE
HEAD_PAD = 256
V_HEAD = 128
V_ROWS = 144
ROPE_THETA = 10000.0
LOG2_E = 1.4426950408889634
MLA_SCALE = QK_HEAD ** -0.5
MLA_SCALE_LOG2 = MLA_SCALE * LOG2_E
N_MEM = 256
MEM_HEADS = 4
MEM_HEAD_DIM = 256
MEM_W = MEM_HEADS * MEM_HEAD_DIM
MEM_SCALE = MEM_HEAD_DIM ** -0.5
D_FF = 4 * D_MODEL
NEG_INF = -1e30

Z_U = 0
Z_QLAT = Z_U + POOL_W
Z_KVLAT = Z_QLAT + Q_LORA
Z_MQ = Z_KVLAT + KV_LORA
W_IN_KPE = Z_MQ
W_IN_MQ = W_IN_KPE + QK_ROPE
W_IN_GATE = W_IN_MQ + MEM_W

BF16_SUBLANES = 16
V7X_VMEM_BYTES = 64 * 1024 * 1024
VMEM_CEILING = V7X_VMEM_BYTES - 6 * 1024 * 1024


def _params(semantics, vmem_bytes):
    return pltpu.CompilerParams(dimension_semantics=semantics,
                                vmem_limit_bytes=int(min(vmem_bytes, VMEM_CEILING)))


def _as_bf16(w_ref, copy_ref):
    w = w_ref[...].astype(BF16)
    if copy_ref is not None:
        copy_ref[...] = w
    return w


def _rms(x, g):
    return x * lax.rsqrt(jnp.mean(x * x, axis=-1, keepdims=True) + EPS) * g


def _dot(a, b):
    return jnp.dot(a, b, preferred_element_type=F32)


def _dot_nt(a, b):
    return lax.dot_general(a, b, (((1,), (1,)), ((), ())), preferred_element_type=F32)


def _rope128(x, c, sa, sb):
    return x * c + pltpu.roll(x, 32, 1) * sa + pltpu.roll(x, 96, 1) * sb


def _in_proj_kernel(x_ref, gmix_ref, wa_ref, wkpe_ref, wmq_ref, gq_ref, gkv_ref, gmq_ref,
                    u_ref, hq_ref, ckv_ref, kpe_ref, qm_ref, wa_copy, wkpe_copy, wmq_copy):
    def rows(ref, copy, lo, hi):
        w = ref[lo:hi, :].astype(BF16)
        copy[lo:hi, :] = w
        return w

    h = _rms(x_ref[...], gmix_ref[...]).astype(BF16)
    u_ref[...] = _dot_nt(h, rows(wa_ref, wa_copy, Z_U, Z_QLAT))
    hq_ref[...] = _rms(_dot_nt(h, rows(wa_ref, wa_copy, Z_QLAT, Z_KVLAT)), gq_ref[...]).astype(BF16)
    ckv_ref[...] = _rms(_dot_nt(h, rows(wa_ref, wa_copy, Z_KVLAT, Z_MQ)), gkv_ref[...])
    kpe_ref[:, :QK_ROPE] = _dot_nt(h, rows(wkpe_ref, wkpe_copy, 0, QK_ROPE))
    kpe_ref[:, QK_ROPE:] = jnp.zeros((h.shape[0], 128 - QK_ROPE), F32)
    for hh in range(MEM_HEADS):
        lo = hh * MEM_HEAD_DIM
        m = _dot_nt(h, rows(wmq_ref, wmq_copy, lo, lo + MEM_HEAD_DIM))
        qm_ref[:, lo:lo + MEM_HEAD_DIM] = _rms(m, gmq_ref[...]).astype(BF16)


def _w_in_rows(n, start):
    return pl.BlockSpec((pl.Element(n), pl.Element(D_MODEL)),
                        lambda i: (pl.multiple_of(start + 0 * i, BF16_SUBLANES), 0))


def _front_kernel(x_ref, gmix_ref, wa_ref, wkpe_ref, wmq_ref, gq_ref, gkv_ref, gmq_ref, left_ref, mk_ref, mv_ref,
                  d_ref, hq_ref, ckv_ref, kpe_ref, am_ref, utail_ref, xp_sc, qm_sc, *, tm, blocks_per_seq, pos0):
    sb = pl.program_id(0) % blocks_per_seq
    h = _rms(x_ref[...], gmix_ref[...]).astype(BF16)

    @pl.when(sb == 0)
    def _():
        xp_sc[:POOL_HALO, :] = left_ref[...]

    @pl.when(sb != 0)
    def _():
        xp_sc[:POOL_HALO, :] = xp_sc[tm:, :]

    xp_sc[POOL_HALO:, :] = _dot_nt(h, wa_ref[Z_U:Z_QLAT, :])
    utail_ref[...] = xp_sc[tm:, :]
    hq_ref[...] = _rms(_dot_nt(h, wa_ref[Z_QLAT:Z_KVLAT, :]), gq_ref[...]).astype(BF16)
    ckv_ref[...] = _rms(_dot_nt(h, wa_ref[Z_KVLAT:Z_MQ, :]), gkv_ref[...])
    kpe_ref[:, :QK_ROPE] = _dot_nt(h, wkpe_ref[...])
    kpe_ref[:, QK_ROPE:] = jnp.zeros((tm, 128 - QK_ROPE), F32)
    for hh in range(MEM_HEADS):
        lo = hh * MEM_HEAD_DIM
        m = _dot_nt(h, wmq_ref[lo:lo + MEM_HEAD_DIM, :])
        qm_sc[:, lo:lo + MEM_HEAD_DIM] = _rms(m, gmq_ref[...]).astype(BF16)
    _pool_windows(xp_sc, pos0 + sb * tm, tm, d_ref)
    _mem_attn_kernel(qm_sc, mk_ref, mv_ref, am_ref)


def _front(x, gmix, wa_t, wkpe_t, wmq_t, gq, gkv, gmq_scaled, left, mem_kv, tm, blocks_per_seq, pos0):
    t = x.shape[0]
    n_seq = t // (tm * blocks_per_seq)
    row = lambda w: pl.BlockSpec((tm, w), lambda i: (i, 0))
    full = lambda a: pl.BlockSpec(a.shape, lambda i: (0, 0))
    w_elems = wa_t.size + wkpe_t.size + wmq_t.size
    vmem = 2 * (tm * D_MODEL * 4 + w_elems * 2 + tm * (POOL_W * 2 + Q_LORA * 2 + KV_LORA * 4 + 128 * 4 + MEM_W * 2)
                + 2 * N_MEM * MEM_W * 4 + 2 * POOL_HALO * POOL_W * 4)
    vmem += (tm + POOL_HALO) * POOL_W * 4 + tm * MEM_W * 2 + 8 * tm * POOL_W * 4
    kern = functools.partial(_front_kernel, tm=tm, blocks_per_seq=blocks_per_seq, pos0=pos0)
    return pl.pallas_call(
        kern,
        grid=(t // tm,),
        in_specs=[row(D_MODEL), full(gmix), full(wa_t), full(wkpe_t), full(wmq_t),
                  full(gq), full(gkv), full(gmq_scaled),
                  pl.BlockSpec((None, POOL_HALO, POOL_W), lambda i: (i // blocks_per_seq, 0, 0)),
                  pl.BlockSpec((N_MEM, MEM_W), lambda i: (i // blocks_per_seq, 0)),
                  pl.BlockSpec((N_MEM, MEM_W), lambda i: (i // blocks_per_seq, 1))],
        out_specs=[row(POOL_W), row(Q_LORA), row(KV_LORA), row(128), row(MEM_W),
                   pl.BlockSpec((None, POOL_HALO, POOL_W), lambda i: (i // blocks_per_seq, 0, 0))],
        out_shape=[jax.ShapeDtypeStruct((t, POOL_W), BF16), jax.ShapeDtypeStruct((t, Q_LORA), BF16),
                   jax.ShapeDtypeStruct((t, KV_LORA), F32), jax.ShapeDtypeStruct((t, 128), F32),
                   jax.ShapeDtypeStruct((t, MEM_W), BF16), jax.ShapeDtypeStruct((n_seq, POOL_HALO, POOL_W), F32)],
        scratch_shapes=[pltpu.VMEM((tm + POOL_HALO, POOL_W), F32), pltpu.VMEM((tm, MEM_W), BF16)],
        compiler_params=_params(("arbitrary",), vmem),
        name="front",
    )(x, gmix, wa_t, wkpe_t, wmq_t, gq, gkv, gmq_scaled, left, mem_kv, mem_kv)


def _in_proj(x, gmix, w_in_t, gq, gkv, gmq_scaled):
    t = x.shape[0]
    row = lambda w: pl.BlockSpec((t, w), lambda i: (0, 0))
    full = lambda a: pl.BlockSpec(a.shape, lambda i: (0, 0))
    pieces = ((Z_MQ, 0), (QK_ROPE, W_IN_KPE), (MEM_W, W_IN_MQ))
    w_elems = sum(n for n, _ in pieces) * D_MODEL
    vmem = t * D_MODEL * 4 + w_elems * 6 + 2 * t * (POOL_W * 4 + Q_LORA * 2 + KV_LORA * 4 + 128 * 4 + MEM_W * 2)
    vmem += w_elems * 2 + 6 * t * POOL_W * 4
    return pl.pallas_call(
        _in_proj_kernel,
        grid=(1,),
        in_specs=[row(D_MODEL), full(gmix)] + [_w_in_rows(n, start) for n, start in pieces]
                 + [full(gq), full(gkv), full(gmq_scaled)],
        out_specs=[row(POOL_W), row(Q_LORA), row(KV_LORA), row(128), row(MEM_W)]
                  + [pl.BlockSpec((n, D_MODEL), lambda i: (0, 0)) for n, _ in pieces],
        out_shape=[jax.ShapeDtypeStruct((t, POOL_W), F32), jax.ShapeDtypeStruct((t, Q_LORA), BF16),
                   jax.ShapeDtypeStruct((t, KV_LORA), F32), jax.ShapeDtypeStruct((t, 128), F32),
                   jax.ShapeDtypeStruct((t, MEM_W), BF16)]
                  + [jax.ShapeDtypeStruct((n, D_MODEL), BF16) for n, _ in pieces],
        compiler_params=_params(("arbitrary",), vmem),
        name="in_proj",
    )(x, gmix, w_in_t, w_in_t, w_in_t, gq, gkv, gmq_scaled)


def _q_heads(hq, wq_ref, gq_ref, qtab, q_ref):
    first = lax.broadcasted_iota(jnp.int32, (1, 128), 1) < QK_ROPE
    for h in range(MLA_HEADS):
        lo = h * HEAD_PAD
        qh = _dot(hq, wq_ref[:, lo:lo + HEAD_PAD])
        qn, qr = qh[:, :QK_NOPE], qh[:, QK_NOPE:]
        ss = jnp.sum(qn * qn + jnp.where(first, qr * qr, 0.0), axis=-1, keepdims=True)
        r = lax.rsqrt(ss * (1.0 / QK_HEAD) + EPS)
        q_ref[:, lo:lo + QK_NOPE] = (qn * r * gq_ref[:, :QK_NOPE]).astype(BF16)
        q_ref[:, lo + QK_NOPE:lo + HEAD_PAD] = (qr * r * gq_ref[:, QK_NOPE:] * qtab).astype(BF16)


def _qkv_kernel(hq_ref, ckv_ref, kpe_ref, wq_ref, wkbt_ref, wvbt_ref, gq_ref, gkn_ref, gkr_ref,
                qtab_ref, c_ref, sa_ref, sb_ref, q_ref, k_ref, vt_ref):
    _q_heads(hq_ref[...], wq_ref, gq_ref, qtab_ref[...], q_ref)
    ckv = ckv_ref[...].astype(BF16)
    kpe = kpe_ref[...]
    ss_pe = jnp.sum(kpe * kpe, axis=-1, keepdims=True)
    kr = _rope128(kpe * gkr_ref[...], c_ref[...], sa_ref[...], sb_ref[...])
    kr = kr + pltpu.roll(kr, QK_ROPE, 1)
    for p in range(MLA_HEADS // 2):
        kn2 = _dot_nt(ckv, wkbt_ref[p * 2 * QK_NOPE:(p + 1) * 2 * QK_NOPE, :])
        for s in range(2):
            lo = (2 * p + s) * HEAD_PAD
            kn = kn2[:, s * QK_NOPE:(s + 1) * QK_NOPE]
            r = lax.rsqrt((jnp.sum(kn * kn, axis=-1, keepdims=True) + ss_pe) * (1.0 / QK_HEAD) + EPS)
            k_ref[:, lo:lo + QK_NOPE] = (kn * r * gkn_ref[...]).astype(BF16)
            k_ref[:, lo + QK_NOPE:lo + HEAD_PAD] = (kr * r).astype(BF16)
    vt = _dot_nt(wvbt_ref[...], ckv)
    tm = vt.shape[1]
    ones_row = (lax.broadcasted_iota(jnp.int32, (V_ROWS - V_HEAD, tm), 0) == 0).astype(BF16)
    for h in range(MLA_HEADS):
        vt_ref[h, :V_HEAD, :] = vt[h * V_HEAD:(h + 1) * V_HEAD].astype(BF16)
        vt_ref[h, V_HEAD:, :] = ones_row


def _qkv(hq, ckv, kpe, wq, wkbt, wvbt, gq, gkn, gkr, tabs, tm, seq_blocks):
    t = hq.shape[0]
    row = lambda w: pl.BlockSpec((tm, w), lambda i: (i, 0))
    full = lambda a: pl.BlockSpec(a.shape, lambda i: (0, 0))
    tab = pl.BlockSpec((tm, 128), lambda i: (i % seq_blocks, 0))
    vmem = 2 * (tm * (Q_LORA * 2 + KV_LORA * 4 + 128 * 4 + 4 * 128 * 4) + (wq.size + wkbt.size + wvbt.size) * 2
                + tm * (2 * MLA_HEADS * HEAD_PAD * 2 + MLA_HEADS * V_ROWS * 2))
    vmem += 8 * tm * HEAD_PAD * 4 + 2 * tm * MLA_HEADS * V_HEAD * 4
    return pl.pallas_call(
        _qkv_kernel,
        grid=(t // tm,),
        in_specs=[row(Q_LORA), row(KV_LORA), row(128), full(wq), full(wkbt), full(wvbt),
                  full(gq), full(gkn), full(gkr), tab, tab, tab, tab],
        out_specs=[row(MLA_HEADS * HEAD_PAD), row(MLA_HEADS * HEAD_PAD),
                   pl.BlockSpec((None, MLA_HEADS, None, V_ROWS, tm),
                                lambda i: (i // seq_blocks, 0, i % seq_blocks, 0, 0))],
        out_shape=[jax.ShapeDtypeStruct((t, MLA_HEADS * HEAD_PAD), BF16),
                   jax.ShapeDtypeStruct((t, MLA_HEADS * HEAD_PAD), BF16),
                   jax.ShapeDtypeStruct((t // (tm * seq_blocks), MLA_HEADS, seq_blocks, V_ROWS, tm), BF16)],
        compiler_params=_params(("parallel",), vmem),
        name="qkv_proj",
    )(hq, ckv, kpe, wq, wkbt, wvbt, gq, gkn, gkr, *tabs)


def _q_only_kernel(hq_ref, wq_ref, gq_ref, qtab_ref, q_ref):
    _q_heads(hq_ref[...], wq_ref, gq_ref, qtab_ref[...], q_ref)


def _q_only(hq, wq, gq, qtab, tm):
    t = hq.shape[0]
    row = lambda w: pl.BlockSpec((tm, w), lambda i: (i, 0))
    full = lambda a: pl.BlockSpec(a.shape, lambda i: (0, 0))
    vmem = 2 * (tm * (Q_LORA * 2 + 128 * 4 + MLA_HEADS * HEAD_PAD * 2) + wq.size * 2) + 8 * tm * HEAD_PAD * 4
    return pl.pallas_call(
        _q_only_kernel,
        grid=(t // tm,),
        in_specs=[row(Q_LORA), full(wq), full(gq), pl.BlockSpec((tm, 128), lambda i: (0, 0))],
        out_specs=row(MLA_HEADS * HEAD_PAD),
        out_shape=jax.ShapeDtypeStruct((t, MLA_HEADS * HEAD_PAD), BF16),
        compiler_params=_params(("parallel",), vmem),
        name="q_proj",
    )(hq, wq, gq, qtab)


def _flash_kernel(q_ref, k_ref, vt_ref, o_ref, s0_sc, s1_sc, m_sc, acc_sc, *, tq, tk):
    qi = pl.program_id(2)
    m_sc[...] = jnp.full(m_sc.shape, NEG_INF, F32)
    acc_sc[...] = jnp.zeros(acc_sc.shape, F32)

    def scores(kj, s_sc, q_lo=0):
        start = pl.multiple_of(kj * tk, tk)
        s_sc[:, q_lo:] = _dot_nt(k_ref[pl.ds(start, tk), :], q_ref[q_lo:, :])

    def consume(kj, s_sc, key_chunk0=None, q_lo=0):
        s = s_sc[:, q_lo:]
        if key_chunk0 is not None:
            kc = key_chunk0 + lax.broadcasted_iota(jnp.int32, s.shape, 0) // CHUNK
            qc = (q_lo + lax.broadcasted_iota(jnp.int32, s.shape, 1)) // CHUNK
            s = jnp.where(qc >= kc, s, NEG_INF)
        m = m_sc[:, q_lo:]
        m_new = jnp.maximum(m, jnp.max(s, axis=0, keepdims=True))
        alpha = jnp.exp2(m - m_new)
        p = jnp.exp2(s - m_new).astype(BF16)
        acc_sc[:, q_lo:] = alpha * acc_sc[:, q_lo:] + _dot(vt_ref[kj], p)
        m_sc[:, q_lo:] = m_new

    scores(0, s0_sc)

    def pair(t, carry):
        scores(2 * t + 1, s1_sc)
        consume(2 * t, s0_sc)
        scores(2 * t + 2, s0_sc)
        consume(2 * t + 1, s1_sc)
        return carry

    n_diag = tq // tk
    lax.fori_loop(0, qi * (n_diag // 2), pair, 0)
    base = qi * n_diag
    bufs = (s0_sc, s1_sc)
    for d in range(n_diag):
        if d + 1 < n_diag:
            scores(base + d + 1, bufs[(d + 1) % 2], (d + 1) * tk)
        consume(base + d, bufs[d % 2], d * tk // CHUNK, d * tk)
    acc = acc_sc[...]
    o_ref[...] = (acc[:V_HEAD] / acc[V_HEAD:V_HEAD + 1]).T.astype(BF16)


def _flash(q, k, vt, batch, seq, tq):
    tk = vt.shape[-1]
    assert tq % (2 * tk) == 0 and tk % CHUNK == 0 and seq % tq == 0
    nq = seq // tq
    vmem = 2 * (tq * HEAD_PAD * 2 + seq * HEAD_PAD * 2 + seq * V_ROWS * 2 + tq * V_HEAD * 2)
    vmem += 2 * tk * tq * 4 + V_ROWS * tq * 4 + 4 * tk * tq * 4
    return pl.pallas_call(
        functools.partial(_flash_kernel, tq=tq, tk=tk),
        grid=(batch, MLA_HEADS, nq),
        in_specs=[pl.BlockSpec((tq, HEAD_PAD), lambda b, h, i: (b * nq + i, h)),
                  pl.BlockSpec((seq, HEAD_PAD), lambda b, h, i: (b, h)),
                  pl.BlockSpec((None, None, seq // tk, V_ROWS, tk), lambda b, h, i: (b, h, 0, 0, 0))],
        out_specs=pl.BlockSpec((tq, V_HEAD), lambda b, h, i: (b * nq + i, h)),
        out_shape=jax.ShapeDtypeStruct((batch * seq, MLA_HEADS * V_HEAD), BF16),
        scratch_shapes=[pltpu.VMEM((tk, tq), F32), pltpu.VMEM((tk, tq), F32),
                        pltpu.VMEM((1, tq), F32), pltpu.VMEM((V_ROWS, tq), F32)],
        compiler_params=_params(("parallel", "parallel", "arbitrary"), vmem),
        name="mla_flash",
    )(q, k, vt)


def _q_absorb_kernel(q_ref, wkbt_ref, qa_ref, qr_ref):
    qa_ref[...] = _dot(q_ref[:, :QK_NOPE], wkbt_ref[...]).astype(BF16)
    qr_ref[...] = q_ref[:, QK_NOPE:]


def _q_absorb(q, wkbt):
    t = q.shape[0]
    vmem = 2 * (t * HEAD_PAD * 2 + KV_LORA * QK_NOPE * 2 + t * KV_LORA * 2 + t * 128 * 2) + 2 * t * KV_LORA * 4
    return pl.pallas_call(
        _q_absorb_kernel,
        grid=(MLA_HEADS,),
        in_specs=[pl.BlockSpec((t, HEAD_PAD), lambda h: (0, h)),
                  pl.BlockSpec((QK_NOPE, KV_LORA), lambda h: (h, 0))],
        out_specs=[pl.BlockSpec((None, t, KV_LORA), lambda h: (h, 0, 0)),
                   pl.BlockSpec((None, t, 128), lambda h: (h, 0, 0))],
        out_shape=[jax.ShapeDtypeStruct((MLA_HEADS, t, KV_LORA), BF16),
                   jax.ShapeDtypeStruct((MLA_HEADS, t, 128), BF16)],
        compiler_params=_params(("parallel",), vmem),
        name="q_absorb",
    )(q, wkbt)


def _cache_attn_kernel(wkbt_ref, qa_ref, qr_ref, lat_ref, kpet_ref, ct_ref, st_ref,
                       nlat_ref, nkpet_ref, nct_ref, nst_ref, gkr_ref,
                       o_ref, s_sc, m_sc, l_sc, acc_sc, *, nq, n_new, past, tk):
    j = pl.program_id(1)
    last = pl.num_programs(1) - 1
    rows = MLA_HEADS * nq
    half = QK_ROPE // 2

    @pl.when(j == 0)
    def _():
        m_sc[...] = jnp.full(m_sc.shape, NEG_INF, F32)
        l_sc[...] = jnp.zeros(l_sc.shape, F32)
        acc_sc[...] = jnp.zeros(acc_sc.shape, F32)

    def block(lat, kpet, ct, st, width, visible):
        c = lat.astype(BF16)
        kt = _dot_nt(wkbt_ref[...], c)
        s = _dot_nt(qa_ref[...].reshape(rows, KV_LORA), c)
        ss_pe = jnp.sum(kpet * kpet, axis=0, keepdims=True)
        kr = kpet * gkr_ref[...]
        k1, k2 = kr[:half], kr[half:]
        o1, o2 = k1 * ct - k2 * st, k2 * ct + k1 * st
        krot = jnp.concatenate([o1, o2, o1, o2], axis=0).astype(BF16)
        s = s + _dot(qr_ref[...].reshape(rows, 128), krot)
        for h in range(MLA_HEADS):
            kh = kt[h * QK_NOPE:(h + 1) * QK_NOPE]
            r = lax.rsqrt((jnp.sum(kh * kh, axis=0, keepdims=True) + ss_pe) * (1.0 / QK_HEAD) + EPS)
            s_sc[h * nq:(h + 1) * nq, :width] = s[h * nq:(h + 1) * nq] * r
        s = s_sc[:, :width]
        if visible is not None:
            s = jnp.where(visible, s, NEG_INF)
        m = m_sc[...]
        m_new = jnp.maximum(m, jnp.max(s, axis=-1, keepdims=True))
        alpha = jnp.exp2(m - m_new)
        p = jnp.exp2(s - m_new)
        l_sc[...] = alpha * l_sc[...] + jnp.sum(p, axis=-1, keepdims=True)
        m_sc[...] = m_new
        acc_sc[...] = alpha * acc_sc[...] + _dot(p.astype(BF16), c)

    @pl.when(j < last)
    def _():
        block(lat_ref[...], kpet_ref[...], ct_ref[...], st_ref[...], tk, None)

    @pl.when(j == last)
    def _():
        kk = lax.broadcasted_iota(jnp.int32, (1, 128), 1)
        visible = kk < n_new
        if (past + n_new - 1) // CHUNK != past // CHUNK:
            qpos = past + lax.broadcasted_iota(jnp.int32, (rows, 1), 0) % nq
            visible = visible & (qpos // CHUNK >= (past + kk) // CHUNK)
        block(nlat_ref[...], nkpet_ref[...], nct_ref[...], nst_ref[...], 128, visible)
        o_ref[...] = (acc_sc[...] / l_sc[...]).astype(BF16)


def _cache_attn(wkbt, qa, qr, lat, kpet, ct, st, nlat, nkpet, nct, nst, gkr, nq, n_new, tk):
    batch, past, _ = lat.shape
    assert past % tk == 0 and past >= tk
    nkb = past // tk
    rows = MLA_HEADS * nq
    cj = lambda j: jnp.minimum(j, nkb - 1)
    kern = functools.partial(_cache_attn_kernel, nq=nq, n_new=n_new, past=past, tk=tk)
    vmem = 2 * (wkbt.size * 2 + rows * (KV_LORA + 128) * 2 + tk * KV_LORA * 4 + 3 * 64 * tk * 4 + rows * KV_LORA * 2)
    vmem += (MLA_HEADS * QK_NOPE + 4 * rows) * tk * 4 + tk * KV_LORA * 2 + rows * KV_LORA * 8
    return pl.pallas_call(
        kern,
        grid=(batch, nkb + 1),
        in_specs=[pl.BlockSpec(wkbt.shape, lambda b, j: (0, 0)),
                  pl.BlockSpec((MLA_HEADS, nq, KV_LORA), lambda b, j: (0, b, 0)),
                  pl.BlockSpec((MLA_HEADS, nq, 128), lambda b, j: (0, b, 0)),
                  pl.BlockSpec((None, tk, KV_LORA), lambda b, j: (b, cj(j), 0)),
                  pl.BlockSpec((None, QK_ROPE, tk), lambda b, j: (b, 0, cj(j))),
                  pl.BlockSpec((QK_ROPE // 2, tk), lambda b, j: (0, cj(j))),
                  pl.BlockSpec((QK_ROPE // 2, tk), lambda b, j: (0, cj(j))),
                  pl.BlockSpec((None, 128, KV_LORA), lambda b, j: (b, 0, 0)),
                  pl.BlockSpec((None, QK_ROPE, 128), lambda b, j: (b, 0, 0)),
                  pl.BlockSpec((QK_ROPE // 2, 128), lambda b, j: (0, 0)),
                  pl.BlockSpec((QK_ROPE // 2, 128), lambda b, j: (0, 0)),
                  pl.BlockSpec((QK_ROPE, 1), lambda b, j: (0, 0))],
        out_specs=pl.BlockSpec((None, rows, KV_LORA), lambda b, j: (b, 0, 0)),
        out_shape=jax.ShapeDtypeStruct((batch, rows, KV_LORA), BF16),
        scratch_shapes=[pltpu.VMEM((rows, tk), F32), pltpu.VMEM((rows, 1), F32),
                        pltpu.VMEM((rows, 1), F32), pltpu.VMEM((rows, KV_LORA), F32)],
        compiler_params=_params(("parallel", "arbitrary"), vmem),
        name="mla_cache_attn",
    )(wkbt, qa, qr, lat, kpet, ct, st, nlat, nkpet, nct, nst, gkr)


def _v_up_kernel(ol_ref, wvbt_ref, o_ref):
    b, nq, _ = ol_ref.shape
    o_ref[...] = _dot_nt(ol_ref[...].reshape(b * nq, KV_LORA), wvbt_ref[...]).astype(BF16)


def _v_up(o_lat, wvbt, nq):
    batch = o_lat.shape[0]
    t = batch * nq
    vmem = 2 * (t * KV_LORA * 2 + KV_LORA * V_HEAD * 2 + t * V_HEAD * 2) + 2 * t * V_HEAD * 4
    return pl.pallas_call(
        _v_up_kernel,
        grid=(MLA_HEADS,),
        in_specs=[pl.BlockSpec((batch, nq, KV_LORA), lambda h: (0, h, 0)),
                  pl.BlockSpec((V_HEAD, KV_LORA), lambda h: (h, 0))],
        out_specs=pl.BlockSpec((t, V_HEAD), lambda h: (0, h)),
        out_shape=jax.ShapeDtypeStruct((t, MLA_HEADS * V_HEAD), BF16),
        compiler_params=_params(("parallel",), vmem),
        name="v_up",
    )(o_lat, wvbt)


def _mem_kv_kernel(mem_ref, gmem_ref, w_ref, gk_ref, kv_ref):
    j = pl.program_id(0)
    h = _rms(mem_ref[...], gmem_ref[...]).astype(BF16)

    @pl.when(j == 0)
    def _():
        for hh in range(MEM_HEADS):
            lo = hh * MEM_HEAD_DIM
            w = w_ref[:, lo:lo + MEM_HEAD_DIM].astype(BF16)
            kv_ref[:, lo:lo + MEM_HEAD_DIM] = _rms(_dot(h, w), gk_ref[...])

    @pl.when(j == 1)
    def _():
        kv_ref[...] = _dot(h, w_ref[...].astype(BF16))


def _mem_kv(mem, gmem, w, gk):
    t = mem.shape[0]
    vmem = 2 * (t * D_MODEL * 4 + D_MODEL * MEM_W * w.dtype.itemsize + t * MEM_W * 4) + t * D_MODEL * 6
    vmem += t * MEM_W * 4 + D_MODEL * MEM_W * 2
    return pl.pallas_call(
        _mem_kv_kernel,
        grid=(2,),
        in_specs=[pl.BlockSpec((t, D_MODEL), lambda j: (0, 0)), pl.BlockSpec((1, D_MODEL), lambda j: (0, 0)),
                  pl.BlockSpec((D_MODEL, MEM_W), lambda j: (0, j)), pl.BlockSpec((1, MEM_HEAD_DIM), lambda j: (0, 0))],
        out_specs=pl.BlockSpec((t, MEM_W), lambda j: (0, j)),
        out_shape=jax.ShapeDtypeStruct((t, 2 * MEM_W), F32),
        compiler_params=_params(("parallel",), vmem),
        name="memory_kv",
    )(mem, gmem, w, gk)


def _mem_attn_kernel(qm_ref, k_ref, v_ref, o_ref):
    for hh in range(MEM_HEADS):
        sl = slice(hh * MEM_HEAD_DIM, (hh + 1) * MEM_HEAD_DIM)
        s = _dot_nt(qm_ref[:, sl], k_ref[:, sl].astype(BF16))
        p = jnp.exp(s - jnp.max(s, axis=-1, keepdims=True))
        l = jnp.sum(p, axis=-1, keepdims=True)
        o_ref[:, sl] = (_dot(p.astype(BF16), v_ref[:, sl].astype(BF16)) / l).astype(BF16)


def _mem_attn(qm, k_arr, v_arr, k_col, v_col, tm, blocks_per_batch):
    t = qm.shape[0]
    vmem = 2 * (2 * tm * MEM_W * 2 + 2 * N_MEM * MEM_W * 4) + 6 * tm * N_MEM * 4 + 2 * N_MEM * MEM_W * 2
    return pl.pallas_call(
        _mem_attn_kernel,
        grid=(t // tm,),
        in_specs=[pl.BlockSpec((tm, MEM_W), lambda i: (i, 0)),
                  pl.BlockSpec((N_MEM, MEM_W), lambda i: (i // blocks_per_batch, k_col)),
                  pl.BlockSpec((N_MEM, MEM_W), lambda i: (i // blocks_per_batch, v_col))],
        out_specs=pl.BlockSpec((tm, MEM_W), lambda i: (i, 0)),
        out_shape=jax.ShapeDtypeStruct((t, MEM_W), BF16),
        compiler_params=_params(("parallel",), vmem),
        name="memory_attn",
    )(qm, k_arr, v_arr)


def _pool_windows(xp_sc, first_pos, tm, d_ref):
    pos = first_pos + lax.broadcasted_iota(jnp.int32, (tm, 1), 0)
    for g, win in enumerate(POOL_WINDOWS):
        sl = slice(g * POOL_GW, (g + 1) * POOL_GW)
        x = xp_sc[:, sl]
        wsum, k = x, 1
        while k < win:
            wsum = wsum + pltpu.roll(wsum, k, 0)
            k *= 2
        cnt = jnp.minimum(pos + 1, win).astype(F32)
        d_ref[:, sl] = (wsum[POOL_HALO:] / cnt - x[POOL_HALO:]).astype(BF16)


def _pool_kernel(u_ref, prev_ref, left_ref, d_ref, xp_sc, *, tm, blocks_per_seq, pos0):
    sb = pl.program_id(0) % blocks_per_seq
    xp_sc[:POOL_HALO, :] = jnp.where(sb == 0, left_ref[...], prev_ref[...])
    xp_sc[POOL_HALO:, :] = u_ref[...]
    _pool_windows(xp_sc, pos0 + sb * tm, tm, d_ref)


def _pool(u, left, tm, blocks_per_seq, pos0):
    t = u.shape[0]
    halo_per_blk = tm // POOL_HALO
    kern = functools.partial(_pool_kernel, tm=tm, blocks_per_seq=blocks_per_seq, pos0=pos0)
    vmem = 2 * (tm * POOL_W * 4 + 2 * POOL_HALO * POOL_W * 4 + tm * POOL_W * 2) + (tm + POOL_HALO) * POOL_W * 4
    vmem += 6 * tm * POOL_GW * 4
    return pl.pallas_call(
        kern,
        grid=(t // tm,),
        in_specs=[pl.BlockSpec((tm, POOL_W), lambda i: (i, 0)),
                  pl.BlockSpec((POOL_HALO, POOL_W), lambda i: (jnp.maximum(i * halo_per_blk - 1, 0), 0)),
                  pl.BlockSpec((None, POOL_HALO, POOL_W), lambda i: (i // blocks_per_seq, 0, 0))],
        out_specs=pl.BlockSpec((tm, POOL_W), lambda i: (i, 0)),
        out_shape=jax.ShapeDtypeStruct((t, POOL_W), BF16),
        scratch_shapes=[pltpu.VMEM((tm + POOL_HALO, POOL_W), F32)],
        compiler_params=_params(("parallel",), vmem),
        name="pool_windows",
    )(u, u, left)


def _merge_kernel(x_ref, gmix_ref, wg0_ref, wg1_ref, wg2_ref, bg0_ref, bg1_ref, bg2_ref,
                  d_ref, wpool_ref, pscale_ref, o_ref, wmla_ref, am_ref, wmem_ref,
                  out_ref, *rest):
    *w_copies, h_sc = rest
    wg_copies, (wpool_copy, wmla_copy, wmem_copy) = (w_copies[:3], w_copies[3:]) if w_copies else ((None,) * 3,) * 2

    @pl.when(pl.program_id(1) == 0)
    def _():
        h_sc[...] = _rms(x_ref[...], gmix_ref[...]).astype(BF16)

    h = h_sc[...]
    y_pool = _dot(d_ref[...], _as_bf16(wpool_ref, wpool_copy)) * pscale_ref[...]
    y_mla = _dot(o_ref[...], _as_bf16(wmla_ref, wmla_copy))
    y_mem = _dot(am_ref[...], _as_bf16(wmem_ref, wmem_copy))
    gate = lambda w_ref, copy, b_ref: jax.nn.sigmoid(_dot_nt(h, _as_bf16(w_ref, copy)) + b_ref[...])
    merged = gate(wg0_ref, wg_copies[0], bg0_ref) * y_pool
    merged += gate(wg1_ref, wg_copies[1], bg1_ref) * y_mla
    merged += gate(wg2_ref, wg_copies[2], bg2_ref) * y_mem
    out_ref[...] = merged.astype(BF16)


def _merge(x, gmix, wgates, bgate, d, wpool, pscale, o, wmla, am, wmem, tm, tn, emit_bf16=False):
    t = x.shape[0]
    assert not emit_bf16 or t == tm
    assert POOL_OUT_GW % tn == 0
    nj = D_MODEL // tn
    per_group = POOL_OUT_GW // tn
    rowfull = lambda w: pl.BlockSpec((tm, w), lambda i, j: (i, 0))
    gate_rows = pl.BlockSpec((tn, D_MODEL), lambda i, j: (j, 0))
    if emit_bf16:
        gate_in = [pl.BlockSpec((pl.Element(tn), pl.Element(D_MODEL)),
                                lambda i, j, b=b: (pl.multiple_of(W_IN_GATE + (b * nj + j) * tn, BF16_SUBLANES), 0))
                   for b in range(3)]
        gate_args = (wgates,) * 3
    else:
        gate_in = [gate_rows] * 3
        gate_args = tuple(wgates)
    gate_b = lambda b: pl.BlockSpec((1, tn), lambda i, j: (0, b * nj + j))
    col = lambda k: pl.BlockSpec((k, tn), lambda i, j: (0, j))
    pool_spec = pl.BlockSpec((None, POOL_GW, tn), lambda i, j: (j // per_group, 0, j % per_group))
    w_bytes = wmla.dtype.itemsize + 2 * emit_bf16
    w_rows = 3 * D_MODEL + POOL_GW + D_MODEL + MEM_W
    vmem = 2 * (tm * D_MODEL * 4 + w_rows * tn * w_bytes + tm * POOL_GW * 2 + tm * D_MODEL * 2
                + tm * MEM_W * 2 + tm * tn * 2)
    vmem += tm * D_MODEL * 2 + tm * D_MODEL * 4 + 8 * tm * tn * 4 + w_rows * tn * 2 * emit_bf16
    y_spec = pl.BlockSpec((tm, tn), lambda i, j: (i, j))
    y_shape = jax.ShapeDtypeStruct((t, D_MODEL), BF16)
    copies = ([jax.ShapeDtypeStruct((D_MODEL, D_MODEL), BF16)] * 3
              + [jax.ShapeDtypeStruct(w.shape, BF16) for w in (wpool, wmla, wmem)])
    return pl.pallas_call(
        _merge_kernel,
        grid=(t // tm, nj),
        in_specs=[rowfull(D_MODEL), pl.BlockSpec((1, D_MODEL), lambda i, j: (0, 0))] + gate_in
                 + [gate_b(0), gate_b(1), gate_b(2),
                    pl.BlockSpec((tm, POOL_GW), lambda i, j: (i, j // per_group)), pool_spec,
                    pl.BlockSpec((1, tn), lambda i, j: (0, j)),
                    rowfull(D_MODEL), col(D_MODEL), rowfull(MEM_W), col(MEM_W)],
        out_specs=[y_spec] + [gate_rows] * 3 + [pool_spec, col(D_MODEL), col(MEM_W)] if emit_bf16 else y_spec,
        out_shape=[y_shape] + copies if emit_bf16 else y_shape,
        scratch_shapes=[pltpu.VMEM((tm, D_MODEL), BF16)],
        compiler_params=_params(("parallel", "arbitrary"), vmem),
        name="gated_merge",
    )(x, gmix, *gate_args, bgate, bgate, bgate, d, wpool, pscale, o, wmla, am, wmem)


def _out_proj_kernel(x_ref, m_ref, w_ref, y_ref, *w_copy):
    y_ref[...] = x_ref[...] + _dot(m_ref[...], _as_bf16(w_ref, *w_copy or (None,)))


def _out_proj(x, merged, w, tm, tn, emit_bf16=False):
    t = x.shape[0]
    assert not emit_bf16 or t == tm
    w_spec = pl.BlockSpec((D_MODEL, tn), lambda i, j: (0, j))
    vmem = 2 * (2 * tm * tn * 4 + tm * D_MODEL * 2 + D_MODEL * tn * (w.dtype.itemsize + 2 * emit_bf16)) + 2 * tm * tn * 4
    y_spec = pl.BlockSpec((tm, tn), lambda i, j: (i, j))
    y_shape = jax.ShapeDtypeStruct((t, D_MODEL), F32)
    return pl.pallas_call(
        _out_proj_kernel,
        grid=(t // tm, D_MODEL // tn),
        in_specs=[pl.BlockSpec((tm, tn), lambda i, j: (i, j)),
                  pl.BlockSpec((tm, D_MODEL), lambda i, j: (i, 0)), w_spec],
        out_specs=[y_spec, w_spec] if emit_bf16 else y_spec,
        out_shape=[y_shape, jax.ShapeDtypeStruct(w.shape, BF16)] if emit_bf16 else y_shape,
        compiler_params=_params(("parallel", "arbitrary" if emit_bf16 else "parallel"), vmem),
        name="out_proj",
    )(x, merged, w)


def _ffn_kernel(x_ref, g_ref, wup_ref, wdown_ref, y_ref, *rest):
    *w_copies, h_sc, acc_sc = rest
    wup_copy, wdown_copy = w_copies or (None, None)
    j = pl.program_id(1)

    @pl.when(j == 0)
    def _():
        h_sc[...] = _rms(x_ref[...], g_ref[...]).astype(BF16)
        acc_sc[...] = jnp.zeros(acc_sc.shape, F32)

    f = jnp.maximum(_dot(h_sc[...], _as_bf16(wup_ref, wup_copy)), 0.0)
    acc_sc[...] += _dot((f * f).astype(BF16), _as_bf16(wdown_ref, wdown_copy))

    @pl.when(j == pl.num_programs(1) - 1)
    def _():
        y_ref[...] = x_ref[...] + acc_sc[...]


def _ffn(x, g, wup, wdown, tm, tf, emit_bf16=False):
    t = x.shape[0]
    assert not emit_bf16 or t == tm
    w_bytes = wup.dtype.itemsize + 2 * emit_bf16
    vmem = 2 * (2 * tm * D_MODEL * 4 + 2 * D_MODEL * tf * w_bytes) + tm * D_MODEL * 6 + 3 * tm * tf * 4 + tm * D_MODEL * 4
    vmem += 2 * D_MODEL * tf * 2 * emit_bf16
    up_spec = pl.BlockSpec((D_MODEL, tf), lambda i, j: (0, j))
    down_spec = pl.BlockSpec((tf, D_MODEL), lambda i, j: (j, 0))
    y_spec = pl.BlockSpec((tm, D_MODEL), lambda i, j: (i, 0))
    y_shape = jax.ShapeDtypeStruct((t, D_MODEL), F32)
    return pl.pallas_call(
        _ffn_kernel,
        grid=(t // tm, D_FF // tf),
        in_specs=[pl.BlockSpec((tm, D_MODEL), lambda i, j: (i, 0)),
                  pl.BlockSpec((1, D_MODEL), lambda i, j: (0, 0)), up_spec, down_spec],
        out_specs=[y_spec, up_spec, down_spec] if emit_bf16 else y_spec,
        out_shape=([y_shape, jax.ShapeDtypeStruct(wup.shape, BF16), jax.ShapeDtypeStruct(wdown.shape, BF16)]
                   if emit_bf16 else y_shape),
        scratch_shapes=[pltpu.VMEM((tm, D_MODEL), BF16), pltpu.VMEM((tm, D_MODEL), F32)],
        compiler_params=_params(("parallel", "arbitrary"), vmem),
        name="ffn",
    )(x, g, wup, wdown)


def _rope_tables(pos):
    half = QK_ROPE // 2
    inv = 1.0 / (ROPE_THETA ** (jnp.arange(half, dtype=F32) * (2.0 / QK_ROPE)))
    ang = pos.astype(F32)[:, None] * inv[None, :]
    return jnp.cos(ang), jnp.sin(ang)


def _lane_tables(cos, sin):
    z = jnp.zeros_like(cos)
    c = jnp.concatenate([cos, cos, z, z], axis=1)
    sa = jnp.concatenate([z, sin, z, z], axis=1)
    sb = jnp.concatenate([-sin, z, z, z], axis=1)
    return c, sa, sb


def _query_table(cos, sin):
    return jnp.concatenate([cos, cos, -sin, sin], axis=1)


def _row(v):
    return v.astype(F32).reshape(1, -1)


def _pad_last(a, width):
    return jnp.pad(a, [(0, 0)] * (a.ndim - 1) + [(0, width - a.shape[-1])])


def kernel(x_prompt, mem_prompt, x_sample, cache_mla_latent, cache_mla_kpe, state_pool, cache_mem_k, cache_mem_v,
           g_mix, w_in, b_gate, w_pool, pool_scale, g_q_lat, w_qb, g_q_head, g_kv_lat, w_kb, w_vb, g_k_head,
           w_mla_o, g_mem, w_mem_kv, g_mem_q, g_mem_k, w_mem_o, w_out, g_ff, w_up, w_down):
    batch, seq, _ = x_prompt.shape
    dec_batch, dec_seq, _ = x_sample.shape
    past = cache_mla_latent.shape[1]

    w_in_t = w_in.T
    bgate = _row(b_gate)
    half = QK_ROPE // 2

    def swap_halves(a):
        return jnp.concatenate([a[..., half:], a[..., :half]], axis=-1)

    def head_layout(a):
        rope = a[..., QK_NOPE:]
        return jnp.concatenate([a, swap_halves(rope)], axis=-1)

    wq = head_layout(w_qb.reshape(Q_LORA, MLA_HEADS, QK_HEAD)).reshape(Q_LORA, MLA_HEADS * HEAD_PAD).astype(BF16)
    wkbt = w_kb.T.astype(BF16)
    wvbt = w_vb.T.astype(BF16)
    gq_p = _row(head_layout(g_q_head.astype(F32) * MLA_SCALE_LOG2))
    gkn = _row(g_k_head[:QK_NOPE])
    gkr = _row(_pad_last(g_k_head[QK_NOPE:], 128))
    gq_s = _row(head_layout(jnp.concatenate([g_q_head[:QK_NOPE].astype(F32) * g_k_head[:QK_NOPE].astype(F32),
                                             g_q_head[QK_NOPE:].astype(F32)]) * MLA_SCALE_LOG2))
    gmq = _row(g_mem_q.astype(F32) * MEM_SCALE)
    gmix, gql, gkvl, gff, gmem, gmk, pscale = map(_row, (g_mix, g_q_lat, g_kv_lat, g_ff, g_mem, g_mem_k, pool_scale))

    ts = dec_batch * dec_seq
    xs = x_sample.reshape(ts, D_MODEL)
    u_s, hq_s, ckv_s, kpe_s, qm_s, wa_t, wkpe_t, wmq_t = _in_proj(xs, gmix, w_in_t, gql, gkvl, gmq)
    pos_s = past + jnp.arange(dec_seq, dtype=jnp.int32)
    cos_s, sin_s = _rope_tables(pos_s)
    q_s = _q_only(hq_s, wq, gq_s, jnp.tile(_query_table(cos_s, sin_s), (dec_batch, 1)), ts)
    qa, qr = _q_absorb(q_s, wkbt)
    cos_c, sin_c = _rope_tables(jnp.arange(past, dtype=jnp.int32))
    pad_new = 128 - dec_seq
    nlat = jnp.pad(ckv_s.reshape(dec_batch, dec_seq, KV_LORA), ((0, 0), (0, pad_new), (0, 0)))
    nkpet = jnp.pad(jnp.swapaxes(kpe_s[:, :QK_ROPE].reshape(dec_batch, dec_seq, QK_ROPE), 1, 2),
                    ((0, 0), (0, 0), (0, pad_new)))
    o_lat = _cache_attn(wkbt, qa, qr, cache_mla_latent, jnp.swapaxes(cache_mla_kpe, 1, 2),
                        cos_c.T, sin_c.T, nlat, nkpet,
                        jnp.pad(cos_s.T, ((0, 0), (0, pad_new))), jnp.pad(sin_s.T, ((0, 0), (0, pad_new))),
                        g_k_head[QK_NOPE:].astype(F32).reshape(QK_ROPE, 1),
                        dec_seq, dec_seq, 1024)
    o_s = _v_up(o_lat, wvbt, dec_seq)
    am_s = _mem_attn(qm_s, cache_mem_k.reshape(dec_batch * N_MEM, MEM_W), cache_mem_v.reshape(dec_batch * N_MEM, MEM_W),
                     0, 0, dec_seq, 1)
    left_s = jnp.pad(state_pool.astype(F32), ((0, 0), (POOL_HALO - POOL_STATE, 0), (0, 0)))
    d_s = _pool(u_s, left_s, dec_seq, 1, past)
    merged_s, *wgates, wpool, wmla, wmem_o = _merge(xs, gmix, w_in_t, bgate, d_s, w_pool, pscale, o_s, w_mla_o,
                                                    am_s, w_mem_o, ts, 256, emit_bf16=True)
    x1_s, wout = _out_proj(xs, merged_s, w_out, ts, 512, emit_bf16=True)
    y_s, wup, wdown = _ffn(x1_s, gff, w_up, w_down, ts, 512, emit_bf16=True)
    y_s = y_s.reshape(dec_batch, dec_seq, D_MODEL)
    lat_s = ckv_s.reshape(dec_batch, dec_seq, KV_LORA)
    kpe_s_out = kpe_s[:, :QK_ROPE].reshape(dec_batch, dec_seq, QK_ROPE)
    pool_s = u_s.reshape(dec_batch, dec_seq, POOL_W)[:, dec_seq - POOL_STATE:]

    tm = 512
    xp = x_prompt.reshape(batch * seq, D_MODEL)
    mem_kv = _mem_kv(mem_prompt.reshape(batch * N_MEM, D_MODEL), gmem, w_mem_kv, gmk)
    d, hq, ckv, kpe, am, u_tail = _front(xp, gmix, wa_t, wkpe_t, wmq_t, gql, gkvl, gmq,
                                         jnp.zeros((batch, POOL_HALO, POOL_W), F32), mem_kv, tm, seq // tm, 0)
    cos_p, sin_p = _rope_tables(jnp.arange(seq, dtype=jnp.int32))
    q, k, vt = _qkv(hq, ckv, kpe, wq, wkbt, wvbt, gq_p, gkn, gkr,
                    (_query_table(cos_p, sin_p),) + _lane_tables(cos_p, sin_p), tm, seq // tm)
    o = _flash(q, k, vt, batch, seq, 2048)
    merged = _merge(xp, gmix, wgates, bgate, d, wpool, pscale, o, wmla, am, wmem_o, tm, POOL_OUT_GW)
    y_p = _ffn(_out_proj(xp, merged, wout, tm, 1024), gff, wup, wdown, tm, 1024).reshape(batch, seq, D_MODEL)
    lat_p = ckv.reshape(batch, seq, KV_LORA)
    kpe_p = kpe[:, :QK_ROPE].reshape(batch, seq, QK_ROPE)
    pool_p = u_tail[:, POOL_HALO - POOL_STATE:]
    mem_k_p = mem_kv[:, :MEM_W].reshape(batch, N_MEM, MEM_HEADS, MEM_HEAD_DIM)
    mem_v_p = mem_kv[:, MEM_W:].reshape(batch, N_MEM, MEM_HEADS, MEM_HEAD_DIM)

    return (y_p, y_s, lat_p, kpe_p, pool_p, mem_k_p, mem_v_p, lat_s, kpe_s_out, pool_s)
```

```python
import functools

import jax
import jax.numpy as jnp
from jax import lax
from jax.experimental import pallas as pl
from jax.experimental.pallas import tpu as pltpu

F32 = jnp.float32
BF16 = jnp.bfloat16

EPS = 1e-6
CHUNK = 64
D_MODEL = 2048
POOL_WINDOWS = (2, 4, 8, 16)
POOL_W = 1024
POOL_GW = 256
POOL_OUT_GW = 512
POOL_STATE = 15
POOL_HALO = 16
assert all(w & (w - 1) == 0 and w <= POOL_HALO for w in POOL_WINDOWS)
MLA_HEADS = 16
Q_LORA = 512
KV_LORA = 512
QK_NOPE = 128
QK_ROPE = 64
QK_HEAD = QK_NOPE + QK_ROPE
HEAD_PAD = 256
V_HEAD = 128
V_ROWS = 144
ROPE_THETA = 10000.0
LOG2_E = 1.4426950408889634
MLA_SCALE = QK_HEAD ** -0.5
MLA_SCALE_LOG2 = MLA_SCALE * LOG2_E
N_MEM = 256
MEM_HEADS = 4
MEM_HEAD_DIM = 256
MEM_W = MEM_HEADS * MEM_HEAD_DIM
MEM_SCALE = MEM_HEAD_DIM ** -0.5
D_FF = 4 * D_MODEL
NEG_INF = -1e30

Z_U = 0
Z_QLAT = Z_U + POOL_W
Z_KVLAT = Z_QLAT + Q_LORA
Z_MQ = Z_KVLAT + KV_LORA
W_IN_KPE = Z_MQ
W_IN_MQ = W_IN_KPE + QK_ROPE
W_IN_GATE = W_IN_MQ + MEM_W

BF16_SUBLANES = 16
V7X_VMEM_BYTES = 64 * 1024 * 1024
VMEM_CEILING = V7X_VMEM_BYTES - 6 * 1024 * 1024


def _params(semantics, vmem_bytes):
    return pltpu.CompilerParams(dimension_semantics=semantics,
                                vmem_limit_bytes=int(min(vmem_bytes, VMEM_CEILING)))


def _as_bf16(w_ref, copy_ref):
    w = w_ref[...].astype(BF16)
    if copy_ref is not None:
        copy_ref[...] = w
    return w


def _rms(x, g):
    return x * lax.rsqrt(jnp.mean(x * x, axis=-1, keepdims=True) + EPS) * g


def _dot(a, b):
    return jnp.dot(a, b, preferred_element_type=F32)


def _dot_nt(a, b):
    return lax.dot_general(a, b, (((1,), (1,)), ((), ())), preferred_element_type=F32)


def _rope128(x, c, sa, sb):
    return x * c + pltpu.roll(x, 32, 1) * sa + pltpu.roll(x, 96, 1) * sb


def _in_proj_kernel(x_ref, gmix_ref, wa_ref, wkpe_ref, wmq_ref, gq_ref, gkv_ref, gmq_ref,
                    h_ref, u_ref, hq_ref, ckv_ref, kpe_ref, qm_ref, wa_copy, wkpe_copy, wmq_copy):
    def rows(ref, copy, lo, hi):
        w = ref[lo:hi, :].astype(BF16)
        copy[lo:hi, :] = w
        return w

    h = _rms(x_ref[...], gmix_ref[...]).astype(BF16)
    h_ref[...] = h
    u_ref[...] = _dot_nt(h, rows(wa_ref, wa_copy, Z_U, Z_QLAT))
    hq_ref[...] = _rms(_dot_nt(h, rows(wa_ref, wa_copy, Z_QLAT, Z_KVLAT)), gq_ref[...]).astype(BF16)
    ckv_ref[...] = _rms(_dot_nt(h, rows(wa_ref, wa_copy, Z_KVLAT, Z_MQ)), gkv_ref[...])
    kpe_ref[:, :QK_ROPE] = _dot_nt(h, rows(wkpe_ref, wkpe_copy, 0, QK_ROPE))
    kpe_ref[:, QK_ROPE:] = jnp.zeros((h.shape[0], 128 - QK_ROPE), F32)
    for hh in range(MEM_HEADS):
        lo = hh * MEM_HEAD_DIM
        m = _dot_nt(h, rows(wmq_ref, wmq_copy, lo, lo + MEM_HEAD_DIM))
        qm_ref[:, lo:lo + MEM_HEAD_DIM] = _rms(m, gmq_ref[...]).astype(BF16)


def _w_in_rows(n, start):
    return pl.BlockSpec((pl.Element(n), pl.Element(D_MODEL)),
                        lambda i: (pl.multiple_of(start + 0 * i, BF16_SUBLANES), 0))


def _front_kernel(x_ref, gmix_ref, wa_ref, wkpe_ref, wmq_ref, gq_ref, gkv_ref, gmq_ref, left_ref, mk_ref, mv_ref,
                  h_ref, d_ref, hq_ref, ckv_ref, kpe_ref, am_ref, utail_ref, xp_sc, qm_sc,
                  *, tm, blocks_per_seq, pos0):
    sb = pl.program_id(0) % blocks_per_seq
    h = _rms(x_ref[...], gmix_ref[...]).astype(BF16)
    h_ref[...] = h

    @pl.when(sb == 0)
    def _():
        xp_sc[:POOL_HALO, :] = left_ref[...]

    @pl.when(sb != 0)
    def _():
        xp_sc[:POOL_HALO, :] = xp_sc[tm:, :]

    xp_sc[POOL_HALO:, :] = _dot_nt(h, wa_ref[Z_U:Z_QLAT, :])
    utail_ref[...] = xp_sc[tm:, :]
    hq_ref[...] = _rms(_dot_nt(h, wa_ref[Z_QLAT:Z_KVLAT, :]), gq_ref[...]).astype(BF16)
    ckv_ref[...] = _rms(_dot_nt(h, wa_ref[Z_KVLAT:Z_MQ, :]), gkv_ref[...])
    kpe_ref[:, :QK_ROPE] = _dot_nt(h, wkpe_ref[...])
    kpe_ref[:, QK_ROPE:] = jnp.zeros((tm, 128 - QK_ROPE), F32)
    for hh in range(MEM_HEADS):
        lo = hh * MEM_HEAD_DIM
        m = _dot_nt(h, wmq_ref[lo:lo + MEM_HEAD_DIM, :])
        qm_sc[:, lo:lo + MEM_HEAD_DIM] = _rms(m, gmq_ref[...]).astype(BF16)
    _pool_windows(xp_sc, pos0 + sb * tm, tm, d_ref)
    _mem_attn_kernel(qm_sc, mk_ref, mv_ref, am_ref)


def _front(x, gmix, wa_t, wkpe_t, wmq_t, gq, gkv, gmq_scaled, left, mem_kv, tm, blocks_per_seq, pos0):
    t = x.shape[0]
    n_seq = t // (tm * blocks_per_seq)
    row = lambda w: pl.BlockSpec((tm, w), lambda i: (i, 0))
    full = lambda a: pl.BlockSpec(a.shape, lambda i: (0, 0))
    w_elems = wa_t.size + wkpe_t.size + wmq_t.size
    vmem = 2 * (tm * D_MODEL * 6 + w_elems * 2 + tm * (POOL_W * 2 + Q_LORA * 2 + KV_LORA * 4 + 128 * 4 + MEM_W * 2)
                + 2 * N_MEM * MEM_W * 4 + 2 * POOL_HALO * POOL_W * 4)
    vmem += (tm + POOL_HALO) * POOL_W * 4 + tm * MEM_W * 2 + 8 * tm * POOL_W * 4
    kern = functools.partial(_front_kernel, tm=tm, blocks_per_seq=blocks_per_seq, pos0=pos0)
    return pl.pallas_call(
        kern,
        grid=(t // tm,),
        in_specs=[row(D_MODEL), full(gmix), full(wa_t), full(wkpe_t), full(wmq_t),
                  full(gq), full(gkv), full(gmq_scaled),
                  pl.BlockSpec((None, POOL_HALO, POOL_W), lambda i: (i // blocks_per_seq, 0, 0)),
                  pl.BlockSpec((N_MEM, MEM_W), lambda i: (i // blocks_per_seq, 0)),
                  pl.BlockSpec((N_MEM, MEM_W), lambda i: (i // blocks_per_seq, 1))],
        out_specs=[row(D_MODEL), row(POOL_W), row(Q_LORA), row(KV_LORA), row(128), row(MEM_W),
                   pl.BlockSpec((None, POOL_HALO, POOL_W), lambda i: (i // blocks_per_seq, 0, 0))],
        out_shape=[jax.ShapeDtypeStruct((t, D_MODEL), BF16),
                   jax.ShapeDtypeStruct((t, POOL_W), BF16), jax.ShapeDtypeStruct((t, Q_LORA), BF16),
                   jax.ShapeDtypeStruct((t, KV_LORA), F32), jax.ShapeDtypeStruct((t, 128), F32),
                   jax.ShapeDtypeStruct((t, MEM_W), BF16), jax.ShapeDtypeStruct((n_seq, POOL_HALO, POOL_W), F32)],
        scratch_shapes=[pltpu.VMEM((tm + POOL_HALO, POOL_W), F32), pltpu.VMEM((tm, MEM_W), BF16)],
        compiler_params=_params(("arbitrary",), vmem),
        name="front",
    )(x, gmix, wa_t, wkpe_t, wmq_t, gq, gkv, gmq_scaled, left, mem_kv, mem_kv)


def _in_proj(x, gmix, w_in_t, gq, gkv, gmq_scaled):
    t = x.shape[0]
    row = lambda w: pl.BlockSpec((t, w), lambda i: (0, 0))
    full = lambda a: pl.BlockSpec(a.shape, lambda i: (0, 0))
    pieces = ((Z_MQ, 0), (QK_ROPE, W_IN_KPE), (MEM_W, W_IN_MQ))
    w_elems = sum(n for n, _ in pieces) * D_MODEL
    vmem = t * D_MODEL * 8 + w_elems * 6 + 2 * t * (POOL_W * 4 + Q_LORA * 2 + KV_LORA * 4 + 128 * 4 + MEM_W * 2)
    vmem += w_elems * 2 + 6 * t * POOL_W * 4
    return pl.pallas_call(
        _in_proj_kernel,
        grid=(1,),
        in_specs=[row(D_MODEL), full(gmix)] + [_w_in_rows(n, start) for n, start in pieces]
                 + [full(gq), full(gkv), full(gmq_scaled)],
        out_specs=[row(D_MODEL), row(POOL_W), row(Q_LORA), row(KV_LORA), row(128), row(MEM_W)]
                  + [pl.BlockSpec((n, D_MODEL), lambda i: (0, 0)) for n, _ in pieces],
        out_shape=[jax.ShapeDtypeStruct((t, D_MODEL), BF16),
                   jax.ShapeDtypeStruct((t, POOL_W), F32), jax.ShapeDtypeStruct((t, Q_LORA), BF16),
                   jax.ShapeDtypeStruct((t, KV_LORA), F32), jax.ShapeDtypeStruct((t, 128), F32),
                   jax.ShapeDtypeStruct((t, MEM_W), BF16)]
                  + [jax.ShapeDtypeStruct((n, D_MODEL), BF16) for n, _ in pieces],
        compiler_params=_params(("arbitrary",), vmem),
        name="in_proj",
    )(x, gmix, w_in_t, w_in_t, w_in_t, gq, gkv, gmq_scaled)


def _q_heads(hq, wq_ref, gq_ref, qtab, q_ref):
    first = lax.broadcasted_iota(jnp.int32, (1, 128), 1) < QK_ROPE
    for h in range(MLA_HEADS):
        lo = h * HEAD_PAD
        qh = _dot(hq, wq_ref[:, lo:lo + HEAD_PAD])
        qn, qr = qh[:, :QK_NOPE], qh[:, QK_NOPE:]
        ss = jnp.sum(qn * qn + jnp.where(first, qr * qr, 0.0), axis=-1, keepdims=True)
        r = lax.rsqrt(ss * (1.0 / QK_HEAD) + EPS)
        q_ref[:, lo:lo + QK_NOPE] = (qn * r * gq_ref[:, :QK_NOPE]).astype(BF16)
        q_ref[:, lo + QK_NOPE:lo + HEAD_PAD] = (qr * r * gq_ref[:, QK_NOPE:] * qtab).astype(BF16)


def _qkv_kernel(hq_ref, ckv_ref, kpe_ref, wq_ref, wkbt_ref, wvbt_ref, gq_ref, gkn_ref, gkr_ref,
                qtab_ref, c_ref, sa_ref, sb_ref, q_ref, k_ref, vt_ref):
    _q_heads(hq_ref[...], wq_ref, gq_ref, qtab_ref[...], q_ref)
    ckv = ckv_ref[...].astype(BF16)
    kpe = kpe_ref[...]
    ss_pe = jnp.sum(kpe * kpe, axis=-1, keepdims=True)
    kr = _rope128(kpe * gkr_ref[...], c_ref[...], sa_ref[...], sb_ref[...])
    kr = kr + pltpu.roll(kr, QK_ROPE, 1)
    for p in range(MLA_HEADS // 2):
        kn2 = _dot_nt(ckv, wkbt_ref[p * 2 * QK_NOPE:(p + 1) * 2 * QK_NOPE, :])
        for s in range(2):
            lo = (2 * p + s) * HEAD_PAD
            kn = kn2[:, s * QK_NOPE:(s + 1) * QK_NOPE]
            r = lax.rsqrt((jnp.sum(kn * kn, axis=-1, keepdims=True) + ss_pe) * (1.0 / QK_HEAD) + EPS)
            k_ref[:, lo:lo + QK_NOPE] = (kn * r * gkn_ref[...]).astype(BF16)
            k_ref[:, lo + QK_NOPE:lo + HEAD_PAD] = (kr * r).astype(BF16)
    vt = _dot_nt(wvbt_ref[...], ckv)
    tm = vt.shape[1]
    ones_row = (lax.broadcasted_iota(jnp.int32, (V_ROWS - V_HEAD, tm), 0) == 0).astype(BF16)
    for h in range(MLA_HEADS):
        vt_ref[h, :V_HEAD, :] = vt[h * V_HEAD:(h + 1) * V_HEAD].astype(BF16)
        vt_ref[h, V_HEAD:, :] = ones_row


def _qkv(hq, ckv, kpe, wq, wkbt, wvbt, gq, gkn, gkr, tabs, tm, seq_blocks):
    t = hq.shape[0]
    row = lambda w: pl.BlockSpec((tm, w), lambda i: (i, 0))
    full = lambda a: pl.BlockSpec(a.shape, lambda i: (0, 0))
    tab = pl.BlockSpec((tm, 128), lambda i: (i % seq_blocks, 0))
    vmem = 2 * (tm * (Q_LORA * 2 + KV_LORA * 4 + 128 * 4 + 4 * 128 * 4) + (wq.size + wkbt.size + wvbt.size) * 2
                + tm * (2 * MLA_HEADS * HEAD_PAD * 2 + MLA_HEADS * V_ROWS * 2))
    vmem += 8 * tm * HEAD_PAD * 4 + 2 * tm * MLA_HEADS * V_HEAD * 4
    return pl.pallas_call(
        _qkv_kernel,
        grid=(t // tm,),
        in_specs=[row(Q_LORA), row(KV_LORA), row(128), full(wq), full(wkbt), full(wvbt),
                  full(gq), full(gkn), full(gkr), tab, tab, tab, tab],
        out_specs=[row(MLA_HEADS * HEAD_PAD), row(MLA_HEADS * HEAD_PAD),
                   pl.BlockSpec((None, MLA_HEADS, None, V_ROWS, tm),
                                lambda i: (i // seq_blocks, 0, i % seq_blocks, 0, 0))],
        out_shape=[jax.ShapeDtypeStruct((t, MLA_HEADS * HEAD_PAD), BF16),
                   jax.ShapeDtypeStruct((t, MLA_HEADS * HEAD_PAD), BF16),
                   jax.ShapeDtypeStruct((t // (tm * seq_blocks), MLA_HEADS, seq_blocks, V_ROWS, tm), BF16)],
        compiler_params=_params(("parallel",), vmem),
        name="qkv_proj",
    )(hq, ckv, kpe, wq, wkbt, wvbt, gq, gkn, gkr, *tabs)


def _q_only_kernel(hq_ref, wq_ref, gq_ref, qtab_ref, q_ref):
    _q_heads(hq_ref[...], wq_ref, gq_ref, qtab_ref[...], q_ref)


def _q_only(hq, wq, gq, qtab, tm):
    t = hq.shape[0]
    row = lambda w: pl.BlockSpec((tm, w), lambda i: (i, 0))
    full = lambda a: pl.BlockSpec(a.shape, lambda i: (0, 0))
    vmem = 2 * (tm * (Q_LORA * 2 + 128 * 4 + MLA_HEADS * HEAD_PAD * 2) + wq.size * 2) + 8 * tm * HEAD_PAD * 4
    return pl.pallas_call(
        _q_only_kernel,
        grid=(t // tm,),
        in_specs=[row(Q_LORA), full(wq), full(gq), pl.BlockSpec((tm, 128), lambda i: (0, 0))],
        out_specs=row(MLA_HEADS * HEAD_PAD),
        out_shape=jax.ShapeDtypeStruct((t, MLA_HEADS * HEAD_PAD), BF16),
        compiler_params=_params(("parallel",), vmem),
        name="q_proj",
    )(hq, wq, gq, qtab)


def _flash_kernel(q_ref, k_ref, vt_ref, o_ref, s0_sc, s1_sc, m_sc, acc_sc, *, tq, tk):
    qi = pl.program_id(2)
    m_sc[...] = jnp.full(m_sc.shape, NEG_INF, F32)
    acc_sc[...] = jnp.zeros(acc_sc.shape, F32)

    def scores(kj, s_sc, q_lo=0):
        start = pl.multiple_of(kj * tk, tk)
        s_sc[:, q_lo:] = _dot_nt(k_ref[pl.ds(start, tk), :], q_ref[q_lo:, :])

    def consume(kj, s_sc, key_chunk0=None, q_lo=0):
        s = s_sc[:, q_lo:]
        if key_chunk0 is not None:
            kc = key_chunk0 + lax.broadcasted_iota(jnp.int32, s.shape, 0) // CHUNK
            qc = (q_lo + lax.broadcasted_iota(jnp.int32, s.shape, 1)) // CHUNK
            s = jnp.where(qc >= kc, s, NEG_INF)
        m = m_sc[:, q_lo:]
        m_new = jnp.maximum(m, jnp.max(s, axis=0, keepdims=True))
        alpha = jnp.exp2(m - m_new)
        p = jnp.exp2(s - m_new).astype(BF16)
        acc_sc[:, q_lo:] = alpha * acc_sc[:, q_lo:] + _dot(vt_ref[kj], p)
        m_sc[:, q_lo:] = m_new

    scores(0, s0_sc)

    def pair(t, carry):
        scores(2 * t + 1, s1_sc)
        consume(2 * t, s0_sc)
        scores(2 * t + 2, s0_sc)
        consume(2 * t + 1, s1_sc)
        return carry

    n_diag = tq // tk
    lax.fori_loop(0, qi * (n_diag // 2), pair, 0)
    base = qi * n_diag
    bufs = (s0_sc, s1_sc)
    for d in range(n_diag):
        if d + 1 < n_diag:
            scores(base + d + 1, bufs[(d + 1) % 2], (d + 1) * tk)
        consume(base + d, bufs[d % 2], d * tk // CHUNK, d * tk)
    acc = acc_sc[...]
    o_ref[...] = (acc[:V_HEAD] / acc[V_HEAD:V_HEAD + 1]).T.astype(BF16)


def _flash(q, k, vt, batch, seq, tq):
    tk = vt.shape[-1]
    assert tq % (2 * tk) == 0 and tk % CHUNK == 0 and seq % tq == 0
    nq = seq // tq
    vmem = 2 * (tq * HEAD_PAD * 2 + seq * HEAD_PAD * 2 + seq * V_ROWS * 2 + tq * V_HEAD * 2)
    vmem += 2 * tk * tq * 4 + V_ROWS * tq * 4 + 4 * tk * tq * 4
    return pl.pallas_call(
        functools.partial(_flash_kernel, tq=tq, tk=tk),
        grid=(batch, MLA_HEADS, nq),
        in_specs=[pl.BlockSpec((tq, HEAD_PAD), lambda b, h, i: (b * nq + i, h)),
                  pl.BlockSpec((seq, HEAD_PAD), lambda b, h, i: (b, h)),
                  pl.BlockSpec((None, None, seq // tk, V_ROWS, tk), lambda b, h, i: (b, h, 0, 0, 0))],
        out_specs=pl.BlockSpec((tq, V_HEAD), lambda b, h, i: (b * nq + i, h)),
        out_shape=jax.ShapeDtypeStruct((batch * seq, MLA_HEADS * V_HEAD), BF16),
        scratch_shapes=[pltpu.VMEM((tk, tq), F32), pltpu.VMEM((tk, tq), F32),
                        pltpu.VMEM((1, tq), F32), pltpu.VMEM((V_ROWS, tq), F32)],
        compiler_params=_params(("parallel", "parallel", "arbitrary"), vmem),
        name="mla_flash",
    )(q, k, vt)


def _q_absorb_kernel(q_ref, wkbt_ref, qa_ref, qr_ref):
    qa_ref[...] = _dot(q_ref[:, :QK_NOPE], wkbt_ref[...]).astype(BF16)
    qr_ref[...] = q_ref[:, QK_NOPE:]


def _q_absorb(q, wkbt):
    t = q.shape[0]
    vmem = 2 * (t * HEAD_PAD * 2 + KV_LORA * QK_NOPE * 2 + t * KV_LORA * 2 + t * 128 * 2) + 2 * t * KV_LORA * 4
    return pl.pallas_call(
        _q_absorb_kernel,
        grid=(MLA_HEADS,),
        in_specs=[pl.BlockSpec((t, HEAD_PAD), lambda h: (0, h)),
                  pl.BlockSpec((QK_NOPE, KV_LORA), lambda h: (h, 0))],
        out_specs=[pl.BlockSpec((None, t, KV_LORA), lambda h: (h, 0, 0)),
                   pl.BlockSpec((None, t, 128), lambda h: (h, 0, 0))],
        out_shape=[jax.ShapeDtypeStruct((MLA_HEADS, t, KV_LORA), BF16),
                   jax.ShapeDtypeStruct((MLA_HEADS, t, 128), BF16)],
        compiler_params=_params(("parallel",), vmem),
        name="q_absorb",
    )(q, wkbt)


def _cache_attn_kernel(wkbt_ref, qa_ref, qr_ref, lat_ref, kpet_ref, ct_ref, st_ref,
                       nlat_ref, nkpet_ref, nct_ref, nst_ref, gkr_ref,
                       o_ref, s_sc, m_sc, l_sc, acc_sc, *, nq, n_new, past, tk):
    j = pl.program_id(1)
    last = pl.num_programs(1) - 1
    rows = MLA_HEADS * nq
    half = QK_ROPE // 2

    @pl.when(j == 0)
    def _():
        m_sc[...] = jnp.full(m_sc.shape, NEG_INF, F32)
        l_sc[...] = jnp.zeros(l_sc.shape, F32)
        acc_sc[...] = jnp.zeros(acc_sc.shape, F32)

    def block(lat, kpet, ct, st, width, visible):
        c = lat.astype(BF16)
        kt = _dot_nt(wkbt_ref[...], c)
        s = _dot_nt(qa_ref[...].reshape(rows, KV_LORA), c)
        ss_pe = jnp.sum(kpet * kpet, axis=0, keepdims=True)
        kr = kpet * gkr_ref[...]
        k1, k2 = kr[:half], kr[half:]
        o1, o2 = k1 * ct - k2 * st, k2 * ct + k1 * st
        krot = jnp.concatenate([o1, o2, o1, o2], axis=0).astype(BF16)
        s = s + _dot(qr_ref[...].reshape(rows, 128), krot)
        for h in range(MLA_HEADS):
            kh = kt[h * QK_NOPE:(h + 1) * QK_NOPE]
            r = lax.rsqrt((jnp.sum(kh * kh, axis=0, keepdims=True) + ss_pe) * (1.0 / QK_HEAD) + EPS)
            s_sc[h * nq:(h + 1) * nq, :width] = s[h * nq:(h + 1) * nq] * r
        s = s_sc[:, :width]
        if visible is not None:
            s = jnp.where(visible, s, NEG_INF)
        m = m_sc[...]
        m_new = jnp.maximum(m, jnp.max(s, axis=-1, keepdims=True))
        alpha = jnp.exp2(m - m_new)
        p = jnp.exp2(s - m_new)
        l_sc[...] = alpha * l_sc[...] + jnp.sum(p, axis=-1, keepdims=True)
        m_sc[...] = m_new
        acc_sc[...] = alpha * acc_sc[...] + _dot(p.astype(BF16), c)

    @pl.when(j < last)
    def _():
        block(lat_ref[...], kpet_ref[...], ct_ref[...], st_ref[...], tk, None)

    @pl.when(j == last)
    def _():
        kk = lax.broadcasted_iota(jnp.int32, (1, 128), 1)
        visible = kk < n_new
        if (past + n_new - 1) // CHUNK != past // CHUNK:
            qpos = past + lax.broadcasted_iota(jnp.int32, (rows, 1), 0) % nq
            visible = visible & (qpos // CHUNK >= (past + kk) // CHUNK)
        block(nlat_ref[...], nkpet_ref[...], nct_ref[...], nst_ref[...], 128, visible)
        o_ref[...] = (acc_sc[...] / l_sc[...]).astype(BF16)


def _cache_attn(wkbt, qa, qr, lat, kpet, ct, st, nlat, nkpet, nct, nst, gkr, nq, n_new, tk):
    batch, past, _ = lat.shape
    assert past % tk == 0 and past >= tk
    nkb = past // tk
    rows = MLA_HEADS * nq
    cj = lambda j: jnp.minimum(j, nkb - 1)
    kern = functools.partial(_cache_attn_kernel, nq=nq, n_new=n_new, past=past, tk=tk)
    vmem = 2 * (wkbt.size * 2 + rows * (KV_LORA + 128) * 2 + tk * KV_LORA * 4 + 3 * 64 * tk * 4 + rows * KV_LORA * 2)
    vmem += (MLA_HEADS * QK_NOPE + 4 * rows) * tk * 4 + tk * KV_LORA * 2 + rows * KV_LORA * 8
    return pl.pallas_call(
        kern,
        grid=(batch, nkb + 1),
        in_specs=[pl.BlockSpec(wkbt.shape, lambda b, j: (0, 0)),
                  pl.BlockSpec((MLA_HEADS, nq, KV_LORA), lambda b, j: (0, b, 0)),
                  pl.BlockSpec((MLA_HEADS, nq, 128), lambda b, j: (0, b, 0)),
                  pl.BlockSpec((None, tk, KV_LORA), lambda b, j: (b, cj(j), 0)),
                  pl.BlockSpec((None, QK_ROPE, tk), lambda b, j: (b, 0, cj(j))),
                  pl.BlockSpec((QK_ROPE // 2, tk), lambda b, j: (0, cj(j))),
                  pl.BlockSpec((QK_ROPE // 2, tk), lambda b, j: (0, cj(j))),
                  pl.BlockSpec((None, 128, KV_LORA), lambda b, j: (b, 0, 0)),
                  pl.BlockSpec((None, QK_ROPE, 128), lambda b, j: (b, 0, 0)),
                  pl.BlockSpec((QK_ROPE // 2, 128), lambda b, j: (0, 0)),
                  pl.BlockSpec((QK_ROPE // 2, 128), lambda b, j: (0, 0)),
                  pl.BlockSpec((QK_ROPE, 1), lambda b, j: (0, 0))],
        out_specs=pl.BlockSpec((None, rows, KV_LORA), lambda b, j: (b, 0, 0)),
        out_shape=jax.ShapeDtypeStruct((batch, rows, KV_LORA), BF16),
        scratch_shapes=[pltpu.VMEM((rows, tk), F32), pltpu.VMEM((rows, 1), F32),
                        pltpu.VMEM((rows, 1), F32), pltpu.VMEM((rows, KV_LORA), F32)],
        compiler_params=_params(("parallel", "arbitrary"), vmem),
        name="mla_cache_attn",
    )(wkbt, qa, qr, lat, kpet, ct, st, nlat, nkpet, nct, nst, gkr)


def _v_up_kernel(ol_ref, wvbt_ref, o_ref):
    b, nq, _ = ol_ref.shape
    o_ref[...] = _dot_nt(ol_ref[...].reshape(b * nq, KV_LORA), wvbt_ref[...]).astype(BF16)


def _v_up(o_lat, wvbt, nq):
    batch = o_lat.shape[0]
    t = batch * nq
    vmem = 2 * (t * KV_LORA * 2 + KV_LORA * V_HEAD * 2 + t * V_HEAD * 2) + 2 * t * V_HEAD * 4
    return pl.pallas_call(
        _v_up_kernel,
        grid=(MLA_HEADS,),
        in_specs=[pl.BlockSpec((batch, nq, KV_LORA), lambda h: (0, h, 0)),
                  pl.BlockSpec((V_HEAD, KV_LORA), lambda h: (h, 0))],
        out_specs=pl.BlockSpec((t, V_HEAD), lambda h: (0, h)),
        out_shape=jax.ShapeDtypeStruct((t, MLA_HEADS * V_HEAD), BF16),
        compiler_params=_params(("parallel",), vmem),
        name="v_up",
    )(o_lat, wvbt)


def _mem_kv_kernel(mem_ref, gmem_ref, w_ref, gk_ref, kv_ref):
    j = pl.program_id(0)
    h = _rms(mem_ref[...], gmem_ref[...]).astype(BF16)

    @pl.when(j == 0)
    def _():
        for hh in range(MEM_HEADS):
            lo = hh * MEM_HEAD_DIM
            w = w_ref[:, lo:lo + MEM_HEAD_DIM].astype(BF16)
            kv_ref[:, lo:lo + MEM_HEAD_DIM] = _rms(_dot(h, w), gk_ref[...])

    @pl.when(j == 1)
    def _():
        kv_ref[...] = _dot(h, w_ref[...].astype(BF16))


def _mem_kv(mem, gmem, w, gk):
    t = mem.shape[0]
    vmem = 2 * (t * D_MODEL * 4 + D_MODEL * MEM_W * w.dtype.itemsize + t * MEM_W * 4) + t * D_MODEL * 6
    vmem += t * MEM_W * 4 + D_MODEL * MEM_W * 2
    return pl.pallas_call(
        _mem_kv_kernel,
        grid=(2,),
        in_specs=[pl.BlockSpec((t, D_MODEL), lambda j: (0, 0)), pl.BlockSpec((1, D_MODEL), lambda j: (0, 0)),
                  pl.BlockSpec((D_MODEL, MEM_W), lambda j: (0, j)), pl.BlockSpec((1, MEM_HEAD_DIM), lambda j: (0, 0))],
        out_specs=pl.BlockSpec((t, MEM_W), lambda j: (0, j)),
        out_shape=jax.ShapeDtypeStruct((t, 2 * MEM_W), F32),
        compiler_params=_params(("parallel",), vmem),
        name="memory_kv",
    )(mem, gmem, w, gk)


def _mem_attn_kernel(qm_ref, k_ref, v_ref, o_ref):
    for hh in range(MEM_HEADS):
        sl = slice(hh * MEM_HEAD_DIM, (hh + 1) * MEM_HEAD_DIM)
        s = _dot_nt(qm_ref[:, sl], k_ref[:, sl].astype(BF16))
        p = jnp.exp(s - jnp.max(s, axis=-1, keepdims=True))
        l = jnp.sum(p, axis=-1, keepdims=True)
        o_ref[:, sl] = (_dot(p.astype(BF16), v_ref[:, sl].astype(BF16)) / l).astype(BF16)


def _mem_attn(qm, k_arr, v_arr, k_col, v_col, tm, blocks_per_batch):
    t = qm.shape[0]
    vmem = 2 * (2 * tm * MEM_W * 2 + 2 * N_MEM * MEM_W * 4) + 6 * tm * N_MEM * 4 + 2 * N_MEM * MEM_W * 2
    return pl.pallas_call(
        _mem_attn_kernel,
        grid=(t // tm,),
        in_specs=[pl.BlockSpec((tm, MEM_W), lambda i: (i, 0)),
                  pl.BlockSpec((N_MEM, MEM_W), lambda i: (i // blocks_per_batch, k_col)),
                  pl.BlockSpec((N_MEM, MEM_W), lambda i: (i // blocks_per_batch, v_col))],
        out_specs=pl.BlockSpec((tm, MEM_W), lambda i: (i, 0)),
        out_shape=jax.ShapeDtypeStruct((t, MEM_W), BF16),
        compiler_params=_params(("parallel",), vmem),
        name="memory_attn",
    )(qm, k_arr, v_arr)


def _pool_windows(xp_sc, first_pos, tm, d_ref):
    pos = first_pos + lax.broadcasted_iota(jnp.int32, (tm, 1), 0)
    for g, win in enumerate(POOL_WINDOWS):
        sl = slice(g * POOL_GW, (g + 1) * POOL_GW)
        x = xp_sc[:, sl]
        wsum, k = x, 1
        while k < win:
            wsum = wsum + pltpu.roll(wsum, k, 0)
            k *= 2
        cnt = jnp.minimum(pos + 1, win).astype(F32)
        d_ref[:, sl] = (wsum[POOL_HALO:] / cnt - x[POOL_HALO:]).astype(BF16)


def _pool_kernel(u_ref, prev_ref, left_ref, d_ref, xp_sc, *, tm, blocks_per_seq, pos0):
    sb = pl.program_id(0) % blocks_per_seq
    xp_sc[:POOL_HALO, :] = jnp.where(sb == 0, left_ref[...], prev_ref[...])
    xp_sc[POOL_HALO:, :] = u_ref[...]
    _pool_windows(xp_sc, pos0 + sb * tm, tm, d_ref)


def _pool(u, left, tm, blocks_per_seq, pos0):
    t = u.shape[0]
    halo_per_blk = tm // POOL_HALO
    kern = functools.partial(_pool_kernel, tm=tm, blocks_per_seq=blocks_per_seq, pos0=pos0)
    vmem = 2 * (tm * POOL_W * 4 + 2 * POOL_HALO * POOL_W * 4 + tm * POOL_W * 2) + (tm + POOL_HALO) * POOL_W * 4
    vmem += 6 * tm * POOL_GW * 4
    return pl.pallas_call(
        kern,
        grid=(t // tm,),
        in_specs=[pl.BlockSpec((tm, POOL_W), lambda i: (i, 0)),
                  pl.BlockSpec((POOL_HALO, POOL_W), lambda i: (jnp.maximum(i * halo_per_blk - 1, 0), 0)),
                  pl.BlockSpec((None, POOL_HALO, POOL_W), lambda i: (i // blocks_per_seq, 0, 0))],
        out_specs=pl.BlockSpec((tm, POOL_W), lambda i: (i, 0)),
        out_shape=jax.ShapeDtypeStruct((t, POOL_W), BF16),
        scratch_shapes=[pltpu.VMEM((tm + POOL_HALO, POOL_W), F32)],
        compiler_params=_params(("parallel",), vmem),
        name="pool_windows",
    )(u, u, left)


def _merge_kernel(h_ref, wg0_ref, wg1_ref, wg2_ref, bg0_ref, bg1_ref, bg2_ref,
                  d_ref, wpool_ref, pscale_ref, o_ref, wmla_ref, am_ref, wmem_ref,
                  out_ref, *w_copies):
    wg_copies, (wpool_copy, wmla_copy, wmem_copy) = (w_copies[:3], w_copies[3:]) if w_copies else ((None,) * 3,) * 2
    h = h_ref[...]
    y_pool = _dot(d_ref[...], _as_bf16(wpool_ref, wpool_copy)) * pscale_ref[...]
    y_mla = _dot(o_ref[...], _as_bf16(wmla_ref, wmla_copy))
    y_mem = _dot(am_ref[...], _as_bf16(wmem_ref, wmem_copy))
    gate = lambda w_ref, copy, b_ref: jax.nn.sigmoid(_dot_nt(h, _as_bf16(w_ref, copy)) + b_ref[...])
    merged = gate(wg0_ref, wg_copies[0], bg0_ref) * y_pool
    merged += gate(wg1_ref, wg_copies[1], bg1_ref) * y_mla
    merged += gate(wg2_ref, wg_copies[2], bg2_ref) * y_mem
    out_ref[...] = merged.astype(BF16)


def _merge(h, wgates, bgate, d, wpool, pscale, o, wmla, am, wmem, tm, tn, emit_bf16=False):
    t = h.shape[0]
    assert not emit_bf16 or t == tm
    assert POOL_OUT_GW % tn == 0
    nj = D_MODEL // tn
    per_group = POOL_OUT_GW // tn
    rowfull = lambda w: pl.BlockSpec((tm, w), lambda i, j: (i, 0))
    gate_rows = pl.BlockSpec((tn, D_MODEL), lambda i, j: (j, 0))
    if emit_bf16:
        gate_in = [pl.BlockSpec((pl.Element(tn), pl.Element(D_MODEL)),
                                lambda i, j, b=b: (pl.multiple_of(W_IN_GATE + (b * nj + j) * tn, BF16_SUBLANES), 0))
                   for b in range(3)]
        gate_args = (wgates,) * 3
    else:
        gate_in = [gate_rows] * 3
        gate_args = tuple(wgates)
    gate_b = lambda b: pl.BlockSpec((1, tn), lambda i, j: (0, b * nj + j))
    col = lambda k: pl.BlockSpec((k, tn), lambda i, j: (0, j))
    pool_spec = pl.BlockSpec((None, POOL_GW, tn), lambda i, j: (j // per_group, 0, j % per_group))
    w_bytes = wmla.dtype.itemsize + 2 * emit_bf16
    w_rows = 3 * D_MODEL + POOL_GW + D_MODEL + MEM_W
    vmem = 2 * (tm * D_MODEL * 2 + w_rows * tn * w_bytes + tm * POOL_GW * 2 + tm * D_MODEL * 2
                + tm * MEM_W * 2 + tm * tn * 2)
    vmem += 8 * tm * tn * 4 + w_rows * tn * 2 * emit_bf16
    y_spec = pl.BlockSpec((tm, tn), lambda i, j: (i, j))
    y_shape = jax.ShapeDtypeStruct((t, D_MODEL), BF16)
    copies = ([jax.ShapeDtypeStruct((D_MODEL, D_MODEL), BF16)] * 3
              + [jax.ShapeDtypeStruct(w.shape, BF16) for w in (wpool, wmla, wmem)])
    return pl.pallas_call(
        _merge_kernel,
        grid=(t // tm, nj),
        in_specs=[rowfull(D_MODEL)] + gate_in
                 + [gate_b(0), gate_b(1), gate_b(2),
                    pl.BlockSpec((tm, POOL_GW), lambda i, j: (i, j // per_group)), pool_spec,
                    pl.BlockSpec((1, tn), lambda i, j: (0, j)),
                    rowfull(D_MODEL), col(D_MODEL), rowfull(MEM_W), col(MEM_W)],
        out_specs=[y_spec] + [gate_rows] * 3 + [pool_spec, col(D_MODEL), col(MEM_W)] if emit_bf16 else y_spec,
        out_shape=[y_shape] + copies if emit_bf16 else y_shape,
        compiler_params=_params(("parallel", "arbitrary" if emit_bf16 else "parallel"), vmem),
        name="gated_merge",
    )(h, *gate_args, bgate, bgate, bgate, d, wpool, pscale, o, wmla, am, wmem)


def _out_proj_kernel(x_ref, m_ref, w_ref, y_ref, *w_copy):
    y_ref[...] = x_ref[...] + _dot(m_ref[...], _as_bf16(w_ref, *w_copy or (None,)))


def _out_proj(x, merged, w, tm, tn, emit_bf16=False):
    t = x.shape[0]
    assert not emit_bf16 or t == tm
    w_spec = pl.BlockSpec((D_MODEL, tn), lambda i, j: (0, j))
    vmem = 2 * (2 * tm * tn * 4 + tm * D_MODEL * 2 + D_MODEL * tn * (w.dtype.itemsize + 2 * emit_bf16)) + 2 * tm * tn * 4
    y_spec = pl.BlockSpec((tm, tn), lambda i, j: (i, j))
    y_shape = jax.ShapeDtypeStruct((t, D_MODEL), F32)
    return pl.pallas_call(
        _out_proj_kernel,
        grid=(t // tm, D_MODEL // tn),
        in_specs=[pl.BlockSpec((tm, tn), lambda i, j: (i, j)),
                  pl.BlockSpec((tm, D_MODEL), lambda i, j: (i, 0)), w_spec],
        out_specs=[y_spec, w_spec] if emit_bf16 else y_spec,
        out_shape=[y_shape, jax.ShapeDtypeStruct(w.shape, BF16)] if emit_bf16 else y_shape,
        compiler_params=_params(("parallel", "arbitrary" if emit_bf16 else "parallel"), vmem),
        name="out_proj",
    )(x, merged, w)


def _ffn_kernel(x_ref, g_ref, wup_ref, wdown_ref, y_ref, *rest):
    *w_copies, h_sc, acc_sc = rest
    wup_copy, wdown_copy = w_copies or (None, None)
    j = pl.program_id(1)

    @pl.when(j == 0)
    def _():
        h_sc[...] = _rms(x_ref[...], g_ref[...]).astype(BF16)
        acc_sc[...] = jnp.zeros(acc_sc.shape, F32)

    f = jnp.maximum(_dot(h_sc[...], _as_bf16(wup_ref, wup_copy)), 0.0)
    acc_sc[...] += _dot((f * f).astype(BF16), _as_bf16(wdown_ref, wdown_copy))

    @pl.when(j == pl.num_programs(1) - 1)
    def _():
        y_ref[...] = x_ref[...] + acc_sc[...]


def _ffn(x, g, wup, wdown, tm, tf, emit_bf16=False):
    t = x.shape[0]
    assert not emit_bf16 or t == tm
    w_bytes = wup.dtype.itemsize + 2 * emit_bf16
    vmem = 2 * (2 * tm * D_MODEL * 4 + 2 * D_MODEL * tf * w_bytes) + tm * D_MODEL * 6 + 3 * tm * tf * 4 + tm * D_MODEL * 4
    vmem += 2 * D_MODEL * tf * 2 * emit_bf16
    up_spec = pl.BlockSpec((D_MODEL, tf), lambda i, j: (0, j))
    down_spec = pl.BlockSpec((tf, D_MODEL), lambda i, j: (j, 0))
    y_spec = pl.BlockSpec((tm, D_MODEL), lambda i, j: (i, 0))
    y_shape = jax.ShapeDtypeStruct((t, D_MODEL), F32)
    return pl.pallas_call(
        _ffn_kernel,
        grid=(t // tm, D_FF // tf),
        in_specs=[pl.BlockSpec((tm, D_MODEL), lambda i, j: (i, 0)),
                  pl.BlockSpec((1, D_MODEL), lambda i, j: (0, 0)), up_spec, down_spec],
        out_specs=[y_spec, up_spec, down_spec] if emit_bf16 else y_spec,
        out_shape=([y_shape, jax.ShapeDtypeStruct(wup.shape, BF16), jax.ShapeDtypeStruct(wdown.shape, BF16)]
                   if emit_bf16 else y_shape),
        scratch_shapes=[pltpu.VMEM((tm, D_MODEL), BF16), pltpu.VMEM((tm, D_MODEL), F32)],
        compiler_params=_params(("parallel", "arbitrary"), vmem),
        name="ffn",
    )(x, g, wup, wdown)


def _rope_tables(pos):
    half = QK_ROPE // 2
    inv = 1.0 / (ROPE_THETA ** (jnp.arange(half, dtype=F32) * (2.0 / QK_ROPE)))
    ang = pos.astype(F32)[:, None] * inv[None, :]
    return jnp.cos(ang), jnp.sin(ang)


def _lane_tables(cos, sin):
    z = jnp.zeros_like(cos)
    c = jnp.concatenate([cos, cos, z, z], axis=1)
    sa = jnp.concatenate([z, sin, z, z], axis=1)
    sb = jnp.concatenate([-sin, z, z, z], axis=1)
    return c, sa, sb


def _query_table(cos, sin):
    return jnp.concatenate([cos, cos, -sin, sin], axis=1)


def _row(v):
    return v.astype(F32).reshape(1, -1)


def _pad_last(a, width):
    return jnp.pad(a, [(0, 0)] * (a.ndim - 1) + [(0, width - a.shape[-1])])


def kernel(x_prompt, mem_prompt, x_sample, cache_mla_latent, cache_mla_kpe, state_pool, cache_mem_k, cache_mem_v,
           g_mix, w_in, b_gate, w_pool, pool_scale, g_q_lat, w_qb, g_q_head, g_kv_lat, w_kb, w_vb, g_k_head,
           w_mla_o, g_mem, w_mem_kv, g_mem_q, g_mem_k, w_mem_o, w_out, g_ff, w_up, w_down):
    batch, seq, _ = x_prompt.shape
    dec_batch, dec_seq, _ = x_sample.shape
    past = cache_mla_latent.shape[1]

    w_in_t = w_in.T
    bgate = _row(b_gate)
    half = QK_ROPE // 2

    def swap_halves(a):
        return jnp.concatenate([a[..., half:], a[..., :half]], axis=-1)

    def head_layout(a):
        rope = a[..., QK_NOPE:]
        return jnp.concatenate([a, swap_halves(rope)], axis=-1)

    wq = head_layout(w_qb.reshape(Q_LORA, MLA_HEADS, QK_HEAD)).reshape(Q_LORA, MLA_HEADS * HEAD_PAD).astype(BF16)
    wkbt = w_kb.T.astype(BF16)
    wvbt = w_vb.T.astype(BF16)
    gq_p = _row(head_layout(g_q_head.astype(F32) * MLA_SCALE_LOG2))
    gkn = _row(g_k_head[:QK_NOPE])
    gkr = _row(_pad_last(g_k_head[QK_NOPE:], 128))
    gq_s = _row(head_layout(jnp.concatenate([g_q_head[:QK_NOPE].astype(F32) * g_k_head[:QK_NOPE].astype(F32),
                                             g_q_head[QK_NOPE:].astype(F32)]) * MLA_SCALE_LOG2))
    gmq = _row(g_mem_q.astype(F32) * MEM_SCALE)
    gmix, gql, gkvl, gff, gmem, gmk, pscale = map(_row, (g_mix, g_q_lat, g_kv_lat, g_ff, g_mem, g_mem_k, pool_scale))

    ts = dec_batch * dec_seq
    xs = x_sample.reshape(ts, D_MODEL)
    h_s, u_s, hq_s, ckv_s, kpe_s, qm_s, wa_t, wkpe_t, wmq_t = _in_proj(xs, gmix, w_in_t, gql, gkvl, gmq)
    pos_s = past + jnp.arange(dec_seq, dtype=jnp.int32)
    cos_s, sin_s = _rope_tables(pos_s)
    q_s = _q_only(hq_s, wq, gq_s, jnp.tile(_query_table(cos_s, sin_s), (dec_batch, 1)), ts)
    qa, qr = _q_absorb(q_s, wkbt)
    cos_c, sin_c = _rope_tables(jnp.arange(past, dtype=jnp.int32))
    pad_new = 128 - dec_seq
    nlat = jnp.pad(ckv_s.reshape(dec_batch, dec_seq, KV_LORA), ((0, 0), (0, pad_new), (0, 0)))
    nkpet = jnp.pad(jnp.swapaxes(kpe_s[:, :QK_ROPE].reshape(dec_batch, dec_seq, QK_ROPE), 1, 2),
                    ((0, 0), (0, 0), (0, pad_new)))
    o_lat = _cache_attn(wkbt, qa, qr, cache_mla_latent, jnp.swapaxes(cache_mla_kpe, 1, 2),
                        cos_c.T, sin_c.T, nlat, nkpet,
                        jnp.pad(cos_s.T, ((0, 0), (0, pad_new))), jnp.pad(sin_s.T, ((0, 0), (0, pad_new))),
                        g_k_head[QK_NOPE:].astype(F32).reshape(QK_ROPE, 1),
                        dec_seq, dec_seq, 1024)
    o_s = _v_up(o_lat, wvbt, dec_seq)
    am_s = _mem_attn(qm_s, cache_mem_k.reshape(dec_batch * N_MEM, MEM_W), cache_mem_v.reshape(dec_batch * N_MEM, MEM_W),
                     0, 0, dec_seq, 1)
    left_s = jnp.pad(state_pool.astype(F32), ((0, 0), (POOL_HALO - POOL_STATE, 0), (0, 0)))
    d_s = _pool(u_s, left_s, dec_seq, 1, past)
    merged_s, *wgates, wpool, wmla, wmem_o = _merge(h_s, w_in_t, bgate, d_s, w_pool, pscale, o_s, w_mla_o,
                                                    am_s, w_mem_o, ts, 256, emit_bf16=True)
    x1_s, wout = _out_proj(xs, merged_s, w_out, ts, 512, emit_bf16=True)
    y_s, wup, wdown = _ffn(x1_s, gff, w_up, w_down, ts, 512, emit_bf16=True)
    y_s = y_s.reshape(dec_batch, dec_seq, D_MODEL)
    lat_s = ckv_s.reshape(dec_batch, dec_seq, KV_LORA)
    kpe_s_out = kpe_s[:, :QK_ROPE].reshape(dec_batch, dec_seq, QK_ROPE)
    pool_s = u_s.reshape(dec_batch, dec_seq, POOL_W)[:, dec_seq - POOL_STATE:]

    tm = 512
    xp = x_prompt.reshape(batch * seq, D_MODEL)
    mem_kv = _mem_kv(mem_prompt.reshape(batch * N_MEM, D_MODEL), gmem, w_mem_kv, gmk)
    h, d, hq, ckv, kpe, am, u_tail = _front(xp, gmix, wa_t, wkpe_t, wmq_t, gql, gkvl, gmq,
                                            jnp.zeros((batch, POOL_HALO, POOL_W), F32), mem_kv, tm, seq // tm, 0)
    cos_p, sin_p = _rope_tables(jnp.arange(seq, dtype=jnp.int32))
    q, k, vt = _qkv(hq, ckv, kpe, wq, wkbt, wvbt, gq_p, gkn, gkr,
                    (_query_table(cos_p, sin_p),) + _lane_tables(cos_p, sin_p), tm, seq // tm)
    o = _flash(q, k, vt, batch, seq, 2048)
    merged = _merge(h, wgates, bgate, d, wpool, pscale, o, wmla, am, wmem_o, 2 * tm, POOL_OUT_GW)
    y_p = _ffn(_out_proj(xp, merged, wout, 2 * tm, 1024), gff, wup, wdown, tm, 1024).reshape(batch, seq, D_MODEL)
    lat_p = ckv.reshape(batch, seq, KV_LORA)
    kpe_p = kpe[:, :QK_ROPE].reshape(batch, seq, QK_ROPE)
    pool_p = u_tail[:, POOL_HALO - POOL_STATE:]
    mem_k_p = mem_kv[:, :MEM_W].reshape(batch, N_MEM, MEM_HEADS, MEM_HEAD_DIM)
    mem_v_p = mem_kv[:, MEM_W:].reshape(batch, N_MEM, MEM_HEADS, MEM_HEAD_DIM)

    return (y_p, y_s, lat_p, kpe_p, pool_p, mem_k_p, mem_v_p, lat_s, kpe_s_out, pool_s)
```

```python
import functools

import jax
import jax.numpy as jnp
from jax import lax
from jax.experimental import pallas as pl
from jax.experimental.pallas import tpu as pltpu

F32 = jnp.float32
BF16 = jnp.bfloat16

EPS = 1e-6
CHUNK = 64
D_MODEL = 2048
POOL_WINDOWS = (2, 4, 8, 16)
POOL_W = 1024
POOL_GW = 256
POOL_OUT_GW = 512
POOL_STATE = 15
POOL_HALO = 16
assert all(w & (w - 1) == 0 and w <= POOL_HALO for w in POOL_WINDOWS)
MLA_HEADS = 16
Q_LORA = 512
KV_LORA = 512
QK_NOPE = 128
QK_ROPE = 64
QK_HEAD = QK_NOPE + QK_ROPE
HEAD_PAD = 256
V_HEAD = 128
V_ROWS = 144
ROPE_THETA = 10000.0
LOG2_E = 1.4426950408889634
MLA_SCALE = QK_HEAD ** -0.5
MLA_SCALE_LOG2 = MLA_SCALE * LOG2_E
N_MEM = 256
MEM_HEADS = 4
MEM_HEAD_DIM = 256
MEM_W = MEM_HEADS * MEM_HEAD_DIM
MEM_SCALE = MEM_HEAD_DIM ** -0.5
D_FF = 4 * D_MODEL
NEG_INF = -1e30

Z_U = 0
Z_QLAT = Z_U + POOL_W
Z_KVLAT = Z_QLAT + Q_LORA
Z_MQ = Z_KVLAT + KV_LORA
W_IN_KPE = Z_MQ
W_IN_MQ = W_IN_KPE + QK_ROPE
W_IN_GATE = W_IN_MQ + MEM_W

BF16_SUBLANES = 16
V7X_VMEM_BYTES = 64 * 1024 * 1024
VMEM_CEILING = V7X_VMEM_BYTES - 6 * 1024 * 1024


def _params(semantics, vmem_bytes):
    return pltpu.CompilerParams(dimension_semantics=semantics,
                                vmem_limit_bytes=int(min(vmem_bytes, VMEM_CEILING)))


def _as_bf16(w_ref, copy_ref):
    w = w_ref[...].astype(BF16)
    if copy_ref is not None:
        copy_ref[...] = w
    return w


def _rms(x, g):
    return x * lax.rsqrt(jnp.mean(x * x, axis=-1, keepdims=True) + EPS) * g


def _dot(a, b):
    return jnp.dot(a, b, preferred_element_type=F32)


def _dot_nt(a, b):
    return lax.dot_general(a, b, (((1,), (1,)), ((), ())), preferred_element_type=F32)


def _rope128(x, c, sa, sb):
    half = QK_ROPE // 2
    return x * c + pltpu.roll(x, half, 1) * sa + pltpu.roll(x, x.shape[1] - half, 1) * sb


def _in_proj_kernel(x_ref, gmix_ref, wa_ref, wkpe_ref, wmq_ref, gq_ref, gkv_ref, gmq_ref,
                    h_ref, u_ref, hq_ref, ckv_ref, kpe_ref, qm_ref, wa_copy, wkpe_copy, wmq_copy):
    def rows(ref, copy, lo, hi):
        w = ref[lo:hi, :].astype(BF16)
        copy[lo:hi, :] = w
        return w

    h = _rms(x_ref[...], gmix_ref[...]).astype(BF16)
    h_ref[...] = h
    u_ref[...] = _dot_nt(h, rows(wa_ref, wa_copy, Z_U, Z_QLAT))
    hq_ref[...] = _rms(_dot_nt(h, rows(wa_ref, wa_copy, Z_QLAT, Z_KVLAT)), gq_ref[...]).astype(BF16)
    ckv_ref[...] = _rms(_dot_nt(h, rows(wa_ref, wa_copy, Z_KVLAT, Z_MQ)), gkv_ref[...])
    kpe_ref[:, :QK_ROPE] = _dot_nt(h, rows(wkpe_ref, wkpe_copy, 0, QK_ROPE))
    kpe_ref[:, QK_ROPE:] = jnp.zeros((h.shape[0], 128 - QK_ROPE), F32)
    for hh in range(MEM_HEADS):
        lo = hh * MEM_HEAD_DIM
        m = _dot_nt(h, rows(wmq_ref, wmq_copy, lo, lo + MEM_HEAD_DIM))
        qm_ref[:, lo:lo + MEM_HEAD_DIM] = _rms(m, gmq_ref[...]).astype(BF16)


def _w_in_rows(n, start):
    return pl.BlockSpec((pl.Element(n), pl.Element(D_MODEL)),
                        lambda i: (pl.multiple_of(start + 0 * i, BF16_SUBLANES), 0))


def _front_kernel(x_ref, gmix_ref, wa_ref, wkpe_ref, wmq_ref, gq_ref, gkv_ref, gmq_ref, left_ref, mk_ref, mv_ref,
                  h_ref, d_ref, hq_ref, ckv_ref, kpe_ref, am_ref, utail_ref, xp_sc, qm_sc,
                  *, tm, blocks_per_seq, pos0):
    sb = pl.program_id(0) % blocks_per_seq
    h = _rms(x_ref[...], gmix_ref[...]).astype(BF16)
    h_ref[...] = h

    @pl.when(sb == 0)
    def _():
        xp_sc[:POOL_HALO, :] = left_ref[...]

    @pl.when(sb != 0)
    def _():
        xp_sc[:POOL_HALO, :] = xp_sc[tm:, :]

    xp_sc[POOL_HALO:, :] = _dot_nt(h, wa_ref[Z_U:Z_QLAT, :])
    utail_ref[...] = xp_sc[tm:, :]
    hq_ref[...] = _rms(_dot_nt(h, wa_ref[Z_QLAT:Z_KVLAT, :]), gq_ref[...]).astype(BF16)
    ckv_ref[...] = _rms(_dot_nt(h, wa_ref[Z_KVLAT:Z_MQ, :]), gkv_ref[...])
    kpe_ref[:, :QK_ROPE] = _dot_nt(h, wkpe_ref[...])
    kpe_ref[:, QK_ROPE:] = jnp.zeros((tm, 128 - QK_ROPE), F32)
    for hh in range(MEM_HEADS):
        lo = hh * MEM_HEAD_DIM
        m = _dot_nt(h, wmq_ref[lo:lo + MEM_HEAD_DIM, :])
        qm_sc[:, lo:lo + MEM_HEAD_DIM] = _rms(m, gmq_ref[...]).astype(BF16)
    _pool_windows(xp_sc, pos0 + sb * tm, tm, d_ref)
    _mem_attn_kernel(qm_sc, mk_ref, mv_ref, am_ref)


def _front(x, gmix, wa_t, wkpe_t, wmq_t, gq, gkv, gmq_scaled, left, mem_kv, tm, blocks_per_seq, pos0):
    t = x.shape[0]
    n_seq = t // (tm * blocks_per_seq)
    row = lambda w: pl.BlockSpec((tm, w), lambda i: (i, 0))
    full = lambda a: pl.BlockSpec(a.shape, lambda i: (0, 0))
    w_elems = wa_t.size + wkpe_t.size + wmq_t.size
    vmem = 2 * (tm * D_MODEL * 6 + w_elems * 2 + tm * (POOL_W * 2 + Q_LORA * 2 + KV_LORA * 4 + 128 * 4 + MEM_W * 2)
                + 2 * N_MEM * MEM_W * 4 + 2 * POOL_HALO * POOL_W * 4)
    vmem += (tm + POOL_HALO) * POOL_W * 4 + tm * MEM_W * 2 + 8 * tm * POOL_W * 4
    kern = functools.partial(_front_kernel, tm=tm, blocks_per_seq=blocks_per_seq, pos0=pos0)
    return pl.pallas_call(
        kern,
        grid=(t // tm,),
        in_specs=[row(D_MODEL), full(gmix), full(wa_t), full(wkpe_t), full(wmq_t),
                  full(gq), full(gkv), full(gmq_scaled),
                  pl.BlockSpec((None, POOL_HALO, POOL_W), lambda i: (i // blocks_per_seq, 0, 0)),
                  pl.BlockSpec((N_MEM, MEM_W), lambda i: (i // blocks_per_seq, 0)),
                  pl.BlockSpec((N_MEM, MEM_W), lambda i: (i // blocks_per_seq, 1))],
        out_specs=[row(D_MODEL), row(POOL_W), row(Q_LORA), row(KV_LORA), row(128), row(MEM_W),
                   pl.BlockSpec((None, POOL_HALO, POOL_W), lambda i: (i // blocks_per_seq, 0, 0))],
        out_shape=[jax.ShapeDtypeStruct((t, D_MODEL), BF16),
                   jax.ShapeDtypeStruct((t, POOL_W), BF16), jax.ShapeDtypeStruct((t, Q_LORA), BF16),
                   jax.ShapeDtypeStruct((t, KV_LORA), F32), jax.ShapeDtypeStruct((t, 128), F32),
                   jax.ShapeDtypeStruct((t, MEM_W), BF16), jax.ShapeDtypeStruct((n_seq, POOL_HALO, POOL_W), F32)],
        scratch_shapes=[pltpu.VMEM((tm + POOL_HALO, POOL_W), F32), pltpu.VMEM((tm, MEM_W), BF16)],
        compiler_params=_params(("arbitrary",), vmem),
        name="front",
    )(x, gmix, wa_t, wkpe_t, wmq_t, gq, gkv, gmq_scaled, left, mem_kv, mem_kv)


def _in_proj(x, gmix, w_in_t, gq, gkv, gmq_scaled):
    t = x.shape[0]
    row = lambda w: pl.BlockSpec((t, w), lambda i: (0, 0))
    full = lambda a: pl.BlockSpec(a.shape, lambda i: (0, 0))
    pieces = ((Z_MQ, 0), (QK_ROPE, W_IN_KPE), (MEM_W, W_IN_MQ))
    w_elems = sum(n for n, _ in pieces) * D_MODEL
    vmem = t * D_MODEL * 8 + w_elems * 6 + 2 * t * (POOL_W * 4 + Q_LORA * 2 + KV_LORA * 4 + 128 * 4 + MEM_W * 2)
    vmem += w_elems * 2 + 6 * t * POOL_W * 4
    return pl.pallas_call(
        _in_proj_kernel,
        grid=(1,),
        in_specs=[row(D_MODEL), full(gmix)] + [_w_in_rows(n, start) for n, start in pieces]
                 + [full(gq), full(gkv), full(gmq_scaled)],
        out_specs=[row(D_MODEL), row(POOL_W), row(Q_LORA), row(KV_LORA), row(128), row(MEM_W)]
                  + [pl.BlockSpec((n, D_MODEL), lambda i: (0, 0)) for n, _ in pieces],
        out_shape=[jax.ShapeDtypeStruct((t, D_MODEL), BF16),
                   jax.ShapeDtypeStruct((t, POOL_W), F32), jax.ShapeDtypeStruct((t, Q_LORA), BF16),
                   jax.ShapeDtypeStruct((t, KV_LORA), F32), jax.ShapeDtypeStruct((t, 128), F32),
                   jax.ShapeDtypeStruct((t, MEM_W), BF16)]
                  + [jax.ShapeDtypeStruct((n, D_MODEL), BF16) for n, _ in pieces],
        compiler_params=_params(("arbitrary",), vmem),
        name="in_proj",
    )(x, gmix, w_in_t, w_in_t, w_in_t, gq, gkv, gmq_scaled)


def _head_store(ref, head_major):
    def store(h, nope, rope):
        if head_major:
            ref[h, :, :QK_NOPE] = nope
            ref[h, :, QK_NOPE:] = rope
        else:
            lo = h * HEAD_PAD
            ref[:, lo:lo + QK_NOPE] = nope
            ref[:, lo + QK_NOPE:lo + HEAD_PAD] = rope
    return store


def _q_heads(hq, wq_ref, gq_ref, qtab, store):
    first = lax.broadcasted_iota(jnp.int32, (1, 128), 1) < QK_ROPE
    for h in range(MLA_HEADS):
        lo = h * HEAD_PAD
        qh = _dot(hq, wq_ref[:, lo:lo + HEAD_PAD])
        qn, qr = qh[:, :QK_NOPE], qh[:, QK_NOPE:]
        ss = jnp.sum(qn * qn + jnp.where(first, qr * qr, 0.0), axis=-1, keepdims=True)
        r = lax.rsqrt(ss * (1.0 / QK_HEAD) + EPS)
        store(h, (qn * r * gq_ref[:, :QK_NOPE]).astype(BF16), (qr * r * gq_ref[:, QK_NOPE:] * qtab).astype(BF16))


def _qkv_kernel(hq_ref, ckv_ref, kpe_ref, wq_ref, wkbt_ref, wvbt_ref, gq_ref, gkn_ref, gkr_ref,
                qtab_ref, c_ref, sa_ref, sb_ref, q_ref, k_ref, vt_ref):
    _q_heads(hq_ref[...], wq_ref, gq_ref, qtab_ref[...], _head_store(q_ref, True))
    store_k = _head_store(k_ref, True)
    ckv = ckv_ref[...].astype(BF16)
    kpe = kpe_ref[...]
    ss_pe = jnp.sum(kpe * kpe, axis=-1, keepdims=True)
    kr = _rope128(kpe * gkr_ref[...], c_ref[...], sa_ref[...], sb_ref[...])
    kr = kr + pltpu.roll(kr, QK_ROPE, 1)
    for p in range(MLA_HEADS // 2):
        kn2 = _dot_nt(ckv, wkbt_ref[p * 2 * QK_NOPE:(p + 1) * 2 * QK_NOPE, :])
        for s in range(2):
            kn = kn2[:, s * QK_NOPE:(s + 1) * QK_NOPE]
            r = lax.rsqrt((jnp.sum(kn * kn, axis=-1, keepdims=True) + ss_pe) * (1.0 / QK_HEAD) + EPS)
            store_k(2 * p + s, (kn * r * gkn_ref[...]).astype(BF16), (kr * r).astype(BF16))
    vt = _dot_nt(wvbt_ref[...], ckv)
    tm = vt.shape[1]
    ones_row = (lax.broadcasted_iota(jnp.int32, (V_ROWS - V_HEAD, tm), 0) == 0).astype(BF16)
    for h in range(MLA_HEADS):
        vt_ref[h, :V_HEAD, :] = vt[h * V_HEAD:(h + 1) * V_HEAD].astype(BF16)
        vt_ref[h, V_HEAD:, :] = ones_row


def _qkv(hq, ckv, kpe, wq, wkbt, wvbt, gq, gkn, gkr, tabs, tm, seq_blocks):
    t = hq.shape[0]
    n_seq = t // (tm * seq_blocks)
    qk_spec = pl.BlockSpec((None, MLA_HEADS, tm, HEAD_PAD), lambda i: (i // seq_blocks, 0, i % seq_blocks, 0))
    qk_shape = jax.ShapeDtypeStruct((n_seq, MLA_HEADS, tm * seq_blocks, HEAD_PAD), BF16)
    row = lambda w: pl.BlockSpec((tm, w), lambda i: (i, 0))
    full = lambda a: pl.BlockSpec(a.shape, lambda i: (0, 0))
    tab = pl.BlockSpec((tm, 128), lambda i: (i % seq_blocks, 0))
    vmem = 2 * (tm * (Q_LORA * 2 + KV_LORA * 4 + 128 * 4 + 4 * 128 * 4) + (wq.size + wkbt.size + wvbt.size) * 2
                + tm * (2 * MLA_HEADS * HEAD_PAD * 2 + MLA_HEADS * V_ROWS * 2))
    vmem += 8 * tm * HEAD_PAD * 4 + 2 * tm * MLA_HEADS * V_HEAD * 4
    return pl.pallas_call(
        _qkv_kernel,
        grid=(t // tm,),
        in_specs=[row(Q_LORA), row(KV_LORA), row(128), full(wq), full(wkbt), full(wvbt),
                  full(gq), full(gkn), full(gkr), tab, tab, tab, tab],
        out_specs=[qk_spec, qk_spec,
                   pl.BlockSpec((None, MLA_HEADS, None, V_ROWS, tm),
                                lambda i: (i // seq_blocks, 0, i % seq_blocks, 0, 0))],
        out_shape=[qk_shape, qk_shape,
                   jax.ShapeDtypeStruct((n_seq, MLA_HEADS, seq_blocks, V_ROWS, tm), BF16)],
        compiler_params=_params(("parallel",), vmem),
        name="qkv_proj",
    )(hq, ckv, kpe, wq, wkbt, wvbt, gq, gkn, gkr, *tabs)


def _q_only_kernel(hq_ref, wq_ref, gq_ref, qtab_ref, q_ref):
    _q_heads(hq_ref[...], wq_ref, gq_ref, qtab_ref[...], _head_store(q_ref, False))


def _q_only(hq, wq, gq, qtab, tm):
    t = hq.shape[0]
    row = lambda w: pl.BlockSpec((tm, w), lambda i: (i, 0))
    full = lambda a: pl.BlockSpec(a.shape, lambda i: (0, 0))
    vmem = 2 * (tm * (Q_LORA * 2 + 128 * 4 + MLA_HEADS * HEAD_PAD * 2) + wq.size * 2) + 8 * tm * HEAD_PAD * 4
    return pl.pallas_call(
        _q_only_kernel,
        grid=(t // tm,),
        in_specs=[row(Q_LORA), full(wq), full(gq), pl.BlockSpec((tm, 128), lambda i: (0, 0))],
        out_specs=row(MLA_HEADS * HEAD_PAD),
        out_shape=jax.ShapeDtypeStruct((t, MLA_HEADS * HEAD_PAD), BF16),
        compiler_params=_params(("parallel",), vmem),
        name="q_proj",
    )(hq, wq, gq, qtab)


def _flash_kernel(q_ref, k_ref, vt_ref, o_ref, s0_sc, s1_sc, m_sc, acc_sc, *, tq, tk):
    qi = pl.program_id(2)
    m_sc[...] = jnp.full(m_sc.shape, NEG_INF, F32)
    acc_sc[...] = jnp.zeros(acc_sc.shape, F32)

    def scores(kj, s_sc, q_lo=0):
        start = pl.multiple_of(kj * tk, tk)
        s_sc[:, q_lo:] = _dot_nt(k_ref[pl.ds(start, tk), :], q_ref[q_lo:, :])

    def consume(kj, s_sc, key_chunk0=None, q_lo=0):
        s = s_sc[:, q_lo:]
        if key_chunk0 is not None:
            kc = key_chunk0 + lax.broadcasted_iota(jnp.int32, s.shape, 0) // CHUNK
            qc = (q_lo + lax.broadcasted_iota(jnp.int32, s.shape, 1)) // CHUNK
            s = jnp.where(qc >= kc, s, NEG_INF)
        m = m_sc[:, q_lo:]
        m_new = jnp.maximum(m, jnp.max(s, axis=0, keepdims=True))
        alpha = jnp.exp2(m - m_new)
        p = jnp.exp2(s - m_new).astype(BF16)
        acc_sc[:, q_lo:] = alpha * acc_sc[:, q_lo:] + _dot(vt_ref[kj], p)
        m_sc[:, q_lo:] = m_new

    scores(0, s0_sc)

    def pair(t, carry):
        scores(2 * t + 1, s1_sc)
        consume(2 * t, s0_sc)
        scores(2 * t + 2, s0_sc)
        consume(2 * t + 1, s1_sc)
        return carry

    n_diag = tq // tk
    lax.fori_loop(0, qi * (n_diag // 2), pair, 0)
    base = qi * n_diag
    bufs = (s0_sc, s1_sc)
    for d in range(n_diag):
        if d + 1 < n_diag:
            scores(base + d + 1, bufs[(d + 1) % 2], (d + 1) * tk)
        consume(base + d, bufs[d % 2], d * tk // CHUNK, d * tk)
    acc = acc_sc[...]
    o_ref[...] = (acc[:V_HEAD] / acc[V_HEAD:V_HEAD + 1]).T.astype(BF16)


def _flash(q, k, vt, batch, seq, tq):
    tk = vt.shape[-1]
    assert tq % (2 * tk) == 0 and tk % CHUNK == 0 and seq % tq == 0
    nq = seq // tq
    vmem = 2 * (tq * HEAD_PAD * 2 + seq * HEAD_PAD * 2 + seq * V_ROWS * 2 + tq * V_HEAD * 2)
    vmem += 2 * tk * tq * 4 + V_ROWS * tq * 4 + 4 * tk * tq * 4
    return pl.pallas_call(
        functools.partial(_flash_kernel, tq=tq, tk=tk),
        grid=(batch, MLA_HEADS, nq),
        in_specs=[pl.BlockSpec((None, None, tq, HEAD_PAD), lambda b, h, i: (b, h, i, 0)),
                  pl.BlockSpec((None, None, seq, HEAD_PAD), lambda b, h, i: (b, h, 0, 0)),
                  pl.BlockSpec((None, None, seq // tk, V_ROWS, tk), lambda b, h, i: (b, h, 0, 0, 0))],
        out_specs=pl.BlockSpec((tq, V_HEAD), lambda b, h, i: (b * nq + i, h)),
        out_shape=jax.ShapeDtypeStruct((batch * seq, MLA_HEADS * V_HEAD), BF16),
        scratch_shapes=[pltpu.VMEM((tk, tq), F32), pltpu.VMEM((tk, tq), F32),
                        pltpu.VMEM((1, tq), F32), pltpu.VMEM((V_ROWS, tq), F32)],
        compiler_params=_params(("parallel", "parallel", "arbitrary"), vmem),
        name="mla_flash",
    )(q, k, vt)


def _q_absorb_kernel(q_ref, wkbt_ref, qa_ref, qr_ref):
    qa_ref[...] = _dot(q_ref[:, :QK_NOPE], wkbt_ref[...]).astype(BF16)
    qr_ref[...] = q_ref[:, QK_NOPE:]


def _q_absorb(q, wkbt):
    t = q.shape[0]
    vmem = 2 * (t * HEAD_PAD * 2 + KV_LORA * QK_NOPE * 2 + t * KV_LORA * 2 + t * 128 * 2) + 2 * t * KV_LORA * 4
    return pl.pallas_call(
        _q_absorb_kernel,
        grid=(MLA_HEADS,),
        in_specs=[pl.BlockSpec((t, HEAD_PAD), lambda h: (0, h)),
                  pl.BlockSpec((QK_NOPE, KV_LORA), lambda h: (h, 0))],
        out_specs=[pl.BlockSpec((None, t, KV_LORA), lambda h: (h, 0, 0)),
                   pl.BlockSpec((None, t, 128), lambda h: (h, 0, 0))],
        out_shape=[jax.ShapeDtypeStruct((MLA_HEADS, t, KV_LORA), BF16),
                   jax.ShapeDtypeStruct((MLA_HEADS, t, 128), BF16)],
        compiler_params=_params(("parallel",), vmem),
        name="q_absorb",
    )(q, wkbt)


def _cache_attn_kernel(wkbt_ref, qa_ref, qr_ref, lat_ref, kpet_ref, ct_ref, st_ref,
                       nlat_ref, nkpet_ref, nct_ref, nst_ref, gkr_ref,
                       o_ref, s_sc, m_sc, l_sc, acc_sc, *, nq, n_new, past, tk):
    j = pl.program_id(1)
    last = pl.num_programs(1) - 1
    rows = MLA_HEADS * nq
    half = QK_ROPE // 2

    @pl.when(j == 0)
    def _():
        m_sc[...] = jnp.full(m_sc.shape, NEG_INF, F32)
        l_sc[...] = jnp.zeros(l_sc.shape, F32)
        acc_sc[...] = jnp.zeros(acc_sc.shape, F32)

    def block(lat, kpet, ct, st, width, visible):
        c = lat.astype(BF16)
        kt = _dot_nt(wkbt_ref[...], c)
        s = _dot_nt(qa_ref[...].reshape(rows, KV_LORA), c)
        ss_pe = jnp.sum(kpet * kpet, axis=0, keepdims=True)
        kr = kpet * gkr_ref[...]
        k1, k2 = kr[:half], kr[half:]
        o1, o2 = k1 * ct - k2 * st, k2 * ct + k1 * st
        krot = jnp.concatenate([o1, o2, o1, o2], axis=0).astype(BF16)
        s = s + _dot(qr_ref[...].reshape(rows, 128), krot)
        for h in range(MLA_HEADS):
            kh = kt[h * QK_NOPE:(h + 1) * QK_NOPE]
            r = lax.rsqrt((jnp.sum(kh * kh, axis=0, keepdims=True) + ss_pe) * (1.0 / QK_HEAD) + EPS)
            s_sc[h * nq:(h + 1) * nq, :width] = s[h * nq:(h + 1) * nq] * r
        s = s_sc[:, :width]
        if visible is not None:
            s = jnp.where(visible, s, NEG_INF)
        m = m_sc[...]
        m_new = jnp.maximum(m, jnp.max(s, axis=-1, keepdims=True))
        alpha = jnp.exp2(m - m_new)
        p = jnp.exp2(s - m_new)
        l_sc[...] = alpha * l_sc[...] + jnp.sum(p, axis=-1, keepdims=True)
        m_sc[...] = m_new
        acc_sc[...] = alpha * acc_sc[...] + _dot(p.astype(BF16), c)

    @pl.when(j < last)
    def _():
        block(lat_ref[...], kpet_ref[...], ct_ref[...], st_ref[...], tk, None)

    @pl.when(j == last)
    def _():
        kk = lax.broadcasted_iota(jnp.int32, (1, 128), 1)
        visible = kk < n_new
        if (past + n_new - 1) // CHUNK != past // CHUNK:
            qpos = past + lax.broadcasted_iota(jnp.int32, (rows, 1), 0) % nq
            visible = visible & (qpos // CHUNK >= (past + kk) // CHUNK)
        block(nlat_ref[...], nkpet_ref[...], nct_ref[...], nst_ref[...], 128, visible)
        o_ref[...] = (acc_sc[...] / l_sc[...]).astype(BF16)


def _cache_attn(wkbt, qa, qr, lat, kpet, ct, st, nlat, nkpet, nct, nst, gkr, nq, n_new, tk):
    batch, past, _ = lat.shape
    assert past % tk == 0 and past >= tk
    nkb = past // tk
    rows = MLA_HEADS * nq
    cj = lambda j: jnp.minimum(j, nkb - 1)
    kern = functools.partial(_cache_attn_kernel, nq=nq, n_new=n_new, past=past, tk=tk)
    vmem = 2 * (wkbt.size * 2 + rows * (KV_LORA + 128) * 2 + tk * KV_LORA * 4 + 3 * 64 * tk * 4 + rows * KV_LORA * 2)
    vmem += (MLA_HEADS * QK_NOPE + 4 * rows) * tk * 4 + tk * KV_LORA * 2 + rows * KV_LORA * 8
    return pl.pallas_call(
        kern,
        grid=(batch, nkb + 1),
        in_specs=[pl.BlockSpec(wkbt.shape, lambda b, j: (0, 0)),
                  pl.BlockSpec((MLA_HEADS, nq, KV_LORA), lambda b, j: (0, b, 0)),
                  pl.BlockSpec((MLA_HEADS, nq, 128), lambda b, j: (0, b, 0)),
                  pl.BlockSpec((None, tk, KV_LORA), lambda b, j: (b, cj(j), 0)),
                  pl.BlockSpec((None, QK_ROPE, tk), lambda b, j: (b, 0, cj(j))),
                  pl.BlockSpec((QK_ROPE // 2, tk), lambda b, j: (0, cj(j))),
                  pl.BlockSpec((QK_ROPE // 2, tk), lambda b, j: (0, cj(j))),
                  pl.BlockSpec((None, 128, KV_LORA), lambda b, j: (b, 0, 0)),
                  pl.BlockSpec((None, QK_ROPE, 128), lambda b, j: (b, 0, 0)),
                  pl.BlockSpec((QK_ROPE // 2, 128), lambda b, j: (0, 0)),
                  pl.BlockSpec((QK_ROPE // 2, 128), lambda b, j: (0, 0)),
                  pl.BlockSpec((QK_ROPE, 1), lambda b, j: (0, 0))],
        out_specs=pl.BlockSpec((None, rows, KV_LORA), lambda b, j: (b, 0, 0)),
        out_shape=jax.ShapeDtypeStruct((batch, rows, KV_LORA), BF16),
        scratch_shapes=[pltpu.VMEM((rows, tk), F32), pltpu.VMEM((rows, 1), F32),
                        pltpu.VMEM((rows, 1), F32), pltpu.VMEM((rows, KV_LORA), F32)],
        compiler_params=_params(("parallel", "arbitrary"), vmem),
        name="mla_cache_attn",
    )(wkbt, qa, qr, lat, kpet, ct, st, nlat, nkpet, nct, nst, gkr)


def _v_up_kernel(ol_ref, wvbt_ref, o_ref):
    b, nq, _ = ol_ref.shape
    o_ref[...] = _dot_nt(ol_ref[...].reshape(b * nq, KV_LORA), wvbt_ref[...]).astype(BF16)


def _v_up(o_lat, wvbt, nq):
    batch = o_lat.shape[0]
    t = batch * nq
    vmem = 2 * (t * KV_LORA * 2 + KV_LORA * V_HEAD * 2 + t * V_HEAD * 2) + 2 * t * V_HEAD * 4
    return pl.pallas_call(
        _v_up_kernel,
        grid=(MLA_HEADS,),
        in_specs=[pl.BlockSpec((batch, nq, KV_LORA), lambda h: (0, h, 0)),
                  pl.BlockSpec((V_HEAD, KV_LORA), lambda h: (h, 0))],
        out_specs=pl.BlockSpec((t, V_HEAD), lambda h: (0, h)),
        out_shape=jax.ShapeDtypeStruct((t, MLA_HEADS * V_HEAD), BF16),
        compiler_params=_params(("parallel",), vmem),
        name="v_up",
    )(o_lat, wvbt)


def _mem_kv_kernel(mem_ref, gmem_ref, w_ref, gk_ref, kv_ref):
    j = pl.program_id(0)
    h = _rms(mem_ref[...], gmem_ref[...]).astype(BF16)

    @pl.when(j == 0)
    def _():
        for hh in range(MEM_HEADS):
            lo = hh * MEM_HEAD_DIM
            w = w_ref[:, lo:lo + MEM_HEAD_DIM].astype(BF16)
            kv_ref[:, lo:lo + MEM_HEAD_DIM] = _rms(_dot(h, w), gk_ref[...])

    @pl.when(j == 1)
    def _():
        kv_ref[...] = _dot(h, w_ref[...].astype(BF16))


def _mem_kv(mem, gmem, w, gk):
    t = mem.shape[0]
    vmem = 2 * (t * D_MODEL * 4 + D_MODEL * MEM_W * w.dtype.itemsize + t * MEM_W * 4) + t * D_MODEL * 6
    vmem += t * MEM_W * 4 + D_MODEL * MEM_W * 2
    return pl.pallas_call(
        _mem_kv_kernel,
        grid=(2,),
        in_specs=[pl.BlockSpec((t, D_MODEL), lambda j: (0, 0)), pl.BlockSpec((1, D_MODEL), lambda j: (0, 0)),
                  pl.BlockSpec((D_MODEL, MEM_W), lambda j: (0, j)), pl.BlockSpec((1, MEM_HEAD_DIM), lambda j: (0, 0))],
        out_specs=pl.BlockSpec((t, MEM_W), lambda j: (0, j)),
        out_shape=jax.ShapeDtypeStruct((t, 2 * MEM_W), F32),
        compiler_params=_params(("parallel",), vmem),
        name="memory_kv",
    )(mem, gmem, w, gk)


def _mem_attn_kernel(qm_ref, k_ref, v_ref, o_ref):
    for hh in range(MEM_HEADS):
        sl = slice(hh * MEM_HEAD_DIM, (hh + 1) * MEM_HEAD_DIM)
        s = _dot_nt(qm_ref[:, sl], k_ref[:, sl].astype(BF16))
        p = jnp.exp(s - jnp.max(s, axis=-1, keepdims=True))
        l = jnp.sum(p, axis=-1, keepdims=True)
        o_ref[:, sl] = (_dot(p.astype(BF16), v_ref[:, sl].astype(BF16)) / l).astype(BF16)


def _mem_attn(qm, k_arr, v_arr, k_col, v_col, tm, blocks_per_batch):
    t = qm.shape[0]
    vmem = 2 * (2 * tm * MEM_W * 2 + 2 * N_MEM * MEM_W * 4) + 6 * tm * N_MEM * 4 + 2 * N_MEM * MEM_W * 2
    return pl.pallas_call(
        _mem_attn_kernel,
        grid=(t // tm,),
        in_specs=[pl.BlockSpec((tm, MEM_W), lambda i: (i, 0)),
                  pl.BlockSpec((N_MEM, MEM_W), lambda i: (i // blocks_per_batch, k_col)),
                  pl.BlockSpec((N_MEM, MEM_W), lambda i: (i // blocks_per_batch, v_col))],
        out_specs=pl.BlockSpec((tm, MEM_W), lambda i: (i, 0)),
        out_shape=jax.ShapeDtypeStruct((t, MEM_W), BF16),
        compiler_params=_params(("parallel",), vmem),
        name="memory_attn",
    )(qm, k_arr, v_arr)


def _pool_windows(xp_sc, first_pos, tm, d_ref):
    pos = first_pos + lax.broadcasted_iota(jnp.int32, (tm, 1), 0)
    for g, win in enumerate(POOL_WINDOWS):
        sl = slice(g * POOL_GW, (g + 1) * POOL_GW)
        x = xp_sc[:, sl]
        wsum, k = x, 1
        while k < win:
            wsum = wsum + pltpu.roll(wsum, k, 0)
            k *= 2
        cnt = jnp.minimum(pos + 1, win).astype(F32)
        d_ref[:, sl] = (wsum[POOL_HALO:] / cnt - x[POOL_HALO:]).astype(BF16)


def _pool_kernel(u_ref, prev_ref, left_ref, d_ref, xp_sc, *, tm, blocks_per_seq, pos0):
    sb = pl.program_id(0) % blocks_per_seq
    xp_sc[:POOL_HALO, :] = jnp.where(sb == 0, left_ref[...], prev_ref[...])
    xp_sc[POOL_HALO:, :] = u_ref[...]
    _pool_windows(xp_sc, pos0 + sb * tm, tm, d_ref)


def _pool(u, left, tm, blocks_per_seq, pos0):
    t = u.shape[0]
    halo_per_blk = tm // POOL_HALO
    kern = functools.partial(_pool_kernel, tm=tm, blocks_per_seq=blocks_per_seq, pos0=pos0)
    vmem = 2 * (tm * POOL_W * 4 + 2 * POOL_HALO * POOL_W * 4 + tm * POOL_W * 2) + (tm + POOL_HALO) * POOL_W * 4
    vmem += 6 * tm * POOL_GW * 4
    return pl.pallas_call(
        kern,
        grid=(t // tm,),
        in_specs=[pl.BlockSpec((tm, POOL_W), lambda i: (i, 0)),
                  pl.BlockSpec((POOL_HALO, POOL_W), lambda i: (jnp.maximum(i * halo_per_blk - 1, 0), 0)),
                  pl.BlockSpec((None, POOL_HALO, POOL_W), lambda i: (i // blocks_per_seq, 0, 0))],
        out_specs=pl.BlockSpec((tm, POOL_W), lambda i: (i, 0)),
        out_shape=jax.ShapeDtypeStruct((t, POOL_W), BF16),
        scratch_shapes=[pltpu.VMEM((tm + POOL_HALO, POOL_W), F32)],
        compiler_params=_params(("parallel",), vmem),
        name="pool_windows",
    )(u, u, left)


def _merge_kernel(h_ref, wg0_ref, wg1_ref, wg2_ref, bg0_ref, bg1_ref, bg2_ref,
                  d_ref, wpool_ref, pscale_ref, o_ref, wmla_ref, am_ref, wmem_ref,
                  out_ref, *w_copies):
    wg_copies, (wpool_copy, wmla_copy, wmem_copy) = (w_copies[:3], w_copies[3:]) if w_copies else ((None,) * 3,) * 2
    h = h_ref[...]
    y_pool = _dot(d_ref[...], _as_bf16(wpool_ref, wpool_copy)) * pscale_ref[...]
    y_mla = _dot(o_ref[...], _as_bf16(wmla_ref, wmla_copy))
    y_mem = _dot(am_ref[...], _as_bf16(wmem_ref, wmem_copy))
    gate = lambda w_ref, copy, b_ref: jax.nn.sigmoid(_dot_nt(h, _as_bf16(w_ref, copy)) + b_ref[...])
    merged = gate(wg0_ref, wg_copies[0], bg0_ref) * y_pool
    merged += gate(wg1_ref, wg_copies[1], bg1_ref) * y_mla
    merged += gate(wg2_ref, wg_copies[2], bg2_ref) * y_mem
    out_ref[...] = merged.astype(BF16)


def _merge(h, wgates, bgate, d, wpool, pscale, o, wmla, am, wmem, tm, tn, emit_bf16=False):
    t = h.shape[0]
    assert not emit_bf16 or t == tm
    assert POOL_OUT_GW % tn == 0
    nj = D_MODEL // tn
    per_group = POOL_OUT_GW // tn
    rowfull = lambda w: pl.BlockSpec((tm, w), lambda i, j: (i, 0))
    gate_rows = pl.BlockSpec((tn, D_MODEL), lambda i, j: (j, 0))
    if emit_bf16:
        gate_in = [pl.BlockSpec((pl.Element(tn), pl.Element(D_MODEL)),
                                lambda i, j, b=b: (pl.multiple_of(W_IN_GATE + (b * nj + j) * tn, BF16_SUBLANES), 0))
                   for b in range(3)]
        gate_args = (wgates,) * 3
    else:
        gate_in = [gate_rows] * 3
        gate_args = tuple(wgates)
    gate_b = lambda b: pl.BlockSpec((1, tn), lambda i, j: (0, b * nj + j))
    col = lambda k: pl.BlockSpec((k, tn), lambda i, j: (0, j))
    pool_spec = pl.BlockSpec((None, POOL_GW, tn), lambda i, j: (j // per_group, 0, j % per_group))
    w_bytes = wmla.dtype.itemsize + 2 * emit_bf16
    w_rows = 3 * D_MODEL + POOL_GW + D_MODEL + MEM_W
    vmem = 2 * (tm * D_MODEL * 2 + w_rows * tn * w_bytes + tm * POOL_GW * 2 + tm * D_MODEL * 2
                + tm * MEM_W * 2 + tm * tn * 2)
    vmem += 8 * tm * tn * 4 + w_rows * tn * 2 * emit_bf16
    y_spec = pl.BlockSpec((tm, tn), lambda i, j: (i, j))
    y_shape = jax.ShapeDtypeStruct((t, D_MODEL), BF16)
    copies = ([jax.ShapeDtypeStruct((D_MODEL, D_MODEL), BF16)] * 3
              + [jax.ShapeDtypeStruct(w.shape, BF16) for w in (wpool, wmla, wmem)])
    return pl.pallas_call(
        _merge_kernel,
        grid=(t // tm, nj),
        in_specs=[rowfull(D_MODEL)] + gate_in
                 + [gate_b(0), gate_b(1), gate_b(2),
                    pl.BlockSpec((tm, POOL_GW), lambda i, j: (i, j // per_group)), pool_spec,
                    pl.BlockSpec((1, tn), lambda i, j: (0, j)),
                    rowfull(D_MODEL), col(D_MODEL), rowfull(MEM_W), col(MEM_W)],
        out_specs=[y_spec] + [gate_rows] * 3 + [pool_spec, col(D_MODEL), col(MEM_W)] if emit_bf16 else y_spec,
        out_shape=[y_shape] + copies if emit_bf16 else y_shape,
        compiler_params=_params(("parallel", "arbitrary" if emit_bf16 else "parallel"), vmem),
        name="gated_merge",
    )(h, *gate_args, bgate, bgate, bgate, d, wpool, pscale, o, wmla, am, wmem)


def _out_proj_kernel(x_ref, m_ref, w_ref, y_ref, *w_copy):
    y_ref[...] = x_ref[...] + _dot(m_ref[...], _as_bf16(w_ref, *w_copy or (None,)))


def _out_proj(x, merged, w, tm, tn, emit_bf16=False):
    t = x.shape[0]
    assert not emit_bf16 or t == tm
    w_spec = pl.BlockSpec((D_MODEL, tn), lambda i, j: (0, j))
    vmem = 2 * (2 * tm * tn * 4 + tm * D_MODEL * 2 + D_MODEL * tn * (w.dtype.itemsize + 2 * emit_bf16)) + 2 * tm * tn * 4
    y_spec = pl.BlockSpec((tm, tn), lambda i, j: (i, j))
    y_shape = jax.ShapeDtypeStruct((t, D_MODEL), F32)
    return pl.pallas_call(
        _out_proj_kernel,
        grid=(t // tm, D_MODEL // tn),
        in_specs=[pl.BlockSpec((tm, tn), lambda i, j: (i, j)),
                  pl.BlockSpec((tm, D_MODEL), lambda i, j: (i, 0)), w_spec],
        out_specs=[y_spec, w_spec] if emit_bf16 else y_spec,
        out_shape=[y_shape, jax.ShapeDtypeStruct(w.shape, BF16)] if emit_bf16 else y_shape,
        compiler_params=_params(("parallel", "arbitrary" if emit_bf16 else "parallel"), vmem),
        name="out_proj",
    )(x, merged, w)


def _ffn_kernel(x_ref, g_ref, wup_ref, wdown_ref, y_ref, *rest):
    *w_copies, h_sc, acc_sc = rest
    wup_copy, wdown_copy = w_copies or (None, None)
    j = pl.program_id(1)

    @pl.when(j == 0)
    def _():
        h_sc[...] = _rms(x_ref[...], g_ref[...]).astype(BF16)
        acc_sc[...] = jnp.zeros(acc_sc.shape, F32)

    f = jnp.maximum(_dot(h_sc[...], _as_bf16(wup_ref, wup_copy)), 0.0)
    acc_sc[...] += _dot((f * f).astype(BF16), _as_bf16(wdown_ref, wdown_copy))

    @pl.when(j == pl.num_programs(1) - 1)
    def _():
        y_ref[...] = x_ref[...] + acc_sc[...]


def _ffn(x, g, wup, wdown, tm, tf, emit_bf16=False):
    t = x.shape[0]
    assert not emit_bf16 or t == tm
    w_bytes = wup.dtype.itemsize + 2 * emit_bf16
    vmem = 2 * (2 * tm * D_MODEL * 4 + 2 * D_MODEL * tf * w_bytes) + tm * D_MODEL * 6 + 3 * tm * tf * 4 + tm * D_MODEL * 4
    vmem += 2 * D_MODEL * tf * 2 * emit_bf16
    up_spec = pl.BlockSpec((D_MODEL, tf), lambda i, j: (0, j))
    down_spec = pl.BlockSpec((tf, D_MODEL), lambda i, j: (j, 0))
    y_spec = pl.BlockSpec((tm, D_MODEL), lambda i, j: (i, 0))
    y_shape = jax.ShapeDtypeStruct((t, D_MODEL), F32)
    return pl.pallas_call(
        _ffn_kernel,
        grid=(t // tm, D_FF // tf),
        in_specs=[pl.BlockSpec((tm, D_MODEL), lambda i, j: (i, 0)),
                  pl.BlockSpec((1, D_MODEL), lambda i, j: (0, 0)), up_spec, down_spec],
        out_specs=[y_spec, up_spec, down_spec] if emit_bf16 else y_spec,
        out_shape=([y_shape, jax.ShapeDtypeStruct(wup.shape, BF16), jax.ShapeDtypeStruct(wdown.shape, BF16)]
                   if emit_bf16 else y_shape),
        scratch_shapes=[pltpu.VMEM((tm, D_MODEL), BF16), pltpu.VMEM((tm, D_MODEL), F32)],
        compiler_params=_params(("parallel", "arbitrary"), vmem),
        name="ffn",
    )(x, g, wup, wdown)


def _rope_tables(pos):
    half = QK_ROPE // 2
    inv = 1.0 / (ROPE_THETA ** (jnp.arange(half, dtype=F32) * (2.0 / QK_ROPE)))
    ang = pos.astype(F32)[:, None] * inv[None, :]
    return jnp.cos(ang), jnp.sin(ang)


def _lane_tables(cos, sin):
    z = jnp.zeros_like(cos)
    c = jnp.concatenate([cos, cos, z, z], axis=1)
    sa = jnp.concatenate([z, sin, z, z], axis=1)
    sb = jnp.concatenate([-sin, z, z, z], axis=1)
    return c, sa, sb


def _query_table(cos, sin):
    return jnp.concatenate([cos, cos, -sin, sin], axis=1)


def _row(v):
    return v.astype(F32).reshape(1, -1)


def _pad_last(a, width):
    return jnp.pad(a, [(0, 0)] * (a.ndim - 1) + [(0, width - a.shape[-1])])


def kernel(x_prompt, mem_prompt, x_sample, cache_mla_latent, cache_mla_kpe, state_pool, cache_mem_k, cache_mem_v,
           g_mix, w_in, b_gate, w_pool, pool_scale, g_q_lat, w_qb, g_q_head, g_kv_lat, w_kb, w_vb, g_k_head,
           w_mla_o, g_mem, w_mem_kv, g_mem_q, g_mem_k, w_mem_o, w_out, g_ff, w_up, w_down):
    batch, seq, _ = x_prompt.shape
    dec_batch, dec_seq, _ = x_sample.shape
    past = cache_mla_latent.shape[1]

    w_in_t = w_in.T
    bgate = _row(b_gate)
    half = QK_ROPE // 2

    def swap_halves(a):
        return jnp.concatenate([a[..., half:], a[..., :half]], axis=-1)

    def head_layout(a):
        rope = a[..., QK_NOPE:]
        return jnp.concatenate([a, swap_halves(rope)], axis=-1)

    wq = head_layout(w_qb.reshape(Q_LORA, MLA_HEADS, QK_HEAD)).reshape(Q_LORA, MLA_HEADS * HEAD_PAD).astype(BF16)
    wkbt = w_kb.T.astype(BF16)
    wvbt = w_vb.T.astype(BF16)
    gq_p = _row(head_layout(g_q_head.astype(F32) * MLA_SCALE_LOG2))
    gkn = _row(g_k_head[:QK_NOPE])
    gkr = _row(_pad_last(g_k_head[QK_NOPE:], 128))
    gq_s = _row(head_layout(jnp.concatenate([g_q_head[:QK_NOPE].astype(F32) * g_k_head[:QK_NOPE].astype(F32),
                                             g_q_head[QK_NOPE:].astype(F32)]) * MLA_SCALE_LOG2))
    gmq = _row(g_mem_q.astype(F32) * MEM_SCALE)
    gmix, gql, gkvl, gff, gmem, gmk, pscale = map(_row, (g_mix, g_q_lat, g_kv_lat, g_ff, g_mem, g_mem_k, pool_scale))

    ts = dec_batch * dec_seq
    xs = x_sample.reshape(ts, D_MODEL)
    h_s, u_s, hq_s, ckv_s, kpe_s, qm_s, wa_t, wkpe_t, wmq_t = _in_proj(xs, gmix, w_in_t, gql, gkvl, gmq)
    pos_s = past + jnp.arange(dec_seq, dtype=jnp.int32)
    cos_s, sin_s = _rope_tables(pos_s)
    q_s = _q_only(hq_s, wq, gq_s, jnp.tile(_query_table(cos_s, sin_s), (dec_batch, 1)), ts)
    qa, qr = _q_absorb(q_s, wkbt)
    cos_c, sin_c = _rope_tables(jnp.arange(past, dtype=jnp.int32))
    pad_new = 128 - dec_seq
    nlat = jnp.pad(ckv_s.reshape(dec_batch, dec_seq, KV_LORA), ((0, 0), (0, pad_new), (0, 0)))
    nkpet = jnp.pad(jnp.swapaxes(kpe_s[:, :QK_ROPE].reshape(dec_batch, dec_seq, QK_ROPE), 1, 2),
                    ((0, 0), (0, 0), (0, pad_new)))
    o_lat = _cache_attn(wkbt, qa, qr, cache_mla_latent, jnp.swapaxes(cache_mla_kpe, 1, 2),
                        cos_c.T, sin_c.T, nlat, nkpet,
                        jnp.pad(cos_s.T, ((0, 0), (0, pad_new))), jnp.pad(sin_s.T, ((0, 0), (0, pad_new))),
                        g_k_head[QK_NOPE:].astype(F32).reshape(QK_ROPE, 1),
                        dec_seq, dec_seq, 1024)
    o_s = _v_up(o_lat, wvbt, dec_seq)
    am_s = _mem_attn(qm_s, cache_mem_k.reshape(dec_batch * N_MEM, MEM_W), cache_mem_v.reshape(dec_batch * N_MEM, MEM_W),
                     0, 0, dec_seq, 1)
    left_s = jnp.pad(state_pool.astype(F32), ((0, 0), (POOL_HALO - POOL_STATE, 0), (0, 0)))
    d_s = _pool(u_s, left_s, dec_seq, 1, past)
    merged_s, *wgates, wpool, wmla, wmem_o = _merge(h_s, w_in_t, bgate, d_s, w_pool, pscale, o_s, w_mla_o,
                                                    am_s, w_mem_o, ts, 256, emit_bf16=True)
    x1_s, wout = _out_proj(xs, merged_s, w_out, ts, 512, emit_bf16=True)
    y_s, wup, wdown = _ffn(x1_s, gff, w_up, w_down, ts, 512, emit_bf16=True)
    y_s = y_s.reshape(dec_batch, dec_seq, D_MODEL)
    lat_s = ckv_s.reshape(dec_batch, dec_seq, KV_LORA)
    kpe_s_out = kpe_s[:, :QK_ROPE].reshape(dec_batch, dec_seq, QK_ROPE)
    pool_s = u_s.reshape(dec_batch, dec_seq, POOL_W)[:, dec_seq - POOL_STATE:]

    tm = 512
    xp = x_prompt.reshape(batch * seq, D_MODEL)
    mem_kv = _mem_kv(mem_prompt.reshape(batch * N_MEM, D_MODEL), gmem, w_mem_kv, gmk)
    h, d, hq, ckv, kpe, am, u_tail = _front(xp, gmix, wa_t, wkpe_t, wmq_t, gql, gkvl, gmq,
                                            jnp.zeros((batch, POOL_HALO, POOL_W), F32), mem_kv, tm, seq // tm, 0)
    cos_p, sin_p = _rope_tables(jnp.arange(seq, dtype=jnp.int32))
    q, k, vt = _qkv(hq, ckv, kpe, wq, wkbt, wvbt, gq_p, gkn, gkr,
                    (_query_table(cos_p, sin_p),) + _lane_tables(cos_p, sin_p), tm, seq // tm)
    o = _flash(q, k, vt, batch, seq, 2048)
    merged = _merge(h, wgates, bgate, d, wpool, pscale, o, wmla, am, wmem_o, 2 * tm, POOL_OUT_GW)
    y_p = _ffn(_out_proj(xp, merged, wout, 2 * tm, 1024), gff, wup, wdown, tm, 1024).reshape(batch, seq, D_MODEL)
    lat_p = ckv.reshape(batch, seq, KV_LORA)
    kpe_p = kpe[:, :QK_ROPE].reshape(batch, seq, QK_ROPE)
    pool_p = u_tail[:, POOL_HALO - POOL_STATE:]
    mem_k_p = mem_kv[:, :MEM_W].reshape(batch, N_MEM, MEM_HEADS, MEM_HEAD_DIM)
    mem_v_p = mem_kv[:, MEM_W:].reshape(batch, N_MEM, MEM_HEADS, MEM_HEAD_DIM)

    return (y_p, y_s, lat_p, kpe_p, pool_p, mem_k_p, mem_v_p, lat_s, kpe_s_out, pool_s)
```

```python
import functools

import jax
import jax.numpy as jnp
from jax import lax
from jax.experimental import pallas as pl
from jax.experimental.pallas import tpu as pltpu

F32 = jnp.float32
BF16 = jnp.bfloat16

EPS = 1e-6
CHUNK = 64
D_MODEL = 2048
POOL_WINDOWS = (2, 4, 8, 16)
POOL_W = 1024
POOL_GW = 256
POOL_OUT_GW = 512
POOL_STATE = 15
POOL_HALO = 16
assert all(w & (w - 1) == 0 and w <= POOL_HALO for w in POOL_WINDOWS)
MLA_HEADS = 16
Q_LORA = 512
KV_LORA = 512
QK_NOPE = 128
QK_ROPE = 64
QK_HEAD = QK_NOPE + QK_ROPE
HEAD_PAD = 256
V_HEAD = 128
V_ROWS = 144
ROPE_THETA = 10000.0
LOG2_E = 1.4426950408889634
MLA_SCALE = QK_HEAD ** -0.5
MLA_SCALE_LOG2 = MLA_SCALE * LOG2_E
N_MEM = 256
MEM_HEADS = 4
MEM_HEAD_DIM = 256
MEM_W = MEM_HEADS * MEM_HEAD_DIM
MEM_SCALE = MEM_HEAD_DIM ** -0.5
D_FF = 4 * D_MODEL
NEG_INF = -1e30
FLASH_AHEAD = 2

Z_U = 0
Z_QLAT = Z_U + POOL_W
Z_KVLAT = Z_QLAT + Q_LORA
Z_MQ = Z_KVLAT + KV_LORA
W_IN_KPE = Z_MQ
W_IN_MQ = W_IN_KPE + QK_ROPE
W_IN_GATE = W_IN_MQ + MEM_W

BF16_SUBLANES = 16
V7X_VMEM_BYTES = 64 * 1024 * 1024
VMEM_CEILING = V7X_VMEM_BYTES - 6 * 1024 * 1024


def _params(semantics, vmem_bytes):
    return pltpu.CompilerParams(dimension_semantics=semantics,
                                vmem_limit_bytes=int(min(vmem_bytes, VMEM_CEILING)))


def _as_bf16(w_ref, copy_ref):
    w = w_ref[...].astype(BF16)
    if copy_ref is not None:
        copy_ref[...] = w
    return w


def _rms(x, g):
    return x * lax.rsqrt(jnp.mean(x * x, axis=-1, keepdims=True) + EPS) * g


def _dot(a, b):
    return jnp.dot(a, b, preferred_element_type=F32)


def _dot_nt(a, b):
    return lax.dot_general(a, b, (((1,), (1,)), ((), ())), preferred_element_type=F32)


def _rope128(x, c, sa, sb):
    half = QK_ROPE // 2
    return x * c + pltpu.roll(x, half, 1) * sa + pltpu.roll(x, x.shape[1] - half, 1) * sb


def _in_proj_kernel(x_ref, gmix_ref, wa_ref, wkpe_ref, wmq_ref, gq_ref, gkv_ref, gmq_ref,
                    h_ref, u_ref, hq_ref, ckv_ref, kpe_ref, qm_ref, wa_copy, wkpe_copy, wmq_copy):
    def rows(ref, copy, lo, hi):
        w = ref[lo:hi, :].astype(BF16)
        copy[lo:hi, :] = w
        return w

    h = _rms(x_ref[...], gmix_ref[...]).astype(BF16)
    h_ref[...] = h
    u_ref[...] = _dot_nt(h, rows(wa_ref, wa_copy, Z_U, Z_QLAT))
    hq_ref[...] = _rms(_dot_nt(h, rows(wa_ref, wa_copy, Z_QLAT, Z_KVLAT)), gq_ref[...]).astype(BF16)
    ckv_ref[...] = _rms(_dot_nt(h, rows(wa_ref, wa_copy, Z_KVLAT, Z_MQ)), gkv_ref[...])
    kpe_ref[:, :QK_ROPE] = _dot_nt(h, rows(wkpe_ref, wkpe_copy, 0, QK_ROPE))
    kpe_ref[:, QK_ROPE:] = jnp.zeros((h.shape[0], 128 - QK_ROPE), F32)
    for hh in range(MEM_HEADS):
        lo = hh * MEM_HEAD_DIM
        m = _dot_nt(h, rows(wmq_ref, wmq_copy, lo, lo + MEM_HEAD_DIM))
        qm_ref[:, lo:lo + MEM_HEAD_DIM] = _rms(m, gmq_ref[...]).astype(BF16)


def _w_in_rows(n, start):
    return pl.BlockSpec((pl.Element(n), pl.Element(D_MODEL)),
                        lambda i: (pl.multiple_of(start + 0 * i, BF16_SUBLANES), 0))


def _front_kernel(x_ref, gmix_ref, wa_ref, wkpe_ref, wmq_ref, gq_ref, gkv_ref, gmq_ref, left_ref, mk_ref, mv_ref,
                  h_ref, d_ref, hq_ref, ckv_ref, kpe_ref, am_ref, utail_ref, xp_sc, qm_sc,
                  *, tm, blocks_per_seq, pos0):
    sb = pl.program_id(0) % blocks_per_seq
    h = _rms(x_ref[...], gmix_ref[...]).astype(BF16)
    h_ref[...] = h

    @pl.when(sb == 0)
    def _():
        xp_sc[:POOL_HALO, :] = left_ref[...]

    @pl.when(sb != 0)
    def _():
        xp_sc[:POOL_HALO, :] = xp_sc[tm:, :]

    xp_sc[POOL_HALO:, :] = _dot_nt(h, wa_ref[Z_U:Z_QLAT, :])
    utail_ref[...] = xp_sc[tm:, :]
    hq_ref[...] = _rms(_dot_nt(h, wa_ref[Z_QLAT:Z_KVLAT, :]), gq_ref[...]).astype(BF16)
    ckv_ref[...] = _rms(_dot_nt(h, wa_ref[Z_KVLAT:Z_MQ, :]), gkv_ref[...])
    kpe_ref[:, :QK_ROPE] = _dot_nt(h, wkpe_ref[...])
    kpe_ref[:, QK_ROPE:] = jnp.zeros((tm, 128 - QK_ROPE), F32)
    for hh in range(MEM_HEADS):
        lo = hh * MEM_HEAD_DIM
        m = _dot_nt(h, wmq_ref[lo:lo + MEM_HEAD_DIM, :])
        qm_sc[:, lo:lo + MEM_HEAD_DIM] = _rms(m, gmq_ref[...]).astype(BF16)
    _pool_windows(xp_sc, pos0 + sb * tm, tm, d_ref)
    _mem_attn_kernel(qm_sc, mk_ref, mv_ref, am_ref)


def _front(x, gmix, wa_t, wkpe_t, wmq_t, gq, gkv, gmq_scaled, left, mem_kv, tm, blocks_per_seq, pos0):
    t = x.shape[0]
    n_seq = t // (tm * blocks_per_seq)
    row = lambda w: pl.BlockSpec((tm, w), lambda i: (i, 0))
    full = lambda a: pl.BlockSpec(a.shape, lambda i: (0, 0))
    w_elems = wa_t.size + wkpe_t.size + wmq_t.size
    vmem = 2 * (tm * D_MODEL * 6 + w_elems * 2 + tm * (POOL_W * 2 + Q_LORA * 2 + KV_LORA * 4 + 128 * 4 + MEM_W * 2)
                + 2 * N_MEM * MEM_W * 4 + 2 * POOL_HALO * POOL_W * 4)
    vmem += (tm + POOL_HALO) * POOL_W * 4 + tm * MEM_W * 2 + 8 * tm * POOL_W * 4
    kern = functools.partial(_front_kernel, tm=tm, blocks_per_seq=blocks_per_seq, pos0=pos0)
    return pl.pallas_call(
        kern,
        grid=(t // tm,),
        in_specs=[row(D_MODEL), full(gmix), full(wa_t), full(wkpe_t), full(wmq_t),
                  full(gq), full(gkv), full(gmq_scaled),
                  pl.BlockSpec((None, POOL_HALO, POOL_W), lambda i: (i // blocks_per_seq, 0, 0)),
                  pl.BlockSpec((N_MEM, MEM_W), lambda i: (i // blocks_per_seq, 0)),
                  pl.BlockSpec((N_MEM, MEM_W), lambda i: (i // blocks_per_seq, 1))],
        out_specs=[row(D_MODEL), row(POOL_W), row(Q_LORA), row(KV_LORA), row(128), row(MEM_W),
                   pl.BlockSpec((None, POOL_HALO, POOL_W), lambda i: (i // blocks_per_seq, 0, 0))],
        out_shape=[jax.ShapeDtypeStruct((t, D_MODEL), BF16),
                   jax.ShapeDtypeStruct((t, POOL_W), BF16), jax.ShapeDtypeStruct((t, Q_LORA), BF16),
                   jax.ShapeDtypeStruct((t, KV_LORA), F32), jax.ShapeDtypeStruct((t, 128), F32),
                   jax.ShapeDtypeStruct((t, MEM_W), BF16), jax.ShapeDtypeStruct((n_seq, POOL_HALO, POOL_W), F32)],
        scratch_shapes=[pltpu.VMEM((tm + POOL_HALO, POOL_W), F32), pltpu.VMEM((tm, MEM_W), BF16)],
        compiler_params=_params(("arbitrary",), vmem),
        name="front",
    )(x, gmix, wa_t, wkpe_t, wmq_t, gq, gkv, gmq_scaled, left, mem_kv, mem_kv)


def _in_proj(x, gmix, w_in_t, gq, gkv, gmq_scaled):
    t = x.shape[0]
    row = lambda w: pl.BlockSpec((t, w), lambda i: (0, 0))
    full = lambda a: pl.BlockSpec(a.shape, lambda i: (0, 0))
    pieces = ((Z_MQ, 0), (QK_ROPE, W_IN_KPE), (MEM_W, W_IN_MQ))
    w_elems = sum(n for n, _ in pieces) * D_MODEL
    vmem = t * D_MODEL * 8 + w_elems * 6 + 2 * t * (POOL_W * 4 + Q_LORA * 2 + KV_LORA * 4 + 128 * 4 + MEM_W * 2)
    vmem += w_elems * 2 + 6 * t * POOL_W * 4
    return pl.pallas_call(
        _in_proj_kernel,
        grid=(1,),
        in_specs=[row(D_MODEL), full(gmix)] + [_w_in_rows(n, start) for n, start in pieces]
                 + [full(gq), full(gkv), full(gmq_scaled)],
        out_specs=[row(D_MODEL), row(POOL_W), row(Q_LORA), row(KV_LORA), row(128), row(MEM_W)]
                  + [pl.BlockSpec((n, D_MODEL), lambda i: (0, 0)) for n, _ in pieces],
        out_shape=[jax.ShapeDtypeStruct((t, D_MODEL), BF16),
                   jax.ShapeDtypeStruct((t, POOL_W), F32), jax.ShapeDtypeStruct((t, Q_LORA), BF16),
                   jax.ShapeDtypeStruct((t, KV_LORA), F32), jax.ShapeDtypeStruct((t, 128), F32),
                   jax.ShapeDtypeStruct((t, MEM_W), BF16)]
                  + [jax.ShapeDtypeStruct((n, D_MODEL), BF16) for n, _ in pieces],
        compiler_params=_params(("arbitrary",), vmem),
        name="in_proj",
    )(x, gmix, w_in_t, w_in_t, w_in_t, gq, gkv, gmq_scaled)


def _head_store(ref, head_major):
    def store(h, nope, rope):
        if head_major:
            ref[h, :, :QK_NOPE] = nope
            ref[h, :, QK_NOPE:] = rope
        else:
            lo = h * HEAD_PAD
            ref[:, lo:lo + QK_NOPE] = nope
            ref[:, lo + QK_NOPE:lo + HEAD_PAD] = rope
    return store


def _q_heads(hq, wq_ref, gq_ref, qtab, store):
    first = lax.broadcasted_iota(jnp.int32, (1, 128), 1) < QK_ROPE
    for h in range(MLA_HEADS):
        lo = h * HEAD_PAD
        qh = _dot(hq, wq_ref[:, lo:lo + HEAD_PAD])
        qn, qr = qh[:, :QK_NOPE], qh[:, QK_NOPE:]
        ss = jnp.sum(qn * qn + jnp.where(first, qr * qr, 0.0), axis=-1, keepdims=True)
        r = lax.rsqrt(ss * (1.0 / QK_HEAD) + EPS)
        store(h, (qn * r * gq_ref[:, :QK_NOPE]).astype(BF16), (qr * r * gq_ref[:, QK_NOPE:] * qtab).astype(BF16))


def _qkv_kernel(hq_ref, ckv_ref, kpe_ref, wq_ref, wkbt_ref, wvbt_ref, gq_ref, gkn_ref, gkr_ref,
                qtab_ref, c_ref, sa_ref, sb_ref, q_ref, k_ref, vt_ref):
    _q_heads(hq_ref[...], wq_ref, gq_ref, qtab_ref[...], _head_store(q_ref, True))
    store_k = _head_store(k_ref, True)
    ckv = ckv_ref[...].astype(BF16)
    kpe = kpe_ref[...]
    ss_pe = jnp.sum(kpe * kpe, axis=-1, keepdims=True)
    kr = _rope128(kpe * gkr_ref[...], c_ref[...], sa_ref[...], sb_ref[...])
    kr = kr + pltpu.roll(kr, QK_ROPE, 1)
    for p in range(MLA_HEADS // 2):
        kn2 = _dot_nt(ckv, wkbt_ref[p * 2 * QK_NOPE:(p + 1) * 2 * QK_NOPE, :])
        for s in range(2):
            kn = kn2[:, s * QK_NOPE:(s + 1) * QK_NOPE]
            r = lax.rsqrt((jnp.sum(kn * kn, axis=-1, keepdims=True) + ss_pe) * (1.0 / QK_HEAD) + EPS)
            store_k(2 * p + s, (kn * r * gkn_ref[...]).astype(BF16), (kr * r).astype(BF16))
    vt = _dot_nt(wvbt_ref[...], ckv)
    tm = vt.shape[1]
    ones_row = (lax.broadcasted_iota(jnp.int32, (V_ROWS - V_HEAD, tm), 0) == 0).astype(BF16)
    for h in range(MLA_HEADS):
        vt_ref[h, :V_HEAD, :] = vt[h * V_HEAD:(h + 1) * V_HEAD].astype(BF16)
        vt_ref[h, V_HEAD:, :] = ones_row


def _qkv(hq, ckv, kpe, wq, wkbt, wvbt, gq, gkn, gkr, tabs, tm, seq_blocks):
    t = hq.shape[0]
    n_seq = t // (tm * seq_blocks)
    qk_spec = pl.BlockSpec((None, MLA_HEADS, tm, HEAD_PAD), lambda i: (i // seq_blocks, 0, i % seq_blocks, 0))
    qk_shape = jax.ShapeDtypeStruct((n_seq, MLA_HEADS, tm * seq_blocks, HEAD_PAD), BF16)
    row = lambda w: pl.BlockSpec((tm, w), lambda i: (i, 0))
    full = lambda a: pl.BlockSpec(a.shape, lambda i: (0, 0))
    tab = pl.BlockSpec((tm, 128), lambda i: (i % seq_blocks, 0))
    vmem = 2 * (tm * (Q_LORA * 2 + KV_LORA * 4 + 128 * 4 + 4 * 128 * 4) + (wq.size + wkbt.size + wvbt.size) * 2
                + tm * (2 * MLA_HEADS * HEAD_PAD * 2 + MLA_HEADS * V_ROWS * 2))
    vmem += 8 * tm * HEAD_PAD * 4 + 2 * tm * MLA_HEADS * V_HEAD * 4
    return pl.pallas_call(
        _qkv_kernel,
        grid=(t // tm,),
        in_specs=[row(Q_LORA), row(KV_LORA), row(128), full(wq), full(wkbt), full(wvbt),
                  full(gq), full(gkn), full(gkr), tab, tab, tab, tab],
        out_specs=[qk_spec, qk_spec,
                   pl.BlockSpec((None, MLA_HEADS, None, V_ROWS, tm),
                                lambda i: (i // seq_blocks, 0, i % seq_blocks, 0, 0))],
        out_shape=[qk_shape, qk_shape,
                   jax.ShapeDtypeStruct((n_seq, MLA_HEADS, seq_blocks, V_ROWS, tm), BF16)],
        compiler_params=_params(("parallel",), vmem),
        name="qkv_proj",
    )(hq, ckv, kpe, wq, wkbt, wvbt, gq, gkn, gkr, *tabs)


def _q_only_kernel(hq_ref, wq_ref, gq_ref, qtab_ref, q_ref):
    _q_heads(hq_ref[...], wq_ref, gq_ref, qtab_ref[...], _head_store(q_ref, False))


def _q_only(hq, wq, gq, qtab, tm):
    t = hq.shape[0]
    row = lambda w: pl.BlockSpec((tm, w), lambda i: (i, 0))
    full = lambda a: pl.BlockSpec(a.shape, lambda i: (0, 0))
    vmem = 2 * (tm * (Q_LORA * 2 + 128 * 4 + MLA_HEADS * HEAD_PAD * 2) + wq.size * 2) + 8 * tm * HEAD_PAD * 4
    return pl.pallas_call(
        _q_only_kernel,
        grid=(t // tm,),
        in_specs=[row(Q_LORA), full(wq), full(gq), pl.BlockSpec((tm, 128), lambda i: (0, 0))],
        out_specs=row(MLA_HEADS * HEAD_PAD),
        out_shape=jax.ShapeDtypeStruct((t, MLA_HEADS * HEAD_PAD), BF16),
        compiler_params=_params(("parallel",), vmem),
        name="q_proj",
    )(hq, wq, gq, qtab)


def _flash_kernel(q_ref, k_ref, vt_ref, o_ref, *scratch, tq, tk, nq):
    *bufs, m_sc, acc_sc = scratch
    qi = pl.program_id(2)
    n_diag = tq // tk

    def scores(kj, s_sc, q_lo=0):
        s_sc[:, q_lo:] = _dot_nt(k_ref[kj * tk:(kj + 1) * tk, :], q_ref[q_lo:, :])

    def consume(kj, s_sc, key_chunk0=None, q_lo=0):
        s = s_sc[:, q_lo:]
        if key_chunk0 is not None:
            kc = key_chunk0 + lax.broadcasted_iota(jnp.int32, s.shape, 0) // CHUNK
            qc = (q_lo + lax.broadcasted_iota(jnp.int32, s.shape, 1)) // CHUNK
            s = jnp.where(qc >= kc, s, NEG_INF)
        m = m_sc[:, q_lo:]
        m_new = jnp.maximum(m, jnp.max(s, axis=0, keepdims=True))
        alpha = jnp.exp2(m - m_new)
        p = jnp.exp2(s - m_new).astype(BF16)
        acc_sc[:, q_lo:] = alpha * acc_sc[:, q_lo:] + _dot(vt_ref[kj], p)
        m_sc[:, q_lo:] = m_new

    def run(q_block):
        tiles = [(kj, 0, None) for kj in range(q_block * n_diag)]
        tiles += [(q_block * n_diag + d, d * tk, d * tk // CHUNK) for d in range(n_diag)]
        m_sc[...] = jnp.full(m_sc.shape, NEG_INF, F32)
        acc_sc[...] = jnp.zeros(acc_sc.shape, F32)
        for i in range(min(FLASH_AHEAD, len(tiles))):
            scores(tiles[i][0], bufs[i % len(bufs)], tiles[i][1])
        for i, (kj, q_lo, key_chunk0) in enumerate(tiles):
            if i + FLASH_AHEAD < len(tiles):
                nkj, nq_lo, _ = tiles[i + FLASH_AHEAD]
                scores(nkj, bufs[(i + FLASH_AHEAD) % len(bufs)], nq_lo)
            consume(kj, bufs[i % len(bufs)], key_chunk0, q_lo)
        acc = acc_sc[...]
        o_ref[...] = (acc[:V_HEAD] / acc[V_HEAD:V_HEAD + 1]).T.astype(BF16)

    for q_block in range(nq):
        pl.when(qi == q_block)(functools.partial(run, q_block))


def _flash(q, k, vt, batch, seq, tq):
    tk = vt.shape[-1]
    assert tq % tk == 0 and tk % CHUNK == 0 and seq % tq == 0
    nq = seq // tq
    n_bufs = FLASH_AHEAD + 1
    vmem = 2 * (tq * HEAD_PAD * 2 + seq * HEAD_PAD * 2 + seq * V_ROWS * 2 + tq * V_HEAD * 2)
    vmem += n_bufs * tk * tq * 4 + V_ROWS * tq * 4 + 4 * tk * tq * 4
    return pl.pallas_call(
        functools.partial(_flash_kernel, tq=tq, tk=tk, nq=nq),
        grid=(batch, MLA_HEADS, nq),
        in_specs=[pl.BlockSpec((None, None, tq, HEAD_PAD), lambda b, h, i: (b, h, i, 0)),
                  pl.BlockSpec((None, None, seq, HEAD_PAD), lambda b, h, i: (b, h, 0, 0)),
                  pl.BlockSpec((None, None, seq // tk, V_ROWS, tk), lambda b, h, i: (b, h, 0, 0, 0))],
        out_specs=pl.BlockSpec((tq, V_HEAD), lambda b, h, i: (b * nq + i, h)),
        out_shape=jax.ShapeDtypeStruct((batch * seq, MLA_HEADS * V_HEAD), BF16),
        scratch_shapes=[pltpu.VMEM((tk, tq), F32)] * n_bufs + [pltpu.VMEM((1, tq), F32),
                                                                pltpu.VMEM((V_ROWS, tq), F32)],
        compiler_params=_params(("parallel", "parallel", "arbitrary"), vmem),
        name="mla_flash",
    )(q, k, vt)


def _q_absorb_kernel(q_ref, wkbt_ref, qa_ref, qr_ref):
    qa_ref[...] = _dot(q_ref[:, :QK_NOPE], wkbt_ref[...]).astype(BF16)
    qr_ref[...] = q_ref[:, QK_NOPE:]


def _q_absorb(q, wkbt):
    t = q.shape[0]
    vmem = 2 * (t * HEAD_PAD * 2 + KV_LORA * QK_NOPE * 2 + t * KV_LORA * 2 + t * 128 * 2) + 2 * t * KV_LORA * 4
    return pl.pallas_call(
        _q_absorb_kernel,
        grid=(MLA_HEADS,),
        in_specs=[pl.BlockSpec((t, HEAD_PAD), lambda h: (0, h)),
                  pl.BlockSpec((QK_NOPE, KV_LORA), lambda h: (h, 0))],
        out_specs=[pl.BlockSpec((None, t, KV_LORA), lambda h: (h, 0, 0)),
                   pl.BlockSpec((None, t, 128), lambda h: (h, 0, 0))],
        out_shape=[jax.ShapeDtypeStruct((MLA_HEADS, t, KV_LORA), BF16),
                   jax.ShapeDtypeStruct((MLA_HEADS, t, 128), BF16)],
        compiler_params=_params(("parallel",), vmem),
        name="q_absorb",
    )(q, wkbt)


def _cache_attn_kernel(wkbt_ref, qa_ref, qr_ref, lat_ref, kpet_ref, ct_ref, st_ref,
                       nlat_ref, nkpet_ref, nct_ref, nst_ref, gkr_ref,
                       o_ref, s_sc, m_sc, l_sc, acc_sc, *, nq, n_new, past, tk):
    j = pl.program_id(1)
    last = pl.num_programs(1) - 1
    rows = MLA_HEADS * nq
    half = QK_ROPE // 2

    @pl.when(j == 0)
    def _():
        m_sc[...] = jnp.full(m_sc.shape, NEG_INF, F32)
        l_sc[...] = jnp.zeros(l_sc.shape, F32)
        acc_sc[...] = jnp.zeros(acc_sc.shape, F32)

    def block(lat, kpet, ct, st, width, visible):
        c = lat.astype(BF16)
        kt = _dot_nt(wkbt_ref[...], c)
        s = _dot_nt(qa_ref[...].reshape(rows, KV_LORA), c)
        ss_pe = jnp.sum(kpet * kpet, axis=0, keepdims=True)
        kr = kpet * gkr_ref[...]
        k1, k2 = kr[:half], kr[half:]
        o1, o2 = k1 * ct - k2 * st, k2 * ct + k1 * st
        krot = jnp.concatenate([o1, o2, o1, o2], axis=0).astype(BF16)
        s = s + _dot(qr_ref[...].reshape(rows, 128), krot)
        for h in range(MLA_HEADS):
            kh = kt[h * QK_NOPE:(h + 1) * QK_NOPE]
            r = lax.rsqrt((jnp.sum(kh * kh, axis=0, keepdims=True) + ss_pe) * (1.0 / QK_HEAD) + EPS)
            s_sc[h * nq:(h + 1) * nq, :width] = s[h * nq:(h + 1) * nq] * r
        s = s_sc[:, :width]
        if visible is not None:
            s = jnp.where(visible, s, NEG_INF)
        m = m_sc[...]
        m_new = jnp.maximum(m, jnp.max(s, axis=-1, keepdims=True))
        alpha = jnp.exp2(m - m_new)
        p = jnp.exp2(s - m_new)
        l_sc[...] = alpha * l_sc[...] + jnp.sum(p, axis=-1, keepdims=True)
        m_sc[...] = m_new
        acc_sc[...] = alpha * acc_sc[...] + _dot(p.astype(BF16), c)

    @pl.when(j < last)
    def _():
        block(lat_ref[...], kpet_ref[...], ct_ref[...], st_ref[...], tk, None)

    @pl.when(j == last)
    def _():
        kk = lax.broadcasted_iota(jnp.int32, (1, 128), 1)
        visible = kk < n_new
        if (past + n_new - 1) // CHUNK != past // CHUNK:
            qpos = past + lax.broadcasted_iota(jnp.int32, (rows, 1), 0) % nq
            visible = visible & (qpos // CHUNK >= (past + kk) // CHUNK)
        block(nlat_ref[...], nkpet_ref[...], nct_ref[...], nst_ref[...], 128, visible)
        o_ref[...] = (acc_sc[...] / l_sc[...]).astype(BF16)


def _cache_attn(wkbt, qa, qr, lat, kpet, ct, st, nlat, nkpet, nct, nst, gkr, nq, n_new, tk):
    batch, past, _ = lat.shape
    assert past % tk == 0 and past >= tk
    nkb = past // tk
    rows = MLA_HEADS * nq
    cj = lambda j: jnp.minimum(j, nkb - 1)
    kern = functools.partial(_cache_attn_kernel, nq=nq, n_new=n_new, past=past, tk=tk)
    vmem = 2 * (wkbt.size * 2 + rows * (KV_LORA + 128) * 2 + tk * KV_LORA * 4 + 3 * 64 * tk * 4 + rows * KV_LORA * 2)
    vmem += (MLA_HEADS * QK_NOPE + 4 * rows) * tk * 4 + tk * KV_LORA * 2 + rows * KV_LORA * 8
    return pl.pallas_call(
        kern,
        grid=(batch, nkb + 1),
        in_specs=[pl.BlockSpec(wkbt.shape, lambda b, j: (0, 0)),
                  pl.BlockSpec((MLA_HEADS, nq, KV_LORA), lambda b, j: (0, b, 0)),
                  pl.BlockSpec((MLA_HEADS, nq, 128), lambda b, j: (0, b, 0)),
                  pl.BlockSpec((None, tk, KV_LORA), lambda b, j: (b, cj(j), 0)),
                  pl.BlockSpec((None, QK_ROPE, tk), lambda b, j: (b, 0, cj(j))),
                  pl.BlockSpec((QK_ROPE // 2, tk), lambda b, j: (0, cj(j))),
                  pl.BlockSpec((QK_ROPE // 2, tk), lambda b, j: (0, cj(j))),
                  pl.BlockSpec((None, 128, KV_LORA), lambda b, j: (b, 0, 0)),
                  pl.BlockSpec((None, QK_ROPE, 128), lambda b, j: (b, 0, 0)),
                  pl.BlockSpec((QK_ROPE // 2, 128), lambda b, j: (0, 0)),
                  pl.BlockSpec((QK_ROPE // 2, 128), lambda b, j: (0, 0)),
                  pl.BlockSpec((QK_ROPE, 1), lambda b, j: (0, 0))],
        out_specs=pl.BlockSpec((None, rows, KV_LORA), lambda b, j: (b, 0, 0)),
        out_shape=jax.ShapeDtypeStruct((batch, rows, KV_LORA), BF16),
        scratch_shapes=[pltpu.VMEM((rows, tk), F32), pltpu.VMEM((rows, 1), F32),
                        pltpu.VMEM((rows, 1), F32), pltpu.VMEM((rows, KV_LORA), F32)],
        compiler_params=_params(("parallel", "arbitrary"), vmem),
        name="mla_cache_attn",
    )(wkbt, qa, qr, lat, kpet, ct, st, nlat, nkpet, nct, nst, gkr)


def _v_up_kernel(ol_ref, wvbt_ref, o_ref):
    b, nq, _ = ol_ref.shape
    o_ref[...] = _dot_nt(ol_ref[...].reshape(b * nq, KV_LORA), wvbt_ref[...]).astype(BF16)


def _v_up(o_lat, wvbt, nq):
    batch = o_lat.shape[0]
    t = batch * nq
    vmem = 2 * (t * KV_LORA * 2 + KV_LORA * V_HEAD * 2 + t * V_HEAD * 2) + 2 * t * V_HEAD * 4
    return pl.pallas_call(
        _v_up_kernel,
        grid=(MLA_HEADS,),
        in_specs=[pl.BlockSpec((batch, nq, KV_LORA), lambda h: (0, h, 0)),
                  pl.BlockSpec((V_HEAD, KV_LORA), lambda h: (h, 0))],
        out_specs=pl.BlockSpec((t, V_HEAD), lambda h: (0, h)),
        out_shape=jax.ShapeDtypeStruct((t, MLA_HEADS * V_HEAD), BF16),
        compiler_params=_params(("parallel",), vmem),
        name="v_up",
    )(o_lat, wvbt)


def _mem_kv_kernel(mem_ref, gmem_ref, w_ref, gk_ref, kv_ref):
    j = pl.program_id(0)
    h = _rms(mem_ref[...], gmem_ref[...]).astype(BF16)

    @pl.when(j == 0)
    def _():
        for hh in range(MEM_HEADS):
            lo = hh * MEM_HEAD_DIM
            w = w_ref[:, lo:lo + MEM_HEAD_DIM].astype(BF16)
            kv_ref[:, lo:lo + MEM_HEAD_DIM] = _rms(_dot(h, w), gk_ref[...])

    @pl.when(j == 1)
    def _():
        kv_ref[...] = _dot(h, w_ref[...].astype(BF16))


def _mem_kv(mem, gmem, w, gk):
    t = mem.shape[0]
    vmem = 2 * (t * D_MODEL * 4 + D_MODEL * MEM_W * w.dtype.itemsize + t * MEM_W * 4) + t * D_MODEL * 6
    vmem += t * MEM_W * 4 + D_MODEL * MEM_W * 2
    return pl.pallas_call(
        _mem_kv_kernel,
        grid=(2,),
        in_specs=[pl.BlockSpec((t, D_MODEL), lambda j: (0, 0)), pl.BlockSpec((1, D_MODEL), lambda j: (0, 0)),
                  pl.BlockSpec((D_MODEL, MEM_W), lambda j: (0, j)), pl.BlockSpec((1, MEM_HEAD_DIM), lambda j: (0, 0))],
        out_specs=pl.BlockSpec((t, MEM_W), lambda j: (0, j)),
        out_shape=jax.ShapeDtypeStruct((t, 2 * MEM_W), F32),
        compiler_params=_params(("parallel",), vmem),
        name="memory_kv",
    )(mem, gmem, w, gk)


def _mem_attn_kernel(qm_ref, k_ref, v_ref, o_ref):
    for hh in range(MEM_HEADS):
        sl = slice(hh * MEM_HEAD_DIM, (hh + 1) * MEM_HEAD_DIM)
        s = _dot_nt(qm_ref[:, sl], k_ref[:, sl].astype(BF16))
        p = jnp.exp(s - jnp.max(s, axis=-1, keepdims=True))
        l = jnp.sum(p, axis=-1, keepdims=True)
        o_ref[:, sl] = (_dot(p.astype(BF16), v_ref[:, sl].astype(BF16)) / l).astype(BF16)


def _mem_attn(qm, k_arr, v_arr, k_col, v_col, tm, blocks_per_batch):
    t = qm.shape[0]
    vmem = 2 * (2 * tm * MEM_W * 2 + 2 * N_MEM * MEM_W * 4) + 6 * tm * N_MEM * 4 + 2 * N_MEM * MEM_W * 2
    return pl.pallas_call(
        _mem_attn_kernel,
        grid=(t // tm,),
        in_specs=[pl.BlockSpec((tm, MEM_W), lambda i: (i, 0)),
                  pl.BlockSpec((N_MEM, MEM_W), lambda i: (i // blocks_per_batch, k_col)),
                  pl.BlockSpec((N_MEM, MEM_W), lambda i: (i // blocks_per_batch, v_col))],
        out_specs=pl.BlockSpec((tm, MEM_W), lambda i: (i, 0)),
        out_shape=jax.ShapeDtypeStruct((t, MEM_W), BF16),
        compiler_params=_params(("parallel",), vmem),
        name="memory_attn",
    )(qm, k_arr, v_arr)


def _pool_windows(xp_sc, first_pos, tm, d_ref):
    pos = first_pos + lax.broadcasted_iota(jnp.int32, (tm, 1), 0)
    for g, win in enumerate(POOL_WINDOWS):
        sl = slice(g * POOL_GW, (g + 1) * POOL_GW)
        x = xp_sc[:, sl]
        wsum, k = x, 1
        while k < win:
            wsum = wsum + pltpu.roll(wsum, k, 0)
            k *= 2
        cnt = jnp.minimum(pos + 1, win).astype(F32)
        d_ref[:, sl] = (wsum[POOL_HALO:] / cnt - x[POOL_HALO:]).astype(BF16)


def _pool_kernel(u_ref, prev_ref, left_ref, d_ref, xp_sc, *, tm, blocks_per_seq, pos0):
    sb = pl.program_id(0) % blocks_per_seq
    xp_sc[:POOL_HALO, :] = jnp.where(sb == 0, left_ref[...], prev_ref[...])
    xp_sc[POOL_HALO:, :] = u_ref[...]
    _pool_windows(xp_sc, pos0 + sb * tm, tm, d_ref)


def _pool(u, left, tm, blocks_per_seq, pos0):
    t = u.shape[0]
    halo_per_blk = tm // POOL_HALO
    kern = functools.partial(_pool_kernel, tm=tm, blocks_per_seq=blocks_per_seq, pos0=pos0)
    vmem = 2 * (tm * POOL_W * 4 + 2 * POOL_HALO * POOL_W * 4 + tm * POOL_W * 2) + (tm + POOL_HALO) * POOL_W * 4
    vmem += 6 * tm * POOL_GW * 4
    return pl.pallas_call(
        kern,
        grid=(t // tm,),
        in_specs=[pl.BlockSpec((tm, POOL_W), lambda i: (i, 0)),
                  pl.BlockSpec((POOL_HALO, POOL_W), lambda i: (jnp.maximum(i * halo_per_blk - 1, 0), 0)),
                  pl.BlockSpec((None, POOL_HALO, POOL_W), lambda i: (i // blocks_per_seq, 0, 0))],
        out_specs=pl.BlockSpec((tm, POOL_W), lambda i: (i, 0)),
        out_shape=jax.ShapeDtypeStruct((t, POOL_W), BF16),
        scratch_shapes=[pltpu.VMEM((tm + POOL_HALO, POOL_W), F32)],
        compiler_params=_params(("parallel",), vmem),
        name="pool_windows",
    )(u, u, left)


def _merge_kernel(h_ref, wg0_ref, wg1_ref, wg2_ref, bg0_ref, bg1_ref, bg2_ref,
                  d_ref, wpool_ref, pscale_ref, o_ref, wmla_ref, am_ref, wmem_ref,
                  out_ref, *w_copies):
    wg_copies, (wpool_copy, wmla_copy, wmem_copy) = (w_copies[:3], w_copies[3:]) if w_copies else ((None,) * 3,) * 2
    h = h_ref[...]
    y_pool = _dot(d_ref[...], _as_bf16(wpool_ref, wpool_copy)) * pscale_ref[...]
    y_mla = _dot(o_ref[...], _as_bf16(wmla_ref, wmla_copy))
    y_mem = _dot(am_ref[...], _as_bf16(wmem_ref, wmem_copy))
    gate = lambda w_ref, copy, b_ref: jax.nn.sigmoid(_dot_nt(h, _as_bf16(w_ref, copy)) + b_ref[...])
    merged = gate(wg0_ref, wg_copies[0], bg0_ref) * y_pool
    merged += gate(wg1_ref, wg_copies[1], bg1_ref) * y_mla
    merged += gate(wg2_ref, wg_copies[2], bg2_ref) * y_mem
    out_ref[...] = merged.astype(BF16)


def _merge(h, wgates, bgate, d, wpool, pscale, o, wmla, am, wmem, tm, tn, emit_bf16=False):
    t = h.shape[0]
    assert not emit_bf16 or t == tm
    assert POOL_OUT_GW % tn == 0
    nj = D_MODEL // tn
    per_group = POOL_OUT_GW // tn
    rowfull = lambda w: pl.BlockSpec((tm, w), lambda i, j: (i, 0))
    gate_rows = pl.BlockSpec((tn, D_MODEL), lambda i, j: (j, 0))
    if emit_bf16:
        gate_in = [pl.BlockSpec((pl.Element(tn), pl.Element(D_MODEL)),
                                lambda i, j, b=b: (pl.multiple_of(W_IN_GATE + (b * nj + j) * tn, BF16_SUBLANES), 0))
                   for b in range(3)]
        gate_args = (wgates,) * 3
    else:
        gate_in = [gate_rows] * 3
        gate_args = tuple(wgates)
    gate_b = lambda b: pl.BlockSpec((1, tn), lambda i, j: (0, b * nj + j))
    col = lambda k: pl.BlockSpec((k, tn), lambda i, j: (0, j))
    pool_spec = pl.BlockSpec((None, POOL_GW, tn), lambda i, j: (j // per_group, 0, j % per_group))
    w_bytes = wmla.dtype.itemsize + 2 * emit_bf16
    w_rows = 3 * D_MODEL + POOL_GW + D_MODEL + MEM_W
    vmem = 2 * (tm * D_MODEL * 2 + w_rows * tn * w_bytes + tm * POOL_GW * 2 + tm * D_MODEL * 2
                + tm * MEM_W * 2 + tm * tn * 2)
    vmem += 8 * tm * tn * 4 + w_rows * tn * 2 * emit_bf16
    y_spec = pl.BlockSpec((tm, tn), lambda i, j: (i, j))
    y_shape = jax.ShapeDtypeStruct((t, D_MODEL), BF16)
    copies = ([jax.ShapeDtypeStruct((D_MODEL, D_MODEL), BF16)] * 3
              + [jax.ShapeDtypeStruct(w.shape, BF16) for w in (wpool, wmla, wmem)])
    return pl.pallas_call(
        _merge_kernel,
        grid=(t // tm, nj),
        in_specs=[rowfull(D_MODEL)] + gate_in
                 + [gate_b(0), gate_b(1), gate_b(2),
                    pl.BlockSpec((tm, POOL_GW), lambda i, j: (i, j // per_group)), pool_spec,
                    pl.BlockSpec((1, tn), lambda i, j: (0, j)),
                    rowfull(D_MODEL), col(D_MODEL), rowfull(MEM_W), col(MEM_W)],
        out_specs=[y_spec] + [gate_rows] * 3 + [pool_spec, col(D_MODEL), col(MEM_W)] if emit_bf16 else y_spec,
        out_shape=[y_shape] + copies if emit_bf16 else y_shape,
        compiler_params=_params(("parallel", "arbitrary" if emit_bf16 else "parallel"), vmem),
        name="gated_merge",
    )(h, *gate_args, bgate, bgate, bgate, d, wpool, pscale, o, wmla, am, wmem)


def _out_proj_kernel(x_ref, m_ref, w_ref, y_ref, *w_copy):
    y_ref[...] = x_ref[...] + _dot(m_ref[...], _as_bf16(w_ref, *w_copy or (None,)))


def _out_proj(x, merged, w, tm, tn, emit_bf16=False):
    t = x.shape[0]
    assert not emit_bf16 or t == tm
    w_spec = pl.BlockSpec((D_MODEL, tn), lambda i, j: (0, j))
    vmem = 2 * (2 * tm * tn * 4 + tm * D_MODEL * 2 + D_MODEL * tn * (w.dtype.itemsize + 2 * emit_bf16)) + 2 * tm * tn * 4
    y_spec = pl.BlockSpec((tm, tn), lambda i, j: (i, j))
    y_shape = jax.ShapeDtypeStruct((t, D_MODEL), F32)
    return pl.pallas_call(
        _out_proj_kernel,
        grid=(t // tm, D_MODEL // tn),
        in_specs=[pl.BlockSpec((tm, tn), lambda i, j: (i, j)),
                  pl.BlockSpec((tm, D_MODEL), lambda i, j: (i, 0)), w_spec],
        out_specs=[y_spec, w_spec] if emit_bf16 else y_spec,
        out_shape=[y_shape, jax.ShapeDtypeStruct(w.shape, BF16)] if emit_bf16 else y_shape,
        compiler_params=_params(("parallel", "arbitrary" if emit_bf16 else "parallel"), vmem),
        name="out_proj",
    )(x, merged, w)


def _ffn_kernel(x_ref, g_ref, wup_ref, wdown_ref, y_ref, *rest):
    *w_copies, h_sc, acc_sc = rest
    wup_copy, wdown_copy = w_copies or (None, None)
    j = pl.program_id(1)

    @pl.when(j == 0)
    def _():
        h_sc[...] = _rms(x_ref[...], g_ref[...]).astype(BF16)
        acc_sc[...] = jnp.zeros(acc_sc.shape, F32)

    f = jnp.maximum(_dot(h_sc[...], _as_bf16(wup_ref, wup_copy)), 0.0)
    acc_sc[...] += _dot((f * f).astype(BF16), _as_bf16(wdown_ref, wdown_copy))

    @pl.when(j == pl.num_programs(1) - 1)
    def _():
        y_ref[...] = x_ref[...] + acc_sc[...]


def _ffn(x, g, wup, wdown, tm, tf, emit_bf16=False):
    t = x.shape[0]
    assert not emit_bf16 or t == tm
    w_bytes = wup.dtype.itemsize + 2 * emit_bf16
    vmem = 2 * (2 * tm * D_MODEL * 4 + 2 * D_MODEL * tf * w_bytes) + tm * D_MODEL * 6 + 3 * tm * tf * 4 + tm * D_MODEL * 4
    vmem += 2 * D_MODEL * tf * 2 * emit_bf16
    up_spec = pl.BlockSpec((D_MODEL, tf), lambda i, j: (0, j))
    down_spec = pl.BlockSpec((tf, D_MODEL), lambda i, j: (j, 0))
    y_spec = pl.BlockSpec((tm, D_MODEL), lambda i, j: (i, 0))
    y_shape = jax.ShapeDtypeStruct((t, D_MODEL), F32)
    return pl.pallas_call(
        _ffn_kernel,
        grid=(t // tm, D_FF // tf),
        in_specs=[pl.BlockSpec((tm, D_MODEL), lambda i, j: (i, 0)),
                  pl.BlockSpec((1, D_MODEL), lambda i, j: (0, 0)), up_spec, down_spec],
        out_specs=[y_spec, up_spec, down_spec] if emit_bf16 else y_spec,
        out_shape=([y_shape, jax.ShapeDtypeStruct(wup.shape, BF16), jax.ShapeDtypeStruct(wdown.shape, BF16)]
                   if emit_bf16 else y_shape),
        scratch_shapes=[pltpu.VMEM((tm, D_MODEL), BF16), pltpu.VMEM((tm, D_MODEL), F32)],
        compiler_params=_params(("parallel", "arbitrary"), vmem),
        name="ffn",
    )(x, g, wup, wdown)


def _rope_tables(pos):
    half = QK_ROPE // 2
    inv = 1.0 / (ROPE_THETA ** (jnp.arange(half, dtype=F32) * (2.0 / QK_ROPE)))
    ang = pos.astype(F32)[:, None] * inv[None, :]
    return jnp.cos(ang), jnp.sin(ang)


def _lane_tables(cos, sin):
    z = jnp.zeros_like(cos)
    c = jnp.concatenate([cos, cos, z, z], axis=1)
    sa = jnp.concatenate([z, sin, z, z], axis=1)
    sb = jnp.concatenate([-sin, z, z, z], axis=1)
    return c, sa, sb


def _query_table(cos, sin):
    return jnp.concatenate([cos, cos, -sin, sin], axis=1)


def _row(v):
    return v.astype(F32).reshape(1, -1)


def _pad_last(a, width):
    return jnp.pad(a, [(0, 0)] * (a.ndim - 1) + [(0, width - a.shape[-1])])


def kernel(x_prompt, mem_prompt, x_sample, cache_mla_latent, cache_mla_kpe, state_pool, cache_mem_k, cache_mem_v,
           g_mix, w_in, b_gate, w_pool, pool_scale, g_q_lat, w_qb, g_q_head, g_kv_lat, w_kb, w_vb, g_k_head,
           w_mla_o, g_mem, w_mem_kv, g_mem_q, g_mem_k, w_mem_o, w_out, g_ff, w_up, w_down):
    batch, seq, _ = x_prompt.shape
    dec_batch, dec_seq, _ = x_sample.shape
    past = cache_mla_latent.shape[1]

    w_in_t = w_in.T
    bgate = _row(b_gate)
    half = QK_ROPE // 2

    def swap_halves(a):
        return jnp.concatenate([a[..., half:], a[..., :half]], axis=-1)

    def head_layout(a):
        rope = a[..., QK_NOPE:]
        return jnp.concatenate([a, swap_halves(rope)], axis=-1)

    wq = head_layout(w_qb.reshape(Q_LORA, MLA_HEADS, QK_HEAD)).reshape(Q_LORA, MLA_HEADS * HEAD_PAD).astype(BF16)
    wkbt = w_kb.T.astype(BF16)
    wvbt = w_vb.T.astype(BF16)
    gq_p = _row(head_layout(g_q_head.astype(F32) * MLA_SCALE_LOG2))
    gkn = _row(g_k_head[:QK_NOPE])
    gkr = _row(_pad_last(g_k_head[QK_NOPE:], 128))
    gq_s = _row(head_layout(jnp.concatenate([g_q_head[:QK_NOPE].astype(F32) * g_k_head[:QK_NOPE].astype(F32),
                                             g_q_head[QK_NOPE:].astype(F32)]) * MLA_SCALE_LOG2))
    gmq = _row(g_mem_q.astype(F32) * MEM_SCALE)
    gmix, gql, gkvl, gff, gmem, gmk, pscale = map(_row, (g_mix, g_q_lat, g_kv_lat, g_ff, g_mem, g_mem_k, pool_scale))

    ts = dec_batch * dec_seq
    xs = x_sample.reshape(ts, D_MODEL)
    h_s, u_s, hq_s, ckv_s, kpe_s, qm_s, wa_t, wkpe_t, wmq_t = _in_proj(xs, gmix, w_in_t, gql, gkvl, gmq)
    pos_s = past + jnp.arange(dec_seq, dtype=jnp.int32)
    cos_s, sin_s = _rope_tables(pos_s)
    q_s = _q_only(hq_s, wq, gq_s, jnp.tile(_query_table(cos_s, sin_s), (dec_batch, 1)), ts)
    qa, qr = _q_absorb(q_s, wkbt)
    cos_c, sin_c = _rope_tables(jnp.arange(past, dtype=jnp.int32))
    pad_new = 128 - dec_seq
    nlat = jnp.pad(ckv_s.reshape(dec_batch, dec_seq, KV_LORA), ((0, 0), (0, pad_new), (0, 0)))
    nkpet = jnp.pad(jnp.swapaxes(kpe_s[:, :QK_ROPE].reshape(dec_batch, dec_seq, QK_ROPE), 1, 2),
                    ((0, 0), (0, 0), (0, pad_new)))
    o_lat = _cache_attn(wkbt, qa, qr, cache_mla_latent, jnp.swapaxes(cache_mla_kpe, 1, 2),
                        cos_c.T, sin_c.T, nlat, nkpet,
                        jnp.pad(cos_s.T, ((0, 0), (0, pad_new))), jnp.pad(sin_s.T, ((0, 0), (0, pad_new))),
                        g_k_head[QK_NOPE:].astype(F32).reshape(QK_ROPE, 1),
                        dec_seq, dec_seq, 1024)
    o_s = _v_up(o_lat, wvbt, dec_seq)
    am_s = _mem_attn(qm_s, cache_mem_k.reshape(dec_batch * N_MEM, MEM_W), cache_mem_v.reshape(dec_batch * N_MEM, MEM_W),
                     0, 0, dec_seq, 1)
    left_s = jnp.pad(state_pool.astype(F32), ((0, 0), (POOL_HALO - POOL_STATE, 0), (0, 0)))
    d_s = _pool(u_s, left_s, dec_seq, 1, past)
    merged_s, *wgates, wpool, wmla, wmem_o = _merge(h_s, w_in_t, bgate, d_s, w_pool, pscale, o_s, w_mla_o,
                                                    am_s, w_mem_o, ts, 256, emit_bf16=True)
    x1_s, wout = _out_proj(xs, merged_s, w_out, ts, 512, emit_bf16=True)
    y_s, wup, wdown = _ffn(x1_s, gff, w_up, w_down, ts, 512, emit_bf16=True)
    y_s = y_s.reshape(dec_batch, dec_seq, D_MODEL)
    lat_s = ckv_s.reshape(dec_batch, dec_seq, KV_LORA)
    kpe_s_out = kpe_s[:, :QK_ROPE].reshape(dec_batch, dec_seq, QK_ROPE)
    pool_s = u_s.reshape(dec_batch, dec_seq, POOL_W)[:, dec_seq - POOL_STATE:]

    tm = 512
    xp = x_prompt.reshape(batch * seq, D_MODEL)
    mem_kv = _mem_kv(mem_prompt.reshape(batch * N_MEM, D_MODEL), gmem, w_mem_kv, gmk)
    h, d, hq, ckv, kpe, am, u_tail = _front(xp, gmix, wa_t, wkpe_t, wmq_t, gql, gkvl, gmq,
                                            jnp.zeros((batch, POOL_HALO, POOL_W), F32), mem_kv, tm, seq // tm, 0)
    cos_p, sin_p = _rope_tables(jnp.arange(seq, dtype=jnp.int32))
    q, k, vt = _qkv(hq, ckv, kpe, wq, wkbt, wvbt, gq_p, gkn, gkr,
                    (_query_table(cos_p, sin_p),) + _lane_tables(cos_p, sin_p), tm, seq // tm)
    o = _flash(q, k, vt, batch, seq, 2048)
    merged = _merge(h, wgates, bgate, d, wpool, pscale, o, wmla, am, wmem_o, 2 * tm, POOL_OUT_GW)
    y_p = _ffn(_out_proj(xp, merged, wout, 2 * tm, 1024), gff, wup, wdown, tm, 1024).reshape(batch, seq, D_MODEL)
    lat_p = ckv.reshape(batch, seq, KV_LORA)
    kpe_p = kpe[:, :QK_ROPE].reshape(batch, seq, QK_ROPE)
    pool_p = u_tail[:, POOL_HALO - POOL_STATE:]
    mem_k_p = mem_kv[:, :MEM_W].reshape(batch, N_MEM, MEM_HEADS, MEM_HEAD_DIM)
    mem_v_p = mem_kv[:, MEM_W:].reshape(batch, N_MEM, MEM_HEADS, MEM_HEAD_DIM)

    return (y_p, y_s, lat_p, kpe_p, pool_p, mem_k_p, mem_v_p, lat_s, kpe_s_out, pool_s)
```

```python
import functools

import jax
import jax.numpy as jnp
from jax import lax
from jax.experimental import pallas as pl
from jax.experimental.pallas import tpu as pltpu

F32 = jnp.float32
BF16 = jnp.bfloat16

EPS = 1e-6
CHUNK = 64
D_MODEL = 2048
POOL_WINDOWS = (2, 4, 8, 16)
POOL_W = 1024
POOL_GW = 256
POOL_OUT_GW = 512
POOL_STATE = 15
POOL_HALO = 16
assert all(w & (w - 1) == 0 and w <= POOL_HALO for w in POOL_WINDOWS)
MLA_HEADS = 16
Q_LORA = 512
KV_LORA = 512
QK_NOPE = 128
QK_ROPE = 64
QK_HEAD = QK_NOPE + QK_ROPE
HEAD_PAD = 256
V_HEAD = 128
V_ROWS = 144
ROPE_THETA = 10000.0
LOG2_E = 1.4426950408889634
MLA_SCALE = QK_HEAD ** -0.5
MLA_SCALE_LOG2 = MLA_SCALE * LOG2_E
N_MEM = 256
MEM_HEADS = 4
MEM_HEAD_DIM = 256
MEM_W = MEM_HEADS * MEM_HEAD_DIM
MEM_SCALE = MEM_HEAD_DIM ** -0.5
D_FF = 4 * D_MODEL
NEG_INF = -1e30
FLASH_AHEAD = 2
CACHE_AHEAD = 1

Z_U = 0
Z_QLAT = Z_U + POOL_W
Z_KVLAT = Z_QLAT + Q_LORA
Z_MQ = Z_KVLAT + KV_LORA
W_IN_KPE = Z_MQ
W_IN_MQ = W_IN_KPE + QK_ROPE
W_IN_GATE = W_IN_MQ + MEM_W

BF16_SUBLANES = 16
V7X_VMEM_BYTES = 64 * 1024 * 1024
VMEM_CEILING = V7X_VMEM_BYTES - 6 * 1024 * 1024


def _params(semantics, vmem_bytes):
    return pltpu.CompilerParams(dimension_semantics=semantics,
                                vmem_limit_bytes=int(min(vmem_bytes, VMEM_CEILING)))


def _as_bf16(w_ref, copy_ref):
    w = w_ref[...].astype(BF16)
    if copy_ref is not None:
        copy_ref[...] = w
    return w


def _rms(x, g):
    return x * lax.rsqrt(jnp.mean(x * x, axis=-1, keepdims=True) + EPS) * g


def _dot(a, b):
    return jnp.dot(a, b, preferred_element_type=F32)


def _dot_nt(a, b):
    return lax.dot_general(a, b, (((1,), (1,)), ((), ())), preferred_element_type=F32)


def _rope128(x, c, sa, sb):
    half = QK_ROPE // 2
    return x * c + pltpu.roll(x, half, 1) * sa + pltpu.roll(x, x.shape[1] - half, 1) * sb


def _in_proj_kernel(x_ref, gmix_ref, wa_ref, wkpe_ref, wmq_ref, gq_ref, gkv_ref, gmq_ref,
                    h_ref, u_ref, hq_ref, ckv_ref, kpe_ref, qm_ref, wa_copy, wkpe_copy, wmq_copy):
    def rows(ref, copy, lo, hi):
        w = ref[lo:hi, :].astype(BF16)
        copy[lo:hi, :] = w
        return w

    h = _rms(x_ref[...], gmix_ref[...]).astype(BF16)
    h_ref[...] = h
    u_ref[...] = _dot_nt(h, rows(wa_ref, wa_copy, Z_U, Z_QLAT))
    hq_ref[...] = _rms(_dot_nt(h, rows(wa_ref, wa_copy, Z_QLAT, Z_KVLAT)), gq_ref[...]).astype(BF16)
    ckv_ref[...] = _rms(_dot_nt(h, rows(wa_ref, wa_copy, Z_KVLAT, Z_MQ)), gkv_ref[...])
    kpe_ref[:, :QK_ROPE] = _dot_nt(h, rows(wkpe_ref, wkpe_copy, 0, QK_ROPE))
    kpe_ref[:, QK_ROPE:] = jnp.zeros((h.shape[0], 128 - QK_ROPE), F32)
    for hh in range(MEM_HEADS):
        lo = hh * MEM_HEAD_DIM
        m = _dot_nt(h, rows(wmq_ref, wmq_copy, lo, lo + MEM_HEAD_DIM))
        qm_ref[:, lo:lo + MEM_HEAD_DIM] = _rms(m, gmq_ref[...]).astype(BF16)


def _w_in_rows(n, start):
    return pl.BlockSpec((pl.Element(n), pl.Element(D_MODEL)),
                        lambda i: (pl.multiple_of(start + 0 * i, BF16_SUBLANES), 0))


def _front_kernel(x_ref, gmix_ref, wa_ref, wkpe_ref, wmq_ref, gq_ref, gkv_ref, gmq_ref, left_ref, mk_ref, mv_ref,
                  h_ref, d_ref, hq_ref, ckv_ref, kpe_ref, am_ref, utail_ref, xp_sc, qm_sc,
                  *, tm, blocks_per_seq, pos0):
    sb = pl.program_id(0) % blocks_per_seq
    h = _rms(x_ref[...], gmix_ref[...]).astype(BF16)
    h_ref[...] = h

    @pl.when(sb == 0)
    def _():
        xp_sc[:POOL_HALO, :] = left_ref[...]

    @pl.when(sb != 0)
    def _():
        xp_sc[:POOL_HALO, :] = xp_sc[tm:, :]

    xp_sc[POOL_HALO:, :] = _dot_nt(h, wa_ref[Z_U:Z_QLAT, :])
    utail_ref[...] = xp_sc[tm:, :]
    hq_ref[...] = _rms(_dot_nt(h, wa_ref[Z_QLAT:Z_KVLAT, :]), gq_ref[...]).astype(BF16)
    ckv_ref[...] = _rms(_dot_nt(h, wa_ref[Z_KVLAT:Z_MQ, :]), gkv_ref[...])
    kpe_ref[:, :QK_ROPE] = _dot_nt(h, wkpe_ref[...])
    kpe_ref[:, QK_ROPE:] = jnp.zeros((tm, 128 - QK_ROPE), F32)
    for hh in range(MEM_HEADS):
        lo = hh * MEM_HEAD_DIM
        m = _dot_nt(h, wmq_ref[lo:lo + MEM_HEAD_DIM, :])
        qm_sc[:, lo:lo + MEM_HEAD_DIM] = _rms(m, gmq_ref[...]).astype(BF16)
    _pool_windows(xp_sc, pos0 + sb * tm, tm, d_ref)
    _mem_attn_kernel(qm_sc, mk_ref, mv_ref, am_ref)


def _front(x, gmix, wa_t, wkpe_t, wmq_t, gq, gkv, gmq_scaled, left, mem_kv, tm, blocks_per_seq, pos0):
    t = x.shape[0]
    n_seq = t // (tm * blocks_per_seq)
    row = lambda w: pl.BlockSpec((tm, w), lambda i: (i, 0))
    full = lambda a: pl.BlockSpec(a.shape, lambda i: (0, 0))
    w_elems = wa_t.size + wkpe_t.size + wmq_t.size
    vmem = 2 * (tm * D_MODEL * 6 + w_elems * 2 + tm * (POOL_W * 2 + Q_LORA * 2 + KV_LORA * 4 + 128 * 4 + MEM_W * 2)
                + 2 * N_MEM * MEM_W * 4 + 2 * POOL_HALO * POOL_W * 4)
    vmem += (tm + POOL_HALO) * POOL_W * 4 + tm * MEM_W * 2 + 8 * tm * POOL_W * 4
    kern = functools.partial(_front_kernel, tm=tm, blocks_per_seq=blocks_per_seq, pos0=pos0)
    return pl.pallas_call(
        kern,
        grid=(t // tm,),
        in_specs=[row(D_MODEL), full(gmix), full(wa_t), full(wkpe_t), full(wmq_t),
                  full(gq), full(gkv), full(gmq_scaled),
                  pl.BlockSpec((None, POOL_HALO, POOL_W), lambda i: (i // blocks_per_seq, 0, 0)),
                  pl.BlockSpec((N_MEM, MEM_W), lambda i: (i // blocks_per_seq, 0)),
                  pl.BlockSpec((N_MEM, MEM_W), lambda i: (i // blocks_per_seq, 1))],
        out_specs=[row(D_MODEL), row(POOL_W), row(Q_LORA), row(KV_LORA), row(128), row(MEM_W),
                   pl.BlockSpec((None, POOL_HALO, POOL_W), lambda i: (i // blocks_per_seq, 0, 0))],
        out_shape=[jax.ShapeDtypeStruct((t, D_MODEL), BF16),
                   jax.ShapeDtypeStruct((t, POOL_W), BF16), jax.ShapeDtypeStruct((t, Q_LORA), BF16),
                   jax.ShapeDtypeStruct((t, KV_LORA), F32), jax.ShapeDtypeStruct((t, 128), F32),
                   jax.ShapeDtypeStruct((t, MEM_W), BF16), jax.ShapeDtypeStruct((n_seq, POOL_HALO, POOL_W), F32)],
        scratch_shapes=[pltpu.VMEM((tm + POOL_HALO, POOL_W), F32), pltpu.VMEM((tm, MEM_W), BF16)],
        compiler_params=_params(("arbitrary",), vmem),
        name="front",
    )(x, gmix, wa_t, wkpe_t, wmq_t, gq, gkv, gmq_scaled, left, mem_kv, mem_kv)


def _in_proj(x, gmix, w_in_t, gq, gkv, gmq_scaled):
    t = x.shape[0]
    row = lambda w: pl.BlockSpec((t, w), lambda i: (0, 0))
    full = lambda a: pl.BlockSpec(a.shape, lambda i: (0, 0))
    pieces = ((Z_MQ, 0), (QK_ROPE, W_IN_KPE), (MEM_W, W_IN_MQ))
    w_elems = sum(n for n, _ in pieces) * D_MODEL
    vmem = t * D_MODEL * 8 + w_elems * 6 + 2 * t * (POOL_W * 4 + Q_LORA * 2 + KV_LORA * 4 + 128 * 4 + MEM_W * 2)
    vmem += w_elems * 2 + 6 * t * POOL_W * 4
    return pl.pallas_call(
        _in_proj_kernel,
        grid=(1,),
        in_specs=[row(D_MODEL), full(gmix)] + [_w_in_rows(n, start) for n, start in pieces]
                 + [full(gq), full(gkv), full(gmq_scaled)],
        out_specs=[row(D_MODEL), row(POOL_W), row(Q_LORA), row(KV_LORA), row(128), row(MEM_W)]
                  + [pl.BlockSpec((n, D_MODEL), lambda i: (0, 0)) for n, _ in pieces],
        out_shape=[jax.ShapeDtypeStruct((t, D_MODEL), BF16),
                   jax.ShapeDtypeStruct((t, POOL_W), F32), jax.ShapeDtypeStruct((t, Q_LORA), BF16),
                   jax.ShapeDtypeStruct((t, KV_LORA), F32), jax.ShapeDtypeStruct((t, 128), F32),
                   jax.ShapeDtypeStruct((t, MEM_W), BF16)]
                  + [jax.ShapeDtypeStruct((n, D_MODEL), BF16) for n, _ in pieces],
        compiler_params=_params(("arbitrary",), vmem),
        name="in_proj",
    )(x, gmix, w_in_t, w_in_t, w_in_t, gq, gkv, gmq_scaled)


def _head_store(ref, head_major):
    def store(h, nope, rope):
        if head_major:
            ref[h, :, :QK_NOPE] = nope
            ref[h, :, QK_NOPE:] = rope
        else:
            lo = h * HEAD_PAD
            ref[:, lo:lo + QK_NOPE] = nope
            ref[:, lo + QK_NOPE:lo + HEAD_PAD] = rope
    return store


def _q_heads(hq, wq_ref, gq_ref, qtab, store):
    first = lax.broadcasted_iota(jnp.int32, (1, 128), 1) < QK_ROPE
    for h in range(MLA_HEADS):
        lo = h * HEAD_PAD
        qh = _dot(hq, wq_ref[:, lo:lo + HEAD_PAD])
        qn, qr = qh[:, :QK_NOPE], qh[:, QK_NOPE:]
        ss = jnp.sum(qn * qn + jnp.where(first, qr * qr, 0.0), axis=-1, keepdims=True)
        r = lax.rsqrt(ss * (1.0 / QK_HEAD) + EPS)
        store(h, (qn * r * gq_ref[:, :QK_NOPE]).astype(BF16), (qr * r * gq_ref[:, QK_NOPE:] * qtab).astype(BF16))


def _qkv_kernel(hq_ref, ckv_ref, kpe_ref, wq_ref, wkbt_ref, wvbt_ref, gq_ref, gkn_ref, gkr_ref,
                qtab_ref, c_ref, sa_ref, sb_ref, q_ref, k_ref, vt_ref):
    _q_heads(hq_ref[...], wq_ref, gq_ref, qtab_ref[...], _head_store(q_ref, True))
    store_k = _head_store(k_ref, True)
    ckv = ckv_ref[...].astype(BF16)
    kpe = kpe_ref[...]
    ss_pe = jnp.sum(kpe * kpe, axis=-1, keepdims=True)
    kr = _rope128(kpe * gkr_ref[...], c_ref[...], sa_ref[...], sb_ref[...])
    kr = kr + pltpu.roll(kr, QK_ROPE, 1)
    for p in range(MLA_HEADS // 2):
        kn2 = _dot_nt(ckv, wkbt_ref[p * 2 * QK_NOPE:(p + 1) * 2 * QK_NOPE, :])
        for s in range(2):
            kn = kn2[:, s * QK_NOPE:(s + 1) * QK_NOPE]
            r = lax.rsqrt((jnp.sum(kn * kn, axis=-1, keepdims=True) + ss_pe) * (1.0 / QK_HEAD) + EPS)
            store_k(2 * p + s, (kn * r * gkn_ref[...]).astype(BF16), (kr * r).astype(BF16))
    vt = _dot_nt(wvbt_ref[...], ckv)
    tm = vt.shape[1]
    ones_row = (lax.broadcasted_iota(jnp.int32, (V_ROWS - V_HEAD, tm), 0) == 0).astype(BF16)
    for h in range(MLA_HEADS):
        vt_ref[h, :V_HEAD, :] = vt[h * V_HEAD:(h + 1) * V_HEAD].astype(BF16)
        vt_ref[h, V_HEAD:, :] = ones_row


def _qkv(hq, ckv, kpe, wq, wkbt, wvbt, gq, gkn, gkr, tabs, tm, seq_blocks):
    t = hq.shape[0]
    n_seq = t // (tm * seq_blocks)
    qk_spec = pl.BlockSpec((None, MLA_HEADS, tm, HEAD_PAD), lambda i: (i // seq_blocks, 0, i % seq_blocks, 0))
    qk_shape = jax.ShapeDtypeStruct((n_seq, MLA_HEADS, tm * seq_blocks, HEAD_PAD), BF16)
    row = lambda w: pl.BlockSpec((tm, w), lambda i: (i, 0))
    full = lambda a: pl.BlockSpec(a.shape, lambda i: (0, 0))
    tab = pl.BlockSpec((tm, 128), lambda i: (i % seq_blocks, 0))
    vmem = 2 * (tm * (Q_LORA * 2 + KV_LORA * 4 + 128 * 4 + 4 * 128 * 4) + (wq.size + wkbt.size + wvbt.size) * 2
                + tm * (2 * MLA_HEADS * HEAD_PAD * 2 + MLA_HEADS * V_ROWS * 2))
    vmem += 8 * tm * HEAD_PAD * 4 + 2 * tm * MLA_HEADS * V_HEAD * 4
    return pl.pallas_call(
        _qkv_kernel,
        grid=(t // tm,),
        in_specs=[row(Q_LORA), row(KV_LORA), row(128), full(wq), full(wkbt), full(wvbt),
                  full(gq), full(gkn), full(gkr), tab, tab, tab, tab],
        out_specs=[qk_spec, qk_spec,
                   pl.BlockSpec((None, MLA_HEADS, None, V_ROWS, tm),
                                lambda i: (i // seq_blocks, 0, i % seq_blocks, 0, 0))],
        out_shape=[qk_shape, qk_shape,
                   jax.ShapeDtypeStruct((n_seq, MLA_HEADS, seq_blocks, V_ROWS, tm), BF16)],
        compiler_params=_params(("parallel",), vmem),
        name="qkv_proj",
    )(hq, ckv, kpe, wq, wkbt, wvbt, gq, gkn, gkr, *tabs)


def _q_only_kernel(hq_ref, wq_ref, gq_ref, qtab_ref, q_ref):
    _q_heads(hq_ref[...], wq_ref, gq_ref, qtab_ref[...], _head_store(q_ref, False))


def _q_only(hq, wq, gq, qtab, tm):
    t = hq.shape[0]
    row = lambda w: pl.BlockSpec((tm, w), lambda i: (i, 0))
    full = lambda a: pl.BlockSpec(a.shape, lambda i: (0, 0))
    vmem = 2 * (tm * (Q_LORA * 2 + 128 * 4 + MLA_HEADS * HEAD_PAD * 2) + wq.size * 2) + 8 * tm * HEAD_PAD * 4
    return pl.pallas_call(
        _q_only_kernel,
        grid=(t // tm,),
        in_specs=[row(Q_LORA), full(wq), full(gq), pl.BlockSpec((tm, 128), lambda i: (0, 0))],
        out_specs=row(MLA_HEADS * HEAD_PAD),
        out_shape=jax.ShapeDtypeStruct((t, MLA_HEADS * HEAD_PAD), BF16),
        compiler_params=_params(("parallel",), vmem),
        name="q_proj",
    )(hq, wq, gq, qtab)


def _flash_kernel(q_ref, k_ref, vt_ref, o_ref, *scratch, tq, tk, nq):
    *bufs, m_sc, acc_sc = scratch
    qi = pl.program_id(2)
    n_diag = tq // tk

    def scores(kj, s_sc, q_lo=0):
        s_sc[:, q_lo:] = _dot_nt(k_ref[kj * tk:(kj + 1) * tk, :], q_ref[q_lo:, :])

    def consume(kj, s_sc, key_chunk0=None, q_lo=0):
        s = s_sc[:, q_lo:]
        if key_chunk0 is not None:
            kc = key_chunk0 + lax.broadcasted_iota(jnp.int32, s.shape, 0) // CHUNK
            qc = (q_lo + lax.broadcasted_iota(jnp.int32, s.shape, 1)) // CHUNK
            s = jnp.where(qc >= kc, s, NEG_INF)
        m = m_sc[:, q_lo:]
        m_new = jnp.maximum(m, jnp.max(s, axis=0, keepdims=True))
        alpha = jnp.exp2(m - m_new)
        p = jnp.exp2(s - m_new).astype(BF16)
        acc_sc[:, q_lo:] = alpha * acc_sc[:, q_lo:] + _dot(vt_ref[kj], p)
        m_sc[:, q_lo:] = m_new

    def run(q_block):
        tiles = [(kj, 0, None) for kj in range(q_block * n_diag)]
        tiles += [(q_block * n_diag + d, d * tk, d * tk // CHUNK) for d in range(n_diag)]
        m_sc[...] = jnp.full(m_sc.shape, NEG_INF, F32)
        acc_sc[...] = jnp.zeros(acc_sc.shape, F32)
        for i in range(min(FLASH_AHEAD, len(tiles))):
            scores(tiles[i][0], bufs[i % len(bufs)], tiles[i][1])
        for i, (kj, q_lo, key_chunk0) in enumerate(tiles):
            if i + FLASH_AHEAD < len(tiles):
                nkj, nq_lo, _ = tiles[i + FLASH_AHEAD]
                scores(nkj, bufs[(i + FLASH_AHEAD) % len(bufs)], nq_lo)
            consume(kj, bufs[i % len(bufs)], key_chunk0, q_lo)
        acc = acc_sc[...]
        o_ref[...] = (acc[:V_HEAD] / acc[V_HEAD:V_HEAD + 1]).T.astype(BF16)

    for q_block in range(nq):
        pl.when(qi == q_block)(functools.partial(run, q_block))


def _flash(q, k, vt, batch, seq, tq):
    tk = vt.shape[-1]
    assert tq % tk == 0 and tk % CHUNK == 0 and seq % tq == 0
    nq = seq // tq
    n_bufs = FLASH_AHEAD + 1
    vmem = 2 * (tq * HEAD_PAD * 2 + seq * HEAD_PAD * 2 + seq * V_ROWS * 2 + tq * V_HEAD * 2)
    vmem += n_bufs * tk * tq * 4 + V_ROWS * tq * 4 + 4 * tk * tq * 4
    return pl.pallas_call(
        functools.partial(_flash_kernel, tq=tq, tk=tk, nq=nq),
        grid=(batch, MLA_HEADS, nq),
        in_specs=[pl.BlockSpec((None, None, tq, HEAD_PAD), lambda b, h, i: (b, h, i, 0)),
                  pl.BlockSpec((None, None, seq, HEAD_PAD), lambda b, h, i: (b, h, 0, 0)),
                  pl.BlockSpec((None, None, seq // tk, V_ROWS, tk), lambda b, h, i: (b, h, 0, 0, 0))],
        out_specs=pl.BlockSpec((tq, V_HEAD), lambda b, h, i: (b * nq + i, h)),
        out_shape=jax.ShapeDtypeStruct((batch * seq, MLA_HEADS * V_HEAD), BF16),
        scratch_shapes=[pltpu.VMEM((tk, tq), F32)] * n_bufs + [pltpu.VMEM((1, tq), F32),
                                                                pltpu.VMEM((V_ROWS, tq), F32)],
        compiler_params=_params(("parallel", "parallel", "arbitrary"), vmem),
        name="mla_flash",
    )(q, k, vt)


def _q_absorb_kernel(q_ref, wkbt_ref, qa_ref, qr_ref):
    qa_ref[...] = _dot(q_ref[:, :QK_NOPE], wkbt_ref[...]).astype(BF16)
    qr_ref[...] = q_ref[:, QK_NOPE:]


def _q_absorb(q, wkbt):
    t = q.shape[0]
    vmem = 2 * (t * HEAD_PAD * 2 + KV_LORA * QK_NOPE * 2 + t * KV_LORA * 2 + t * 128 * 2) + 2 * t * KV_LORA * 4
    return pl.pallas_call(
        _q_absorb_kernel,
        grid=(MLA_HEADS,),
        in_specs=[pl.BlockSpec((t, HEAD_PAD), lambda h: (0, h)),
                  pl.BlockSpec((QK_NOPE, KV_LORA), lambda h: (h, 0))],
        out_specs=[pl.BlockSpec((None, t, KV_LORA), lambda h: (h, 0, 0)),
                   pl.BlockSpec((None, t, 128), lambda h: (h, 0, 0))],
        out_shape=[jax.ShapeDtypeStruct((MLA_HEADS, t, KV_LORA), BF16),
                   jax.ShapeDtypeStruct((MLA_HEADS, t, 128), BF16)],
        compiler_params=_params(("parallel",), vmem),
        name="q_absorb",
    )(q, wkbt)


def _cache_attn_kernel(wkbt_ref, qa_ref, qr_ref, lat_ref, kpet_ref, ct_ref, st_ref,
                       nlat_ref, nkpet_ref, nct_ref, nst_ref, gkr_ref,
                       o_ref, *scratch, nq, n_new, past, tk):
    *bufs, m_sc, l_sc, acc_sc = scratch
    rows = MLA_HEADS * nq
    half = QK_ROPE // 2
    m_sc[...] = jnp.full(m_sc.shape, NEG_INF, F32)
    l_sc[...] = jnp.zeros(l_sc.shape, F32)
    acc_sc[...] = jnp.zeros(acc_sc.shape, F32)

    def scores(lat, kpet, ct, st, width, s_sc):
        c = lat.astype(BF16)
        kt = _dot_nt(wkbt_ref[...], c)
        s = _dot_nt(qa_ref[...].reshape(rows, KV_LORA), c)
        ss_pe = jnp.sum(kpet * kpet, axis=0, keepdims=True)
        kr = kpet * gkr_ref[...]
        k1, k2 = kr[:half], kr[half:]
        o1, o2 = k1 * ct - k2 * st, k2 * ct + k1 * st
        krot = jnp.concatenate([o1, o2, o1, o2], axis=0).astype(BF16)
        s = s + _dot(qr_ref[...].reshape(rows, 128), krot)
        for h in range(MLA_HEADS):
            kh = kt[h * QK_NOPE:(h + 1) * QK_NOPE]
            r = lax.rsqrt((jnp.sum(kh * kh, axis=0, keepdims=True) + ss_pe) * (1.0 / QK_HEAD) + EPS)
            s_sc[h * nq:(h + 1) * nq, :width] = s[h * nq:(h + 1) * nq] * r

    def consume(lat, width, s_sc, visible):
        s = s_sc[:, :width]
        if visible is not None:
            s = jnp.where(visible, s, NEG_INF)
        m = m_sc[...]
        m_new = jnp.maximum(m, jnp.max(s, axis=-1, keepdims=True))
        alpha = jnp.exp2(m - m_new)
        p = jnp.exp2(s - m_new)
        l_sc[...] = alpha * l_sc[...] + jnp.sum(p, axis=-1, keepdims=True)
        m_sc[...] = m_new
        acc_sc[...] = alpha * acc_sc[...] + _dot(p.astype(BF16), lat.astype(BF16))

    kk = lax.broadcasted_iota(jnp.int32, (1, 128), 1)
    new_visible = kk < n_new
    if (past + n_new - 1) // CHUNK != past // CHUNK:
        qpos = past + lax.broadcasted_iota(jnp.int32, (rows, 1), 0) % nq
        new_visible = new_visible & (qpos // CHUNK >= (past + kk) // CHUNK)
    blocks = []
    for j in range(past // tk):
        ks = slice(j * tk, (j + 1) * tk)
        blocks.append((lambda ks=ks: (lat_ref[ks, :], kpet_ref[:, ks], ct_ref[:, ks], st_ref[:, ks]), tk, None))
    blocks.append((lambda: (nlat_ref[...], nkpet_ref[...], nct_ref[...], nst_ref[...]), 128, new_visible))

    def issue(i):
        get, width, _ = blocks[i]
        scores(*get(), width, bufs[i % len(bufs)])

    for i in range(min(CACHE_AHEAD, len(blocks))):
        issue(i)
    for i, (get, width, visible) in enumerate(blocks):
        if i + CACHE_AHEAD < len(blocks):
            issue(i + CACHE_AHEAD)
        consume(get()[0], width, bufs[i % len(bufs)], visible)
    o_ref[...] = (acc_sc[...] / l_sc[...]).astype(BF16)


def _cache_attn(wkbt, qa, qr, lat, kpet, ct, st, nlat, nkpet, nct, nst, gkr, nq, n_new, tk):
    batch, past, _ = lat.shape
    assert past % tk == 0
    rows = MLA_HEADS * nq
    n_bufs = CACHE_AHEAD + 1
    kern = functools.partial(_cache_attn_kernel, nq=nq, n_new=n_new, past=past, tk=tk)
    vmem = 2 * (wkbt.size * 2 + rows * (KV_LORA + 128) * 2 + past * KV_LORA * 4 + 3 * 64 * past * 4 + rows * KV_LORA * 2)
    vmem += (MLA_HEADS * QK_NOPE + 4 * rows) * tk * 4 + tk * KV_LORA * 2 + rows * KV_LORA * 8 + n_bufs * rows * tk * 4
    whole = lambda a: pl.BlockSpec(a.shape, lambda b: (0,) * a.ndim)
    per_batch = lambda a: pl.BlockSpec((None,) + a.shape[1:], lambda b: (b,) + (0,) * (a.ndim - 1))
    return pl.pallas_call(
        kern,
        grid=(batch,),
        in_specs=[whole(wkbt),
                  pl.BlockSpec((MLA_HEADS, nq, KV_LORA), lambda b: (0, b, 0)),
                  pl.BlockSpec((MLA_HEADS, nq, 128), lambda b: (0, b, 0)),
                  per_batch(lat), per_batch(kpet), whole(ct), whole(st),
                  per_batch(nlat), per_batch(nkpet), whole(nct), whole(nst), whole(gkr)],
        out_specs=pl.BlockSpec((None, rows, KV_LORA), lambda b: (b, 0, 0)),
        out_shape=jax.ShapeDtypeStruct((batch, rows, KV_LORA), BF16),
        scratch_shapes=[pltpu.VMEM((rows, tk), F32)] * n_bufs
                       + [pltpu.VMEM((rows, 1), F32), pltpu.VMEM((rows, 1), F32), pltpu.VMEM((rows, KV_LORA), F32)],
        compiler_params=_params(("parallel",), vmem),
        name="mla_cache_attn",
    )(wkbt, qa, qr, lat, kpet, ct, st, nlat, nkpet, nct, nst, gkr)


def _v_up_kernel(ol_ref, wvbt_ref, o_ref):
    b, nq, _ = ol_ref.shape
    o_ref[...] = _dot_nt(ol_ref[...].reshape(b * nq, KV_LORA), wvbt_ref[...]).astype(BF16)


def _v_up(o_lat, wvbt, nq):
    batch = o_lat.shape[0]
    t = batch * nq
    vmem = 2 * (t * KV_LORA * 2 + KV_LORA * V_HEAD * 2 + t * V_HEAD * 2) + 2 * t * V_HEAD * 4
    return pl.pallas_call(
        _v_up_kernel,
        grid=(MLA_HEADS,),
        in_specs=[pl.BlockSpec((batch, nq, KV_LORA), lambda h: (0, h, 0)),
                  pl.BlockSpec((V_HEAD, KV_LORA), lambda h: (h, 0))],
        out_specs=pl.BlockSpec((t, V_HEAD), lambda h: (0, h)),
        out_shape=jax.ShapeDtypeStruct((t, MLA_HEADS * V_HEAD), BF16),
        compiler_params=_params(("parallel",), vmem),
        name="v_up",
    )(o_lat, wvbt)


def _mem_kv_kernel(mem_ref, gmem_ref, w_ref, gk_ref, kv_ref):
    j = pl.program_id(0)
    h = _rms(mem_ref[...], gmem_ref[...]).astype(BF16)

    @pl.when(j == 0)
    def _():
        for hh in range(MEM_HEADS):
            lo = hh * MEM_HEAD_DIM
            w = w_ref[:, lo:lo + MEM_HEAD_DIM].astype(BF16)
            kv_ref[:, lo:lo + MEM_HEAD_DIM] = _rms(_dot(h, w), gk_ref[...])

    @pl.when(j == 1)
    def _():
        kv_ref[...] = _dot(h, w_ref[...].astype(BF16))


def _mem_kv(mem, gmem, w, gk):
    t = mem.shape[0]
    vmem = 2 * (t * D_MODEL * 4 + D_MODEL * MEM_W * w.dtype.itemsize + t * MEM_W * 4) + t * D_MODEL * 6
    vmem += t * MEM_W * 4 + D_MODEL * MEM_W * 2
    return pl.pallas_call(
        _mem_kv_kernel,
        grid=(2,),
        in_specs=[pl.BlockSpec((t, D_MODEL), lambda j: (0, 0)), pl.BlockSpec((1, D_MODEL), lambda j: (0, 0)),
                  pl.BlockSpec((D_MODEL, MEM_W), lambda j: (0, j)), pl.BlockSpec((1, MEM_HEAD_DIM), lambda j: (0, 0))],
        out_specs=pl.BlockSpec((t, MEM_W), lambda j: (0, j)),
        out_shape=jax.ShapeDtypeStruct((t, 2 * MEM_W), F32),
        compiler_params=_params(("parallel",), vmem),
        name="memory_kv",
    )(mem, gmem, w, gk)


def _mem_attn_kernel(qm_ref, k_ref, v_ref, o_ref):
    for hh in range(MEM_HEADS):
        sl = slice(hh * MEM_HEAD_DIM, (hh + 1) * MEM_HEAD_DIM)
        s = _dot_nt(qm_ref[:, sl], k_ref[:, sl].astype(BF16))
        p = jnp.exp(s - jnp.max(s, axis=-1, keepdims=True))
        l = jnp.sum(p, axis=-1, keepdims=True)
        o_ref[:, sl] = (_dot(p.astype(BF16), v_ref[:, sl].astype(BF16)) / l).astype(BF16)


def _mem_attn(qm, k_arr, v_arr, k_col, v_col, tm, blocks_per_batch):
    t = qm.shape[0]
    vmem = 2 * (2 * tm * MEM_W * 2 + 2 * N_MEM * MEM_W * 4) + 6 * tm * N_MEM * 4 + 2 * N_MEM * MEM_W * 2
    return pl.pallas_call(
        _mem_attn_kernel,
        grid=(t // tm,),
        in_specs=[pl.BlockSpec((tm, MEM_W), lambda i: (i, 0)),
                  pl.BlockSpec((N_MEM, MEM_W), lambda i: (i // blocks_per_batch, k_col)),
                  pl.BlockSpec((N_MEM, MEM_W), lambda i: (i // blocks_per_batch, v_col))],
        out_specs=pl.BlockSpec((tm, MEM_W), lambda i: (i, 0)),
        out_shape=jax.ShapeDtypeStruct((t, MEM_W), BF16),
        compiler_params=_params(("parallel",), vmem),
        name="memory_attn",
    )(qm, k_arr, v_arr)


def _pool_windows(xp_sc, first_pos, tm, d_ref):
    pos = first_pos + lax.broadcasted_iota(jnp.int32, (tm, 1), 0)
    for g, win in enumerate(POOL_WINDOWS):
        sl = slice(g * POOL_GW, (g + 1) * POOL_GW)
        x = xp_sc[:, sl]
        wsum, k = x, 1
        while k < win:
            wsum = wsum + pltpu.roll(wsum, k, 0)
            k *= 2
        cnt = jnp.minimum(pos + 1, win).astype(F32)
        d_ref[:, sl] = (wsum[POOL_HALO:] / cnt - x[POOL_HALO:]).astype(BF16)


def _pool_kernel(u_ref, prev_ref, left_ref, d_ref, xp_sc, *, tm, blocks_per_seq, pos0):
    sb = pl.program_id(0) % blocks_per_seq
    xp_sc[:POOL_HALO, :] = jnp.where(sb == 0, left_ref[...], prev_ref[...])
    xp_sc[POOL_HALO:, :] = u_ref[...]
    _pool_windows(xp_sc, pos0 + sb * tm, tm, d_ref)


def _pool(u, left, tm, blocks_per_seq, pos0):
    t = u.shape[0]
    halo_per_blk = tm // POOL_HALO
    kern = functools.partial(_pool_kernel, tm=tm, blocks_per_seq=blocks_per_seq, pos0=pos0)
    vmem = 2 * (tm * POOL_W * 4 + 2 * POOL_HALO * POOL_W * 4 + tm * POOL_W * 2) + (tm + POOL_HALO) * POOL_W * 4
    vmem += 6 * tm * POOL_GW * 4
    return pl.pallas_call(
        kern,
        grid=(t // tm,),
        in_specs=[pl.BlockSpec((tm, POOL_W), lambda i: (i, 0)),
                  pl.BlockSpec((POOL_HALO, POOL_W), lambda i: (jnp.maximum(i * halo_per_blk - 1, 0), 0)),
                  pl.BlockSpec((None, POOL_HALO, POOL_W), lambda i: (i // blocks_per_seq, 0, 0))],
        out_specs=pl.BlockSpec((tm, POOL_W), lambda i: (i, 0)),
        out_shape=jax.ShapeDtypeStruct((t, POOL_W), BF16),
        scratch_shapes=[pltpu.VMEM((tm + POOL_HALO, POOL_W), F32)],
        compiler_params=_params(("parallel",), vmem),
        name="pool_windows",
    )(u, u, left)


def _merge_kernel(h_ref, wg0_ref, wg1_ref, wg2_ref, bg0_ref, bg1_ref, bg2_ref,
                  d_ref, wpool_ref, pscale_ref, o_ref, wmla_ref, am_ref, wmem_ref,
                  out_ref, *w_copies):
    wg_copies, (wpool_copy, wmla_copy, wmem_copy) = (w_copies[:3], w_copies[3:]) if w_copies else ((None,) * 3,) * 2
    h = h_ref[...]
    y_pool = _dot(d_ref[...], _as_bf16(wpool_ref, wpool_copy)) * pscale_ref[...]
    y_mla = _dot(o_ref[...], _as_bf16(wmla_ref, wmla_copy))
    y_mem = _dot(am_ref[...], _as_bf16(wmem_ref, wmem_copy))
    gate = lambda w_ref, copy, b_ref: jax.nn.sigmoid(_dot_nt(h, _as_bf16(w_ref, copy)) + b_ref[...])
    merged = gate(wg0_ref, wg_copies[0], bg0_ref) * y_pool
    merged += gate(wg1_ref, wg_copies[1], bg1_ref) * y_mla
    merged += gate(wg2_ref, wg_copies[2], bg2_ref) * y_mem
    out_ref[...] = merged.astype(BF16)


def _merge(h, wgates, bgate, d, wpool, pscale, o, wmla, am, wmem, tm, tn, emit_bf16=False):
    t = h.shape[0]
    assert not emit_bf16 or t == tm
    assert POOL_OUT_GW % tn == 0
    nj = D_MODEL // tn
    per_group = POOL_OUT_GW // tn
    rowfull = lambda w: pl.BlockSpec((tm, w), lambda i, j: (i, 0))
    gate_rows = pl.BlockSpec((tn, D_MODEL), lambda i, j: (j, 0))
    if emit_bf16:
        gate_in = [pl.BlockSpec((pl.Element(tn), pl.Element(D_MODEL)),
                                lambda i, j, b=b: (pl.multiple_of(W_IN_GATE + (b * nj + j) * tn, BF16_SUBLANES), 0))
                   for b in range(3)]
        gate_args = (wgates,) * 3
    else:
        gate_in = [gate_rows] * 3
        gate_args = tuple(wgates)
    gate_b = lambda b: pl.BlockSpec((1, tn), lambda i, j: (0, b * nj + j))
    col = lambda k: pl.BlockSpec((k, tn), lambda i, j: (0, j))
    pool_spec = pl.BlockSpec((None, POOL_GW, tn), lambda i, j: (j // per_group, 0, j % per_group))
    w_bytes = wmla.dtype.itemsize + 2 * emit_bf16
    w_rows = 3 * D_MODEL + POOL_GW + D_MODEL + MEM_W
    vmem = 2 * (tm * D_MODEL * 2 + w_rows * tn * w_bytes + tm * POOL_GW * 2 + tm * D_MODEL * 2
                + tm * MEM_W * 2 + tm * tn * 2)
    vmem += 8 * tm * tn * 4 + w_rows * tn * 2 * emit_bf16
    y_spec = pl.BlockSpec((tm, tn), lambda i, j: (i, j))
    y_shape = jax.ShapeDtypeStruct((t, D_MODEL), BF16)
    copies = ([jax.ShapeDtypeStruct((D_MODEL, D_MODEL), BF16)] * 3
              + [jax.ShapeDtypeStruct(w.shape, BF16) for w in (wpool, wmla, wmem)])
    return pl.pallas_call(
        _merge_kernel,
        grid=(t // tm, nj),
        in_specs=[rowfull(D_MODEL)] + gate_in
                 + [gate_b(0), gate_b(1), gate_b(2),
                    pl.BlockSpec((tm, POOL_GW), lambda i, j: (i, j // per_group)), pool_spec,
                    pl.BlockSpec((1, tn), lambda i, j: (0, j)),
                    rowfull(D_MODEL), col(D_MODEL), rowfull(MEM_W), col(MEM_W)],
        out_specs=[y_spec] + [gate_rows] * 3 + [pool_spec, col(D_MODEL), col(MEM_W)] if emit_bf16 else y_spec,
        out_shape=[y_shape] + copies if emit_bf16 else y_shape,
        compiler_params=_params(("parallel", "arbitrary" if emit_bf16 else "parallel"), vmem),
        name="gated_merge",
    )(h, *gate_args, bgate, bgate, bgate, d, wpool, pscale, o, wmla, am, wmem)


def _out_proj_kernel(x_ref, m_ref, w_ref, y_ref, *w_copy):
    y_ref[...] = x_ref[...] + _dot(m_ref[...], _as_bf16(w_ref, *w_copy or (None,)))


def _out_proj(x, merged, w, tm, tn, emit_bf16=False):
    t = x.shape[0]
    assert not emit_bf16 or t == tm
    w_spec = pl.BlockSpec((D_MODEL, tn), lambda i, j: (0, j))
    vmem = 2 * (2 * tm * tn * 4 + tm * D_MODEL * 2 + D_MODEL * tn * (w.dtype.itemsize + 2 * emit_bf16)) + 2 * tm * tn * 4
    y_spec = pl.BlockSpec((tm, tn), lambda i, j: (i, j))
    y_shape = jax.ShapeDtypeStruct((t, D_MODEL), F32)
    return pl.pallas_call(
        _out_proj_kernel,
        grid=(t // tm, D_MODEL // tn),
        in_specs=[pl.BlockSpec((tm, tn), lambda i, j: (i, j)),
                  pl.BlockSpec((tm, D_MODEL), lambda i, j: (i, 0)), w_spec],
        out_specs=[y_spec, w_spec] if emit_bf16 else y_spec,
        out_shape=[y_shape, jax.ShapeDtypeStruct(w.shape, BF16)] if emit_bf16 else y_shape,
        compiler_params=_params(("parallel", "arbitrary" if emit_bf16 else "parallel"), vmem),
        name="out_proj",
    )(x, merged, w)


def _ffn_kernel(x_ref, g_ref, wup_ref, wdown_ref, y_ref, *rest):
    *w_copies, h_sc, acc_sc = rest
    wup_copy, wdown_copy = w_copies or (None, None)
    j = pl.program_id(1)

    @pl.when(j == 0)
    def _():
        h_sc[...] = _rms(x_ref[...], g_ref[...]).astype(BF16)
        acc_sc[...] = jnp.zeros(acc_sc.shape, F32)

    f = jnp.maximum(_dot(h_sc[...], _as_bf16(wup_ref, wup_copy)), 0.0)
    acc_sc[...] += _dot((f * f).astype(BF16), _as_bf16(wdown_ref, wdown_copy))

    @pl.when(j == pl.num_programs(1) - 1)
    def _():
        y_ref[...] = x_ref[...] + acc_sc[...]


def _ffn(x, g, wup, wdown, tm, tf, emit_bf16=False):
    t = x.shape[0]
    assert not emit_bf16 or t == tm
    w_bytes = wup.dtype.itemsize + 2 * emit_bf16
    vmem = 2 * (2 * tm * D_MODEL * 4 + 2 * D_MODEL * tf * w_bytes) + tm * D_MODEL * 6 + 3 * tm * tf * 4 + tm * D_MODEL * 4
    vmem += 2 * D_MODEL * tf * 2 * emit_bf16
    up_spec = pl.BlockSpec((D_MODEL, tf), lambda i, j: (0, j))
    down_spec = pl.BlockSpec((tf, D_MODEL), lambda i, j: (j, 0))
    y_spec = pl.BlockSpec((tm, D_MODEL), lambda i, j: (i, 0))
    y_shape = jax.ShapeDtypeStruct((t, D_MODEL), F32)
    return pl.pallas_call(
        _ffn_kernel,
        grid=(t // tm, D_FF // tf),
        in_specs=[pl.BlockSpec((tm, D_MODEL), lambda i, j: (i, 0)),
                  pl.BlockSpec((1, D_MODEL), lambda i, j: (0, 0)), up_spec, down_spec],
        out_specs=[y_spec, up_spec, down_spec] if emit_bf16 else y_spec,
        out_shape=([y_shape, jax.ShapeDtypeStruct(wup.shape, BF16), jax.ShapeDtypeStruct(wdown.shape, BF16)]
                   if emit_bf16 else y_shape),
        scratch_shapes=[pltpu.VMEM((tm, D_MODEL), BF16), pltpu.VMEM((tm, D_MODEL), F32)],
        compiler_params=_params(("parallel", "arbitrary"), vmem),
        name="ffn",
    )(x, g, wup, wdown)


def _rope_tables(pos):
    half = QK_ROPE // 2
    inv = 1.0 / (ROPE_THETA ** (jnp.arange(half, dtype=F32) * (2.0 / QK_ROPE)))
    ang = pos.astype(F32)[:, None] * inv[None, :]
    return jnp.cos(ang), jnp.sin(ang)


def _lane_tables(cos, sin):
    z = jnp.zeros_like(cos)
    c = jnp.concatenate([cos, cos, z, z], axis=1)
    sa = jnp.concatenate([z, sin, z, z], axis=1)
    sb = jnp.concatenate([-sin, z, z, z], axis=1)
    return c, sa, sb


def _query_table(cos, sin):
    return jnp.concatenate([cos, cos, -sin, sin], axis=1)


def _row(v):
    return v.astype(F32).reshape(1, -1)


def _pad_last(a, width):
    return jnp.pad(a, [(0, 0)] * (a.ndim - 1) + [(0, width - a.shape[-1])])


def kernel(x_prompt, mem_prompt, x_sample, cache_mla_latent, cache_mla_kpe, state_pool, cache_mem_k, cache_mem_v,
           g_mix, w_in, b_gate, w_pool, pool_scale, g_q_lat, w_qb, g_q_head, g_kv_lat, w_kb, w_vb, g_k_head,
           w_mla_o, g_mem, w_mem_kv, g_mem_q, g_mem_k, w_mem_o, w_out, g_ff, w_up, w_down):
    batch, seq, _ = x_prompt.shape
    dec_batch, dec_seq, _ = x_sample.shape
    past = cache_mla_latent.shape[1]

    w_in_t = w_in.T
    bgate = _row(b_gate)
    half = QK_ROPE // 2

    def swap_halves(a):
        return jnp.concatenate([a[..., half:], a[..., :half]], axis=-1)

    def head_layout(a):
        rope = a[..., QK_NOPE:]
        return jnp.concatenate([a, swap_halves(rope)], axis=-1)

    wq = head_layout(w_qb.reshape(Q_LORA, MLA_HEADS, QK_HEAD)).reshape(Q_LORA, MLA_HEADS * HEAD_PAD).astype(BF16)
    wkbt = w_kb.T.astype(BF16)
    wvbt = w_vb.T.astype(BF16)
    gq_p = _row(head_layout(g_q_head.astype(F32) * MLA_SCALE_LOG2))
    gkn = _row(g_k_head[:QK_NOPE])
    gkr = _row(_pad_last(g_k_head[QK_NOPE:], 128))
    gq_s = _row(head_layout(jnp.concatenate([g_q_head[:QK_NOPE].astype(F32) * g_k_head[:QK_NOPE].astype(F32),
                                             g_q_head[QK_NOPE:].astype(F32)]) * MLA_SCALE_LOG2))
    gmq = _row(g_mem_q.astype(F32) * MEM_SCALE)
    gmix, gql, gkvl, gff, gmem, gmk, pscale = map(_row, (g_mix, g_q_lat, g_kv_lat, g_ff, g_mem, g_mem_k, pool_scale))

    ts = dec_batch * dec_seq
    xs = x_sample.reshape(ts, D_MODEL)
    h_s, u_s, hq_s, ckv_s, kpe_s, qm_s, wa_t, wkpe_t, wmq_t = _in_proj(xs, gmix, w_in_t, gql, gkvl, gmq)
    pos_s = past + jnp.arange(dec_seq, dtype=jnp.int32)
    cos_s, sin_s = _rope_tables(pos_s)
    q_s = _q_only(hq_s, wq, gq_s, jnp.tile(_query_table(cos_s, sin_s), (dec_batch, 1)), ts)
    qa, qr = _q_absorb(q_s, wkbt)
    cos_c, sin_c = _rope_tables(jnp.arange(past, dtype=jnp.int32))
    pad_new = 128 - dec_seq
    nlat = jnp.pad(ckv_s.reshape(dec_batch, dec_seq, KV_LORA), ((0, 0), (0, pad_new), (0, 0)))
    nkpet = jnp.pad(jnp.swapaxes(kpe_s[:, :QK_ROPE].reshape(dec_batch, dec_seq, QK_ROPE), 1, 2),
                    ((0, 0), (0, 0), (0, pad_new)))
    o_lat = _cache_attn(wkbt, qa, qr, cache_mla_latent, jnp.swapaxes(cache_mla_kpe, 1, 2),
                        cos_c.T, sin_c.T, nlat, nkpet,
                        jnp.pad(cos_s.T, ((0, 0), (0, pad_new))), jnp.pad(sin_s.T, ((0, 0), (0, pad_new))),
                        g_k_head[QK_NOPE:].astype(F32).reshape(QK_ROPE, 1),
                        dec_seq, dec_seq, 1024)
    o_s = _v_up(o_lat, wvbt, dec_seq)
    am_s = _mem_attn(qm_s, cache_mem_k.reshape(dec_batch * N_MEM, MEM_W), cache_mem_v.reshape(dec_batch * N_MEM, MEM_W),
                     0, 0, dec_seq, 1)
    left_s = jnp.pad(state_pool.astype(F32), ((0, 0), (POOL_HALO - POOL_STATE, 0), (0, 0)))
    d_s = _pool(u_s, left_s, dec_seq, 1, past)
    merged_s, *wgates, wpool, wmla, wmem_o = _merge(h_s, w_in_t, bgate, d_s, w_pool, pscale, o_s, w_mla_o,
                                                    am_s, w_mem_o, ts, 256, emit_bf16=True)
    x1_s, wout = _out_proj(xs, merged_s, w_out, ts, 512, emit_bf16=True)
    y_s, wup, wdown = _ffn(x1_s, gff, w_up, w_down, ts, 512, emit_bf16=True)
    y_s = y_s.reshape(dec_batch, dec_seq, D_MODEL)
    lat_s = ckv_s.reshape(dec_batch, dec_seq, KV_LORA)
    kpe_s_out = kpe_s[:, :QK_ROPE].reshape(dec_batch, dec_seq, QK_ROPE)
    pool_s = u_s.reshape(dec_batch, dec_seq, POOL_W)[:, dec_seq - POOL_STATE:]

    tm = 512
    xp = x_prompt.reshape(batch * seq, D_MODEL)
    mem_kv = _mem_kv(mem_prompt.reshape(batch * N_MEM, D_MODEL), gmem, w_mem_kv, gmk)
    h, d, hq, ckv, kpe, am, u_tail = _front(xp, gmix, wa_t, wkpe_t, wmq_t, gql, gkvl, gmq,
                                            jnp.zeros((batch, POOL_HALO, POOL_W), F32), mem_kv, tm, seq // tm, 0)
    cos_p, sin_p = _rope_tables(jnp.arange(seq, dtype=jnp.int32))
    q, k, vt = _qkv(hq, ckv, kpe, wq, wkbt, wvbt, gq_p, gkn, gkr,
                    (_query_table(cos_p, sin_p),) + _lane_tables(cos_p, sin_p), tm, seq // tm)
    o = _flash(q, k, vt, batch, seq, 2048)
    merged = _merge(h, wgates, bgate, d, wpool, pscale, o, wmla, am, wmem_o, 2 * tm, POOL_OUT_GW)
    y_p = _ffn(_out_proj(xp, merged, wout, 2 * tm, 1024), gff, wup, wdown, tm, 1024).reshape(batch, seq, D_MODEL)
    lat_p = ckv.reshape(batch, seq, KV_LORA)
    kpe_p = kpe[:, :QK_ROPE].reshape(batch, seq, QK_ROPE)
    pool_p = u_tail[:, POOL_HALO - POOL_STATE:]
    mem_k_p = mem_kv[:, :MEM_W].reshape(batch, N_MEM, MEM_HEADS, MEM_HEAD_DIM)
    mem_v_p = mem_kv[:, MEM_W:].reshape(batch, N_MEM, MEM_HEADS, MEM_HEAD_DIM)

    return (y_p, y_s, lat_p, kpe_p, pool_p, mem_k_p, mem_v_p, lat_s, kpe_s_out, pool_s)
```

```python
import functools

import jax
import jax.numpy as jnp
from jax import lax
from jax.experimental import pallas as pl
from jax.experimental.pallas import tpu as pltpu

F32 = jnp.float32
BF16 = jnp.bfloat16

EPS = 1e-6
CHUNK = 64
D_MODEL = 2048
POOL_WINDOWS = (2, 4, 8, 16)
POOL_W = 1024
POOL_GW = 256
POOL_OUT_GW = 512
POOL_STATE = 15
POOL_HALO = 16
assert all(w & (w - 1) == 0 and w <= POOL_HALO for w in POOL_WINDOWS)
MLA_HEADS = 16
Q_LORA = 512
KV_LORA = 512
QK_NOPE = 128
QK_ROPE = 64
QK_HEAD = QK_NOPE + QK_ROPE
HEAD_PAD = 256
V_HEAD = 128
V_ROWS = 144
ROPE_THETA = 10000.0
LOG2_E = 1.4426950408889634
MLA_SCALE = QK_HEAD ** -0.5
MLA_SCALE_LOG2 = MLA_SCALE * LOG2_E
N_MEM = 256
MEM_HEADS = 4
MEM_HEAD_DIM = 256
MEM_W = MEM_HEADS * MEM_HEAD_DIM
MEM_SCALE = MEM_HEAD_DIM ** -0.5
D_FF = 4 * D_MODEL
NEG_INF = -1e30
FLASH_AHEAD = 2
CACHE_AHEAD = 1
FRONT_ROW_GROUPS = 2

Z_U = 0
Z_QLAT = Z_U + POOL_W
Z_KVLAT = Z_QLAT + Q_LORA
Z_MQ = Z_KVLAT + KV_LORA
W_IN_KPE = Z_MQ
W_IN_MQ = W_IN_KPE + QK_ROPE
W_IN_GATE = W_IN_MQ + MEM_W

BF16_SUBLANES = 16
V7X_VMEM_BYTES = 64 * 1024 * 1024
VMEM_CEILING = V7X_VMEM_BYTES - 6 * 1024 * 1024


def _params(semantics, vmem_bytes):
    return pltpu.CompilerParams(dimension_semantics=semantics,
                                vmem_limit_bytes=int(min(vmem_bytes, VMEM_CEILING)))


def _as_bf16(w_ref, copy_ref):
    w = w_ref[...].astype(BF16)
    if copy_ref is not None:
        copy_ref[...] = w
    return w


def _rms(x, g):
    return x * lax.rsqrt(jnp.mean(x * x, axis=-1, keepdims=True) + EPS) * g


def _dot(a, b):
    return jnp.dot(a, b, preferred_element_type=F32)


def _dot_nt(a, b):
    return lax.dot_general(a, b, (((1,), (1,)), ((), ())), preferred_element_type=F32)


def _rope128(x, c, sa, sb):
    half = QK_ROPE // 2
    return x * c + pltpu.roll(x, half, 1) * sa + pltpu.roll(x, x.shape[1] - half, 1) * sb


def _in_proj_kernel(x_ref, gmix_ref, wa_ref, wkpe_ref, wmq_ref, gq_ref, gkv_ref, gmq_ref,
                    h_ref, u_ref, hq_ref, ckv_ref, kpe_ref, qm_ref, wa_copy, wkpe_copy, wmq_copy):
    def rows(ref, copy, lo, hi):
        w = ref[lo:hi, :].astype(BF16)
        copy[lo:hi, :] = w
        return w

    h = _rms(x_ref[...], gmix_ref[...]).astype(BF16)
    h_ref[...] = h
    u_ref[...] = _dot_nt(h, rows(wa_ref, wa_copy, Z_U, Z_QLAT))
    hq_ref[...] = _rms(_dot_nt(h, rows(wa_ref, wa_copy, Z_QLAT, Z_KVLAT)), gq_ref[...]).astype(BF16)
    ckv_ref[...] = _rms(_dot_nt(h, rows(wa_ref, wa_copy, Z_KVLAT, Z_MQ)), gkv_ref[...])
    kpe_ref[:, :QK_ROPE] = _dot_nt(h, rows(wkpe_ref, wkpe_copy, 0, QK_ROPE))
    kpe_ref[:, QK_ROPE:] = jnp.zeros((h.shape[0], 128 - QK_ROPE), F32)
    for hh in range(MEM_HEADS):
        lo = hh * MEM_HEAD_DIM
        m = _dot_nt(h, rows(wmq_ref, wmq_copy, lo, lo + MEM_HEAD_DIM))
        qm_ref[:, lo:lo + MEM_HEAD_DIM] = _rms(m, gmq_ref[...]).astype(BF16)


def _w_in_rows(n, start):
    return pl.BlockSpec((pl.Element(n), pl.Element(D_MODEL)),
                        lambda i: (pl.multiple_of(start + 0 * i, BF16_SUBLANES), 0))


def _front_kernel(x_ref, gmix_ref, wa_ref, wkpe_ref, wmq_ref, gq_ref, gkv_ref, gmq_ref, left_ref, mk_ref, mv_ref,
                  h_ref, d_ref, hq_ref, ckv_ref, kpe_ref, am_ref, utail_ref, xp_sc, qm_sc,
                  *, tm, blocks_per_seq, pos0):
    sb = pl.program_id(0) % blocks_per_seq

    @pl.when(sb == 0)
    def _():
        xp_sc[:POOL_HALO, :] = left_ref[...]

    @pl.when(sb != 0)
    def _():
        xp_sc[:POOL_HALO, :] = xp_sc[tm:, :]

    rows_per = tm // FRONT_ROW_GROUPS
    for r in range(FRONT_ROW_GROUPS):
        rs = slice(r * rows_per, (r + 1) * rows_per)
        h = _rms(x_ref[rs, :], gmix_ref[...]).astype(BF16)
        h_ref[rs, :] = h
        xp_sc[POOL_HALO + r * rows_per:POOL_HALO + (r + 1) * rows_per, :] = _dot_nt(h, wa_ref[Z_U:Z_QLAT, :])
        hq_ref[rs, :] = _rms(_dot_nt(h, wa_ref[Z_QLAT:Z_KVLAT, :]), gq_ref[...]).astype(BF16)
        ckv_ref[rs, :] = _rms(_dot_nt(h, wa_ref[Z_KVLAT:Z_MQ, :]), gkv_ref[...])
        kpe_ref[rs, :QK_ROPE] = _dot_nt(h, wkpe_ref[...])
        kpe_ref[rs, QK_ROPE:] = jnp.zeros((rows_per, 128 - QK_ROPE), F32)
        for hh in range(MEM_HEADS):
            lo = hh * MEM_HEAD_DIM
            m = _dot_nt(h, wmq_ref[lo:lo + MEM_HEAD_DIM, :])
            qm_sc[rs, lo:lo + MEM_HEAD_DIM] = _rms(m, gmq_ref[...]).astype(BF16)
    utail_ref[...] = xp_sc[tm:, :]
    _pool_windows(xp_sc, pos0 + sb * tm, tm, d_ref)
    _mem_attn_kernel(qm_sc, mk_ref, mv_ref, am_ref)


def _front(x, gmix, wa_t, wkpe_t, wmq_t, gq, gkv, gmq_scaled, left, mem_kv, tm, blocks_per_seq, pos0):
    t = x.shape[0]
    n_seq = t // (tm * blocks_per_seq)
    row = lambda w: pl.BlockSpec((tm, w), lambda i: (i, 0))
    full = lambda a: pl.BlockSpec(a.shape, lambda i: (0, 0))
    w_elems = wa_t.size + wkpe_t.size + wmq_t.size
    vmem = 2 * (tm * D_MODEL * 6 + w_elems * 2 + tm * (POOL_W * 2 + Q_LORA * 2 + KV_LORA * 4 + 128 * 4 + MEM_W * 2)
                + 2 * N_MEM * MEM_W * 4 + 2 * POOL_HALO * POOL_W * 4)
    vmem += (tm + POOL_HALO) * POOL_W * 4 + tm * MEM_W * 2 + 8 * tm * POOL_W * 4
    kern = functools.partial(_front_kernel, tm=tm, blocks_per_seq=blocks_per_seq, pos0=pos0)
    return pl.pallas_call(
        kern,
        grid=(t // tm,),
        in_specs=[row(D_MODEL), full(gmix), full(wa_t), full(wkpe_t), full(wmq_t),
                  full(gq), full(gkv), full(gmq_scaled),
                  pl.BlockSpec((None, POOL_HALO, POOL_W), lambda i: (i // blocks_per_seq, 0, 0)),
                  pl.BlockSpec((N_MEM, MEM_W), lambda i: (i // blocks_per_seq, 0)),
                  pl.BlockSpec((N_MEM, MEM_W), lambda i: (i // blocks_per_seq, 1))],
        out_specs=[row(D_MODEL), row(POOL_W), row(Q_LORA), row(KV_LORA), row(128), row(MEM_W),
                   pl.BlockSpec((None, POOL_HALO, POOL_W), lambda i: (i // blocks_per_seq, 0, 0))],
        out_shape=[jax.ShapeDtypeStruct((t, D_MODEL), BF16),
                   jax.ShapeDtypeStruct((t, POOL_W), BF16), jax.ShapeDtypeStruct((t, Q_LORA), BF16),
                   jax.ShapeDtypeStruct((t, KV_LORA), F32), jax.ShapeDtypeStruct((t, 128), F32),
                   jax.ShapeDtypeStruct((t, MEM_W), BF16), jax.ShapeDtypeStruct((n_seq, POOL_HALO, POOL_W), F32)],
        scratch_shapes=[pltpu.VMEM((tm + POOL_HALO, POOL_W), F32), pltpu.VMEM((tm, MEM_W), BF16)],
        compiler_params=_params(("arbitrary",), vmem),
        name="front",
    )(x, gmix, wa_t, wkpe_t, wmq_t, gq, gkv, gmq_scaled, left, mem_kv, mem_kv)


def _in_proj(x, gmix, w_in_t, gq, gkv, gmq_scaled):
    t = x.shape[0]
    row = lambda w: pl.BlockSpec((t, w), lambda i: (0, 0))
    full = lambda a: pl.BlockSpec(a.shape, lambda i: (0, 0))
    pieces = ((Z_MQ, 0), (QK_ROPE, W_IN_KPE), (MEM_W, W_IN_MQ))
    w_elems = sum(n for n, _ in pieces) * D_MODEL
    vmem = t * D_MODEL * 8 + w_elems * 6 + 2 * t * (POOL_W * 4 + Q_LORA * 2 + KV_LORA * 4 + 128 * 4 + MEM_W * 2)
    vmem += w_elems * 2 + 6 * t * POOL_W * 4
    return pl.pallas_call(
        _in_proj_kernel,
        grid=(1,),
        in_specs=[row(D_MODEL), full(gmix)] + [_w_in_rows(n, start) for n, start in pieces]
                 + [full(gq), full(gkv), full(gmq_scaled)],
        out_specs=[row(D_MODEL), row(POOL_W), row(Q_LORA), row(KV_LORA), row(128), row(MEM_W)]
                  + [pl.BlockSpec((n, D_MODEL), lambda i: (0, 0)) for n, _ in pieces],
        out_shape=[jax.ShapeDtypeStruct((t, D_MODEL), BF16),
                   jax.ShapeDtypeStruct((t, POOL_W), F32), jax.ShapeDtypeStruct((t, Q_LORA), BF16),
                   jax.ShapeDtypeStruct((t, KV_LORA), F32), jax.ShapeDtypeStruct((t, 128), F32),
                   jax.ShapeDtypeStruct((t, MEM_W), BF16)]
                  + [jax.ShapeDtypeStruct((n, D_MODEL), BF16) for n, _ in pieces],
        compiler_params=_params(("arbitrary",), vmem),
        name="in_proj",
    )(x, gmix, w_in_t, w_in_t, w_in_t, gq, gkv, gmq_scaled)


def _head_store(ref, head_major):
    def store(h, nope, rope):
        if head_major:
            ref[h, :, :QK_NOPE] = nope
            ref[h, :, QK_NOPE:] = rope
        else:
            lo = h * HEAD_PAD
            ref[:, lo:lo + QK_NOPE] = nope
            ref[:, lo + QK_NOPE:lo + HEAD_PAD] = rope
    return store


def _q_heads(hq, wq_ref, gq_ref, qtab, store):
    first = lax.broadcasted_iota(jnp.int32, (1, 128), 1) < QK_ROPE
    for h in range(MLA_HEADS):
        lo = h * HEAD_PAD
        qh = _dot(hq, wq_ref[:, lo:lo + HEAD_PAD])
        qn, qr = qh[:, :QK_NOPE], qh[:, QK_NOPE:]
        ss = jnp.sum(qn * qn + jnp.where(first, qr * qr, 0.0), axis=-1, keepdims=True)
        r = lax.rsqrt(ss * (1.0 / QK_HEAD) + EPS)
        store(h, (qn * r * gq_ref[:, :QK_NOPE]).astype(BF16), (qr * r * gq_ref[:, QK_NOPE:] * qtab).astype(BF16))


def _qkv_kernel(hq_ref, ckv_ref, kpe_ref, wq_ref, wkbt_ref, wvbt_ref, gq_ref, gkn_ref, gkr_ref,
                qtab_ref, c_ref, sa_ref, sb_ref, q_ref, k_ref, vt_ref):
    _q_heads(hq_ref[...], wq_ref, gq_ref, qtab_ref[...], _head_store(q_ref, True))
    store_k = _head_store(k_ref, True)
    ckv = ckv_ref[...].astype(BF16)
    kpe = kpe_ref[...]
    ss_pe = jnp.sum(kpe * kpe, axis=-1, keepdims=True)
    kr = _rope128(kpe * gkr_ref[...], c_ref[...], sa_ref[...], sb_ref[...])
    kr = kr + pltpu.roll(kr, QK_ROPE, 1)
    for p in range(MLA_HEADS // 2):
        kn2 = _dot_nt(ckv, wkbt_ref[p * 2 * QK_NOPE:(p + 1) * 2 * QK_NOPE, :])
        for s in range(2):
            kn = kn2[:, s * QK_NOPE:(s + 1) * QK_NOPE]
            r = lax.rsqrt((jnp.sum(kn * kn, axis=-1, keepdims=True) + ss_pe) * (1.0 / QK_HEAD) + EPS)
            store_k(2 * p + s, (kn * r * gkn_ref[...]).astype(BF16), (kr * r).astype(BF16))
    vt = _dot_nt(wvbt_ref[...], ckv)
    tm = vt.shape[1]
    ones_row = (lax.broadcasted_iota(jnp.int32, (V_ROWS - V_HEAD, tm), 0) == 0).astype(BF16)
    for h in range(MLA_HEADS):
        vt_ref[h, :V_HEAD, :] = vt[h * V_HEAD:(h + 1) * V_HEAD].astype(BF16)
        vt_ref[h, V_HEAD:, :] = ones_row


def _qkv(hq, ckv, kpe, wq, wkbt, wvbt, gq, gkn, gkr, tabs, tm, seq_blocks):
    t = hq.shape[0]
    n_seq = t // (tm * seq_blocks)
    qk_spec = pl.BlockSpec((None, MLA_HEADS, tm, HEAD_PAD), lambda i: (i // seq_blocks, 0, i % seq_blocks, 0))
    qk_shape = jax.ShapeDtypeStruct((n_seq, MLA_HEADS, tm * seq_blocks, HEAD_PAD), BF16)
    row = lambda w: pl.BlockSpec((tm, w), lambda i: (i, 0))
    full = lambda a: pl.BlockSpec(a.shape, lambda i: (0, 0))
    tab = pl.BlockSpec((tm, 128), lambda i: (i % seq_blocks, 0))
    vmem = 2 * (tm * (Q_LORA * 2 + KV_LORA * 4 + 128 * 4 + 4 * 128 * 4) + (wq.size + wkbt.size + wvbt.size) * 2
                + tm * (2 * MLA_HEADS * HEAD_PAD * 2 + MLA_HEADS * V_ROWS * 2))
    vmem += 8 * tm * HEAD_PAD * 4 + 2 * tm * MLA_HEADS * V_HEAD * 4
    return pl.pallas_call(
        _qkv_kernel,
        grid=(t // tm,),
        in_specs=[row(Q_LORA), row(KV_LORA), row(128), full(wq), full(wkbt), full(wvbt),
                  full(gq), full(gkn), full(gkr), tab, tab, tab, tab],
        out_specs=[qk_spec, qk_spec,
                   pl.BlockSpec((None, MLA_HEADS, None, V_ROWS, tm),
                                lambda i: (i // seq_blocks, 0, i % seq_blocks, 0, 0))],
        out_shape=[qk_shape, qk_shape,
                   jax.ShapeDtypeStruct((n_seq, MLA_HEADS, seq_blocks, V_ROWS, tm), BF16)],
        compiler_params=_params(("parallel",), vmem),
        name="qkv_proj",
    )(hq, ckv, kpe, wq, wkbt, wvbt, gq, gkn, gkr, *tabs)


def _q_only_kernel(hq_ref, wq_ref, gq_ref, qtab_ref, q_ref):
    _q_heads(hq_ref[...], wq_ref, gq_ref, qtab_ref[...], _head_store(q_ref, False))


def _q_only(hq, wq, gq, qtab, tm):
    t = hq.shape[0]
    row = lambda w: pl.BlockSpec((tm, w), lambda i: (i, 0))
    full = lambda a: pl.BlockSpec(a.shape, lambda i: (0, 0))
    vmem = 2 * (tm * (Q_LORA * 2 + 128 * 4 + MLA_HEADS * HEAD_PAD * 2) + wq.size * 2) + 8 * tm * HEAD_PAD * 4
    return pl.pallas_call(
        _q_only_kernel,
        grid=(t // tm,),
        in_specs=[row(Q_LORA), full(wq), full(gq), pl.BlockSpec((tm, 128), lambda i: (0, 0))],
        out_specs=row(MLA_HEADS * HEAD_PAD),
        out_shape=jax.ShapeDtypeStruct((t, MLA_HEADS * HEAD_PAD), BF16),
        compiler_params=_params(("parallel",), vmem),
        name="q_proj",
    )(hq, wq, gq, qtab)


def _flash_kernel(q_ref, k_ref, vt_ref, o_ref, *scratch, tq, tk, nq):
    *bufs, m_sc, acc_sc = scratch
    qi = pl.program_id(2)
    n_diag = tq // tk

    def scores(kj, s_sc, q_lo=0):
        s_sc[:, q_lo:] = _dot_nt(k_ref[kj * tk:(kj + 1) * tk, :], q_ref[q_lo:, :])

    def consume(kj, s_sc, key_chunk0=None, q_lo=0):
        s = s_sc[:, q_lo:]
        if key_chunk0 is not None:
            kc = key_chunk0 + lax.broadcasted_iota(jnp.int32, s.shape, 0) // CHUNK
            qc = (q_lo + lax.broadcasted_iota(jnp.int32, s.shape, 1)) // CHUNK
            s = jnp.where(qc >= kc, s, NEG_INF)
        m = m_sc[:, q_lo:]
        m_new = jnp.maximum(m, jnp.max(s, axis=0, keepdims=True))
        alpha = jnp.exp2(m - m_new)
        p = jnp.exp2(s - m_new).astype(BF16)
        acc_sc[:, q_lo:] = alpha * acc_sc[:, q_lo:] + _dot(vt_ref[kj], p)
        m_sc[:, q_lo:] = m_new

    def run(q_block):
        tiles = [(kj, 0, None) for kj in range(q_block * n_diag)]
        tiles += [(q_block * n_diag + d, d * tk, d * tk // CHUNK) for d in range(n_diag)]
        m_sc[...] = jnp.full(m_sc.shape, NEG_INF, F32)
        acc_sc[...] = jnp.zeros(acc_sc.shape, F32)
        for i in range(min(FLASH_AHEAD, len(tiles))):
            scores(tiles[i][0], bufs[i % len(bufs)], tiles[i][1])
        for i, (kj, q_lo, key_chunk0) in enumerate(tiles):
            if i + FLASH_AHEAD < len(tiles):
                nkj, nq_lo, _ = tiles[i + FLASH_AHEAD]
                scores(nkj, bufs[(i + FLASH_AHEAD) % len(bufs)], nq_lo)
            consume(kj, bufs[i % len(bufs)], key_chunk0, q_lo)
        acc = acc_sc[...]
        o_ref[...] = (acc[:V_HEAD] / acc[V_HEAD:V_HEAD + 1]).T.astype(BF16)

    for q_block in range(nq):
        pl.when(qi == q_block)(functools.partial(run, q_block))


def _flash(q, k, vt, batch, seq, tq):
    tk = vt.shape[-1]
    assert tq % tk == 0 and tk % CHUNK == 0 and seq % tq == 0
    nq = seq // tq
    n_bufs = FLASH_AHEAD + 1
    vmem = 2 * (tq * HEAD_PAD * 2 + seq * HEAD_PAD * 2 + seq * V_ROWS * 2 + tq * V_HEAD * 2)
    vmem += n_bufs * tk * tq * 4 + V_ROWS * tq * 4 + 4 * tk * tq * 4
    return pl.pallas_call(
        functools.partial(_flash_kernel, tq=tq, tk=tk, nq=nq),
        grid=(batch, MLA_HEADS, nq),
        in_specs=[pl.BlockSpec((None, None, tq, HEAD_PAD), lambda b, h, i: (b, h, i, 0)),
                  pl.BlockSpec((None, None, seq, HEAD_PAD), lambda b, h, i: (b, h, 0, 0)),
                  pl.BlockSpec((None, None, seq // tk, V_ROWS, tk), lambda b, h, i: (b, h, 0, 0, 0))],
        out_specs=pl.BlockSpec((tq, V_HEAD), lambda b, h, i: (b * nq + i, h)),
        out_shape=jax.ShapeDtypeStruct((batch * seq, MLA_HEADS * V_HEAD), BF16),
        scratch_shapes=[pltpu.VMEM((tk, tq), F32)] * n_bufs + [pltpu.VMEM((1, tq), F32),
                                                                pltpu.VMEM((V_ROWS, tq), F32)],
        compiler_params=_params(("parallel", "parallel", "arbitrary"), vmem),
        name="mla_flash",
    )(q, k, vt)


def _q_absorb_kernel(q_ref, wkbt_ref, qa_ref, qr_ref):
    qa_ref[...] = _dot(q_ref[:, :QK_NOPE], wkbt_ref[...]).astype(BF16)
    qr_ref[...] = q_ref[:, QK_NOPE:]


def _q_absorb(q, wkbt):
    t = q.shape[0]
    vmem = 2 * (t * HEAD_PAD * 2 + KV_LORA * QK_NOPE * 2 + t * KV_LORA * 2 + t * 128 * 2) + 2 * t * KV_LORA * 4
    return pl.pallas_call(
        _q_absorb_kernel,
        grid=(MLA_HEADS,),
        in_specs=[pl.BlockSpec((t, HEAD_PAD), lambda h: (0, h)),
                  pl.BlockSpec((QK_NOPE, KV_LORA), lambda h: (h, 0))],
        out_specs=[pl.BlockSpec((None, t, KV_LORA), lambda h: (h, 0, 0)),
                   pl.BlockSpec((None, t, 128), lambda h: (h, 0, 0))],
        out_shape=[jax.ShapeDtypeStruct((MLA_HEADS, t, KV_LORA), BF16),
                   jax.ShapeDtypeStruct((MLA_HEADS, t, 128), BF16)],
        compiler_params=_params(("parallel",), vmem),
        name="q_absorb",
    )(q, wkbt)


def _cache_attn_kernel(wkbt_ref, qa_ref, qr_ref, lat_ref, kpet_ref, ct_ref, st_ref,
                       nlat_ref, nkpet_ref, nct_ref, nst_ref, gkr_ref,
                       o_ref, *scratch, nq, n_new, past, tk):
    *bufs, m_sc, l_sc, acc_sc = scratch
    rows = MLA_HEADS * nq
    half = QK_ROPE // 2
    m_sc[...] = jnp.full(m_sc.shape, NEG_INF, F32)
    l_sc[...] = jnp.zeros(l_sc.shape, F32)
    acc_sc[...] = jnp.zeros(acc_sc.shape, F32)

    def scores(lat, kpet, ct, st, width, s_sc):
        c = lat.astype(BF16)
        kt = _dot_nt(wkbt_ref[...], c)
        s = _dot_nt(qa_ref[...].reshape(rows, KV_LORA), c)
        ss_pe = jnp.sum(kpet * kpet, axis=0, keepdims=True)
        kr = kpet * gkr_ref[...]
        k1, k2 = kr[:half], kr[half:]
        o1, o2 = k1 * ct - k2 * st, k2 * ct + k1 * st
        krot = jnp.concatenate([o1, o2, o1, o2], axis=0).astype(BF16)
        s = s + _dot(qr_ref[...].reshape(rows, 128), krot)
        for h in range(MLA_HEADS):
            kh = kt[h * QK_NOPE:(h + 1) * QK_NOPE]
            r = lax.rsqrt((jnp.sum(kh * kh, axis=0, keepdims=True) + ss_pe) * (1.0 / QK_HEAD) + EPS)
            s_sc[h * nq:(h + 1) * nq, :width] = s[h * nq:(h + 1) * nq] * r

    def consume(lat, width, s_sc, visible):
        s = s_sc[:, :width]
        if visible is not None:
            s = jnp.where(visible, s, NEG_INF)
        m = m_sc[...]
        m_new = jnp.maximum(m, jnp.max(s, axis=-1, keepdims=True))
        alpha = jnp.exp2(m - m_new)
        p = jnp.exp2(s - m_new)
        l_sc[...] = alpha * l_sc[...] + jnp.sum(p, axis=-1, keepdims=True)
        m_sc[...] = m_new
        acc_sc[...] = alpha * acc_sc[...] + _dot(p.astype(BF16), lat.astype(BF16))

    kk = lax.broadcasted_iota(jnp.int32, (1, 128), 1)
    new_visible = kk < n_new
    if (past + n_new - 1) // CHUNK != past // CHUNK:
        qpos = past + lax.broadcasted_iota(jnp.int32, (rows, 1), 0) % nq
        new_visible = new_visible & (qpos // CHUNK >= (past + kk) // CHUNK)
    blocks = []
    for j in range(past // tk):
        ks = slice(j * tk, (j + 1) * tk)
        blocks.append((lambda ks=ks: (lat_ref[ks, :], kpet_ref[:, ks], ct_ref[:, ks], st_ref[:, ks]), tk, None))
    blocks.append((lambda: (nlat_ref[...], nkpet_ref[...], nct_ref[...], nst_ref[...]), 128, new_visible))

    def issue(i):
        get, width, _ = blocks[i]
        scores(*get(), width, bufs[i % len(bufs)])

    for i in range(min(CACHE_AHEAD, len(blocks))):
        issue(i)
    for i, (get, width, visible) in enumerate(blocks):
        if i + CACHE_AHEAD < len(blocks):
            issue(i + CACHE_AHEAD)
        consume(get()[0], width, bufs[i % len(bufs)], visible)
    o_ref[...] = (acc_sc[...] / l_sc[...]).astype(BF16)


def _cache_attn(wkbt, qa, qr, lat, kpet, ct, st, nlat, nkpet, nct, nst, gkr, nq, n_new, tk):
    batch, past, _ = lat.shape
    assert past % tk == 0
    rows = MLA_HEADS * nq
    n_bufs = CACHE_AHEAD + 1
    kern = functools.partial(_cache_attn_kernel, nq=nq, n_new=n_new, past=past, tk=tk)
    vmem = 2 * (wkbt.size * 2 + rows * (KV_LORA + 128) * 2 + past * KV_LORA * 4 + 3 * 64 * past * 4 + rows * KV_LORA * 2)
    vmem += (MLA_HEADS * QK_NOPE + 4 * rows) * tk * 4 + tk * KV_LORA * 2 + rows * KV_LORA * 8 + n_bufs * rows * tk * 4
    whole = lambda a: pl.BlockSpec(a.shape, lambda b: (0,) * a.ndim)
    per_batch = lambda a: pl.BlockSpec((None,) + a.shape[1:], lambda b: (b,) + (0,) * (a.ndim - 1))
    return pl.pallas_call(
        kern,
        grid=(batch,),
        in_specs=[whole(wkbt),
                  pl.BlockSpec((MLA_HEADS, nq, KV_LORA), lambda b: (0, b, 0)),
                  pl.BlockSpec((MLA_HEADS, nq, 128), lambda b: (0, b, 0)),
                  per_batch(lat), per_batch(kpet), whole(ct), whole(st),
                  per_batch(nlat), per_batch(nkpet), whole(nct), whole(nst), whole(gkr)],
        out_specs=pl.BlockSpec((None, rows, KV_LORA), lambda b: (b, 0, 0)),
        out_shape=jax.ShapeDtypeStruct((batch, rows, KV_LORA), BF16),
        scratch_shapes=[pltpu.VMEM((rows, tk), F32)] * n_bufs
                       + [pltpu.VMEM((rows, 1), F32), pltpu.VMEM((rows, 1), F32), pltpu.VMEM((rows, KV_LORA), F32)],
        compiler_params=_params(("parallel",), vmem),
        name="mla_cache_attn",
    )(wkbt, qa, qr, lat, kpet, ct, st, nlat, nkpet, nct, nst, gkr)


def _v_up_kernel(ol_ref, wvbt_ref, o_ref):
    b, nq, _ = ol_ref.shape
    o_ref[...] = _dot_nt(ol_ref[...].reshape(b * nq, KV_LORA), wvbt_ref[...]).astype(BF16)


def _v_up(o_lat, wvbt, nq):
    batch = o_lat.shape[0]
    t = batch * nq
    vmem = 2 * (t * KV_LORA * 2 + KV_LORA * V_HEAD * 2 + t * V_HEAD * 2) + 2 * t * V_HEAD * 4
    return pl.pallas_call(
        _v_up_kernel,
        grid=(MLA_HEADS,),
        in_specs=[pl.BlockSpec((batch, nq, KV_LORA), lambda h: (0, h, 0)),
                  pl.BlockSpec((V_HEAD, KV_LORA), lambda h: (h, 0))],
        out_specs=pl.BlockSpec((t, V_HEAD), lambda h: (0, h)),
        out_shape=jax.ShapeDtypeStruct((t, MLA_HEADS * V_HEAD), BF16),
        compiler_params=_params(("parallel",), vmem),
        name="v_up",
    )(o_lat, wvbt)


def _mem_kv_kernel(mem_ref, gmem_ref, w_ref, gk_ref, kv_ref):
    j = pl.program_id(0)
    h = _rms(mem_ref[...], gmem_ref[...]).astype(BF16)

    @pl.when(j == 0)
    def _():
        for hh in range(MEM_HEADS):
            lo = hh * MEM_HEAD_DIM
            w = w_ref[:, lo:lo + MEM_HEAD_DIM].astype(BF16)
            kv_ref[:, lo:lo + MEM_HEAD_DIM] = _rms(_dot(h, w), gk_ref[...])

    @pl.when(j == 1)
    def _():
        kv_ref[...] = _dot(h, w_ref[...].astype(BF16))


def _mem_kv(mem, gmem, w, gk):
    t = mem.shape[0]
    vmem = 2 * (t * D_MODEL * 4 + D_MODEL * MEM_W * w.dtype.itemsize + t * MEM_W * 4) + t * D_MODEL * 6
    vmem += t * MEM_W * 4 + D_MODEL * MEM_W * 2
    return pl.pallas_call(
        _mem_kv_kernel,
        grid=(2,),
        in_specs=[pl.BlockSpec((t, D_MODEL), lambda j: (0, 0)), pl.BlockSpec((1, D_MODEL), lambda j: (0, 0)),
                  pl.BlockSpec((D_MODEL, MEM_W), lambda j: (0, j)), pl.BlockSpec((1, MEM_HEAD_DIM), lambda j: (0, 0))],
        out_specs=pl.BlockSpec((t, MEM_W), lambda j: (0, j)),
        out_shape=jax.ShapeDtypeStruct((t, 2 * MEM_W), F32),
        compiler_params=_params(("parallel",), vmem),
        name="memory_kv",
    )(mem, gmem, w, gk)


def _mem_attn_kernel(qm_ref, k_ref, v_ref, o_ref):
    for hh in range(MEM_HEADS):
        sl = slice(hh * MEM_HEAD_DIM, (hh + 1) * MEM_HEAD_DIM)
        s = _dot_nt(qm_ref[:, sl], k_ref[:, sl].astype(BF16))
        p = jnp.exp(s - jnp.max(s, axis=-1, keepdims=True))
        l = jnp.sum(p, axis=-1, keepdims=True)
        o_ref[:, sl] = (_dot(p.astype(BF16), v_ref[:, sl].astype(BF16)) / l).astype(BF16)


def _mem_attn(qm, k_arr, v_arr, k_col, v_col, tm, blocks_per_batch):
    t = qm.shape[0]
    vmem = 2 * (2 * tm * MEM_W * 2 + 2 * N_MEM * MEM_W * 4) + 6 * tm * N_MEM * 4 + 2 * N_MEM * MEM_W * 2
    return pl.pallas_call(
        _mem_attn_kernel,
        grid=(t // tm,),
        in_specs=[pl.BlockSpec((tm, MEM_W), lambda i: (i, 0)),
                  pl.BlockSpec((N_MEM, MEM_W), lambda i: (i // blocks_per_batch, k_col)),
                  pl.BlockSpec((N_MEM, MEM_W), lambda i: (i // blocks_per_batch, v_col))],
        out_specs=pl.BlockSpec((tm, MEM_W), lambda i: (i, 0)),
        out_shape=jax.ShapeDtypeStruct((t, MEM_W), BF16),
        compiler_params=_params(("parallel",), vmem),
        name="memory_attn",
    )(qm, k_arr, v_arr)


def _pool_windows(xp_sc, first_pos, tm, d_ref):
    pos = first_pos + lax.broadcasted_iota(jnp.int32, (tm, 1), 0)
    for g, win in enumerate(POOL_WINDOWS):
        sl = slice(g * POOL_GW, (g + 1) * POOL_GW)
        x = xp_sc[:, sl]
        wsum, k = x, 1
        while k < win:
            wsum = wsum + pltpu.roll(wsum, k, 0)
            k *= 2
        cnt = jnp.minimum(pos + 1, win).astype(F32)
        d_ref[:, sl] = (wsum[POOL_HALO:] / cnt - x[POOL_HALO:]).astype(BF16)


def _pool_kernel(u_ref, prev_ref, left_ref, d_ref, xp_sc, *, tm, blocks_per_seq, pos0):
    sb = pl.program_id(0) % blocks_per_seq
    xp_sc[:POOL_HALO, :] = jnp.where(sb == 0, left_ref[...], prev_ref[...])
    xp_sc[POOL_HALO:, :] = u_ref[...]
    _pool_windows(xp_sc, pos0 + sb * tm, tm, d_ref)


def _pool(u, left, tm, blocks_per_seq, pos0):
    t = u.shape[0]
    halo_per_blk = tm // POOL_HALO
    kern = functools.partial(_pool_kernel, tm=tm, blocks_per_seq=blocks_per_seq, pos0=pos0)
    vmem = 2 * (tm * POOL_W * 4 + 2 * POOL_HALO * POOL_W * 4 + tm * POOL_W * 2) + (tm + POOL_HALO) * POOL_W * 4
    vmem += 6 * tm * POOL_GW * 4
    return pl.pallas_call(
        kern,
        grid=(t // tm,),
        in_specs=[pl.BlockSpec((tm, POOL_W), lambda i: (i, 0)),
                  pl.BlockSpec((POOL_HALO, POOL_W), lambda i: (jnp.maximum(i * halo_per_blk - 1, 0), 0)),
                  pl.BlockSpec((None, POOL_HALO, POOL_W), lambda i: (i // blocks_per_seq, 0, 0))],
        out_specs=pl.BlockSpec((tm, POOL_W), lambda i: (i, 0)),
        out_shape=jax.ShapeDtypeStruct((t, POOL_W), BF16),
        scratch_shapes=[pltpu.VMEM((tm + POOL_HALO, POOL_W), F32)],
        compiler_params=_params(("parallel",), vmem),
        name="pool_windows",
    )(u, u, left)


def _merge_kernel(h_ref, wg0_ref, wg1_ref, wg2_ref, bg0_ref, bg1_ref, bg2_ref,
                  d_ref, wpool_ref, pscale_ref, o_ref, wmla_ref, am_ref, wmem_ref,
                  out_ref, *w_copies):
    wg_copies, (wpool_copy, wmla_copy, wmem_copy) = (w_copies[:3], w_copies[3:]) if w_copies else ((None,) * 3,) * 2
    h = h_ref[...]
    y_pool = _dot(d_ref[...], _as_bf16(wpool_ref, wpool_copy)) * pscale_ref[...]
    y_mla = _dot(o_ref[...], _as_bf16(wmla_ref, wmla_copy))
    y_mem = _dot(am_ref[...], _as_bf16(wmem_ref, wmem_copy))
    gate = lambda w_ref, copy, b_ref: jax.nn.sigmoid(_dot_nt(h, _as_bf16(w_ref, copy)) + b_ref[...])
    merged = gate(wg0_ref, wg_copies[0], bg0_ref) * y_pool
    merged += gate(wg1_ref, wg_copies[1], bg1_ref) * y_mla
    merged += gate(wg2_ref, wg_copies[2], bg2_ref) * y_mem
    out_ref[...] = merged.astype(BF16)


def _merge(h, wgates, bgate, d, wpool, pscale, o, wmla, am, wmem, tm, tn, emit_bf16=False):
    t = h.shape[0]
    assert not emit_bf16 or t == tm
    assert POOL_OUT_GW % tn == 0
    nj = D_MODEL // tn
    per_group = POOL_OUT_GW // tn
    rowfull = lambda w: pl.BlockSpec((tm, w), lambda i, j: (i, 0))
    gate_rows = pl.BlockSpec((tn, D_MODEL), lambda i, j: (j, 0))
    if emit_bf16:
        gate_in = [pl.BlockSpec((pl.Element(tn), pl.Element(D_MODEL)),
                                lambda i, j, b=b: (pl.multiple_of(W_IN_GATE + (b * nj + j) * tn, BF16_SUBLANES), 0))
                   for b in range(3)]
        gate_args = (wgates,) * 3
    else:
        gate_in = [gate_rows] * 3
        gate_args = tuple(wgates)
    gate_b = lambda b: pl.BlockSpec((1, tn), lambda i, j: (0, b * nj + j))
    col = lambda k: pl.BlockSpec((k, tn), lambda i, j: (0, j))
    pool_spec = pl.BlockSpec((None, POOL_GW, tn), lambda i, j: (j // per_group, 0, j % per_group))
    w_bytes = wmla.dtype.itemsize + 2 * emit_bf16
    w_rows = 3 * D_MODEL + POOL_GW + D_MODEL + MEM_W
    vmem = 2 * (tm * D_MODEL * 2 + w_rows * tn * w_bytes + tm * POOL_GW * 2 + tm * D_MODEL * 2
                + tm * MEM_W * 2 + tm * tn * 2)
    vmem += 8 * tm * tn * 4 + w_rows * tn * 2 * emit_bf16
    y_spec = pl.BlockSpec((tm, tn), lambda i, j: (i, j))
    y_shape = jax.ShapeDtypeStruct((t, D_MODEL), BF16)
    copies = ([jax.ShapeDtypeStruct((D_MODEL, D_MODEL), BF16)] * 3
              + [jax.ShapeDtypeStruct(w.shape, BF16) for w in (wpool, wmla, wmem)])
    return pl.pallas_call(
        _merge_kernel,
        grid=(t // tm, nj),
        in_specs=[rowfull(D_MODEL)] + gate_in
                 + [gate_b(0), gate_b(1), gate_b(2),
                    pl.BlockSpec((tm, POOL_GW), lambda i, j: (i, j // per_group)), pool_spec,
                    pl.BlockSpec((1, tn), lambda i, j: (0, j)),
                    rowfull(D_MODEL), col(D_MODEL), rowfull(MEM_W), col(MEM_W)],
        out_specs=[y_spec] + [gate_rows] * 3 + [pool_spec, col(D_MODEL), col(MEM_W)] if emit_bf16 else y_spec,
        out_shape=[y_shape] + copies if emit_bf16 else y_shape,
        compiler_params=_params(("parallel", "arbitrary" if emit_bf16 else "parallel"), vmem),
        name="gated_merge",
    )(h, *gate_args, bgate, bgate, bgate, d, wpool, pscale, o, wmla, am, wmem)


def _out_proj_kernel(x_ref, m_ref, w_ref, y_ref, *w_copy):
    y_ref[...] = x_ref[...] + _dot(m_ref[...], _as_bf16(w_ref, *w_copy or (None,)))


def _out_proj(x, merged, w, tm, tn, emit_bf16=False):
    t = x.shape[0]
    assert not emit_bf16 or t == tm
    w_spec = pl.BlockSpec((D_MODEL, tn), lambda i, j: (0, j))
    vmem = 2 * (2 * tm * tn * 4 + tm * D_MODEL * 2 + D_MODEL * tn * (w.dtype.itemsize + 2 * emit_bf16)) + 2 * tm * tn * 4
    y_spec = pl.BlockSpec((tm, tn), lambda i, j: (i, j))
    y_shape = jax.ShapeDtypeStruct((t, D_MODEL), F32)
    return pl.pallas_call(
        _out_proj_kernel,
        grid=(t // tm, D_MODEL // tn),
        in_specs=[pl.BlockSpec((tm, tn), lambda i, j: (i, j)),
                  pl.BlockSpec((tm, D_MODEL), lambda i, j: (i, 0)), w_spec],
        out_specs=[y_spec, w_spec] if emit_bf16 else y_spec,
        out_shape=[y_shape, jax.ShapeDtypeStruct(w.shape, BF16)] if emit_bf16 else y_shape,
        compiler_params=_params(("parallel", "arbitrary" if emit_bf16 else "parallel"), vmem),
        name="out_proj",
    )(x, merged, w)


def _ffn_kernel(x_ref, g_ref, wup_ref, wdown_ref, y_ref, *rest):
    *w_copies, h_sc, acc_sc = rest
    wup_copy, wdown_copy = w_copies or (None, None)
    j = pl.program_id(1)

    @pl.when(j == 0)
    def _():
        h_sc[...] = _rms(x_ref[...], g_ref[...]).astype(BF16)
        acc_sc[...] = jnp.zeros(acc_sc.shape, F32)

    f = jnp.maximum(_dot(h_sc[...], _as_bf16(wup_ref, wup_copy)), 0.0)
    acc_sc[...] += _dot((f * f).astype(BF16), _as_bf16(wdown_ref, wdown_copy))

    @pl.when(j == pl.num_programs(1) - 1)
    def _():
        y_ref[...] = x_ref[...] + acc_sc[...]


def _ffn(x, g, wup, wdown, tm, tf, emit_bf16=False):
    t = x.shape[0]
    assert not emit_bf16 or t == tm
    w_bytes = wup.dtype.itemsize + 2 * emit_bf16
    vmem = 2 * (2 * tm * D_MODEL * 4 + 2 * D_MODEL * tf * w_bytes) + tm * D_MODEL * 6 + 3 * tm * tf * 4 + tm * D_MODEL * 4
    vmem += 2 * D_MODEL * tf * 2 * emit_bf16
    up_spec = pl.BlockSpec((D_MODEL, tf), lambda i, j: (0, j))
    down_spec = pl.BlockSpec((tf, D_MODEL), lambda i, j: (j, 0))
    y_spec = pl.BlockSpec((tm, D_MODEL), lambda i, j: (i, 0))
    y_shape = jax.ShapeDtypeStruct((t, D_MODEL), F32)
    return pl.pallas_call(
        _ffn_kernel,
        grid=(t // tm, D_FF // tf),
        in_specs=[pl.BlockSpec((tm, D_MODEL), lambda i, j: (i, 0)),
                  pl.BlockSpec((1, D_MODEL), lambda i, j: (0, 0)), up_spec, down_spec],
        out_specs=[y_spec, up_spec, down_spec] if emit_bf16 else y_spec,
        out_shape=([y_shape, jax.ShapeDtypeStruct(wup.shape, BF16), jax.ShapeDtypeStruct(wdown.shape, BF16)]
                   if emit_bf16 else y_shape),
        scratch_shapes=[pltpu.VMEM((tm, D_MODEL), BF16), pltpu.VMEM((tm, D_MODEL), F32)],
        compiler_params=_params(("parallel", "arbitrary"), vmem),
        name="ffn",
    )(x, g, wup, wdown)


def _rope_tables(pos):
    half = QK_ROPE // 2
    inv = 1.0 / (ROPE_THETA ** (jnp.arange(half, dtype=F32) * (2.0 / QK_ROPE)))
    ang = pos.astype(F32)[:, None] * inv[None, :]
    return jnp.cos(ang), jnp.sin(ang)


def _lane_tables(cos, sin):
    z = jnp.zeros_like(cos)
    c = jnp.concatenate([cos, cos, z, z], axis=1)
    sa = jnp.concatenate([z, sin, z, z], axis=1)
    sb = jnp.concatenate([-sin, z, z, z], axis=1)
    return c, sa, sb


def _query_table(cos, sin):
    return jnp.concatenate([cos, cos, -sin, sin], axis=1)


def _row(v):
    return v.astype(F32).reshape(1, -1)


def _pad_last(a, width):
    return jnp.pad(a, [(0, 0)] * (a.ndim - 1) + [(0, width - a.shape[-1])])


def kernel(x_prompt, mem_prompt, x_sample, cache_mla_latent, cache_mla_kpe, state_pool, cache_mem_k, cache_mem_v,
           g_mix, w_in, b_gate, w_pool, pool_scale, g_q_lat, w_qb, g_q_head, g_kv_lat, w_kb, w_vb, g_k_head,
           w_mla_o, g_mem, w_mem_kv, g_mem_q, g_mem_k, w_mem_o, w_out, g_ff, w_up, w_down):
    batch, seq, _ = x_prompt.shape
    dec_batch, dec_seq, _ = x_sample.shape
    past = cache_mla_latent.shape[1]

    w_in_t = w_in.T
    bgate = _row(b_gate)
    half = QK_ROPE // 2

    def swap_halves(a):
        return jnp.concatenate([a[..., half:], a[..., :half]], axis=-1)

    def head_layout(a):
        rope = a[..., QK_NOPE:]
        return jnp.concatenate([a, swap_halves(rope)], axis=-1)

    wq = head_layout(w_qb.reshape(Q_LORA, MLA_HEADS, QK_HEAD)).reshape(Q_LORA, MLA_HEADS * HEAD_PAD).astype(BF16)
    wkbt = w_kb.T.astype(BF16)
    wvbt = w_vb.T.astype(BF16)
    gq_p = _row(head_layout(g_q_head.astype(F32) * MLA_SCALE_LOG2))
    gkn = _row(g_k_head[:QK_NOPE])
    gkr = _row(_pad_last(g_k_head[QK_NOPE:], 128))
    gq_s = _row(head_layout(jnp.concatenate([g_q_head[:QK_NOPE].astype(F32) * g_k_head[:QK_NOPE].astype(F32),
                                             g_q_head[QK_NOPE:].astype(F32)]) * MLA_SCALE_LOG2))
    gmq = _row(g_mem_q.astype(F32) * MEM_SCALE)
    gmix, gql, gkvl, gff, gmem, gmk, pscale = map(_row, (g_mix, g_q_lat, g_kv_lat, g_ff, g_mem, g_mem_k, pool_scale))

    ts = dec_batch * dec_seq
    xs = x_sample.reshape(ts, D_MODEL)
    h_s, u_s, hq_s, ckv_s, kpe_s, qm_s, wa_t, wkpe_t, wmq_t = _in_proj(xs, gmix, w_in_t, gql, gkvl, gmq)
    pos_s = past + jnp.arange(dec_seq, dtype=jnp.int32)
    cos_s, sin_s = _rope_tables(pos_s)
    q_s = _q_only(hq_s, wq, gq_s, jnp.tile(_query_table(cos_s, sin_s), (dec_batch, 1)), ts)
    qa, qr = _q_absorb(q_s, wkbt)
    cos_c, sin_c = _rope_tables(jnp.arange(past, dtype=jnp.int32))
    pad_new = 128 - dec_seq
    nlat = jnp.pad(ckv_s.reshape(dec_batch, dec_seq, KV_LORA), ((0, 0), (0, pad_new), (0, 0)))
    nkpet = jnp.pad(jnp.swapaxes(kpe_s[:, :QK_ROPE].reshape(dec_batch, dec_seq, QK_ROPE), 1, 2),
                    ((0, 0), (0, 0), (0, pad_new)))
    o_lat = _cache_attn(wkbt, qa, qr, cache_mla_latent, jnp.swapaxes(cache_mla_kpe, 1, 2),
                        cos_c.T, sin_c.T, nlat, nkpet,
                        jnp.pad(cos_s.T, ((0, 0), (0, pad_new))), jnp.pad(sin_s.T, ((0, 0), (0, pad_new))),
                        g_k_head[QK_NOPE:].astype(F32).reshape(QK_ROPE, 1),
                        dec_seq, dec_seq, 1024)
    o_s = _v_up(o_lat, wvbt, dec_seq)
    am_s = _mem_attn(qm_s, cache_mem_k.reshape(dec_batch * N_MEM, MEM_W), cache_mem_v.reshape(dec_batch * N_MEM, MEM_W),
                     0, 0, dec_seq, 1)
    left_s = jnp.pad(state_pool.astype(F32), ((0, 0), (POOL_HALO - POOL_STATE, 0), (0, 0)))
    d_s = _pool(u_s, left_s, dec_seq, 1, past)
    merged_s, *wgates, wpool, wmla, wmem_o = _merge(h_s, w_in_t, bgate, d_s, w_pool, pscale, o_s, w_mla_o,
                                                    am_s, w_mem_o, ts, 256, emit_bf16=True)
    x1_s, wout = _out_proj(xs, merged_s, w_out, ts, 512, emit_bf16=True)
    y_s, wup, wdown = _ffn(x1_s, gff, w_up, w_down, ts, 512, emit_bf16=True)
    y_s = y_s.reshape(dec_batch, dec_seq, D_MODEL)
    lat_s = ckv_s.reshape(dec_batch, dec_seq, KV_LORA)
    kpe_s_out = kpe_s[:, :QK_ROPE].reshape(dec_batch, dec_seq, QK_ROPE)
    pool_s = u_s.reshape(dec_batch, dec_seq, POOL_W)[:, dec_seq - POOL_STATE:]

    tm = 512
    xp = x_prompt.reshape(batch * seq, D_MODEL)
    mem_kv = _mem_kv(mem_prompt.reshape(batch * N_MEM, D_MODEL), gmem, w_mem_kv, gmk)
    h, d, hq, ckv, kpe, am, u_tail = _front(xp, gmix, wa_t, wkpe_t, wmq_t, gql, gkvl, gmq,
                                            jnp.zeros((batch, POOL_HALO, POOL_W), F32), mem_kv, tm, seq // tm, 0)
    cos_p, sin_p = _rope_tables(jnp.arange(seq, dtype=jnp.int32))
    q, k, vt = _qkv(hq, ckv, kpe, wq, wkbt, wvbt, gq_p, gkn, gkr,
                    (_query_table(cos_p, sin_p),) + _lane_tables(cos_p, sin_p), tm, seq // tm)
    o = _flash(q, k, vt, batch, seq, 2048)
    merged = _merge(h, wgates, bgate, d, wpool, pscale, o, wmla, am, wmem_o, 2 * tm, POOL_OUT_GW)
    y_p = _ffn(_out_proj(xp, merged, wout, 2 * tm, 1024), gff, wup, wdown, tm, 1024).reshape(batch, seq, D_MODEL)
    lat_p = ckv.reshape(batch, seq, KV_LORA)
    kpe_p = kpe[:, :QK_ROPE].reshape(batch, seq, QK_ROPE)
    pool_p = u_tail[:, POOL_HALO - POOL_STATE:]
    mem_k_p = mem_kv[:, :MEM_W].reshape(batch, N_MEM, MEM_HEADS, MEM_HEAD_DIM)
    mem_v_p = mem_kv[:, MEM_W:].reshape(batch, N_MEM, MEM_HEADS, MEM_HEAD_DIM)

    return (y_p, y_s, lat_p, kpe_p, pool_p, mem_k_p, mem_v_p, lat_s, kpe_s_out, pool_s)
```

```python
import functools

import jax
import jax.numpy as jnp
from jax import lax
from jax.experimental import pallas as pl
from jax.experimental.pallas import tpu as pltpu

F32 = jnp.float32
BF16 = jnp.bfloat16

EPS = 1e-6
CHUNK = 64
D_MODEL = 2048
POOL_WINDOWS = (2, 4, 8, 16)
POOL_W = 1024
POOL_GW = 256
POOL_OUT_GW = 512
POOL_STATE = 15
POOL_HALO = 16
assert all(w & (w - 1) == 0 and w <= POOL_HALO for w in POOL_WINDOWS)
MLA_HEADS = 16
Q_LORA = 512
KV_LORA = 512
QK_NOPE = 128
QK_ROPE = 64
QK_HEAD = QK_NOPE + QK_ROPE
HEAD_PAD = 256
V_HEAD = 128
V_ROWS = 144
ROPE_THETA = 10000.0
LOG2_E = 1.4426950408889634
MLA_SCALE = QK_HEAD ** -0.5
MLA_SCALE_LOG2 = MLA_SCALE * LOG2_E
N_MEM = 256
MEM_HEADS = 4
MEM_HEAD_DIM = 256
MEM_W = MEM_HEADS * MEM_HEAD_DIM
MEM_SCALE = MEM_HEAD_DIM ** -0.5
D_FF = 4 * D_MODEL
NEG_INF = -1e30
FLASH_AHEAD = 2
CACHE_AHEAD = 1
FRONT_ROW_GROUPS = 2
FFN_ROW_GROUPS = 2

Z_U = 0
Z_QLAT = Z_U + POOL_W
Z_KVLAT = Z_QLAT + Q_LORA
Z_MQ = Z_KVLAT + KV_LORA
W_IN_KPE = Z_MQ
W_IN_MQ = W_IN_KPE + QK_ROPE
W_IN_GATE = W_IN_MQ + MEM_W

BF16_SUBLANES = 16
V7X_VMEM_BYTES = 64 * 1024 * 1024
VMEM_CEILING = V7X_VMEM_BYTES - 6 * 1024 * 1024


def _params(semantics, vmem_bytes):
    return pltpu.CompilerParams(dimension_semantics=semantics,
                                vmem_limit_bytes=int(min(vmem_bytes, VMEM_CEILING)))


def _as_bf16(w_ref, copy_ref):
    w = w_ref[...].astype(BF16)
    if copy_ref is not None:
        copy_ref[...] = w
    return w


def _rms(x, g):
    return x * lax.rsqrt(jnp.mean(x * x, axis=-1, keepdims=True) + EPS) * g


def _dot(a, b):
    return jnp.dot(a, b, preferred_element_type=F32)


def _dot_nt(a, b):
    return lax.dot_general(a, b, (((1,), (1,)), ((), ())), preferred_element_type=F32)


def _rope128(x, c, sa, sb):
    half = QK_ROPE // 2
    return x * c + pltpu.roll(x, half, 1) * sa + pltpu.roll(x, x.shape[1] - half, 1) * sb


def _in_proj_kernel(x_ref, gmix_ref, wa_ref, wkpe_ref, wmq_ref, gq_ref, gkv_ref, gmq_ref,
                    h_ref, u_ref, hq_ref, ckv_ref, kpe_ref, qm_ref, wa_copy, wkpe_copy, wmq_copy):
    def rows(ref, copy, lo, hi):
        w = ref[lo:hi, :].astype(BF16)
        copy[lo:hi, :] = w
        return w

    h = _rms(x_ref[...], gmix_ref[...]).astype(BF16)
    h_ref[...] = h
    u_ref[...] = _dot_nt(h, rows(wa_ref, wa_copy, Z_U, Z_QLAT))
    hq_ref[...] = _rms(_dot_nt(h, rows(wa_ref, wa_copy, Z_QLAT, Z_KVLAT)), gq_ref[...]).astype(BF16)
    ckv_ref[...] = _rms(_dot_nt(h, rows(wa_ref, wa_copy, Z_KVLAT, Z_MQ)), gkv_ref[...])
    kpe_ref[:, :QK_ROPE] = _dot_nt(h, rows(wkpe_ref, wkpe_copy, 0, QK_ROPE))
    kpe_ref[:, QK_ROPE:] = jnp.zeros((h.shape[0], 128 - QK_ROPE), F32)
    for hh in range(MEM_HEADS):
        lo = hh * MEM_HEAD_DIM
        m = _dot_nt(h, rows(wmq_ref, wmq_copy, lo, lo + MEM_HEAD_DIM))
        qm_ref[:, lo:lo + MEM_HEAD_DIM] = _rms(m, gmq_ref[...]).astype(BF16)


def _w_in_rows(n, start):
    return pl.BlockSpec((pl.Element(n), pl.Element(D_MODEL)),
                        lambda i: (pl.multiple_of(start + 0 * i, BF16_SUBLANES), 0))


def _front_kernel(x_ref, gmix_ref, wa_ref, wkpe_ref, wmq_ref, gq_ref, gkv_ref, gmq_ref, left_ref, mk_ref, mv_ref,
                  h_ref, d_ref, hq_ref, ckv_ref, kpe_ref, am_ref, utail_ref, xp_sc, qm_sc,
                  *, tm, blocks_per_seq, pos0):
    sb = pl.program_id(0) % blocks_per_seq

    @pl.when(sb == 0)
    def _():
        xp_sc[:POOL_HALO, :] = left_ref[...]

    @pl.when(sb != 0)
    def _():
        xp_sc[:POOL_HALO, :] = xp_sc[tm:, :]

    rows_per = tm // FRONT_ROW_GROUPS
    for r in range(FRONT_ROW_GROUPS):
        rs = slice(r * rows_per, (r + 1) * rows_per)
        h = _rms(x_ref[rs, :], gmix_ref[...]).astype(BF16)
        h_ref[rs, :] = h
        xp_sc[POOL_HALO + r * rows_per:POOL_HALO + (r + 1) * rows_per, :] = _dot_nt(h, wa_ref[Z_U:Z_QLAT, :])
        hq_ref[rs, :] = _rms(_dot_nt(h, wa_ref[Z_QLAT:Z_KVLAT, :]), gq_ref[...]).astype(BF16)
        ckv_ref[rs, :] = _rms(_dot_nt(h, wa_ref[Z_KVLAT:Z_MQ, :]), gkv_ref[...])
        kpe_ref[rs, :QK_ROPE] = _dot_nt(h, wkpe_ref[...])
        kpe_ref[rs, QK_ROPE:] = jnp.zeros((rows_per, 128 - QK_ROPE), F32)
        for hh in range(MEM_HEADS):
            lo = hh * MEM_HEAD_DIM
            m = _dot_nt(h, wmq_ref[lo:lo + MEM_HEAD_DIM, :])
            qm_sc[rs, lo:lo + MEM_HEAD_DIM] = _rms(m, gmq_ref[...]).astype(BF16)
    utail_ref[...] = xp_sc[tm:, :]
    _pool_windows(xp_sc, pos0 + sb * tm, tm, d_ref)
    _mem_attn_kernel(qm_sc, mk_ref, mv_ref, am_ref)


def _front(x, gmix, wa_t, wkpe_t, wmq_t, gq, gkv, gmq_scaled, left, mem_kv, tm, blocks_per_seq, pos0):
    t = x.shape[0]
    n_seq = t // (tm * blocks_per_seq)
    row = lambda w: pl.BlockSpec((tm, w), lambda i: (i, 0))
    full = lambda a: pl.BlockSpec(a.shape, lambda i: (0, 0))
    w_elems = wa_t.size + wkpe_t.size + wmq_t.size
    vmem = 2 * (tm * D_MODEL * 6 + w_elems * 2 + tm * (POOL_W * 2 + Q_LORA * 2 + KV_LORA * 4 + 128 * 4 + MEM_W * 2)
                + 2 * N_MEM * MEM_W * 4 + 2 * POOL_HALO * POOL_W * 4)
    vmem += (tm + POOL_HALO) * POOL_W * 4 + tm * MEM_W * 2 + 8 * tm * POOL_W * 4
    kern = functools.partial(_front_kernel, tm=tm, blocks_per_seq=blocks_per_seq, pos0=pos0)
    return pl.pallas_call(
        kern,
        grid=(t // tm,),
        in_specs=[row(D_MODEL), full(gmix), full(wa_t), full(wkpe_t), full(wmq_t),
                  full(gq), full(gkv), full(gmq_scaled),
                  pl.BlockSpec((None, POOL_HALO, POOL_W), lambda i: (i // blocks_per_seq, 0, 0)),
                  pl.BlockSpec((N_MEM, MEM_W), lambda i: (i // blocks_per_seq, 0)),
                  pl.BlockSpec((N_MEM, MEM_W), lambda i: (i // blocks_per_seq, 1))],
        out_specs=[row(D_MODEL), row(POOL_W), row(Q_LORA), row(KV_LORA), row(128), row(MEM_W),
                   pl.BlockSpec((None, POOL_HALO, POOL_W), lambda i: (i // blocks_per_seq, 0, 0))],
        out_shape=[jax.ShapeDtypeStruct((t, D_MODEL), BF16),
                   jax.ShapeDtypeStruct((t, POOL_W), BF16), jax.ShapeDtypeStruct((t, Q_LORA), BF16),
                   jax.ShapeDtypeStruct((t, KV_LORA), F32), jax.ShapeDtypeStruct((t, 128), F32),
                   jax.ShapeDtypeStruct((t, MEM_W), BF16), jax.ShapeDtypeStruct((n_seq, POOL_HALO, POOL_W), F32)],
        scratch_shapes=[pltpu.VMEM((tm + POOL_HALO, POOL_W), F32), pltpu.VMEM((tm, MEM_W), BF16)],
        compiler_params=_params(("arbitrary",), vmem),
        name="front",
    )(x, gmix, wa_t, wkpe_t, wmq_t, gq, gkv, gmq_scaled, left, mem_kv, mem_kv)


def _in_proj(x, gmix, w_in_t, gq, gkv, gmq_scaled):
    t = x.shape[0]
    row = lambda w: pl.BlockSpec((t, w), lambda i: (0, 0))
    full = lambda a: pl.BlockSpec(a.shape, lambda i: (0, 0))
    pieces = ((Z_MQ, 0), (QK_ROPE, W_IN_KPE), (MEM_W, W_IN_MQ))
    w_elems = sum(n for n, _ in pieces) * D_MODEL
    vmem = t * D_MODEL * 8 + w_elems * 6 + 2 * t * (POOL_W * 4 + Q_LORA * 2 + KV_LORA * 4 + 128 * 4 + MEM_W * 2)
    vmem += w_elems * 2 + 6 * t * POOL_W * 4
    return pl.pallas_call(
        _in_proj_kernel,
        grid=(1,),
        in_specs=[row(D_MODEL), full(gmix)] + [_w_in_rows(n, start) for n, start in pieces]
                 + [full(gq), full(gkv), full(gmq_scaled)],
        out_specs=[row(D_MODEL), row(POOL_W), row(Q_LORA), row(KV_LORA), row(128), row(MEM_W)]
                  + [pl.BlockSpec((n, D_MODEL), lambda i: (0, 0)) for n, _ in pieces],
        out_shape=[jax.ShapeDtypeStruct((t, D_MODEL), BF16),
                   jax.ShapeDtypeStruct((t, POOL_W), F32), jax.ShapeDtypeStruct((t, Q_LORA), BF16),
                   jax.ShapeDtypeStruct((t, KV_LORA), F32), jax.ShapeDtypeStruct((t, 128), F32),
                   jax.ShapeDtypeStruct((t, MEM_W), BF16)]
                  + [jax.ShapeDtypeStruct((n, D_MODEL), BF16) for n, _ in pieces],
        compiler_params=_params(("arbitrary",), vmem),
        name="in_proj",
    )(x, gmix, w_in_t, w_in_t, w_in_t, gq, gkv, gmq_scaled)


def _head_store(ref, head_major):
    def store(h, nope, rope):
        if head_major:
            ref[h, :, :QK_NOPE] = nope
            ref[h, :, QK_NOPE:] = rope
        else:
            lo = h * HEAD_PAD
            ref[:, lo:lo + QK_NOPE] = nope
            ref[:, lo + QK_NOPE:lo + HEAD_PAD] = rope
    return store


def _q_heads(hq, wq_ref, gq_ref, qtab, store):
    first = lax.broadcasted_iota(jnp.int32, (1, 128), 1) < QK_ROPE
    for h in range(MLA_HEADS):
        lo = h * HEAD_PAD
        qh = _dot(hq, wq_ref[:, lo:lo + HEAD_PAD])
        qn, qr = qh[:, :QK_NOPE], qh[:, QK_NOPE:]
        ss = jnp.sum(qn * qn + jnp.where(first, qr * qr, 0.0), axis=-1, keepdims=True)
        r = lax.rsqrt(ss * (1.0 / QK_HEAD) + EPS)
        store(h, (qn * r * gq_ref[:, :QK_NOPE]).astype(BF16), (qr * r * gq_ref[:, QK_NOPE:] * qtab).astype(BF16))


def _qkv_kernel(hq_ref, ckv_ref, kpe_ref, wq_ref, wkbt_ref, wvbt_ref, gq_ref, gkn_ref, gkr_ref,
                qtab_ref, c_ref, sa_ref, sb_ref, q_ref, k_ref, vt_ref):
    _q_heads(hq_ref[...], wq_ref, gq_ref, qtab_ref[...], _head_store(q_ref, True))
    store_k = _head_store(k_ref, True)
    ckv = ckv_ref[...].astype(BF16)
    kpe = kpe_ref[...]
    ss_pe = jnp.sum(kpe * kpe, axis=-1, keepdims=True)
    kr = _rope128(kpe * gkr_ref[...], c_ref[...], sa_ref[...], sb_ref[...])
    kr = kr + pltpu.roll(kr, QK_ROPE, 1)
    for p in range(MLA_HEADS // 2):
        kn2 = _dot_nt(ckv, wkbt_ref[p * 2 * QK_NOPE:(p + 1) * 2 * QK_NOPE, :])
        for s in range(2):
            kn = kn2[:, s * QK_NOPE:(s + 1) * QK_NOPE]
            r = lax.rsqrt((jnp.sum(kn * kn, axis=-1, keepdims=True) + ss_pe) * (1.0 / QK_HEAD) + EPS)
            store_k(2 * p + s, (kn * r * gkn_ref[...]).astype(BF16), (kr * r).astype(BF16))
    vt = _dot_nt(wvbt_ref[...], ckv)
    tm = vt.shape[1]
    ones_row = (lax.broadcasted_iota(jnp.int32, (V_ROWS - V_HEAD, tm), 0) == 0).astype(BF16)
    for h in range(MLA_HEADS):
        vt_ref[h, :V_HEAD, :] = vt[h * V_HEAD:(h + 1) * V_HEAD].astype(BF16)
        vt_ref[h, V_HEAD:, :] = ones_row


def _qkv(hq, ckv, kpe, wq, wkbt, wvbt, gq, gkn, gkr, tabs, tm, seq_blocks):
    t = hq.shape[0]
    n_seq = t // (tm * seq_blocks)
    qk_spec = pl.BlockSpec((None, MLA_HEADS, tm, HEAD_PAD), lambda i: (i // seq_blocks, 0, i % seq_blocks, 0))
    qk_shape = jax.ShapeDtypeStruct((n_seq, MLA_HEADS, tm * seq_blocks, HEAD_PAD), BF16)
    row = lambda w: pl.BlockSpec((tm, w), lambda i: (i, 0))
    full = lambda a: pl.BlockSpec(a.shape, lambda i: (0, 0))
    tab = pl.BlockSpec((tm, 128), lambda i: (i % seq_blocks, 0))
    vmem = 2 * (tm * (Q_LORA * 2 + KV_LORA * 4 + 128 * 4 + 4 * 128 * 4) + (wq.size + wkbt.size + wvbt.size) * 2
                + tm * (2 * MLA_HEADS * HEAD_PAD * 2 + MLA_HEADS * V_ROWS * 2))
    vmem += 8 * tm * HEAD_PAD * 4 + 2 * tm * MLA_HEADS * V_HEAD * 4
    return pl.pallas_call(
        _qkv_kernel,
        grid=(t // tm,),
        in_specs=[row(Q_LORA), row(KV_LORA), row(128), full(wq), full(wkbt), full(wvbt),
                  full(gq), full(gkn), full(gkr), tab, tab, tab, tab],
        out_specs=[qk_spec, qk_spec,
                   pl.BlockSpec((None, MLA_HEADS, None, V_ROWS, tm),
                                lambda i: (i // seq_blocks, 0, i % seq_blocks, 0, 0))],
        out_shape=[qk_shape, qk_shape,
                   jax.ShapeDtypeStruct((n_seq, MLA_HEADS, seq_blocks, V_ROWS, tm), BF16)],
        compiler_params=_params(("parallel",), vmem),
        name="qkv_proj",
    )(hq, ckv, kpe, wq, wkbt, wvbt, gq, gkn, gkr, *tabs)


def _q_only_kernel(hq_ref, wq_ref, gq_ref, qtab_ref, q_ref):
    _q_heads(hq_ref[...], wq_ref, gq_ref, qtab_ref[...], _head_store(q_ref, False))


def _q_only(hq, wq, gq, qtab, tm):
    t = hq.shape[0]
    row = lambda w: pl.BlockSpec((tm, w), lambda i: (i, 0))
    full = lambda a: pl.BlockSpec(a.shape, lambda i: (0, 0))
    vmem = 2 * (tm * (Q_LORA * 2 + 128 * 4 + MLA_HEADS * HEAD_PAD * 2) + wq.size * 2) + 8 * tm * HEAD_PAD * 4
    return pl.pallas_call(
        _q_only_kernel,
        grid=(t // tm,),
        in_specs=[row(Q_LORA), full(wq), full(gq), pl.BlockSpec((tm, 128), lambda i: (0, 0))],
        out_specs=row(MLA_HEADS * HEAD_PAD),
        out_shape=jax.ShapeDtypeStruct((t, MLA_HEADS * HEAD_PAD), BF16),
        compiler_params=_params(("parallel",), vmem),
        name="q_proj",
    )(hq, wq, gq, qtab)


def _flash_kernel(q_ref, k_ref, vt_ref, o_ref, *scratch, tq, tk, nq):
    *bufs, m_sc, acc_sc = scratch
    qi = pl.program_id(2)
    n_diag = tq // tk

    def scores(kj, s_sc, q_lo=0):
        s_sc[:, q_lo:] = _dot_nt(k_ref[kj * tk:(kj + 1) * tk, :], q_ref[q_lo:, :])

    def consume(kj, s_sc, key_chunk0=None, q_lo=0):
        s = s_sc[:, q_lo:]
        if key_chunk0 is not None:
            kc = key_chunk0 + lax.broadcasted_iota(jnp.int32, s.shape, 0) // CHUNK
            qc = (q_lo + lax.broadcasted_iota(jnp.int32, s.shape, 1)) // CHUNK
            s = jnp.where(qc >= kc, s, NEG_INF)
        m = m_sc[:, q_lo:]
        m_new = jnp.maximum(m, jnp.max(s, axis=0, keepdims=True))
        alpha = jnp.exp2(m - m_new)
        p = jnp.exp2(s - m_new).astype(BF16)
        acc_sc[:, q_lo:] = alpha * acc_sc[:, q_lo:] + _dot(vt_ref[kj], p)
        m_sc[:, q_lo:] = m_new

    def run(q_block):
        tiles = [(kj, 0, None) for kj in range(q_block * n_diag)]
        tiles += [(q_block * n_diag + d, d * tk, d * tk // CHUNK) for d in range(n_diag)]
        m_sc[...] = jnp.full(m_sc.shape, NEG_INF, F32)
        acc_sc[...] = jnp.zeros(acc_sc.shape, F32)
        for i in range(min(FLASH_AHEAD, len(tiles))):
            scores(tiles[i][0], bufs[i % len(bufs)], tiles[i][1])
        for i, (kj, q_lo, key_chunk0) in enumerate(tiles):
            if i + FLASH_AHEAD < len(tiles):
                nkj, nq_lo, _ = tiles[i + FLASH_AHEAD]
                scores(nkj, bufs[(i + FLASH_AHEAD) % len(bufs)], nq_lo)
            consume(kj, bufs[i % len(bufs)], key_chunk0, q_lo)
        acc = acc_sc[...]
        o_ref[...] = (acc[:V_HEAD] / acc[V_HEAD:V_HEAD + 1]).T.astype(BF16)

    for q_block in range(nq):
        pl.when(qi == q_block)(functools.partial(run, q_block))


def _flash(q, k, vt, batch, seq, tq):
    tk = vt.shape[-1]
    assert tq % tk == 0 and tk % CHUNK == 0 and seq % tq == 0
    nq = seq // tq
    n_bufs = FLASH_AHEAD + 1
    vmem = 2 * (tq * HEAD_PAD * 2 + seq * HEAD_PAD * 2 + seq * V_ROWS * 2 + tq * V_HEAD * 2)
    vmem += n_bufs * tk * tq * 4 + V_ROWS * tq * 4 + 4 * tk * tq * 4
    return pl.pallas_call(
        functools.partial(_flash_kernel, tq=tq, tk=tk, nq=nq),
        grid=(batch, MLA_HEADS, nq),
        in_specs=[pl.BlockSpec((None, None, tq, HEAD_PAD), lambda b, h, i: (b, h, i, 0)),
                  pl.BlockSpec((None, None, seq, HEAD_PAD), lambda b, h, i: (b, h, 0, 0)),
                  pl.BlockSpec((None, None, seq // tk, V_ROWS, tk), lambda b, h, i: (b, h, 0, 0, 0))],
        out_specs=pl.BlockSpec((tq, V_HEAD), lambda b, h, i: (b * nq + i, h)),
        out_shape=jax.ShapeDtypeStruct((batch * seq, MLA_HEADS * V_HEAD), BF16),
        scratch_shapes=[pltpu.VMEM((tk, tq), F32)] * n_bufs + [pltpu.VMEM((1, tq), F32),
                                                                pltpu.VMEM((V_ROWS, tq), F32)],
        compiler_params=_params(("parallel", "parallel", "arbitrary"), vmem),
        name="mla_flash",
    )(q, k, vt)


def _q_absorb_kernel(q_ref, wkbt_ref, qa_ref, qr_ref):
    qa_ref[...] = _dot(q_ref[:, :QK_NOPE], wkbt_ref[...]).astype(BF16)
    qr_ref[...] = q_ref[:, QK_NOPE:]


def _q_absorb(q, wkbt):
    t = q.shape[0]
    vmem = 2 * (t * HEAD_PAD * 2 + KV_LORA * QK_NOPE * 2 + t * KV_LORA * 2 + t * 128 * 2) + 2 * t * KV_LORA * 4
    return pl.pallas_call(
        _q_absorb_kernel,
        grid=(MLA_HEADS,),
        in_specs=[pl.BlockSpec((t, HEAD_PAD), lambda h: (0, h)),
                  pl.BlockSpec((QK_NOPE, KV_LORA), lambda h: (h, 0))],
        out_specs=[pl.BlockSpec((None, t, KV_LORA), lambda h: (h, 0, 0)),
                   pl.BlockSpec((None, t, 128), lambda h: (h, 0, 0))],
        out_shape=[jax.ShapeDtypeStruct((MLA_HEADS, t, KV_LORA), BF16),
                   jax.ShapeDtypeStruct((MLA_HEADS, t, 128), BF16)],
        compiler_params=_params(("parallel",), vmem),
        name="q_absorb",
    )(q, wkbt)


def _cache_attn_kernel(wkbt_ref, qa_ref, qr_ref, lat_ref, kpet_ref, ct_ref, st_ref,
                       nlat_ref, nkpet_ref, nct_ref, nst_ref, gkr_ref,
                       o_ref, *scratch, nq, n_new, past, tk):
    *bufs, m_sc, l_sc, acc_sc = scratch
    rows = MLA_HEADS * nq
    half = QK_ROPE // 2
    m_sc[...] = jnp.full(m_sc.shape, NEG_INF, F32)
    l_sc[...] = jnp.zeros(l_sc.shape, F32)
    acc_sc[...] = jnp.zeros(acc_sc.shape, F32)

    def scores(lat, kpet, ct, st, width, s_sc):
        c = lat.astype(BF16)
        kt = _dot_nt(wkbt_ref[...], c)
        s = _dot_nt(qa_ref[...].reshape(rows, KV_LORA), c)
        ss_pe = jnp.sum(kpet * kpet, axis=0, keepdims=True)
        kr = kpet * gkr_ref[...]
        k1, k2 = kr[:half], kr[half:]
        o1, o2 = k1 * ct - k2 * st, k2 * ct + k1 * st
        krot = jnp.concatenate([o1, o2, o1, o2], axis=0).astype(BF16)
        s = s + _dot(qr_ref[...].reshape(rows, 128), krot)
        for h in range(MLA_HEADS):
            kh = kt[h * QK_NOPE:(h + 1) * QK_NOPE]
            r = lax.rsqrt((jnp.sum(kh * kh, axis=0, keepdims=True) + ss_pe) * (1.0 / QK_HEAD) + EPS)
            s_sc[h * nq:(h + 1) * nq, :width] = s[h * nq:(h + 1) * nq] * r

    def consume(lat, width, s_sc, visible):
        s = s_sc[:, :width]
        if visible is not None:
            s = jnp.where(visible, s, NEG_INF)
        m = m_sc[...]
        m_new = jnp.maximum(m, jnp.max(s, axis=-1, keepdims=True))
        alpha = jnp.exp2(m - m_new)
        p = jnp.exp2(s - m_new)
        l_sc[...] = alpha * l_sc[...] + jnp.sum(p, axis=-1, keepdims=True)
        m_sc[...] = m_new
        acc_sc[...] = alpha * acc_sc[...] + _dot(p.astype(BF16), lat.astype(BF16))

    kk = lax.broadcasted_iota(jnp.int32, (1, 128), 1)
    new_visible = kk < n_new
    if (past + n_new - 1) // CHUNK != past // CHUNK:
        qpos = past + lax.broadcasted_iota(jnp.int32, (rows, 1), 0) % nq
        new_visible = new_visible & (qpos // CHUNK >= (past + kk) // CHUNK)
    blocks = []
    for j in range(past // tk):
        ks = slice(j * tk, (j + 1) * tk)
        blocks.append((lambda ks=ks: (lat_ref[ks, :], kpet_ref[:, ks], ct_ref[:, ks], st_ref[:, ks]), tk, None))
    blocks.append((lambda: (nlat_ref[...], nkpet_ref[...], nct_ref[...], nst_ref[...]), 128, new_visible))

    def issue(i):
        get, width, _ = blocks[i]
        scores(*get(), width, bufs[i % len(bufs)])

    for i in range(min(CACHE_AHEAD, len(blocks))):
        issue(i)
    for i, (get, width, visible) in enumerate(blocks):
        if i + CACHE_AHEAD < len(blocks):
            issue(i + CACHE_AHEAD)
        consume(get()[0], width, bufs[i % len(bufs)], visible)
    o_ref[...] = (acc_sc[...] / l_sc[...]).astype(BF16)


def _cache_attn(wkbt, qa, qr, lat, kpet, ct, st, nlat, nkpet, nct, nst, gkr, nq, n_new, tk):
    batch, past, _ = lat.shape
    assert past % tk == 0
    rows = MLA_HEADS * nq
    n_bufs = CACHE_AHEAD + 1
    kern = functools.partial(_cache_attn_kernel, nq=nq, n_new=n_new, past=past, tk=tk)
    vmem = 2 * (wkbt.size * 2 + rows * (KV_LORA + 128) * 2 + past * KV_LORA * 4 + 3 * 64 * past * 4 + rows * KV_LORA * 2)
    vmem += (MLA_HEADS * QK_NOPE + 4 * rows) * tk * 4 + tk * KV_LORA * 2 + rows * KV_LORA * 8 + n_bufs * rows * tk * 4
    whole = lambda a: pl.BlockSpec(a.shape, lambda b: (0,) * a.ndim)
    per_batch = lambda a: pl.BlockSpec((None,) + a.shape[1:], lambda b: (b,) + (0,) * (a.ndim - 1))
    return pl.pallas_call(
        kern,
        grid=(batch,),
        in_specs=[whole(wkbt),
                  pl.BlockSpec((MLA_HEADS, nq, KV_LORA), lambda b: (0, b, 0)),
                  pl.BlockSpec((MLA_HEADS, nq, 128), lambda b: (0, b, 0)),
                  per_batch(lat), per_batch(kpet), whole(ct), whole(st),
                  per_batch(nlat), per_batch(nkpet), whole(nct), whole(nst), whole(gkr)],
        out_specs=pl.BlockSpec((None, rows, KV_LORA), lambda b: (b, 0, 0)),
        out_shape=jax.ShapeDtypeStruct((batch, rows, KV_LORA), BF16),
        scratch_shapes=[pltpu.VMEM((rows, tk), F32)] * n_bufs
                       + [pltpu.VMEM((rows, 1), F32), pltpu.VMEM((rows, 1), F32), pltpu.VMEM((rows, KV_LORA), F32)],
        compiler_params=_params(("parallel",), vmem),
        name="mla_cache_attn",
    )(wkbt, qa, qr, lat, kpet, ct, st, nlat, nkpet, nct, nst, gkr)


def _v_up_kernel(ol_ref, wvbt_ref, o_ref):
    b, nq, _ = ol_ref.shape
    o_ref[...] = _dot_nt(ol_ref[...].reshape(b * nq, KV_LORA), wvbt_ref[...]).astype(BF16)


def _v_up(o_lat, wvbt, nq):
    batch = o_lat.shape[0]
    t = batch * nq
    vmem = 2 * (t * KV_LORA * 2 + KV_LORA * V_HEAD * 2 + t * V_HEAD * 2) + 2 * t * V_HEAD * 4
    return pl.pallas_call(
        _v_up_kernel,
        grid=(MLA_HEADS,),
        in_specs=[pl.BlockSpec((batch, nq, KV_LORA), lambda h: (0, h, 0)),
                  pl.BlockSpec((V_HEAD, KV_LORA), lambda h: (h, 0))],
        out_specs=pl.BlockSpec((t, V_HEAD), lambda h: (0, h)),
        out_shape=jax.ShapeDtypeStruct((t, MLA_HEADS * V_HEAD), BF16),
        compiler_params=_params(("parallel",), vmem),
        name="v_up",
    )(o_lat, wvbt)


def _mem_kv_kernel(mem_ref, gmem_ref, w_ref, gk_ref, kv_ref):
    j = pl.program_id(0)
    h = _rms(mem_ref[...], gmem_ref[...]).astype(BF16)

    @pl.when(j == 0)
    def _():
        for hh in range(MEM_HEADS):
            lo = hh * MEM_HEAD_DIM
            w = w_ref[:, lo:lo + MEM_HEAD_DIM].astype(BF16)
            kv_ref[:, lo:lo + MEM_HEAD_DIM] = _rms(_dot(h, w), gk_ref[...])

    @pl.when(j == 1)
    def _():
        kv_ref[...] = _dot(h, w_ref[...].astype(BF16))


def _mem_kv(mem, gmem, w, gk):
    t = mem.shape[0]
    vmem = 2 * (t * D_MODEL * 4 + D_MODEL * MEM_W * w.dtype.itemsize + t * MEM_W * 4) + t * D_MODEL * 6
    vmem += t * MEM_W * 4 + D_MODEL * MEM_W * 2
    return pl.pallas_call(
        _mem_kv_kernel,
        grid=(2,),
        in_specs=[pl.BlockSpec((t, D_MODEL), lambda j: (0, 0)), pl.BlockSpec((1, D_MODEL), lambda j: (0, 0)),
                  pl.BlockSpec((D_MODEL, MEM_W), lambda j: (0, j)), pl.BlockSpec((1, MEM_HEAD_DIM), lambda j: (0, 0))],
        out_specs=pl.BlockSpec((t, MEM_W), lambda j: (0, j)),
        out_shape=jax.ShapeDtypeStruct((t, 2 * MEM_W), F32),
        compiler_params=_params(("parallel",), vmem),
        name="memory_kv",
    )(mem, gmem, w, gk)


def _mem_attn_kernel(qm_ref, k_ref, v_ref, o_ref):
    for hh in range(MEM_HEADS):
        sl = slice(hh * MEM_HEAD_DIM, (hh + 1) * MEM_HEAD_DIM)
        s = _dot_nt(qm_ref[:, sl], k_ref[:, sl].astype(BF16))
        p = jnp.exp(s - jnp.max(s, axis=-1, keepdims=True))
        l = jnp.sum(p, axis=-1, keepdims=True)
        o_ref[:, sl] = (_dot(p.astype(BF16), v_ref[:, sl].astype(BF16)) / l).astype(BF16)


def _mem_attn(qm, k_arr, v_arr, k_col, v_col, tm, blocks_per_batch):
    t = qm.shape[0]
    vmem = 2 * (2 * tm * MEM_W * 2 + 2 * N_MEM * MEM_W * 4) + 6 * tm * N_MEM * 4 + 2 * N_MEM * MEM_W * 2
    return pl.pallas_call(
        _mem_attn_kernel,
        grid=(t // tm,),
        in_specs=[pl.BlockSpec((tm, MEM_W), lambda i: (i, 0)),
                  pl.BlockSpec((N_MEM, MEM_W), lambda i: (i // blocks_per_batch, k_col)),
                  pl.BlockSpec((N_MEM, MEM_W), lambda i: (i // blocks_per_batch, v_col))],
        out_specs=pl.BlockSpec((tm, MEM_W), lambda i: (i, 0)),
        out_shape=jax.ShapeDtypeStruct((t, MEM_W), BF16),
        compiler_params=_params(("parallel",), vmem),
        name="memory_attn",
    )(qm, k_arr, v_arr)


def _pool_windows(xp_sc, first_pos, tm, d_ref):
    pos = first_pos + lax.broadcasted_iota(jnp.int32, (tm, 1), 0)
    for g, win in enumerate(POOL_WINDOWS):
        sl = slice(g * POOL_GW, (g + 1) * POOL_GW)
        x = xp_sc[:, sl]
        wsum, k = x, 1
        while k < win:
            wsum = wsum + pltpu.roll(wsum, k, 0)
            k *= 2
        cnt = jnp.minimum(pos + 1, win).astype(F32)
        d_ref[:, sl] = (wsum[POOL_HALO:] / cnt - x[POOL_HALO:]).astype(BF16)


def _pool_kernel(u_ref, prev_ref, left_ref, d_ref, xp_sc, *, tm, blocks_per_seq, pos0):
    sb = pl.program_id(0) % blocks_per_seq
    xp_sc[:POOL_HALO, :] = jnp.where(sb == 0, left_ref[...], prev_ref[...])
    xp_sc[POOL_HALO:, :] = u_ref[...]
    _pool_windows(xp_sc, pos0 + sb * tm, tm, d_ref)


def _pool(u, left, tm, blocks_per_seq, pos0):
    t = u.shape[0]
    halo_per_blk = tm // POOL_HALO
    kern = functools.partial(_pool_kernel, tm=tm, blocks_per_seq=blocks_per_seq, pos0=pos0)
    vmem = 2 * (tm * POOL_W * 4 + 2 * POOL_HALO * POOL_W * 4 + tm * POOL_W * 2) + (tm + POOL_HALO) * POOL_W * 4
    vmem += 6 * tm * POOL_GW * 4
    return pl.pallas_call(
        kern,
        grid=(t // tm,),
        in_specs=[pl.BlockSpec((tm, POOL_W), lambda i: (i, 0)),
                  pl.BlockSpec((POOL_HALO, POOL_W), lambda i: (jnp.maximum(i * halo_per_blk - 1, 0), 0)),
                  pl.BlockSpec((None, POOL_HALO, POOL_W), lambda i: (i // blocks_per_seq, 0, 0))],
        out_specs=pl.BlockSpec((tm, POOL_W), lambda i: (i, 0)),
        out_shape=jax.ShapeDtypeStruct((t, POOL_W), BF16),
        scratch_shapes=[pltpu.VMEM((tm + POOL_HALO, POOL_W), F32)],
        compiler_params=_params(("parallel",), vmem),
        name="pool_windows",
    )(u, u, left)


def _merge_kernel(h_ref, wg0_ref, wg1_ref, wg2_ref, bg0_ref, bg1_ref, bg2_ref,
                  d_ref, wpool_ref, pscale_ref, o_ref, wmla_ref, am_ref, wmem_ref,
                  out_ref, *w_copies):
    wg_copies, (wpool_copy, wmla_copy, wmem_copy) = (w_copies[:3], w_copies[3:]) if w_copies else ((None,) * 3,) * 2
    h = h_ref[...]
    y_pool = _dot(d_ref[...], _as_bf16(wpool_ref, wpool_copy)) * pscale_ref[...]
    y_mla = _dot(o_ref[...], _as_bf16(wmla_ref, wmla_copy))
    y_mem = _dot(am_ref[...], _as_bf16(wmem_ref, wmem_copy))
    gate = lambda w_ref, copy, b_ref: jax.nn.sigmoid(_dot_nt(h, _as_bf16(w_ref, copy)) + b_ref[...])
    merged = gate(wg0_ref, wg_copies[0], bg0_ref) * y_pool
    merged += gate(wg1_ref, wg_copies[1], bg1_ref) * y_mla
    merged += gate(wg2_ref, wg_copies[2], bg2_ref) * y_mem
    out_ref[...] = merged.astype(BF16)


def _merge(h, wgates, bgate, d, wpool, pscale, o, wmla, am, wmem, tm, tn, emit_bf16=False):
    t = h.shape[0]
    assert not emit_bf16 or t == tm
    assert POOL_OUT_GW % tn == 0
    nj = D_MODEL // tn
    per_group = POOL_OUT_GW // tn
    rowfull = lambda w: pl.BlockSpec((tm, w), lambda i, j: (i, 0))
    gate_rows = pl.BlockSpec((tn, D_MODEL), lambda i, j: (j, 0))
    if emit_bf16:
        gate_in = [pl.BlockSpec((pl.Element(tn), pl.Element(D_MODEL)),
                                lambda i, j, b=b: (pl.multiple_of(W_IN_GATE + (b * nj + j) * tn, BF16_SUBLANES), 0))
                   for b in range(3)]
        gate_args = (wgates,) * 3
    else:
        gate_in = [gate_rows] * 3
        gate_args = tuple(wgates)
    gate_b = lambda b: pl.BlockSpec((1, tn), lambda i, j: (0, b * nj + j))
    col = lambda k: pl.BlockSpec((k, tn), lambda i, j: (0, j))
    pool_spec = pl.BlockSpec((None, POOL_GW, tn), lambda i, j: (j // per_group, 0, j % per_group))
    w_bytes = wmla.dtype.itemsize + 2 * emit_bf16
    w_rows = 3 * D_MODEL + POOL_GW + D_MODEL + MEM_W
    vmem = 2 * (tm * D_MODEL * 2 + w_rows * tn * w_bytes + tm * POOL_GW * 2 + tm * D_MODEL * 2
                + tm * MEM_W * 2 + tm * tn * 2)
    vmem += 8 * tm * tn * 4 + w_rows * tn * 2 * emit_bf16
    y_spec = pl.BlockSpec((tm, tn), lambda i, j: (i, j))
    y_shape = jax.ShapeDtypeStruct((t, D_MODEL), BF16)
    copies = ([jax.ShapeDtypeStruct((D_MODEL, D_MODEL), BF16)] * 3
              + [jax.ShapeDtypeStruct(w.shape, BF16) for w in (wpool, wmla, wmem)])
    return pl.pallas_call(
        _merge_kernel,
        grid=(t // tm, nj),
        in_specs=[rowfull(D_MODEL)] + gate_in
                 + [gate_b(0), gate_b(1), gate_b(2),
                    pl.BlockSpec((tm, POOL_GW), lambda i, j: (i, j // per_group)), pool_spec,
                    pl.BlockSpec((1, tn), lambda i, j: (0, j)),
                    rowfull(D_MODEL), col(D_MODEL), rowfull(MEM_W), col(MEM_W)],
        out_specs=[y_spec] + [gate_rows] * 3 + [pool_spec, col(D_MODEL), col(MEM_W)] if emit_bf16 else y_spec,
        out_shape=[y_shape] + copies if emit_bf16 else y_shape,
        compiler_params=_params(("parallel", "arbitrary" if emit_bf16 else "parallel"), vmem),
        name="gated_merge",
    )(h, *gate_args, bgate, bgate, bgate, d, wpool, pscale, o, wmla, am, wmem)


def _out_proj_kernel(x_ref, m_ref, w_ref, y_ref, *w_copy):
    y_ref[...] = x_ref[...] + _dot(m_ref[...], _as_bf16(w_ref, *w_copy or (None,)))


def _out_proj(x, merged, w, tm, tn, emit_bf16=False):
    t = x.shape[0]
    assert not emit_bf16 or t == tm
    w_spec = pl.BlockSpec((D_MODEL, tn), lambda i, j: (0, j))
    vmem = 2 * (2 * tm * tn * 4 + tm * D_MODEL * 2 + D_MODEL * tn * (w.dtype.itemsize + 2 * emit_bf16)) + 2 * tm * tn * 4
    y_spec = pl.BlockSpec((tm, tn), lambda i, j: (i, j))
    y_shape = jax.ShapeDtypeStruct((t, D_MODEL), F32)
    return pl.pallas_call(
        _out_proj_kernel,
        grid=(t // tm, D_MODEL // tn),
        in_specs=[pl.BlockSpec((tm, tn), lambda i, j: (i, j)),
                  pl.BlockSpec((tm, D_MODEL), lambda i, j: (i, 0)), w_spec],
        out_specs=[y_spec, w_spec] if emit_bf16 else y_spec,
        out_shape=[y_shape, jax.ShapeDtypeStruct(w.shape, BF16)] if emit_bf16 else y_shape,
        compiler_params=_params(("parallel", "arbitrary" if emit_bf16 else "parallel"), vmem),
        name="out_proj",
    )(x, merged, w)


def _ffn_kernel(x_ref, g_ref, wup_ref, wdown_ref, y_ref, *rest):
    *w_copies, h_sc = rest
    wup_copy, wdown_copy = w_copies or (None, None)
    j = pl.program_id(1)
    rows_per = x_ref.shape[0] // FFN_ROW_GROUPS

    def step(first):
        wu, wd = _as_bf16(wup_ref, wup_copy), _as_bf16(wdown_ref, wdown_copy)
        for r in range(FFN_ROW_GROUPS):
            rs = slice(r * rows_per, (r + 1) * rows_per)
            if first:
                x = x_ref[rs, :]
                h = _rms(x, g_ref[...]).astype(BF16)
                h_sc[rs, :] = h
            else:
                h = h_sc[rs, :]
            f = jnp.maximum(_dot(h, wu), 0.0)
            upd = _dot((f * f).astype(BF16), wd)
            y_ref[rs, :] = (x if first else y_ref[rs, :]) + upd

    pl.when(j == 0)(functools.partial(step, True))
    pl.when(j > 0)(functools.partial(step, False))


def _ffn(x, g, wup, wdown, tm, tf, emit_bf16=False):
    t = x.shape[0]
    assert not emit_bf16 or t == tm
    w_bytes = wup.dtype.itemsize + 2 * emit_bf16
    vmem = 2 * (2 * tm * D_MODEL * 4 + 2 * D_MODEL * tf * w_bytes) + tm * D_MODEL * 2 + 3 * tm * tf * 4 + tm * D_MODEL * 4
    vmem += 2 * D_MODEL * tf * 2 * emit_bf16
    up_spec = pl.BlockSpec((D_MODEL, tf), lambda i, j: (0, j))
    down_spec = pl.BlockSpec((tf, D_MODEL), lambda i, j: (j, 0))
    y_spec = pl.BlockSpec((tm, D_MODEL), lambda i, j: (i, 0))
    y_shape = jax.ShapeDtypeStruct((t, D_MODEL), F32)
    return pl.pallas_call(
        _ffn_kernel,
        grid=(t // tm, D_FF // tf),
        in_specs=[pl.BlockSpec((tm, D_MODEL), lambda i, j: (i, 0)),
                  pl.BlockSpec((1, D_MODEL), lambda i, j: (0, 0)), up_spec, down_spec],
        out_specs=[y_spec, up_spec, down_spec] if emit_bf16 else y_spec,
        out_shape=([y_shape, jax.ShapeDtypeStruct(wup.shape, BF16), jax.ShapeDtypeStruct(wdown.shape, BF16)]
                   if emit_bf16 else y_shape),
        scratch_shapes=[pltpu.VMEM((tm, D_MODEL), BF16)],
        compiler_params=_params(("parallel", "arbitrary"), vmem),
        name="ffn",
    )(x, g, wup, wdown)


def _rope_tables(pos):
    half = QK_ROPE // 2
    inv = 1.0 / (ROPE_THETA ** (jnp.arange(half, dtype=F32) * (2.0 / QK_ROPE)))
    ang = pos.astype(F32)[:, None] * inv[None, :]
    return jnp.cos(ang), jnp.sin(ang)


def _lane_tables(cos, sin):
    z = jnp.zeros_like(cos)
    c = jnp.concatenate([cos, cos, z, z], axis=1)
    sa = jnp.concatenate([z, sin, z, z], axis=1)
    sb = jnp.concatenate([-sin, z, z, z], axis=1)
    return c, sa, sb


def _query_table(cos, sin):
    return jnp.concatenate([cos, cos, -sin, sin], axis=1)


def _row(v):
    return v.astype(F32).reshape(1, -1)


def _pad_last(a, width):
    return jnp.pad(a, [(0, 0)] * (a.ndim - 1) + [(0, width - a.shape[-1])])


def kernel(x_prompt, mem_prompt, x_sample, cache_mla_latent, cache_mla_kpe, state_pool, cache_mem_k, cache_mem_v,
           g_mix, w_in, b_gate, w_pool, pool_scale, g_q_lat, w_qb, g_q_head, g_kv_lat, w_kb, w_vb, g_k_head,
           w_mla_o, g_mem, w_mem_kv, g_mem_q, g_mem_k, w_mem_o, w_out, g_ff, w_up, w_down):
    batch, seq, _ = x_prompt.shape
    dec_batch, dec_seq, _ = x_sample.shape
    past = cache_mla_latent.shape[1]

    w_in_t = w_in.T
    bgate = _row(b_gate)
    half = QK_ROPE // 2

    def swap_halves(a):
        return jnp.concatenate([a[..., half:], a[..., :half]], axis=-1)

    def head_layout(a):
        rope = a[..., QK_NOPE:]
        return jnp.concatenate([a, swap_halves(rope)], axis=-1)

    wq = head_layout(w_qb.reshape(Q_LORA, MLA_HEADS, QK_HEAD)).reshape(Q_LORA, MLA_HEADS * HEAD_PAD).astype(BF16)
    wkbt = w_kb.T.astype(BF16)
    wvbt = w_vb.T.astype(BF16)
    gq_p = _row(head_layout(g_q_head.astype(F32) * MLA_SCALE_LOG2))
    gkn = _row(g_k_head[:QK_NOPE])
    gkr = _row(_pad_last(g_k_head[QK_NOPE:], 128))
    gq_s = _row(head_layout(jnp.concatenate([g_q_head[:QK_NOPE].astype(F32) * g_k_head[:QK_NOPE].astype(F32),
                                             g_q_head[QK_NOPE:].astype(F32)]) * MLA_SCALE_LOG2))
    gmq = _row(g_mem_q.astype(F32) * MEM_SCALE)
    gmix, gql, gkvl, gff, gmem, gmk, pscale = map(_row, (g_mix, g_q_lat, g_kv_lat, g_ff, g_mem, g_mem_k, pool_scale))

    ts = dec_batch * dec_seq
    xs = x_sample.reshape(ts, D_MODEL)
    h_s, u_s, hq_s, ckv_s, kpe_s, qm_s, wa_t, wkpe_t, wmq_t = _in_proj(xs, gmix, w_in_t, gql, gkvl, gmq)
    pos_s = past + jnp.arange(dec_seq, dtype=jnp.int32)
    cos_s, sin_s = _rope_tables(pos_s)
    q_s = _q_only(hq_s, wq, gq_s, jnp.tile(_query_table(cos_s, sin_s), (dec_batch, 1)), ts)
    qa, qr = _q_absorb(q_s, wkbt)
    cos_c, sin_c = _rope_tables(jnp.arange(past, dtype=jnp.int32))
    pad_new = 128 - dec_seq
    nlat = jnp.pad(ckv_s.reshape(dec_batch, dec_seq, KV_LORA), ((0, 0), (0, pad_new), (0, 0)))
    nkpet = jnp.pad(jnp.swapaxes(kpe_s[:, :QK_ROPE].reshape(dec_batch, dec_seq, QK_ROPE), 1, 2),
                    ((0, 0), (0, 0), (0, pad_new)))
    o_lat = _cache_attn(wkbt, qa, qr, cache_mla_latent, jnp.swapaxes(cache_mla_kpe, 1, 2),
                        cos_c.T, sin_c.T, nlat, nkpet,
                        jnp.pad(cos_s.T, ((0, 0), (0, pad_new))), jnp.pad(sin_s.T, ((0, 0), (0, pad_new))),
                        g_k_head[QK_NOPE:].astype(F32).reshape(QK_ROPE, 1),
                        dec_seq, dec_seq, 1024)
    o_s = _v_up(o_lat, wvbt, dec_seq)
    am_s = _mem_attn(qm_s, cache_mem_k.reshape(dec_batch * N_MEM, MEM_W), cache_mem_v.reshape(dec_batch * N_MEM, MEM_W),
                     0, 0, dec_seq, 1)
    left_s = jnp.pad(state_pool.astype(F32), ((0, 0), (POOL_HALO - POOL_STATE, 0), (0, 0)))
    d_s = _pool(u_s, left_s, dec_seq, 1, past)
    merged_s, *wgates, wpool, wmla, wmem_o = _merge(h_s, w_in_t, bgate, d_s, w_pool, pscale, o_s, w_mla_o,
                                                    am_s, w_mem_o, ts, 256, emit_bf16=True)
    x1_s, wout = _out_proj(xs, merged_s, w_out, ts, 512, emit_bf16=True)
    y_s, wup, wdown = _ffn(x1_s, gff, w_up, w_down, ts, 512, emit_bf16=True)
    y_s = y_s.reshape(dec_batch, dec_seq, D_MODEL)
    lat_s = ckv_s.reshape(dec_batch, dec_seq, KV_LORA)
    kpe_s_out = kpe_s[:, :QK_ROPE].reshape(dec_batch, dec_seq, QK_ROPE)
    pool_s = u_s.reshape(dec_batch, dec_seq, POOL_W)[:, dec_seq - POOL_STATE:]

    tm = 512
    xp = x_prompt.reshape(batch * seq, D_MODEL)
    mem_kv = _mem_kv(mem_prompt.reshape(batch * N_MEM, D_MODEL), gmem, w_mem_kv, gmk)
    h, d, hq, ckv, kpe, am, u_tail = _front(xp, gmix, wa_t, wkpe_t, wmq_t, gql, gkvl, gmq,
                                            jnp.zeros((batch, POOL_HALO, POOL_W), F32), mem_kv, tm, seq // tm, 0)
    cos_p, sin_p = _rope_tables(jnp.arange(seq, dtype=jnp.int32))
    q, k, vt = _qkv(hq, ckv, kpe, wq, wkbt, wvbt, gq_p, gkn, gkr,
                    (_query_table(cos_p, sin_p),) + _lane_tables(cos_p, sin_p), tm, seq // tm)
    o = _flash(q, k, vt, batch, seq, 2048)
    merged = _merge(h, wgates, bgate, d, wpool, pscale, o, wmla, am, wmem_o, 2 * tm, POOL_OUT_GW)
    y_p = _ffn(_out_proj(xp, merged, wout, 2 * tm, 1024), gff, wup, wdown, tm, 1024).reshape(batch, seq, D_MODEL)
    lat_p = ckv.reshape(batch, seq, KV_LORA)
    kpe_p = kpe[:, :QK_ROPE].reshape(batch, seq, QK_ROPE)
    pool_p = u_tail[:, POOL_HALO - POOL_STATE:]
    mem_k_p = mem_kv[:, :MEM_W].reshape(batch, N_MEM, MEM_HEADS, MEM_HEAD_DIM)
    mem_v_p = mem_kv[:, MEM_W:].reshape(batch, N_MEM, MEM_HEADS, MEM_HEAD_DIM)

    return (y_p, y_s, lat_p, kpe_p, pool_p, mem_k_p, mem_v_p, lat_s, kpe_s_out, pool_s)
```

```python
import functools

import jax
import jax.numpy as jnp
from jax import lax
from jax.experimental import pallas as pl
from jax.experimental.pallas import tpu as pltpu

F32 = jnp.float32
BF16 = jnp.bfloat16

EPS = 1e-6
CHUNK = 64
D_MODEL = 2048
POOL_WINDOWS = (2, 4, 8, 16)
POOL_W = 1024
POOL_GW = 256
POOL_OUT_GW = 512
POOL_STATE = 15
POOL_HALO = 16
assert all(w & (w - 1) == 0 and w <= POOL_HALO for w in POOL_WINDOWS)
MLA_HEADS = 16
Q_LORA = 512
KV_LORA = 512
QK_NOPE = 128
QK_ROPE = 64
QK_HEAD = QK_NOPE + QK_ROPE
HEAD_PAD = 256
V_HEAD = 128
V_ROWS = 144
ROPE_THETA = 10000.0
LOG2_E = 1.4426950408889634
MLA_SCALE = QK_HEAD ** -0.5
MLA_SCALE_LOG2 = MLA_SCALE * LOG2_E
N_MEM = 256
MEM_HEADS = 4
MEM_HEAD_DIM = 256
MEM_W = MEM_HEADS * MEM_HEAD_DIM
MEM_SCALE = MEM_HEAD_DIM ** -0.5
D_FF = 4 * D_MODEL
NEG_INF = -1e30
FLASH_AHEAD = 2
CACHE_AHEAD = 1
FRONT_ROW_GROUPS = 2

Z_U = 0
Z_QLAT = Z_U + POOL_W
Z_KVLAT = Z_QLAT + Q_LORA
Z_MQ = Z_KVLAT + KV_LORA
W_IN_KPE = Z_MQ
W_IN_MQ = W_IN_KPE + QK_ROPE
W_IN_GATE = W_IN_MQ + MEM_W

BF16_SUBLANES = 16
V7X_VMEM_BYTES = 64 * 1024 * 1024
VMEM_CEILING = V7X_VMEM_BYTES - 6 * 1024 * 1024


def _params(semantics, vmem_bytes):
    return pltpu.CompilerParams(dimension_semantics=semantics,
                                vmem_limit_bytes=int(min(vmem_bytes, VMEM_CEILING)))


def _as_bf16(w_ref, copy_ref):
    w = w_ref[...].astype(BF16)
    if copy_ref is not None:
        copy_ref[...] = w
    return w


def _col_block(rows, tn, chunk=None):
    if chunk is None:
        return pl.BlockSpec((rows, tn), lambda i, j: (0, j))
    per = chunk // tn
    return pl.BlockSpec((None, rows, tn), lambda i, j: (j // per, 0, j % per))


def _chunk_major(rows, n, chunk):
    return jax.ShapeDtypeStruct((n // chunk, rows, chunk), BF16)


def _rms(x, g):
    return x * lax.rsqrt(jnp.mean(x * x, axis=-1, keepdims=True) + EPS) * g


def _dot(a, b):
    return jnp.dot(a, b, preferred_element_type=F32)


def _dot_nt(a, b):
    return lax.dot_general(a, b, (((1,), (1,)), ((), ())), preferred_element_type=F32)


def _rope128(x, c, sa, sb):
    half = QK_ROPE // 2
    return x * c + pltpu.roll(x, half, 1) * sa + pltpu.roll(x, x.shape[1] - half, 1) * sb


def _in_proj_kernel(x_ref, gmix_ref, wa_ref, wkpe_ref, wmq_ref, gq_ref, gkv_ref, gmq_ref,
                    h_ref, u_ref, hq_ref, ckv_ref, kpe_ref, qm_ref, wa_copy, wkpe_copy, wmq_copy):
    def rows(ref, copy, lo, hi):
        w = ref[lo:hi, :].astype(BF16)
        copy[lo:hi, :] = w
        return w

    h = _rms(x_ref[...], gmix_ref[...]).astype(BF16)
    h_ref[...] = h
    u_ref[...] = _dot_nt(h, rows(wa_ref, wa_copy, Z_U, Z_QLAT))
    hq_ref[...] = _rms(_dot_nt(h, rows(wa_ref, wa_copy, Z_QLAT, Z_KVLAT)), gq_ref[...]).astype(BF16)
    ckv_ref[...] = _rms(_dot_nt(h, rows(wa_ref, wa_copy, Z_KVLAT, Z_MQ)), gkv_ref[...])
    kpe_ref[:, :QK_ROPE] = _dot_nt(h, rows(wkpe_ref, wkpe_copy, 0, QK_ROPE))
    kpe_ref[:, QK_ROPE:] = jnp.zeros((h.shape[0], 128 - QK_ROPE), F32)
    for hh in range(MEM_HEADS):
        lo = hh * MEM_HEAD_DIM
        m = _dot_nt(h, rows(wmq_ref, wmq_copy, lo, lo + MEM_HEAD_DIM))
        qm_ref[:, lo:lo + MEM_HEAD_DIM] = _rms(m, gmq_ref[...]).astype(BF16)


def _w_in_rows(n, start):
    return pl.BlockSpec((pl.Element(n), pl.Element(D_MODEL)),
                        lambda i: (pl.multiple_of(start + 0 * i, BF16_SUBLANES), 0))


def _front_kernel(x_ref, gmix_ref, wa_ref, wkpe_ref, wmq_ref, gq_ref, gkv_ref, gmq_ref, left_ref, mk_ref, mv_ref,
                  h_ref, d_ref, hq_ref, ckv_ref, kpe_ref, am_ref, utail_ref, xp_sc, qm_sc,
                  *, tm, blocks_per_seq, pos0):
    sb = pl.program_id(0) % blocks_per_seq

    @pl.when(sb == 0)
    def _():
        xp_sc[:POOL_HALO, :] = left_ref[...]

    @pl.when(sb != 0)
    def _():
        xp_sc[:POOL_HALO, :] = xp_sc[tm:, :]

    rows_per = tm // FRONT_ROW_GROUPS
    for r in range(FRONT_ROW_GROUPS):
        rs = slice(r * rows_per, (r + 1) * rows_per)
        h = _rms(x_ref[rs, :], gmix_ref[...]).astype(BF16)
        h_ref[rs, :] = h
        xp_sc[POOL_HALO + r * rows_per:POOL_HALO + (r + 1) * rows_per, :] = _dot_nt(h, wa_ref[Z_U:Z_QLAT, :])
        hq_ref[rs, :] = _rms(_dot_nt(h, wa_ref[Z_QLAT:Z_KVLAT, :]), gq_ref[...]).astype(BF16)
        ckv_ref[rs, :] = _rms(_dot_nt(h, wa_ref[Z_KVLAT:Z_MQ, :]), gkv_ref[...])
        kpe_ref[rs, :QK_ROPE] = _dot_nt(h, wkpe_ref[...])
        kpe_ref[rs, QK_ROPE:] = jnp.zeros((rows_per, 128 - QK_ROPE), F32)
        for hh in range(MEM_HEADS):
            lo = hh * MEM_HEAD_DIM
            m = _dot_nt(h, wmq_ref[lo:lo + MEM_HEAD_DIM, :])
            qm_sc[rs, lo:lo + MEM_HEAD_DIM] = _rms(m, gmq_ref[...]).astype(BF16)
    utail_ref[...] = xp_sc[tm:, :]
    _pool_windows(xp_sc, pos0 + sb * tm, tm, d_ref)
    _mem_attn_kernel(qm_sc, mk_ref, mv_ref, am_ref)


def _front(x, gmix, wa_t, wkpe_t, wmq_t, gq, gkv, gmq_scaled, left, mem_kv, tm, blocks_per_seq, pos0):
    t = x.shape[0]
    n_seq = t // (tm * blocks_per_seq)
    row = lambda w: pl.BlockSpec((tm, w), lambda i: (i, 0))
    full = lambda a: pl.BlockSpec(a.shape, lambda i: (0, 0))
    w_elems = wa_t.size + wkpe_t.size + wmq_t.size
    vmem = 2 * (tm * D_MODEL * 6 + w_elems * 2 + tm * (POOL_W * 2 + Q_LORA * 2 + KV_LORA * 4 + 128 * 4 + MEM_W * 2)
                + 2 * N_MEM * MEM_W * 4 + 2 * POOL_HALO * POOL_W * 4)
    vmem += (tm + POOL_HALO) * POOL_W * 4 + tm * MEM_W * 2 + 8 * tm * POOL_W * 4
    kern = functools.partial(_front_kernel, tm=tm, blocks_per_seq=blocks_per_seq, pos0=pos0)
    return pl.pallas_call(
        kern,
        grid=(t // tm,),
        in_specs=[row(D_MODEL), full(gmix), full(wa_t), full(wkpe_t), full(wmq_t),
                  full(gq), full(gkv), full(gmq_scaled),
                  pl.BlockSpec((None, POOL_HALO, POOL_W), lambda i: (i // blocks_per_seq, 0, 0)),
                  pl.BlockSpec((N_MEM, MEM_W), lambda i: (i // blocks_per_seq, 0)),
                  pl.BlockSpec((N_MEM, MEM_W), lambda i: (i // blocks_per_seq, 1))],
        out_specs=[row(D_MODEL), row(POOL_W), row(Q_LORA), row(KV_LORA), row(128), row(MEM_W),
                   pl.BlockSpec((None, POOL_HALO, POOL_W), lambda i: (i // blocks_per_seq, 0, 0))],
        out_shape=[jax.ShapeDtypeStruct((t, D_MODEL), BF16),
                   jax.ShapeDtypeStruct((t, POOL_W), BF16), jax.ShapeDtypeStruct((t, Q_LORA), BF16),
                   jax.ShapeDtypeStruct((t, KV_LORA), F32), jax.ShapeDtypeStruct((t, 128), F32),
                   jax.ShapeDtypeStruct((t, MEM_W), BF16), jax.ShapeDtypeStruct((n_seq, POOL_HALO, POOL_W), F32)],
        scratch_shapes=[pltpu.VMEM((tm + POOL_HALO, POOL_W), F32), pltpu.VMEM((tm, MEM_W), BF16)],
        compiler_params=_params(("arbitrary",), vmem),
        name="front",
    )(x, gmix, wa_t, wkpe_t, wmq_t, gq, gkv, gmq_scaled, left, mem_kv, mem_kv)


def _in_proj(x, gmix, w_in_t, gq, gkv, gmq_scaled):
    t = x.shape[0]
    row = lambda w: pl.BlockSpec((t, w), lambda i: (0, 0))
    full = lambda a: pl.BlockSpec(a.shape, lambda i: (0, 0))
    pieces = ((Z_MQ, 0), (QK_ROPE, W_IN_KPE), (MEM_W, W_IN_MQ))
    w_elems = sum(n for n, _ in pieces) * D_MODEL
    vmem = t * D_MODEL * 8 + w_elems * 6 + 2 * t * (POOL_W * 4 + Q_LORA * 2 + KV_LORA * 4 + 128 * 4 + MEM_W * 2)
    vmem += w_elems * 2 + 6 * t * POOL_W * 4
    return pl.pallas_call(
        _in_proj_kernel,
        grid=(1,),
        in_specs=[row(D_MODEL), full(gmix)] + [_w_in_rows(n, start) for n, start in pieces]
                 + [full(gq), full(gkv), full(gmq_scaled)],
        out_specs=[row(D_MODEL), row(POOL_W), row(Q_LORA), row(KV_LORA), row(128), row(MEM_W)]
                  + [pl.BlockSpec((n, D_MODEL), lambda i: (0, 0)) for n, _ in pieces],
        out_shape=[jax.ShapeDtypeStruct((t, D_MODEL), BF16),
                   jax.ShapeDtypeStruct((t, POOL_W), F32), jax.ShapeDtypeStruct((t, Q_LORA), BF16),
                   jax.ShapeDtypeStruct((t, KV_LORA), F32), jax.ShapeDtypeStruct((t, 128), F32),
                   jax.ShapeDtypeStruct((t, MEM_W), BF16)]
                  + [jax.ShapeDtypeStruct((n, D_MODEL), BF16) for n, _ in pieces],
        compiler_params=_params(("arbitrary",), vmem),
        name="in_proj",
    )(x, gmix, w_in_t, w_in_t, w_in_t, gq, gkv, gmq_scaled)


def _head_store(ref, head_major):
    def store(h, nope, rope):
        if head_major:
            ref[h, :, :QK_NOPE] = nope
            ref[h, :, QK_NOPE:] = rope
        else:
            lo = h * HEAD_PAD
            ref[:, lo:lo + QK_NOPE] = nope
            ref[:, lo + QK_NOPE:lo + HEAD_PAD] = rope
    return store


def _q_heads(hq, wq_ref, gq_ref, qtab, store):
    first = lax.broadcasted_iota(jnp.int32, (1, 128), 1) < QK_ROPE
    for h in range(MLA_HEADS):
        lo = h * HEAD_PAD
        qh = _dot(hq, wq_ref[:, lo:lo + HEAD_PAD])
        qn, qr = qh[:, :QK_NOPE], qh[:, QK_NOPE:]
        ss = jnp.sum(qn * qn + jnp.where(first, qr * qr, 0.0), axis=-1, keepdims=True)
        r = lax.rsqrt(ss * (1.0 / QK_HEAD) + EPS)
        store(h, (qn * r * gq_ref[:, :QK_NOPE]).astype(BF16), (qr * r * gq_ref[:, QK_NOPE:] * qtab).astype(BF16))


def _qkv_kernel(hq_ref, ckv_ref, kpe_ref, wq_ref, wkbt_ref, wvbt_ref, gq_ref, gkn_ref, gkr_ref,
                qtab_ref, c_ref, sa_ref, sb_ref, q_ref, k_ref, vt_ref):
    _q_heads(hq_ref[...], wq_ref, gq_ref, qtab_ref[...], _head_store(q_ref, True))
    store_k = _head_store(k_ref, True)
    ckv = ckv_ref[...].astype(BF16)
    kpe = kpe_ref[...]
    ss_pe = jnp.sum(kpe * kpe, axis=-1, keepdims=True)
    kr = _rope128(kpe * gkr_ref[...], c_ref[...], sa_ref[...], sb_ref[...])
    kr = kr + pltpu.roll(kr, QK_ROPE, 1)
    for p in range(MLA_HEADS // 2):
        kn2 = _dot_nt(ckv, wkbt_ref[p * 2 * QK_NOPE:(p + 1) * 2 * QK_NOPE, :])
        for s in range(2):
            kn = kn2[:, s * QK_NOPE:(s + 1) * QK_NOPE]
            r = lax.rsqrt((jnp.sum(kn * kn, axis=-1, keepdims=True) + ss_pe) * (1.0 / QK_HEAD) + EPS)
            store_k(2 * p + s, (kn * r * gkn_ref[...]).astype(BF16), (kr * r).astype(BF16))
    vt = _dot_nt(wvbt_ref[...], ckv)
    tm = vt.shape[1]
    ones_row = (lax.broadcasted_iota(jnp.int32, (V_ROWS - V_HEAD, tm), 0) == 0).astype(BF16)
    for h in range(MLA_HEADS):
        vt_ref[h, :V_HEAD, :] = vt[h * V_HEAD:(h + 1) * V_HEAD].astype(BF16)
        vt_ref[h, V_HEAD:, :] = ones_row


def _qkv(hq, ckv, kpe, wq, wkbt, wvbt, gq, gkn, gkr, tabs, tm, seq_blocks):
    t = hq.shape[0]
    n_seq = t // (tm * seq_blocks)
    qk_spec = pl.BlockSpec((None, MLA_HEADS, tm, HEAD_PAD), lambda i: (i // seq_blocks, 0, i % seq_blocks, 0))
    qk_shape = jax.ShapeDtypeStruct((n_seq, MLA_HEADS, tm * seq_blocks, HEAD_PAD), BF16)
    row = lambda w: pl.BlockSpec((tm, w), lambda i: (i, 0))
    full = lambda a: pl.BlockSpec(a.shape, lambda i: (0, 0))
    tab = pl.BlockSpec((tm, 128), lambda i: (i % seq_blocks, 0))
    vmem = 2 * (tm * (Q_LORA * 2 + KV_LORA * 4 + 128 * 4 + 4 * 128 * 4) + (wq.size + wkbt.size + wvbt.size) * 2
                + tm * (2 * MLA_HEADS * HEAD_PAD * 2 + MLA_HEADS * V_ROWS * 2))
    vmem += 8 * tm * HEAD_PAD * 4 + 2 * tm * MLA_HEADS * V_HEAD * 4
    return pl.pallas_call(
        _qkv_kernel,
        grid=(t // tm,),
        in_specs=[row(Q_LORA), row(KV_LORA), row(128), full(wq), full(wkbt), full(wvbt),
                  full(gq), full(gkn), full(gkr), tab, tab, tab, tab],
        out_specs=[qk_spec, qk_spec,
                   pl.BlockSpec((None, MLA_HEADS, None, V_ROWS, tm),
                                lambda i: (i // seq_blocks, 0, i % seq_blocks, 0, 0))],
        out_shape=[qk_shape, qk_shape,
                   jax.ShapeDtypeStruct((n_seq, MLA_HEADS, seq_blocks, V_ROWS, tm), BF16)],
        compiler_params=_params(("parallel",), vmem),
        name="qkv_proj",
    )(hq, ckv, kpe, wq, wkbt, wvbt, gq, gkn, gkr, *tabs)


def _q_only_kernel(hq_ref, wq_ref, gq_ref, qtab_ref, q_ref):
    _q_heads(hq_ref[...], wq_ref, gq_ref, qtab_ref[...], _head_store(q_ref, False))


def _q_only(hq, wq, gq, qtab, tm):
    t = hq.shape[0]
    row = lambda w: pl.BlockSpec((tm, w), lambda i: (i, 0))
    full = lambda a: pl.BlockSpec(a.shape, lambda i: (0, 0))
    vmem = 2 * (tm * (Q_LORA * 2 + 128 * 4 + MLA_HEADS * HEAD_PAD * 2) + wq.size * 2) + 8 * tm * HEAD_PAD * 4
    return pl.pallas_call(
        _q_only_kernel,
        grid=(t // tm,),
        in_specs=[row(Q_LORA), full(wq), full(gq), pl.BlockSpec((tm, 128), lambda i: (0, 0))],
        out_specs=row(MLA_HEADS * HEAD_PAD),
        out_shape=jax.ShapeDtypeStruct((t, MLA_HEADS * HEAD_PAD), BF16),
        compiler_params=_params(("parallel",), vmem),
        name="q_proj",
    )(hq, wq, gq, qtab)


def _flash_kernel(q_ref, k_ref, vt_ref, o_ref, *scratch, tq, tk, nq):
    *bufs, m_sc, acc_sc = scratch
    qi = pl.program_id(2)
    n_diag = tq // tk

    def scores(kj, s_sc, q_lo=0):
        s_sc[:, q_lo:] = _dot_nt(k_ref[kj * tk:(kj + 1) * tk, :], q_ref[q_lo:, :])

    def consume(kj, s_sc, key_chunk0=None, q_lo=0):
        s = s_sc[:, q_lo:]
        if key_chunk0 is not None:
            kc = key_chunk0 + lax.broadcasted_iota(jnp.int32, s.shape, 0) // CHUNK
            qc = (q_lo + lax.broadcasted_iota(jnp.int32, s.shape, 1)) // CHUNK
            s = jnp.where(qc >= kc, s, NEG_INF)
        m = m_sc[:, q_lo:]
        m_new = jnp.maximum(m, jnp.max(s, axis=0, keepdims=True))
        alpha = jnp.exp2(m - m_new)
        p = jnp.exp2(s - m_new).astype(BF16)
        acc_sc[:, q_lo:] = alpha * acc_sc[:, q_lo:] + _dot(vt_ref[kj], p)
        m_sc[:, q_lo:] = m_new

    def run(q_block):
        tiles = [(kj, 0, None) for kj in range(q_block * n_diag)]
        tiles += [(q_block * n_diag + d, d * tk, d * tk // CHUNK) for d in range(n_diag)]
        m_sc[...] = jnp.full(m_sc.shape, NEG_INF, F32)
        acc_sc[...] = jnp.zeros(acc_sc.shape, F32)
        for i in range(min(FLASH_AHEAD, len(tiles))):
            scores(tiles[i][0], bufs[i % len(bufs)], tiles[i][1])
        for i, (kj, q_lo, key_chunk0) in enumerate(tiles):
            if i + FLASH_AHEAD < len(tiles):
                nkj, nq_lo, _ = tiles[i + FLASH_AHEAD]
                scores(nkj, bufs[(i + FLASH_AHEAD) % len(bufs)], nq_lo)
            consume(kj, bufs[i % len(bufs)], key_chunk0, q_lo)
        acc = acc_sc[...]
        o_ref[...] = (acc[:V_HEAD] / acc[V_HEAD:V_HEAD + 1]).T.astype(BF16)

    for q_block in range(nq):
        pl.when(qi == q_block)(functools.partial(run, q_block))


def _flash(q, k, vt, batch, seq, tq):
    tk = vt.shape[-1]
    assert tq % tk == 0 and tk % CHUNK == 0 and seq % tq == 0
    nq = seq // tq
    n_bufs = FLASH_AHEAD + 1
    vmem = 2 * (tq * HEAD_PAD * 2 + seq * HEAD_PAD * 2 + seq * V_ROWS * 2 + tq * V_HEAD * 2)
    vmem += n_bufs * tk * tq * 4 + V_ROWS * tq * 4 + 4 * tk * tq * 4
    return pl.pallas_call(
        functools.partial(_flash_kernel, tq=tq, tk=tk, nq=nq),
        grid=(batch, MLA_HEADS, nq),
        in_specs=[pl.BlockSpec((None, None, tq, HEAD_PAD), lambda b, h, i: (b, h, i, 0)),
                  pl.BlockSpec((None, None, seq, HEAD_PAD), lambda b, h, i: (b, h, 0, 0)),
                  pl.BlockSpec((None, None, seq // tk, V_ROWS, tk), lambda b, h, i: (b, h, 0, 0, 0))],
        out_specs=pl.BlockSpec((tq, V_HEAD), lambda b, h, i: (b * nq + i, h)),
        out_shape=jax.ShapeDtypeStruct((batch * seq, MLA_HEADS * V_HEAD), BF16),
        scratch_shapes=[pltpu.VMEM((tk, tq), F32)] * n_bufs + [pltpu.VMEM((1, tq), F32),
                                                                pltpu.VMEM((V_ROWS, tq), F32)],
        compiler_params=_params(("parallel", "parallel", "arbitrary"), vmem),
        name="mla_flash",
    )(q, k, vt)


def _q_absorb_kernel(q_ref, wkbt_ref, qa_ref, qr_ref):
    qa_ref[...] = _dot(q_ref[:, :QK_NOPE], wkbt_ref[...]).astype(BF16)
    qr_ref[...] = q_ref[:, QK_NOPE:]


def _q_absorb(q, wkbt):
    t = q.shape[0]
    vmem = 2 * (t * HEAD_PAD * 2 + KV_LORA * QK_NOPE * 2 + t * KV_LORA * 2 + t * 128 * 2) + 2 * t * KV_LORA * 4
    return pl.pallas_call(
        _q_absorb_kernel,
        grid=(MLA_HEADS,),
        in_specs=[pl.BlockSpec((t, HEAD_PAD), lambda h: (0, h)),
                  pl.BlockSpec((QK_NOPE, KV_LORA), lambda h: (h, 0))],
        out_specs=[pl.BlockSpec((None, t, KV_LORA), lambda h: (h, 0, 0)),
                   pl.BlockSpec((None, t, 128), lambda h: (h, 0, 0))],
        out_shape=[jax.ShapeDtypeStruct((MLA_HEADS, t, KV_LORA), BF16),
                   jax.ShapeDtypeStruct((MLA_HEADS, t, 128), BF16)],
        compiler_params=_params(("parallel",), vmem),
        name="q_absorb",
    )(q, wkbt)


def _cache_attn_kernel(wkbt_ref, qa_ref, qr_ref, lat_ref, kpet_ref, ct_ref, st_ref,
                       nlat_ref, nkpet_ref, nct_ref, nst_ref, gkr_ref,
                       o_ref, *scratch, nq, n_new, past, tk):
    *bufs, m_sc, l_sc, acc_sc = scratch
    rows = MLA_HEADS * nq
    half = QK_ROPE // 2
    m_sc[...] = jnp.full(m_sc.shape, NEG_INF, F32)
    l_sc[...] = jnp.zeros(l_sc.shape, F32)
    acc_sc[...] = jnp.zeros(acc_sc.shape, F32)

    def scores(lat, kpet, ct, st, width, s_sc):
        c = lat.astype(BF16)
        kt = _dot_nt(wkbt_ref[...], c)
        s = _dot_nt(qa_ref[...].reshape(rows, KV_LORA), c)
        ss_pe = jnp.sum(kpet * kpet, axis=0, keepdims=True)
        kr = kpet * gkr_ref[...]
        k1, k2 = kr[:half], kr[half:]
        o1, o2 = k1 * ct - k2 * st, k2 * ct + k1 * st
        krot = jnp.concatenate([o1, o2, o1, o2], axis=0).astype(BF16)
        s = s + _dot(qr_ref[...].reshape(rows, 128), krot)
        for h in range(MLA_HEADS):
            kh = kt[h * QK_NOPE:(h + 1) * QK_NOPE]
            r = lax.rsqrt((jnp.sum(kh * kh, axis=0, keepdims=True) + ss_pe) * (1.0 / QK_HEAD) + EPS)
            s_sc[h * nq:(h + 1) * nq, :width] = s[h * nq:(h + 1) * nq] * r

    def consume(lat, width, s_sc, visible):
        s = s_sc[:, :width]
        if visible is not None:
            s = jnp.where(visible, s, NEG_INF)
        m = m_sc[...]
        m_new = jnp.maximum(m, jnp.max(s, axis=-1, keepdims=True))
        alpha = jnp.exp2(m - m_new)
        p = jnp.exp2(s - m_new)
        l_sc[...] = alpha * l_sc[...] + jnp.sum(p, axis=-1, keepdims=True)
        m_sc[...] = m_new
        acc_sc[...] = alpha * acc_sc[...] + _dot(p.astype(BF16), lat.astype(BF16))

    kk = lax.broadcasted_iota(jnp.int32, (1, 128), 1)
    new_visible = kk < n_new
    if (past + n_new - 1) // CHUNK != past // CHUNK:
        qpos = past + lax.broadcasted_iota(jnp.int32, (rows, 1), 0) % nq
        new_visible = new_visible & (qpos // CHUNK >= (past + kk) // CHUNK)
    blocks = []
    for j in range(past // tk):
        ks = slice(j * tk, (j + 1) * tk)
        blocks.append((lambda ks=ks: (lat_ref[ks, :], kpet_ref[:, ks], ct_ref[:, ks], st_ref[:, ks]), tk, None))
    blocks.append((lambda: (nlat_ref[...], nkpet_ref[...], nct_ref[...], nst_ref[...]), 128, new_visible))

    def issue(i):
        get, width, _ = blocks[i]
        scores(*get(), width, bufs[i % len(bufs)])

    for i in range(min(CACHE_AHEAD, len(blocks))):
        issue(i)
    for i, (get, width, visible) in enumerate(blocks):
        if i + CACHE_AHEAD < len(blocks):
            issue(i + CACHE_AHEAD)
        consume(get()[0], width, bufs[i % len(bufs)], visible)
    o_ref[...] = (acc_sc[...] / l_sc[...]).astype(BF16)


def _cache_attn(wkbt, qa, qr, lat, kpet, ct, st, nlat, nkpet, nct, nst, gkr, nq, n_new, tk):
    batch, past, _ = lat.shape
    assert past % tk == 0
    rows = MLA_HEADS * nq
    n_bufs = CACHE_AHEAD + 1
    kern = functools.partial(_cache_attn_kernel, nq=nq, n_new=n_new, past=past, tk=tk)
    vmem = 2 * (wkbt.size * 2 + rows * (KV_LORA + 128) * 2 + past * KV_LORA * 4 + 3 * 64 * past * 4 + rows * KV_LORA * 2)
    vmem += (MLA_HEADS * QK_NOPE + 4 * rows) * tk * 4 + tk * KV_LORA * 2 + rows * KV_LORA * 8 + n_bufs * rows * tk * 4
    whole = lambda a: pl.BlockSpec(a.shape, lambda b: (0,) * a.ndim)
    per_batch = lambda a: pl.BlockSpec((None,) + a.shape[1:], lambda b: (b,) + (0,) * (a.ndim - 1))
    return pl.pallas_call(
        kern,
        grid=(batch,),
        in_specs=[whole(wkbt),
                  pl.BlockSpec((MLA_HEADS, nq, KV_LORA), lambda b: (0, b, 0)),
                  pl.BlockSpec((MLA_HEADS, nq, 128), lambda b: (0, b, 0)),
                  per_batch(lat), per_batch(kpet), whole(ct), whole(st),
                  per_batch(nlat), per_batch(nkpet), whole(nct), whole(nst), whole(gkr)],
        out_specs=pl.BlockSpec((None, rows, KV_LORA), lambda b: (b, 0, 0)),
        out_shape=jax.ShapeDtypeStruct((batch, rows, KV_LORA), BF16),
        scratch_shapes=[pltpu.VMEM((rows, tk), F32)] * n_bufs
                       + [pltpu.VMEM((rows, 1), F32), pltpu.VMEM((rows, 1), F32), pltpu.VMEM((rows, KV_LORA), F32)],
        compiler_params=_params(("parallel",), vmem),
        name="mla_cache_attn",
    )(wkbt, qa, qr, lat, kpet, ct, st, nlat, nkpet, nct, nst, gkr)


def _v_up_kernel(ol_ref, wvbt_ref, o_ref):
    b, nq, _ = ol_ref.shape
    o_ref[...] = _dot_nt(ol_ref[...].reshape(b * nq, KV_LORA), wvbt_ref[...]).astype(BF16)


def _v_up(o_lat, wvbt, nq):
    batch = o_lat.shape[0]
    t = batch * nq
    vmem = 2 * (t * KV_LORA * 2 + KV_LORA * V_HEAD * 2 + t * V_HEAD * 2) + 2 * t * V_HEAD * 4
    return pl.pallas_call(
        _v_up_kernel,
        grid=(MLA_HEADS,),
        in_specs=[pl.BlockSpec((batch, nq, KV_LORA), lambda h: (0, h, 0)),
                  pl.BlockSpec((V_HEAD, KV_LORA), lambda h: (h, 0))],
        out_specs=pl.BlockSpec((t, V_HEAD), lambda h: (0, h)),
        out_shape=jax.ShapeDtypeStruct((t, MLA_HEADS * V_HEAD), BF16),
        compiler_params=_params(("parallel",), vmem),
        name="v_up",
    )(o_lat, wvbt)


def _mem_kv_kernel(mem_ref, gmem_ref, w_ref, gk_ref, kv_ref):
    j = pl.program_id(0)
    h = _rms(mem_ref[...], gmem_ref[...]).astype(BF16)

    @pl.when(j == 0)
    def _():
        for hh in range(MEM_HEADS):
            lo = hh * MEM_HEAD_DIM
            w = w_ref[:, lo:lo + MEM_HEAD_DIM].astype(BF16)
            kv_ref[:, lo:lo + MEM_HEAD_DIM] = _rms(_dot(h, w), gk_ref[...])

    @pl.when(j == 1)
    def _():
        kv_ref[...] = _dot(h, w_ref[...].astype(BF16))


def _mem_kv(mem, gmem, w, gk):
    t = mem.shape[0]
    vmem = 2 * (t * D_MODEL * 4 + D_MODEL * MEM_W * w.dtype.itemsize + t * MEM_W * 4) + t * D_MODEL * 6
    vmem += t * MEM_W * 4 + D_MODEL * MEM_W * 2
    return pl.pallas_call(
        _mem_kv_kernel,
        grid=(2,),
        in_specs=[pl.BlockSpec((t, D_MODEL), lambda j: (0, 0)), pl.BlockSpec((1, D_MODEL), lambda j: (0, 0)),
                  pl.BlockSpec((D_MODEL, MEM_W), lambda j: (0, j)), pl.BlockSpec((1, MEM_HEAD_DIM), lambda j: (0, 0))],
        out_specs=pl.BlockSpec((t, MEM_W), lambda j: (0, j)),
        out_shape=jax.ShapeDtypeStruct((t, 2 * MEM_W), F32),
        compiler_params=_params(("parallel",), vmem),
        name="memory_kv",
    )(mem, gmem, w, gk)


def _mem_attn_kernel(qm_ref, k_ref, v_ref, o_ref):
    for hh in range(MEM_HEADS):
        sl = slice(hh * MEM_HEAD_DIM, (hh + 1) * MEM_HEAD_DIM)
        s = _dot_nt(qm_ref[:, sl], k_ref[:, sl].astype(BF16))
        p = jnp.exp(s - jnp.max(s, axis=-1, keepdims=True))
        l = jnp.sum(p, axis=-1, keepdims=True)
        o_ref[:, sl] = (_dot(p.astype(BF16), v_ref[:, sl].astype(BF16)) / l).astype(BF16)


def _mem_attn(qm, k_arr, v_arr, k_col, v_col, tm, blocks_per_batch):
    t = qm.shape[0]
    vmem = 2 * (2 * tm * MEM_W * 2 + 2 * N_MEM * MEM_W * 4) + 6 * tm * N_MEM * 4 + 2 * N_MEM * MEM_W * 2
    return pl.pallas_call(
        _mem_attn_kernel,
        grid=(t // tm,),
        in_specs=[pl.BlockSpec((tm, MEM_W), lambda i: (i, 0)),
                  pl.BlockSpec((N_MEM, MEM_W), lambda i: (i // blocks_per_batch, k_col)),
                  pl.BlockSpec((N_MEM, MEM_W), lambda i: (i // blocks_per_batch, v_col))],
        out_specs=pl.BlockSpec((tm, MEM_W), lambda i: (i, 0)),
        out_shape=jax.ShapeDtypeStruct((t, MEM_W), BF16),
        compiler_params=_params(("parallel",), vmem),
        name="memory_attn",
    )(qm, k_arr, v_arr)


def _pool_windows(xp_sc, first_pos, tm, d_ref):
    pos = first_pos + lax.broadcasted_iota(jnp.int32, (tm, 1), 0)
    for g, win in enumerate(POOL_WINDOWS):
        sl = slice(g * POOL_GW, (g + 1) * POOL_GW)
        x = xp_sc[:, sl]
        wsum, k = x, 1
        while k < win:
            wsum = wsum + pltpu.roll(wsum, k, 0)
            k *= 2
        cnt = jnp.minimum(pos + 1, win).astype(F32)
        d_ref[:, sl] = (wsum[POOL_HALO:] / cnt - x[POOL_HALO:]).astype(BF16)


def _pool_kernel(u_ref, prev_ref, left_ref, d_ref, xp_sc, *, tm, blocks_per_seq, pos0):
    sb = pl.program_id(0) % blocks_per_seq
    xp_sc[:POOL_HALO, :] = jnp.where(sb == 0, left_ref[...], prev_ref[...])
    xp_sc[POOL_HALO:, :] = u_ref[...]
    _pool_windows(xp_sc, pos0 + sb * tm, tm, d_ref)


def _pool(u, left, tm, blocks_per_seq, pos0):
    t = u.shape[0]
    halo_per_blk = tm // POOL_HALO
    kern = functools.partial(_pool_kernel, tm=tm, blocks_per_seq=blocks_per_seq, pos0=pos0)
    vmem = 2 * (tm * POOL_W * 4 + 2 * POOL_HALO * POOL_W * 4 + tm * POOL_W * 2) + (tm + POOL_HALO) * POOL_W * 4
    vmem += 6 * tm * POOL_GW * 4
    return pl.pallas_call(
        kern,
        grid=(t // tm,),
        in_specs=[pl.BlockSpec((tm, POOL_W), lambda i: (i, 0)),
                  pl.BlockSpec((POOL_HALO, POOL_W), lambda i: (jnp.maximum(i * halo_per_blk - 1, 0), 0)),
                  pl.BlockSpec((None, POOL_HALO, POOL_W), lambda i: (i // blocks_per_seq, 0, 0))],
        out_specs=pl.BlockSpec((tm, POOL_W), lambda i: (i, 0)),
        out_shape=jax.ShapeDtypeStruct((t, POOL_W), BF16),
        scratch_shapes=[pltpu.VMEM((tm + POOL_HALO, POOL_W), F32)],
        compiler_params=_params(("parallel",), vmem),
        name="pool_windows",
    )(u, u, left)


def _merge_kernel(h_ref, wg0_ref, wg1_ref, wg2_ref, bg0_ref, bg1_ref, bg2_ref,
                  d_ref, wpool_ref, pscale_ref, o_ref, wmla_ref, am_ref, wmem_ref,
                  out_ref, *w_copies):
    wg_copies, (wpool_copy, wmla_copy, wmem_copy) = (w_copies[:3], w_copies[3:]) if w_copies else ((None,) * 3,) * 2
    h = h_ref[...]
    y_pool = _dot(d_ref[...], _as_bf16(wpool_ref, wpool_copy)) * pscale_ref[...]
    y_mla = _dot(o_ref[...], _as_bf16(wmla_ref, wmla_copy))
    y_mem = _dot(am_ref[...], _as_bf16(wmem_ref, wmem_copy))
    gate = lambda w_ref, copy, b_ref: jax.nn.sigmoid(_dot_nt(h, _as_bf16(w_ref, copy)) + b_ref[...])
    merged = gate(wg0_ref, wg_copies[0], bg0_ref) * y_pool
    merged += gate(wg1_ref, wg_copies[1], bg1_ref) * y_mla
    merged += gate(wg2_ref, wg_copies[2], bg2_ref) * y_mem
    out_ref[...] = merged.astype(BF16)


def _merge(h, wgates, bgate, d, wpool, pscale, o, wmla, am, wmem, tm, tn, emit_bf16=False, col_chunk=None):
    t = h.shape[0]
    assert not emit_bf16 or t == tm
    assert POOL_OUT_GW % tn == 0
    nj = D_MODEL // tn
    per_group = POOL_OUT_GW // tn
    rowfull = lambda w: pl.BlockSpec((tm, w), lambda i, j: (i, 0))
    gate_rows = pl.BlockSpec((tn, D_MODEL), lambda i, j: (j, 0))
    if emit_bf16:
        gate_in = [pl.BlockSpec((pl.Element(tn), pl.Element(D_MODEL)),
                                lambda i, j, b=b: (pl.multiple_of(W_IN_GATE + (b * nj + j) * tn, BF16_SUBLANES), 0))
                   for b in range(3)]
        gate_args = (wgates,) * 3
    else:
        gate_in = [gate_rows] * 3
        gate_args = tuple(wgates)
    gate_b = lambda b: pl.BlockSpec((1, tn), lambda i, j: (0, b * nj + j))
    col_in = lambda k: _col_block(k, tn, None if emit_bf16 else tn)
    col_out = lambda k: _col_block(k, tn, col_chunk)
    pool_spec = pl.BlockSpec((None, POOL_GW, tn), lambda i, j: (j // per_group, 0, j % per_group))
    w_bytes = wmla.dtype.itemsize + 2 * emit_bf16
    w_rows = 3 * D_MODEL + POOL_GW + D_MODEL + MEM_W
    vmem = 2 * (tm * D_MODEL * 2 + w_rows * tn * w_bytes + tm * POOL_GW * 2 + tm * D_MODEL * 2
                + tm * MEM_W * 2 + tm * tn * 2)
    vmem += 8 * tm * tn * 4 + w_rows * tn * 2 * emit_bf16
    y_spec = pl.BlockSpec((tm, tn), lambda i, j: (i, j))
    y_shape = jax.ShapeDtypeStruct((t, D_MODEL), BF16)
    copies = [jax.ShapeDtypeStruct((D_MODEL, D_MODEL), BF16)] * 3 + [jax.ShapeDtypeStruct(wpool.shape, BF16)]
    if emit_bf16:
        copies += [_chunk_major(D_MODEL, D_MODEL, col_chunk), _chunk_major(MEM_W, D_MODEL, col_chunk)]
    return pl.pallas_call(
        _merge_kernel,
        grid=(t // tm, nj),
        in_specs=[rowfull(D_MODEL)] + gate_in
                 + [gate_b(0), gate_b(1), gate_b(2),
                    pl.BlockSpec((tm, POOL_GW), lambda i, j: (i, j // per_group)), pool_spec,
                    pl.BlockSpec((1, tn), lambda i, j: (0, j)),
                    rowfull(D_MODEL), col_in(D_MODEL), rowfull(MEM_W), col_in(MEM_W)],
        out_specs=([y_spec] + [gate_rows] * 3 + [pool_spec, col_out(D_MODEL), col_out(MEM_W)]
                   if emit_bf16 else y_spec),
        out_shape=[y_shape] + copies if emit_bf16 else y_shape,
        compiler_params=_params(("parallel", "arbitrary" if emit_bf16 else "parallel"), vmem),
        name="gated_merge",
    )(h, *gate_args, bgate, bgate, bgate, d, wpool, pscale, o, wmla, am, wmem)


def _out_proj_kernel(x_ref, m_ref, w_ref, y_ref, *w_copy):
    y_ref[...] = x_ref[...] + _dot(m_ref[...], _as_bf16(w_ref, *w_copy or (None,)))


def _out_proj(x, merged, w, tm, tn, emit_bf16=False, col_chunk=None):
    t = x.shape[0]
    assert not emit_bf16 or t == tm
    w_spec = _col_block(D_MODEL, tn, None if emit_bf16 else tn)
    vmem = 2 * (2 * tm * tn * 4 + tm * D_MODEL * 2 + D_MODEL * tn * (w.dtype.itemsize + 2 * emit_bf16)) + 2 * tm * tn * 4
    y_spec = pl.BlockSpec((tm, tn), lambda i, j: (i, j))
    y_shape = jax.ShapeDtypeStruct((t, D_MODEL), F32)
    return pl.pallas_call(
        _out_proj_kernel,
        grid=(t // tm, D_MODEL // tn),
        in_specs=[pl.BlockSpec((tm, tn), lambda i, j: (i, j)),
                  pl.BlockSpec((tm, D_MODEL), lambda i, j: (i, 0)), w_spec],
        out_specs=[y_spec, _col_block(D_MODEL, tn, col_chunk)] if emit_bf16 else y_spec,
        out_shape=[y_shape, _chunk_major(D_MODEL, D_MODEL, col_chunk)] if emit_bf16 else y_shape,
        compiler_params=_params(("parallel", "arbitrary" if emit_bf16 else "parallel"), vmem),
        name="out_proj",
    )(x, merged, w)


def _ffn_kernel(x_ref, g_ref, wup_ref, wdown_ref, y_ref, *rest):
    *w_copies, h_sc, acc_sc = rest
    wup_copy, wdown_copy = w_copies or (None, None)
    j = pl.program_id(1)

    @pl.when(j == 0)
    def _():
        h_sc[...] = _rms(x_ref[...], g_ref[...]).astype(BF16)
        acc_sc[...] = jnp.zeros(acc_sc.shape, F32)

    f = jnp.maximum(_dot(h_sc[...], _as_bf16(wup_ref, wup_copy)), 0.0)
    acc_sc[...] += _dot((f * f).astype(BF16), _as_bf16(wdown_ref, wdown_copy))

    @pl.when(j == pl.num_programs(1) - 1)
    def _():
        y_ref[...] = x_ref[...] + acc_sc[...]


def _ffn(x, g, wup, wdown, tm, tf, emit_bf16=False, col_chunk=None):
    t = x.shape[0]
    assert not emit_bf16 or t == tm
    w_bytes = wup.dtype.itemsize + 2 * emit_bf16
    vmem = 2 * (2 * tm * D_MODEL * 4 + 2 * D_MODEL * tf * w_bytes) + tm * D_MODEL * 6 + 3 * tm * tf * 4 + tm * D_MODEL * 4
    vmem += 2 * D_MODEL * tf * 2 * emit_bf16
    up_spec = _col_block(D_MODEL, tf, None if emit_bf16 else tf)
    down_spec = pl.BlockSpec((tf, D_MODEL), lambda i, j: (j, 0))
    y_spec = pl.BlockSpec((tm, D_MODEL), lambda i, j: (i, 0))
    y_shape = jax.ShapeDtypeStruct((t, D_MODEL), F32)
    return pl.pallas_call(
        _ffn_kernel,
        grid=(t // tm, D_FF // tf),
        in_specs=[pl.BlockSpec((tm, D_MODEL), lambda i, j: (i, 0)),
                  pl.BlockSpec((1, D_MODEL), lambda i, j: (0, 0)), up_spec, down_spec],
        out_specs=[y_spec, _col_block(D_MODEL, tf, col_chunk), down_spec] if emit_bf16 else y_spec,
        out_shape=([y_shape, _chunk_major(D_MODEL, D_FF, col_chunk), jax.ShapeDtypeStruct(wdown.shape, BF16)]
                   if emit_bf16 else y_shape),
        scratch_shapes=[pltpu.VMEM((tm, D_MODEL), BF16), pltpu.VMEM((tm, D_MODEL), F32)],
        compiler_params=_params(("parallel", "arbitrary"), vmem),
        name="ffn",
    )(x, g, wup, wdown)


def _rope_tables(pos):
    half = QK_ROPE // 2
    inv = 1.0 / (ROPE_THETA ** (jnp.arange(half, dtype=F32) * (2.0 / QK_ROPE)))
    ang = pos.astype(F32)[:, None] * inv[None, :]
    return jnp.cos(ang), jnp.sin(ang)


def _lane_tables(cos, sin):
    z = jnp.zeros_like(cos)
    c = jnp.concatenate([cos, cos, z, z], axis=1)
    sa = jnp.concatenate([z, sin, z, z], axis=1)
    sb = jnp.concatenate([-sin, z, z, z], axis=1)
    return c, sa, sb


def _query_table(cos, sin):
    return jnp.concatenate([cos, cos, -sin, sin], axis=1)


def _row(v):
    return v.astype(F32).reshape(1, -1)


def _pad_last(a, width):
    return jnp.pad(a, [(0, 0)] * (a.ndim - 1) + [(0, width - a.shape[-1])])


def kernel(x_prompt, mem_prompt, x_sample, cache_mla_latent, cache_mla_kpe, state_pool, cache_mem_k, cache_mem_v,
           g_mix, w_in, b_gate, w_pool, pool_scale, g_q_lat, w_qb, g_q_head, g_kv_lat, w_kb, w_vb, g_k_head,
           w_mla_o, g_mem, w_mem_kv, g_mem_q, g_mem_k, w_mem_o, w_out, g_ff, w_up, w_down):
    batch, seq, _ = x_prompt.shape
    dec_batch, dec_seq, _ = x_sample.shape
    past = cache_mla_latent.shape[1]

    w_in_t = w_in.T
    bgate = _row(b_gate)
    half = QK_ROPE // 2

    def swap_halves(a):
        return jnp.concatenate([a[..., half:], a[..., :half]], axis=-1)

    def head_layout(a):
        rope = a[..., QK_NOPE:]
        return jnp.concatenate([a, swap_halves(rope)], axis=-1)

    wq = head_layout(w_qb.reshape(Q_LORA, MLA_HEADS, QK_HEAD)).reshape(Q_LORA, MLA_HEADS * HEAD_PAD).astype(BF16)
    wkbt = w_kb.T.astype(BF16)
    wvbt = w_vb.T.astype(BF16)
    gq_p = _row(head_layout(g_q_head.astype(F32) * MLA_SCALE_LOG2))
    gkn = _row(g_k_head[:QK_NOPE])
    gkr = _row(_pad_last(g_k_head[QK_NOPE:], 128))
    gq_s = _row(head_layout(jnp.concatenate([g_q_head[:QK_NOPE].astype(F32) * g_k_head[:QK_NOPE].astype(F32),
                                             g_q_head[QK_NOPE:].astype(F32)]) * MLA_SCALE_LOG2))
    gmq = _row(g_mem_q.astype(F32) * MEM_SCALE)
    gmix, gql, gkvl, gff, gmem, gmk, pscale = map(_row, (g_mix, g_q_lat, g_kv_lat, g_ff, g_mem, g_mem_k, pool_scale))

    ts = dec_batch * dec_seq
    xs = x_sample.reshape(ts, D_MODEL)
    h_s, u_s, hq_s, ckv_s, kpe_s, qm_s, wa_t, wkpe_t, wmq_t = _in_proj(xs, gmix, w_in_t, gql, gkvl, gmq)
    pos_s = past + jnp.arange(dec_seq, dtype=jnp.int32)
    cos_s, sin_s = _rope_tables(pos_s)
    q_s = _q_only(hq_s, wq, gq_s, jnp.tile(_query_table(cos_s, sin_s), (dec_batch, 1)), ts)
    qa, qr = _q_absorb(q_s, wkbt)
    cos_c, sin_c = _rope_tables(jnp.arange(past, dtype=jnp.int32))
    pad_new = 128 - dec_seq
    nlat = jnp.pad(ckv_s.reshape(dec_batch, dec_seq, KV_LORA), ((0, 0), (0, pad_new), (0, 0)))
    nkpet = jnp.pad(jnp.swapaxes(kpe_s[:, :QK_ROPE].reshape(dec_batch, dec_seq, QK_ROPE), 1, 2),
                    ((0, 0), (0, 0), (0, pad_new)))
    o_lat = _cache_attn(wkbt, qa, qr, cache_mla_latent, jnp.swapaxes(cache_mla_kpe, 1, 2),
                        cos_c.T, sin_c.T, nlat, nkpet,
                        jnp.pad(cos_s.T, ((0, 0), (0, pad_new))), jnp.pad(sin_s.T, ((0, 0), (0, pad_new))),
                        g_k_head[QK_NOPE:].astype(F32).reshape(QK_ROPE, 1),
                        dec_seq, dec_seq, 1024)
    o_s = _v_up(o_lat, wvbt, dec_seq)
    am_s = _mem_attn(qm_s, cache_mem_k.reshape(dec_batch * N_MEM, MEM_W), cache_mem_v.reshape(dec_batch * N_MEM, MEM_W),
                     0, 0, dec_seq, 1)
    left_s = jnp.pad(state_pool.astype(F32), ((0, 0), (POOL_HALO - POOL_STATE, 0), (0, 0)))
    d_s = _pool(u_s, left_s, dec_seq, 1, past)
    merged_s, *wgates, wpool, wmla, wmem_o = _merge(h_s, w_in_t, bgate, d_s, w_pool, pscale, o_s, w_mla_o,
                                                    am_s, w_mem_o, ts, 256, emit_bf16=True, col_chunk=POOL_OUT_GW)
    x1_s, wout = _out_proj(xs, merged_s, w_out, ts, 512, emit_bf16=True, col_chunk=1024)
    y_s, wup, wdown = _ffn(x1_s, gff, w_up, w_down, ts, 512, emit_bf16=True, col_chunk=1024)
    y_s = y_s.reshape(dec_batch, dec_seq, D_MODEL)
    lat_s = ckv_s.reshape(dec_batch, dec_seq, KV_LORA)
    kpe_s_out = kpe_s[:, :QK_ROPE].reshape(dec_batch, dec_seq, QK_ROPE)
    pool_s = u_s.reshape(dec_batch, dec_seq, POOL_W)[:, dec_seq - POOL_STATE:]

    tm = 512
    xp = x_prompt.reshape(batch * seq, D_MODEL)
    mem_kv = _mem_kv(mem_prompt.reshape(batch * N_MEM, D_MODEL), gmem, w_mem_kv, gmk)
    h, d, hq, ckv, kpe, am, u_tail = _front(xp, gmix, wa_t, wkpe_t, wmq_t, gql, gkvl, gmq,
                                            jnp.zeros((batch, POOL_HALO, POOL_W), F32), mem_kv, tm, seq // tm, 0)
    cos_p, sin_p = _rope_tables(jnp.arange(seq, dtype=jnp.int32))
    q, k, vt = _qkv(hq, ckv, kpe, wq, wkbt, wvbt, gq_p, gkn, gkr,
                    (_query_table(cos_p, sin_p),) + _lane_tables(cos_p, sin_p), tm, seq // tm)
    o = _flash(q, k, vt, batch, seq, 2048)
    merged = _merge(h, wgates, bgate, d, wpool, pscale, o, wmla, am, wmem_o, 2 * tm, POOL_OUT_GW)
    y_p = _ffn(_out_proj(xp, merged, wout, 2 * tm, 1024), gff, wup, wdown, tm, 1024).reshape(batch, seq, D_MODEL)
    lat_p = ckv.reshape(batch, seq, KV_LORA)
    kpe_p = kpe[:, :QK_ROPE].reshape(batch, seq, QK_ROPE)
    pool_p = u_tail[:, POOL_HALO - POOL_STATE:]
    mem_k_p = mem_kv[:, :MEM_W].reshape(batch, N_MEM, MEM_HEADS, MEM_HEAD_DIM)
    mem_v_p = mem_kv[:, MEM_W:].reshape(batch, N_MEM, MEM_HEADS, MEM_HEAD_DIM)

    return (y_p, y_s, lat_p, kpe_p, pool_p, mem_k_p, mem_v_p, lat_s, kpe_s_out, pool_s)
```
